```python
import math
import jax
import jax.numpy as jnp
from jax import lax
import numpy as np

D_MODEL = 1024
BATCH = 32
SEQ = 256
DEPTH = 1
DEC_BATCH = 4
DEC_SEQ = 1024
PAST_LEN = 256

GRID_W = 64
H_A = D_MODEL // 128
DK_A = 64
DV_A = 64
H_B = D_MODEL // 256
DQK_B = 64
DV_B = 128
A_QK = H_A * DK_A
A_V = H_A * DV_A
B_QK = H_B * 2 * DQK_B
B_V = H_B * DV_B
MIX_WIDTH = A_V + B_V
IN_SIZES = (A_QK, A_QK, A_V, A_V, 2 * H_A, 2 * H_A, B_QK, B_QK, B_V)
IN_COLS = 2 * A_QK + 2 * A_V + 4 * H_A + 2 * B_QK + B_V
CONV_K = 5
CHUNK = 64
Q_BLOCK = 128
D_FF = ((8 * D_MODEL // 3 + 127) // 128) * 128
N_MOD = 9
ROPE_THETA = 10000.0
ROPE_AXIS_DIM = DQK_B // 2
ROPE_PAIRS = ROPE_AXIS_DIM // 2
EPS = 1e-6

kernel_name = 'hybrid_deltanet_diffattn_prefix_dit_step'


def rmsnorm(x, gain):
    xf = x.astype(jnp.float32)
    y = xf * lax.rsqrt(jnp.mean(xf * xf, axis=-1, keepdims=True) + EPS)
    return (y * gain.astype(jnp.float32)).astype(x.dtype)


def l2norm(x):
    xf = x.astype(jnp.float32)
    return xf * lax.rsqrt(jnp.sum(xf * xf, axis=-1, keepdims=True) + EPS)


def adaln(cond, w_ada, b_ada):
    m = jax.nn.silu(cond) @ w_ada + b_ada
    return m.reshape(cond.shape[:-1] + (1, N_MOD, D_MODEL))


def modulated_norm(x, gain, shift, scale):
    return rmsnorm(x, gain) * (1 + scale) + shift


def swiglu(h, w_in, w_out):
    gate, up = jnp.split(h @ w_in, 2, axis=-1)
    return (jax.nn.silu(gate) * up) @ w_out


def split_cols(z, sizes):
    out, start = [], 0
    for s in sizes:
        out.append(z[..., start:start + s])
        start += s
    return out


def centred_dwconv(x, w):
    return lax.conv_general_dilated(
        x, w[:, None, :].astype(x.dtype), window_strides=(1,),
        padding=[(CONV_K // 2, CONV_K // 2)],
        dimension_numbers=('NWC', 'WIO', 'NWC'), feature_group_count=x.shape[-1])


def axial_rope(x):
    length = x.shape[1]
    n_rows = length // GRID_W
    pos_row = jnp.repeat(jnp.arange(n_rows), GRID_W).astype(jnp.float32)
    pos_col = jnp.tile(jnp.arange(GRID_W), n_rows).astype(jnp.float32)
    inv = ROPE_THETA ** (-jnp.arange(ROPE_PAIRS, dtype=jnp.float32) / ROPE_PAIRS)

    def rot(v, pos):
        ang = pos[:, None] * inv
        cos = jnp.cos(ang)[None, :, None, None, :]
        sin = jnp.sin(ang)[None, :, None, None, :]
        v1, v2 = v[..., :ROPE_PAIRS], v[..., ROPE_PAIRS:]
        return jnp.concatenate([v1 * cos - v2 * sin, v2 * cos + v1 * sin], axis=-1)

    xf = x.astype(jnp.float32)
    out = jnp.concatenate([rot(xf[..., :ROPE_AXIS_DIM], pos_row), rot(xf[..., ROPE_AXIS_DIM:], pos_col)], axis=-1)
    return out.astype(x.dtype)


def diff_attention(q, k, v, lam):
    bsz, lq = q.shape[0], q.shape[1]
    nb = lq // Q_BLOCK
    qb = jnp.moveaxis(q.reshape((bsz, nb, Q_BLOCK) + q.shape[2:]), 1, 0)
    scale = DQK_B ** -0.5

    def one_block(qblk):
        s = jnp.einsum('bqhmd,bkhmd->bhmqk', qblk, k).astype(jnp.float32) * scale
        p = jax.nn.softmax(s, axis=-1)
        a = p[:, :, 0] - lam * p[:, :, 1]
        return jnp.einsum('bhqk,bkhe->bqhe', a.astype(v.dtype), v)

    o = lax.map(one_block, qb)
    return jnp.moveaxis(o, 0, 1).reshape((bsz, lq) + o.shape[3:])


def gated_delta_chunked(q, k, v, g, beta, s0):
    bsz, length, nh, dk = q.shape
    dv = v.shape[-1]
    n = length // CHUNK

    def blocks(t):
        t = t.reshape((bsz, n, CHUNK, nh) + t.shape[3:])
        return jnp.moveaxis(t, 3, 1)

    qc = blocks(q) * (dk ** -0.5)
    kc = blocks(k)
    vc = blocks(v)
    gc = jnp.cumsum(blocks(g), axis=-1)
    bc = blocks(beta)
    idx = jnp.arange(CHUNK)
    incl = idx[:, None] >= idx[None, :]
    strict = idx[:, None] > idx[None, :]
    decay = jnp.exp(jnp.where(incl, gc[..., :, None] - gc[..., None, :], -jnp.inf))
    kk = jnp.einsum('bhncd,bhnsd->bhncs', kc, kc)
    a_mat = jnp.where(strict, bc[..., :, None] * kk * decay, 0.0) + jnp.eye(CHUNK, dtype=jnp.float32)
    rhs = jnp.concatenate([bc[..., None] * vc, (bc * jnp.exp(gc))[..., None] * kc], axis=-1)
    sol = lax.linalg.triangular_solve(a_mat, rhs, left_side=True, lower=True, unit_diagonal=True)
    u0, w = sol[..., :dv], sol[..., dv:]
    qk = jnp.where(incl, jnp.einsum('bhncd,bhnsd->bhncs', qc, kc) * decay, 0.0)
    qg = qc * jnp.exp(gc)[..., None]
    kg = kc * jnp.exp(gc[..., -1:] - gc)[..., None]
    glast = jnp.exp(gc[..., -1])
    xs = (jnp.moveaxis(qg, 2, 0), jnp.moveaxis(kg, 2, 0), jnp.moveaxis(u0, 2, 0),
          jnp.moveaxis(w, 2, 0), jnp.moveaxis(qk, 2, 0), jnp.moveaxis(glast, 2, 0))

    def step(state, inp):
        qg_n, kg_n, u0_n, w_n, qk_n, gl_n = inp
        u = u0_n - jnp.einsum('bhcd,bhde->bhce', w_n, state)
        o = jnp.einsum('bhcd,bhde->bhce', qg_n, state) + jnp.einsum('bhcs,bhse->bhce', qk_n, u)
        state = state * gl_n[..., None, None] + jnp.einsum('bhcd,bhce->bhde', kg_n, u)
        return state, o

    s_final, o = lax.scan(step, s0, xs)
    o = jnp.moveaxis(jnp.moveaxis(o, 0, 2), 1, 3).reshape(bsz, length, nh, dv)
    return o, s_final


def mixer(h, lp, lam_init, ctx_k, ctx_v, s0):
    bsz, length, _ = h.shape
    aq, ak, av, az, a_dec, a_beta, bq, bk, bv = split_cols(h @ lp['w_in'], IN_SIZES)

    qkv = jax.nn.silu(centred_dwconv(jnp.concatenate([aq, ak, av], axis=-1), lp['conv_w']))
    aq, ak, av = jnp.split(qkv, [A_QK, 2 * A_QK], axis=-1)
    qa = l2norm(aq.reshape(bsz, length, H_A, DK_A))
    ka = l2norm(ak.reshape(bsz, length, H_A, DK_A))
    va = av.reshape(bsz, length, H_A, DV_A).astype(jnp.float32)
    a_dec = a_dec.reshape(bsz, length, 2, H_A).astype(jnp.float32)
    g = -jnp.exp(lp['a_log'].astype(jnp.float32)) * jax.nn.softplus(a_dec + lp['dt_bias'].astype(jnp.float32))
    beta = jax.nn.sigmoid(a_beta.reshape(bsz, length, 2, H_A).astype(jnp.float32))
    if s0 is None:
        s0 = jnp.zeros((bsz, 2, H_A, DK_A, DV_A), jnp.float32)
    s0 = s0.astype(jnp.float32)
    o_f, s_f = gated_delta_chunked(qa, ka, va, g[:, :, 0], beta[:, :, 0], s0[:, 0])
    o_b, s_b = gated_delta_chunked(qa[:, ::-1], ka[:, ::-1], va[:, ::-1],
                                   g[:, ::-1, 1], beta[:, ::-1, 1], s0[:, 1])
    o_a = rmsnorm(o_f + o_b[:, ::-1], lp['delta_norm'])
    o_a = (o_a * jax.nn.silu(az.reshape(bsz, length, H_A, DV_A).astype(jnp.float32))).astype(h.dtype)
    state = jnp.stack([s_f, s_b], axis=1).astype(h.dtype)

    qb = bq.reshape(bsz, length, H_B, 2, DQK_B)
    kb = bk.reshape(bsz, length, H_B, 2, DQK_B)
    vb = bv.reshape(bsz, length, H_B, DV_B)
    f32 = jnp.float32
    lam = (jnp.exp(jnp.sum(lp['lambda_q1'].astype(f32) * lp['lambda_k1'].astype(f32)))
           - jnp.exp(jnp.sum(lp['lambda_q2'].astype(f32) * lp['lambda_k2'].astype(f32))) + lam_init)
    if ctx_k is None:
        o_b_att = diff_attention(qb, kb, vb, lam)
    else:
        keys = jnp.concatenate([ctx_k.astype(kb.dtype), axial_rope(kb)], axis=1)
        vals = jnp.concatenate([ctx_v.astype(vb.dtype), vb], axis=1)
        o_b_att = diff_attention(axial_rope(qb), keys, vals, lam)
    o_b_att = rmsnorm(o_b_att, lp['diff_norm']) * (1 - lam_init)

    y = jnp.concatenate([o_a.reshape(bsz, length, A_V), o_b_att.reshape(bsz, length, B_V)], axis=-1) @ lp['w_out']
    return y, (kb, vb, state)


def trunk_layer(x, mod, lp, lam_init, ctx_k, ctx_v, s0):
    x = x + 0.5 * mod[..., 2, :] * swiglu(modulated_norm(x, lp['norm_ffn1'], mod[..., 0, :], mod[..., 1, :]),
                                          lp['w_ffn1_in'], lp['w_ffn1_out'])
    h = modulated_norm(x, lp['norm_mix'], mod[..., 3, :], mod[..., 4, :])
    y, ctx_tensors = mixer(h, lp, lam_init, ctx_k, ctx_v, s0)
    x = x + mod[..., 5, :] * y
    x = x + 0.5 * mod[..., 8, :] * swiglu(modulated_norm(x, lp['norm_ffn2'], mod[..., 6, :], mod[..., 7, :]),
                                          lp['w_ffn2_in'], lp['w_ffn2_out'])
    return x, ctx_tensors


def setup_inputs(seed: int = 0) -> dict:
    key = jax.random.key(seed)
    ks = jax.random.split(key, 28)
    f32 = jnp.float32

    def nrm(k, shape, scale=1.0):
        return jax.random.normal(k, shape, f32) * scale

    def gain(k, shape):
        return 1.0 + 0.02 * jax.random.normal(k, shape, f32)

    dt = jnp.exp(jax.random.uniform(ks[16], (DEPTH, 2, H_A), f32, math.log(1e-3), math.log(1e-1)))
    return {
        'x_prompt': nrm(ks[0], (BATCH, SEQ, D_MODEL)),
        'x_sample': nrm(ks[1], (DEC_BATCH, DEC_SEQ, D_MODEL)),
        'cache_diff_k': nrm(ks[2], (DEC_BATCH, DEPTH, PAST_LEN, H_B, 2, DQK_B)),
        'cache_diff_v': nrm(ks[3], (DEC_BATCH, DEPTH, PAST_LEN, H_B, DV_B)),
        'state_delta': nrm(ks[4], (DEC_BATCH, DEPTH, 2, H_A, DK_A, DV_A), 0.1),
        'c': nrm(ks[5], (DEC_BATCH, D_MODEL)),
        'c_ctx': nrm(ks[6], (D_MODEL,)),
        'w_ada': nrm(ks[7], (DEPTH, D_MODEL, N_MOD * D_MODEL), 0.5 * D_MODEL ** -0.5),
        'b_ada': nrm(ks[8], (DEPTH, N_MOD * D_MODEL), 0.02),
        'norm_ffn1': gain(ks[9], (DEPTH, D_MODEL)),
        'w_ffn1_in': nrm(ks[10], (DEPTH, D_MODEL, 2 * D_FF), D_MODEL ** -0.5),
        'w_ffn1_out': nrm(ks[11], (DEPTH, D_FF, D_MODEL), D_FF ** -0.5),
        'norm_mix': gain(ks[12], (DEPTH, D_MODEL)),
        'w_in': nrm(ks[13], (DEPTH, D_MODEL, IN_COLS), D_MODEL ** -0.5),
        'conv_w': nrm(ks[14], (DEPTH, CONV_K, 2 * A_QK + A_V), CONV_K ** -0.5),
        'a_log': jnp.log(jax.random.uniform(ks[15], (DEPTH, 2, H_A), f32, 1.0, 16.0)),
        'dt_bias': dt + jnp.log(-jnp.expm1(-dt)),
        'delta_norm': gain(ks[17], (DEPTH, DV_A)),
        'lambda_q1': nrm(ks[18], (DEPTH, DQK_B), 0.1),
        'lambda_k1': nrm(ks[19], (DEPTH, DQK_B), 0.1),
        'lambda_q2': nrm(ks[20], (DEPTH, DQK_B), 0.1),
        'lambda_k2': nrm(ks[21], (DEPTH, DQK_B), 0.1),
        'diff_norm': gain(ks[22], (DEPTH, DV_B)),
        'w_out': nrm(ks[23], (DEPTH, MIX_WIDTH, D_MODEL), MIX_WIDTH ** -0.5),
        'norm_ffn2': gain(ks[24], (DEPTH, D_MODEL)),
        'w_ffn2_in': nrm(ks[25], (DEPTH, D_MODEL, 2 * D_FF), D_MODEL ** -0.5),
        'w_ffn2_out': nrm(ks[26], (DEPTH, D_FF, D_MODEL), D_FF ** -0.5),
        'norm_final': gain(ks[27], (D_MODEL,)),
    }


def reference(x_prompt, x_sample, cache_diff_k, cache_diff_v, state_delta, c, c_ctx,
              w_ada, b_ada, norm_ffn1, w_ffn1_in, w_ffn1_out, norm_mix, w_in, conv_w,
              a_log, dt_bias, delta_norm, lambda_q1, lambda_k1, lambda_q2, lambda_k2,
              diff_norm, w_out, norm_ffn2, w_ffn2_in, w_ffn2_out, norm_final):
    xp = x_prompt
    xs = x_sample
    new_k, new_v, new_s = [], [], []
    for l in range(DEPTH):
        lam_init = 0.8 - 0.6 * math.exp(-0.3 * l)
        lp = {
            'norm_ffn1': norm_ffn1[l], 'w_ffn1_in': w_ffn1_in[l], 'w_ffn1_out': w_ffn1_out[l],
            'norm_mix': norm_mix[l], 'w_in': w_in[l], 'conv_w': conv_w[l],
            'a_log': a_log[l], 'dt_bias': dt_bias[l], 'delta_norm': delta_norm[l],
            'lambda_q1': lambda_q1[l], 'lambda_k1': lambda_k1[l],
            'lambda_q2': lambda_q2[l], 'lambda_k2': lambda_k2[l],
            'diff_norm': diff_norm[l], 'w_out': w_out[l],
            'norm_ffn2': norm_ffn2[l], 'w_ffn2_in': w_ffn2_in[l], 'w_ffn2_out': w_ffn2_out[l],
        }
        xp, (k_ctx, v_ctx, s_ctx) = trunk_layer(xp, adaln(c_ctx, w_ada[l], b_ada[l]), lp, lam_init,
                                                None, None, None)
        new_k.append(k_ctx)
        new_v.append(v_ctx)
        new_s.append(s_ctx)
        xs, _ = trunk_layer(xs, adaln(c, w_ada[l], b_ada[l]), lp, lam_init,
                            cache_diff_k[:, l], cache_diff_v[:, l], state_delta[:, l])
    y_prompt = rmsnorm(xp, norm_final)
    y_sample = rmsnorm(xs, norm_final)
    new_diff_k = jnp.stack(new_k, axis=1)
    new_diff_v = jnp.stack(new_v, axis=1)
    new_state_delta = jnp.stack(new_s, axis=1)
    return (y_prompt, y_sample, new_diff_k, new_diff_v, new_state_delta)
```

```python
import functools
import math

import jax
import jax.numpy as jnp
from jax import lax
from jax.experimental import pallas as pl
from jax.experimental.pallas import tpu as pltpu

F32 = jnp.float32
BF16 = jnp.bfloat16

D_MODEL = 1024
D_FF = 2816
N_MOD = 9
H_A = 8
DK_A = 64
A_W = H_A * DK_A
H_B = 4
DQK_B = 64
DV_B = 128
B_W = H_B * DV_B
CONV_K = 5
CHUNK = 64
GRID_W = 64
ROPE_THETA = 10000.0
EPS = 1e-6
LAM_INIT = 0.8 - 0.6 * math.exp(-0.3 * 0)

LANES = 128
GROUP_W = 4 * DK_A
IN_PACK_COLS = 3 * A_W + A_W + 3 * B_W + 2 * LANES
VMEM_LIMIT = 56 * 1024 * 1024

TOKEN_TILE = 512
Q_TILE = 256


def _cparams(sem):
    return pltpu.CompilerParams(dimension_semantics=sem, vmem_limit_bytes=VMEM_LIMIT)


def _resident(shape):
    nd = len(shape)
    return pl.BlockSpec(shape, lambda *_: (0,) * nd, pipeline_mode=pl.Buffered(1))


def _silu(x):
    return x * jax.nn.sigmoid(x)


def _mod_norm(x, gain, shift, scale):
    y = x * lax.rsqrt(jnp.mean(x * x, axis=-1, keepdims=True) + EPS)
    return (y * gain) * (1.0 + scale) + shift


def _dot(a, b):
    return jnp.dot(a, b, preferred_element_type=F32)


def _dot_split(a, b_bf16):
    hi = a.astype(BF16)
    lo = (a - hi.astype(F32)).astype(BF16)
    return _dot(hi, b_bf16) + _dot(lo, b_bf16)


def _adaln_kernel(c_ref, w_ref, b_ref, o_ref):
    s = _silu(c_ref[...])
    o_ref[...] = _dot(s.astype(BF16), w_ref[...].astype(BF16)) + b_ref[...]


def _adaln(cond, w_ada, b_ada):
    n = N_MOD * D_MODEL
    tn = n // 8
    return pl.pallas_call(
        _adaln_kernel,
        grid=(n // tn,),
        in_specs=[pl.BlockSpec((8, D_MODEL), lambda j: (0, 0)),
                  pl.BlockSpec((D_MODEL, tn), lambda j: (0, j)),
                  pl.BlockSpec((1, tn), lambda j: (0, j))],
        out_specs=pl.BlockSpec((8, tn), lambda j: (0, j)),
        out_shape=jax.ShapeDtypeStruct((8, n), F32),
        compiler_params=_cparams(("arbitrary",)),
        name="adaln",
    )(cond, w_ada, b_ada.reshape(1, n))


def _swiglu_update(x, mod_ref, k0, gain, wg_ref, wu_ref, wo_ref):
    h = _mod_norm(x, gain, mod_ref[k0:k0 + 1, :], mod_ref[k0 + 1:k0 + 2, :]).astype(BF16)
    half = D_FF // 2
    acc = None
    for j in range(2):
        sl = slice(j * half, (j + 1) * half)
        g = _dot(h, wg_ref[:, sl])
        u = _dot(h, wu_ref[:, sl])
        part = _dot((_silu(g) * u).astype(BF16), wo_ref[sl, :])
        acc = part if acc is None else acc + part
    return x + (0.5 * mod_ref[k0 + 2:k0 + 3, :]) * acc


def _ffn1_kernel(x_ref, mod_ref, gain_ref, wg_ref, wu_ref, wo_ref, o_ref):
    o_ref[...] = _swiglu_update(x_ref[...], mod_ref, 0, gain_ref[...], wg_ref, wu_ref, wo_ref)


def _mod_spec(rows_per_cond, tm):
    if rows_per_cond is None:
        return pl.BlockSpec((None, N_MOD, D_MODEL), lambda i: (0, 0, 0))
    return pl.BlockSpec((None, N_MOD, D_MODEL), lambda i: (1 + (i * tm) // rows_per_cond, 0, 0))


def _ffn_weight_specs():
    return [pl.BlockSpec((D_MODEL, D_FF), lambda i: (0, 0), pipeline_mode=pl.Buffered(1)),
            pl.BlockSpec((D_MODEL, D_FF), lambda i: (0, 1), pipeline_mode=pl.Buffered(1)),
            _resident((D_FF, D_MODEL))]


def _ffn1(x, mod, gain, w_in, w_out, rows_per_cond):
    t = x.shape[0]
    tm = TOKEN_TILE
    row = pl.BlockSpec((tm, D_MODEL), lambda i: (i, 0))
    return pl.pallas_call(
        _ffn1_kernel,
        grid=(t // tm,),
        in_specs=[row, _mod_spec(rows_per_cond, tm), _resident((1, D_MODEL))] + _ffn_weight_specs(),
        out_specs=row,
        out_shape=jax.ShapeDtypeStruct((t, D_MODEL), F32),
        compiler_params=_cparams(("arbitrary",)),
        name="ffn1",
    )(x, mod, gain, w_in, w_in, w_out)


_IN_SEGS = (("qkv", 0, 3 * A_W), ("z", 3 * A_W, A_W), ("bq", 4 * A_W, B_W), ("bk", 4 * A_W + B_W, B_W),
            ("bv", 4 * A_W + 2 * B_W, B_W), ("dec", 4 * A_W + 3 * B_W, LANES),
            ("bet", 4 * A_W + 3 * B_W + LANES, LANES))


def _inproj_kernel(x_ref, mod_ref, gain_ref, w_ref, *out_refs):
    h = _mod_norm(x_ref[...], gain_ref[...], mod_ref[3:4, :], mod_ref[4:5, :]).astype(BF16)
    for (_, start, width), o_ref in zip(_IN_SEGS, out_refs):
        o_ref[...] = _dot(h, w_ref[:, start:start + width])


def _inproj(x, mod, gain, w_pack, rows_per_cond):
    t = x.shape[0]
    tm = TOKEN_TILE
    row = pl.BlockSpec((tm, D_MODEL), lambda i: (i, 0))
    return pl.pallas_call(
        _inproj_kernel,
        grid=(t // tm,),
        in_specs=[row, _mod_spec(rows_per_cond, tm), _resident((1, D_MODEL)),
                  _resident((D_MODEL, IN_PACK_COLS))],
        out_specs=[pl.BlockSpec((tm, w), lambda i: (i, 0)) for _, _, w in _IN_SEGS],
        out_shape=[jax.ShapeDtypeStruct((t, w), F32) for _, _, w in _IN_SEGS],
        compiler_params=_cparams(("arbitrary",)),
        name="inproj",
    )(x, mod, gain, w_pack)


def _post_kernel(x_ref, oa_ref, ob_ref, mod_ref, gain_ref, gfin_ref, wmix_ref, wg_ref, wu_ref, wo_ref, o_ref):
    y = _dot(oa_ref[...].astype(BF16), wmix_ref[0:A_W, :]) + _dot(ob_ref[...].astype(BF16), wmix_ref[A_W:, :])
    x = x_ref[...] + mod_ref[5:6, :] * y
    x = _swiglu_update(x, mod_ref, 6, gain_ref[...], wg_ref, wu_ref, wo_ref)
    o_ref[...] = (x * lax.rsqrt(jnp.mean(x * x, axis=-1, keepdims=True) + EPS)) * gfin_ref[...]


def _post(x, oa, ob, mod, gain, gfin, w_mix, w_in, w_out, rows_per_cond):
    t = x.shape[0]
    tm = TOKEN_TILE
    row = pl.BlockSpec((tm, D_MODEL), lambda i: (i, 0))
    half = pl.BlockSpec((tm, A_W), lambda i: (i, 0))
    return pl.pallas_call(
        _post_kernel,
        grid=(t // tm,),
        in_specs=[row, half, half, _mod_spec(rows_per_cond, tm), _resident((1, D_MODEL)),
                  _resident((1, D_MODEL)), _resident((A_W + B_W, D_MODEL))] + _ffn_weight_specs(),
        out_specs=row,
        out_shape=jax.ShapeDtypeStruct((t, D_MODEL), F32),
        compiler_params=_cparams(("arbitrary",)),
        name="post",
    )(x, oa, ob, mod, gain, gfin, w_mix, w_in, w_in, w_out)


def _rope(x, cos, sin_signed):
    w = x.shape[-1]
    lane = lax.broadcasted_iota(jnp.int32, x.shape, 1)
    partner = jnp.where((lane & 31) < 16, pltpu.roll(x, w - 16, axis=1), pltpu.roll(x, 16, axis=1))
    return x * cos + partner * sin_signed


def _attn_core(q, k_ref, v_ref, lam, gain_ref, o_ref):
    tq = q.shape[0]
    lane = lax.broadcasted_iota(jnp.int32, (tq, DV_B), 1)
    qs = q * (DQK_B ** -0.5)
    for h in range(H_B):
        hs = slice(h * DV_B, (h + 1) * DV_B)
        qh = qs[:, hs]
        q2 = jnp.concatenate([jnp.where(lane < DQK_B, qh, 0.0), jnp.where(lane >= DQK_B, qh, 0.0)], axis=0)
        s = lax.dot_general(q2.astype(BF16), k_ref[:, hs], (((1,), (1,)), ((), ())),
                            preferred_element_type=F32)
        e = jnp.exp(s - jnp.max(s, axis=-1, keepdims=True))
        p = e / jnp.sum(e, axis=-1, keepdims=True)
        a = p[:tq] - lam * p[tq:]
        o = _dot(a.astype(BF16), v_ref[:, hs])
        o = o * lax.rsqrt(jnp.mean(o * o, axis=-1, keepdims=True) + EPS)
        o_ref[:, hs] = (o * gain_ref[...]) * (1.0 - LAM_INIT)


def _lambda(lam_ref):
    l = lam_ref[...]
    s1 = jnp.sum(l[0:1] * l[1:2], axis=-1, keepdims=True)
    s2 = jnp.sum(l[2:3] * l[3:4], axis=-1, keepdims=True)
    return jnp.exp(s1) - jnp.exp(s2) + LAM_INIT


def _attn_ctx_kernel(q_ref, k_ref, v_ref, lam_ref, gain_ref, o_ref, k_s, v_s):
    k_s[...] = k_ref[...].astype(BF16)
    v_s[...] = v_ref[...].astype(BF16)
    _attn_core(q_ref[...], k_s, v_s, _lambda(lam_ref), gain_ref, o_ref)


def _attn_ctx(q, k, v, lam_pack, gain):
    b, l, _ = q.shape
    blk = pl.BlockSpec((None, l, B_W), lambda i: (i, 0, 0))
    return pl.pallas_call(
        _attn_ctx_kernel,
        grid=(b,),
        in_specs=[blk, blk, blk, _resident((8, LANES)), _resident((1, DV_B))],
        out_specs=blk,
        out_shape=jax.ShapeDtypeStruct((b, l, B_W), F32),
        scratch_shapes=[pltpu.VMEM((l, B_W), BF16), pltpu.VMEM((l, B_W), BF16)],
        compiler_params=_cparams(("arbitrary",)),
        name="attn_ctx",
    )(q, k, v, lam_pack, gain)


def _attn_lat_kernel(q_ref, k_ref, v_ref, ck_ref, cv_ref, cosq_ref, sinq_ref, cos_ref, sin_ref,
                     lam_ref, gain_ref, o_ref, k_s, v_s, *, past):
    @pl.when(pl.program_id(1) == 0)
    def _():
        k_s[0:past, :] = ck_ref[...].astype(BF16)
        v_s[0:past, :] = cv_ref[...].astype(BF16)
        k_s[past:, :] = _rope(k_ref[...], cos_ref[...], sin_ref[...]).astype(BF16)
        v_s[past:, :] = v_ref[...].astype(BF16)

    q = _rope(q_ref[...], cosq_ref[...], sinq_ref[...])
    _attn_core(q, k_s, v_s, _lambda(lam_ref), gain_ref, o_ref)


def _attn_lat(q, k, v, ck, cv, cos, sin, lam_pack, gain):
    b, l, _ = q.shape
    past = ck.shape[1]
    tq = Q_TILE
    full = pl.BlockSpec((None, l, B_W), lambda i, j: (i, 0, 0))
    cache = pl.BlockSpec((None, past, B_W), lambda i, j: (i, 0, 0))
    qblk = pl.BlockSpec((None, tq, B_W), lambda i, j: (i, j, 0))
    tab_q = pl.BlockSpec((tq, B_W), lambda i, j: (j, 0))
    tab = pl.BlockSpec((l, B_W), lambda i, j: (0, 0))
    return pl.pallas_call(
        functools.partial(_attn_lat_kernel, past=past),
        grid=(b, l // tq),
        in_specs=[qblk, full, full, cache, cache, tab_q, tab_q, tab, tab,
                  pl.BlockSpec((8, LANES), lambda i, j: (0, 0)), pl.BlockSpec((1, DV_B), lambda i, j: (0, 0))],
        out_specs=qblk,
        out_shape=jax.ShapeDtypeStruct((b, l, B_W), F32),
        scratch_shapes=[pltpu.VMEM((past + l, B_W), BF16), pltpu.VMEM((past + l, B_W), BF16)],
        compiler_params=_cparams(("arbitrary", "arbitrary")),
        name="attn_lat",
    )(q, k, v, ck, cv, cos, sin, cos, sin, lam_pack, gain)


def _rope_tables(length):
    pairs = DQK_B // 4
    n_rows = length // GRID_W
    pos_row = jnp.repeat(jnp.arange(n_rows), GRID_W).astype(F32)
    pos_col = jnp.tile(jnp.arange(GRID_W), n_rows).astype(F32)
    inv = ROPE_THETA ** (-jnp.arange(pairs, dtype=F32) / pairs)
    ang_r = pos_row[:, None] * inv
    ang_c = pos_col[:, None] * inv
    cos = jnp.concatenate([jnp.cos(ang_r), jnp.cos(ang_r), jnp.cos(ang_c), jnp.cos(ang_c)], axis=-1)
    sin = jnp.concatenate([-jnp.sin(ang_r), jnp.sin(ang_r), -jnp.sin(ang_c), jnp.sin(ang_c)], axis=-1)
    reps = B_W // DQK_B
    return jnp.tile(cos, (1, reps)), jnp.tile(sin, (1, reps))


def _block_diag(x, mask):
    return jnp.where(mask, jnp.concatenate([x, x, x, x], axis=0), 0)


def _delta_kernel(*refs, length, has_state):
    if has_state:
        (q_ref, k_ref, v_ref, z_ref, dec_ref, bet_ref, cw_ref, gpar_ref, dn_ref, s0_ref,
         o_ref, st_ref, qs, ks, vs, gexp, bexp, qk_s, kk_s, of_s, ob_s, s_s) = refs
    else:
        (q_ref, k_ref, v_ref, z_ref, dec_ref, bet_ref, cw_ref, gpar_ref, dn_ref,
         o_ref, st_ref, qs, ks, vs, gexp, bexp, qk_s, kk_s, of_s, ob_s, s_s) = refs
        s0_ref = None
    L = length
    n_chunks = L // CHUNK
    grp = pl.program_id(1)
    W = GROUP_W

    r256 = lax.broadcasted_iota(jnp.int32, (W, W), 0)
    c256 = lax.broadcasted_iota(jnp.int32, (W, W), 1)
    bd_mask = (r256 >> 6) == (c256 >> 6)
    ones_bd = jnp.where(bd_mask, 1.0, 0.0).astype(BF16)
    ri = lax.broadcasted_iota(jnp.int32, (CHUNK, W), 0)
    cj = lax.broadcasted_iota(jnp.int32, (CHUNK, W), 1) & (CHUNK - 1)
    eye_p = ri == cj
    row_l = lax.broadcasted_iota(jnp.int32, (L, 1), 0)

    def conv_slab(x_ref, slab):
        x = x_ref[...]
        w = cw_ref[...]
        wl = slice(slab * W, (slab + 1) * W)
        acc = x * w[CONV_K // 2:CONV_K // 2 + 1, wl]
        for tap in range(CONV_K):
            off = tap - CONV_K // 2
            if off == 0:
                continue
            shifted = pltpu.roll(x, (-off) % L, axis=0)
            ok = (row_l + off >= 0) & (row_l + off < L)
            acc = acc + jnp.where(ok, shifted, 0.0) * w[tap:tap + 1, wl]
        return _silu(acc)

    def l2n(y):
        ss = _dot_split(y * y, ones_bd)
        return y * lax.rsqrt(ss + EPS)

    qs[...] = l2n(conv_slab(q_ref, 0)) * (DK_A ** -0.5)
    ks[...] = l2n(conv_slab(k_ref, 1))
    vs[...] = conv_slab(v_ref, 2)

    gpar = gpar_ref[...]
    x = dec_ref[...] + gpar[1:2, :]
    softplus = jnp.maximum(x, 0.0) + jnp.log1p(jnp.exp(-jnp.abs(x)))
    lane_g = lax.broadcasted_iota(jnp.int32, (L, LANES), 1)
    g = jnp.where(lane_g < 2 * H_A, -jnp.exp(gpar[0:1, :]) * softplus, 0.0)
    beta = jax.nn.sigmoid(bet_ref[...])
    pos = row_l & (CHUNK - 1)
    gcf = g
    gcb = g
    for s in (1, 2, 4, 8, 16, 32):
        gcf = gcf + jnp.where(pos >= s, pltpu.roll(gcf, s, axis=0), 0.0)
        gcb = gcb + jnp.where(pos < CHUNK - s, pltpu.roll(gcb, L - s, axis=0), 0.0)
    gsel = jnp.where(lane_g < H_A, gcf, gcb)

    er = lax.broadcasted_iota(jnp.int32, (LANES, 2 * W), 0)
    ec = lax.broadcasted_iota(jnp.int32, (LANES, 2 * W), 1)
    expand = jnp.where(er == (ec >> 8) * H_A + grp * 4 + ((ec & (W - 1)) >> 6), 1.0, 0.0).astype(BF16)
    ge_all = _dot_split(gsel, expand)
    be_all = _dot_split(beta, expand)
    for d in range(2):
        gexp[d] = ge_all[:, d * W:(d + 1) * W]
        bexp[d] = be_all[:, d * W:(d + 1) * W]

    def qk_body(c, carry):
        r0 = pl.multiple_of(c * CHUNK, CHUNK)
        rows = pl.ds(r0, CHUNK)
        kc = ks[rows, :]
        kb = kc.astype(BF16)
        lhs = jnp.concatenate([qs[rows, :].astype(BF16), kb], axis=0)
        res = lax.dot_general(lhs, _block_diag(kb, bd_mask), (((1,), (1,)), ((), ())),
                              preferred_element_type=F32)
        qk_s[rows, :] = res[:CHUNK]
        kk_s[rows, :] = res[CHUNK:]
        return carry

    lax.fori_loop(0, n_chunks, qk_body, 0)

    s_s[...] = jnp.zeros_like(s_s)
    if has_state:
        for d in range(2):
            for h in range(4):
                s_s[d, h * DK_A:(h + 1) * DK_A, h * DK_A:(h + 1) * DK_A] = s0_ref[d, h]

    def chain(d, c):
        r0 = pl.multiple_of(c * CHUNK, CHUNK)
        rows = pl.ds(r0, CHUNK)
        ge = gexp[d, rows, :]
        be = bexp[d, rows, :]
        kc = ks[rows, :]
        qc = qs[rows, :]
        vc = vs[rows, :]
        if d == 0:
            incl, strict = ri >= cj, ri > cj
            gl = ge[CHUNK - 1:CHUNK, :]
        else:
            incl, strict = ri <= cj, ri < cj
            gl = ge[0:1, :]
        g_row = jnp.sum(jnp.where(eye_p, ge, 0.0), axis=0, keepdims=True)
        dec_m = jnp.exp(jnp.where(incl, ge - g_row, -jnp.inf))
        nmat = jnp.where(strict, be * kk_s[rows, :] * dec_m, 0.0)
        qkm = jnp.where(incl, qk_s[rows, :] * dec_m, 0.0)
        eg = jnp.exp(ge)
        qg = qc * eg
        kg = kc * jnp.exp(gl - ge)
        rhs_u = (be * vc).astype(BF16)
        rhs_w = (be * eg * kc).astype(BF16)

        t = jnp.where(eye_p, 1.0, 0.0) - jnp.where((ri >> 1) == (cj >> 1), nmat, 0.0)
        for lb in range(1, 6):
            low, high = (cj, ri) if d == 0 else (ri, cj)
            pair = ((ri >> (lb + 1)) == (cj >> (lb + 1))) & (((high >> lb) & 1) == 1) & (((low >> lb) & 1) == 0)
            n_l = jnp.where(pair, nmat, 0.0).astype(BF16)
            tb = t.astype(BF16)
            y = _dot(tb, _block_diag(n_l, bd_mask))
            t = t - _dot(y.astype(BF16), _block_diag(tb, bd_mask))
        tb = t.astype(BF16)
        u0 = _dot(tb, _block_diag(rhs_u, bd_mask))
        w = _dot(tb, _block_diag(rhs_w, bd_mask))

        s_old = s_s[d]
        ws = _dot(jnp.concatenate([w.astype(BF16), qg.astype(BF16)], axis=0), s_old.astype(BF16))
        u = u0 - ws[:CHUNK]
        ub = u.astype(BF16)
        o = ws[CHUNK:] + _dot(qkm.astype(BF16), _block_diag(ub, bd_mask))
        upd = _dot(kg.T.astype(BF16), ub)
        s_s[d] = s_old * jnp.exp(gl) + jnp.where(bd_mask, upd, 0.0)
        if d == 0:
            of_s[rows, :] = o
        else:
            ob_s[rows, :] = o

    def scan_body(i, carry):
        chain(0, i)
        chain(1, n_chunks - 1 - i)
        return carry

    lax.fori_loop(0, n_chunks, scan_body, 0)

    o = of_s[...] + ob_s[...]
    ms = _dot_split(o * o, ones_bd) * (1.0 / DK_A)
    o = o * lax.rsqrt(ms + EPS) * dn_ref[...]
    o_ref[...] = o * _silu(z_ref[...])
    for d in range(2):
        for h in range(4):
            st_ref[d, h] = s_s[d, h * DK_A:(h + 1) * DK_A, h * DK_A:(h + 1) * DK_A]


def _delta(qkv, z, dec, bet, conv_w, gpar, dnorm, s0):
    b, l, _ = qkv.shape
    has_state = s0 is not None
    w = GROUP_W
    n_grp = A_W // w

    def slab(k):
        return pl.BlockSpec((None, l, w), lambda i, g: (i, 0, k * n_grp + g))

    narrow = pl.BlockSpec((None, l, LANES), lambda i, g: (i, 0, 0))
    st_spec = pl.BlockSpec((None, 2, 4, DK_A, DK_A), lambda i, g: (i, 0, g, 0, 0))
    in_specs = [slab(0), slab(1), slab(2),
                pl.BlockSpec((None, l, w), lambda i, g: (i, 0, g)),
                narrow, narrow,
                pl.BlockSpec((None, CONV_K, 3 * w), lambda i, g: (g, 0, 0)),
                pl.BlockSpec((8, LANES), lambda i, g: (0, 0)),
                pl.BlockSpec((1, w), lambda i, g: (0, 0))]
    args = [qkv, qkv, qkv, z, dec, bet, conv_w, gpar, dnorm]
    if has_state:
        in_specs.append(st_spec)
        args.append(s0)
    scratch = [pltpu.VMEM((l, w), F32)] * 3 + [pltpu.VMEM((2, l, w), F32)] * 2 + [pltpu.VMEM((l, w), F32)] * 4 \
        + [pltpu.VMEM((2, w, w), F32)]
    return pl.pallas_call(
        functools.partial(_delta_kernel, length=l, has_state=has_state),
        grid=(b, n_grp),
        in_specs=in_specs,
        out_specs=[pl.BlockSpec((None, l, w), lambda i, g: (i, 0, g)), st_spec],
        out_shape=[jax.ShapeDtypeStruct((b, l, A_W), F32),
                   jax.ShapeDtypeStruct((b, 2, H_A, DK_A, DK_A), F32)],
        scratch_shapes=scratch,
        compiler_params=_cparams(("arbitrary", "arbitrary")),
        name="delta_state" if has_state else "delta",
    )(*args)


def _pad_lanes(x, width):
    return jnp.pad(x, [(0, 0)] * (x.ndim - 1) + [(0, width - x.shape[-1])])


def kernel(x_prompt, x_sample, cache_diff_k, cache_diff_v, state_delta, c, c_ctx, w_ada, b_ada, norm_ffn1,
           w_ffn1_in, w_ffn1_out, norm_mix, w_in, conv_w, a_log, dt_bias, delta_norm, lambda_q1, lambda_k1,
           lambda_q2, lambda_k2, diff_norm, w_out, norm_ffn2, w_ffn2_in, w_ffn2_out, norm_final):
    bp, lp, _ = x_prompt.shape
    bs, ls, _ = x_sample.shape
    past = cache_diff_k.shape[2]

    w1_in = w_ffn1_in[0].astype(BF16)
    w1_out = w_ffn1_out[0].astype(BF16)
    w2_in = w_ffn2_in[0].astype(BF16)
    w2_out = w_ffn2_out[0].astype(BF16)
    w_mix_out = w_out[0].astype(BF16)
    wi = w_in[0]
    a_end = 4 * A_W
    dec_w = wi[:, a_end:a_end + 2 * H_A]
    bet_w = wi[:, a_end + 2 * H_A:a_end + 4 * H_A]
    rest = wi[:, a_end + 4 * H_A:]
    w_pack = jnp.concatenate([wi[:, :a_end], rest, _pad_lanes(dec_w, LANES), _pad_lanes(bet_w, LANES)],
                             axis=1).astype(BF16)
    g1 = norm_ffn1[0].reshape(1, D_MODEL)
    gm = norm_mix[0].reshape(1, D_MODEL)
    g2 = norm_ffn2[0].reshape(1, D_MODEL)
    gf = norm_final.reshape(1, D_MODEL)
    cw = conv_w[0].reshape(CONV_K, 3, 2, GROUP_W).transpose(2, 0, 1, 3).reshape(2, CONV_K, 3 * GROUP_W)
    gpar = jnp.zeros((8, LANES), F32)
    gpar = gpar.at[0, :2 * H_A].set(a_log[0].reshape(-1)).at[1, :2 * H_A].set(dt_bias[0].reshape(-1))
    dnorm = jnp.tile(delta_norm[0], 4).reshape(1, GROUP_W)
    lam_pack = jnp.zeros((8, LANES), F32)
    for r, v in enumerate((lambda_q1, lambda_k1, lambda_q2, lambda_k2)):
        lam_pack = lam_pack.at[r, :DQK_B].set(v[0])
    dgain = diff_norm[0].reshape(1, DV_B)
    cos, sin = _rope_tables(ls)

    cond = jnp.concatenate([c_ctx[None, :], c, jnp.zeros((8 - 1 - bs, D_MODEL), F32)], axis=0)
    mod = _adaln(cond, w_ada[0], b_ada[0]).reshape(8, N_MOD, D_MODEL)

    def trunk(x3, rows_per_cond, ctx):
        b, l, _ = x3.shape
        x = x3.reshape(b * l, D_MODEL)
        x1 = _ffn1(x, mod, g1, w1_in, w1_out, rows_per_cond)
        qkv, z, bq, bk, bv, dec, bet = _inproj(x1, mod, gm, w_pack, rows_per_cond)
        r3 = lambda a: a.reshape(b, l, a.shape[-1])
        s0 = None if ctx is None else ctx[2]
        oa, st = _delta(r3(qkv), r3(z), r3(dec), r3(bet), cw, gpar, dnorm, s0)
        if ctx is None:
            ob = _attn_ctx(r3(bq), r3(bk), r3(bv), lam_pack, dgain)
        else:
            ob = _attn_lat(r3(bq), r3(bk), r3(bv), ctx[0], ctx[1], cos, sin, lam_pack, dgain)
        y = _post(x1, oa.reshape(b * l, A_W), ob.reshape(b * l, B_W), mod, g2, gf, w_mix_out, w2_in, w2_out,
                  rows_per_cond)
        return y.reshape(b, l, D_MODEL), bk, bv, st

    y_prompt, k_ctx, v_ctx, s_ctx = trunk(x_prompt, None, None)
    ctx = (cache_diff_k[:, 0].reshape(bs, past, B_W), cache_diff_v[:, 0].reshape(bs, past, B_W), state_delta[:, 0])
    y_sample, _, _, _ = trunk(x_sample, ls, ctx)
    new_k = k_ctx.reshape(bp, 1, lp, H_B, 2, DQK_B)
    new_v = v_ctx.reshape(bp, 1, lp, H_B, DV_B)
    return (y_prompt, y_sample, new_k, new_v, s_ctx[:, None])
```

```python
import functools
import math

import jax
import jax.numpy as jnp
from jax import lax
from jax.experimental import pallas as pl
from jax.experimental.pallas import tpu as pltpu

F32 = jnp.float32
BF16 = jnp.bfloat16

D_MODEL = 1024
D_FF = 2816
N_MOD = 9
H_A = 8
DK_A = 64
A_W = H_A * DK_A
H_B = 4
DQK_B = 64
DV_B = 128
B_W = H_B * DV_B
CONV_K = 5
CHUNK = 64
GRID_W = 64
ROPE_THETA = 10000.0
EPS = 1e-6
LAM_INIT = 0.8 - 0.6 * math.exp(-0.3 * 0)

LANES = 128
GROUP_W = 4 * DK_A
IN_PACK_COLS = 3 * A_W + A_W + 3 * B_W + 2 * LANES
VMEM_LIMIT = 56 * 1024 * 1024

TOKEN_TILE = 512
Q_TILE = 256


def _cparams(sem):
    return pltpu.CompilerParams(dimension_semantics=sem, vmem_limit_bytes=VMEM_LIMIT)


def _resident(shape):
    nd = len(shape)
    return pl.BlockSpec(shape, lambda *_: (0,) * nd, pipeline_mode=pl.Buffered(1))


def _silu(x):
    return x * jax.nn.sigmoid(x)


def _mod_norm(x, gain, shift, scale):
    y = x * lax.rsqrt(jnp.mean(x * x, axis=-1, keepdims=True) + EPS)
    return (y * gain) * (1.0 + scale) + shift


def _dot(a, b):
    return jnp.dot(a, b, preferred_element_type=F32)


def _dot_split(a, b_bf16):
    hi = a.astype(BF16)
    lo = (a - hi.astype(F32)).astype(BF16)
    return _dot(hi, b_bf16) + _dot(lo, b_bf16)


def _adaln_kernel(c_ref, w_ref, b_ref, o_ref):
    s = _silu(c_ref[...])
    o_ref[...] = _dot(s.astype(BF16), w_ref[...].astype(BF16)) + b_ref[...]


def _adaln(cond, w_ada, b_ada):
    n = N_MOD * D_MODEL
    tn = n // 8
    return pl.pallas_call(
        _adaln_kernel,
        grid=(n // tn,),
        in_specs=[pl.BlockSpec((8, D_MODEL), lambda j: (0, 0)),
                  pl.BlockSpec((D_MODEL, tn), lambda j: (0, j)),
                  pl.BlockSpec((1, tn), lambda j: (0, j))],
        out_specs=pl.BlockSpec((8, tn), lambda j: (0, j)),
        out_shape=jax.ShapeDtypeStruct((8, n), F32),
        compiler_params=_cparams(("arbitrary",)),
        name="adaln",
    )(cond, w_ada, b_ada.reshape(1, n))


def _swiglu_update(x, mod_ref, k0, gain, wg_ref, wu_ref, wo_ref):
    h = _mod_norm(x, gain, mod_ref[k0:k0 + 1, :], mod_ref[k0 + 1:k0 + 2, :]).astype(BF16)
    half = D_FF // 2
    acc = None
    for j in range(2):
        sl = slice(j * half, (j + 1) * half)
        g = _dot(h, wg_ref[:, sl])
        u = _dot(h, wu_ref[:, sl])
        part = _dot((_silu(g) * u).astype(BF16), wo_ref[sl, :])
        acc = part if acc is None else acc + part
    return x + (0.5 * mod_ref[k0 + 2:k0 + 3, :]) * acc


def _ffn1_kernel(x_ref, mod_ref, gain_ref, wg_ref, wu_ref, wo_ref, o_ref):
    o_ref[...] = _swiglu_update(x_ref[...], mod_ref, 0, gain_ref[...], wg_ref, wu_ref, wo_ref)


def _mod_spec(rows_per_cond, tm):
    if rows_per_cond is None:
        return pl.BlockSpec((None, N_MOD, D_MODEL), lambda i: (0, 0, 0))
    return pl.BlockSpec((None, N_MOD, D_MODEL), lambda i: (1 + (i * tm) // rows_per_cond, 0, 0))


def _ffn_weight_specs():
    return [pl.BlockSpec((D_MODEL, D_FF), lambda i: (0, 0), pipeline_mode=pl.Buffered(1)),
            pl.BlockSpec((D_MODEL, D_FF), lambda i: (0, 1), pipeline_mode=pl.Buffered(1)),
            _resident((D_FF, D_MODEL))]


def _ffn1(x, mod, gain, w_in, w_out, rows_per_cond):
    t = x.shape[0]
    tm = TOKEN_TILE
    row = pl.BlockSpec((tm, D_MODEL), lambda i: (i, 0))
    return pl.pallas_call(
        _ffn1_kernel,
        grid=(t // tm,),
        in_specs=[row, _mod_spec(rows_per_cond, tm), _resident((1, D_MODEL))] + _ffn_weight_specs(),
        out_specs=row,
        out_shape=jax.ShapeDtypeStruct((t, D_MODEL), F32),
        compiler_params=_cparams(("arbitrary",)),
        name="ffn1",
    )(x, mod, gain, w_in, w_in, w_out)


_IN_SEGS = (("qkv", 0, 3 * A_W), ("z", 3 * A_W, A_W), ("bq", 4 * A_W, B_W), ("bk", 4 * A_W + B_W, B_W),
            ("bv", 4 * A_W + 2 * B_W, B_W), ("dec", 4 * A_W + 3 * B_W, LANES),
            ("bet", 4 * A_W + 3 * B_W + LANES, LANES))


def _inproj_kernel(x_ref, mod_ref, gain_ref, w_ref, *out_refs):
    h = _mod_norm(x_ref[...], gain_ref[...], mod_ref[3:4, :], mod_ref[4:5, :]).astype(BF16)
    for (_, start, width), o_ref in zip(_IN_SEGS, out_refs):
        o_ref[...] = _dot(h, w_ref[:, start:start + width])


def _inproj(x, mod, gain, w_pack, rows_per_cond):
    t = x.shape[0]
    tm = TOKEN_TILE
    row = pl.BlockSpec((tm, D_MODEL), lambda i: (i, 0))
    return pl.pallas_call(
        _inproj_kernel,
        grid=(t // tm,),
        in_specs=[row, _mod_spec(rows_per_cond, tm), _resident((1, D_MODEL)),
                  _resident((D_MODEL, IN_PACK_COLS))],
        out_specs=[pl.BlockSpec((tm, w), lambda i: (i, 0)) for _, _, w in _IN_SEGS],
        out_shape=[jax.ShapeDtypeStruct((t, w), F32) for _, _, w in _IN_SEGS],
        compiler_params=_cparams(("arbitrary",)),
        name="inproj",
    )(x, mod, gain, w_pack)


def _post_kernel(x_ref, oa_ref, ob_ref, mod_ref, gain_ref, gfin_ref, wmix_ref, wg_ref, wu_ref, wo_ref, o_ref):
    y = _dot(oa_ref[...].astype(BF16), wmix_ref[0:A_W, :]) + _dot(ob_ref[...].astype(BF16), wmix_ref[A_W:, :])
    x = x_ref[...] + mod_ref[5:6, :] * y
    x = _swiglu_update(x, mod_ref, 6, gain_ref[...], wg_ref, wu_ref, wo_ref)
    o_ref[...] = (x * lax.rsqrt(jnp.mean(x * x, axis=-1, keepdims=True) + EPS)) * gfin_ref[...]


def _post(x, oa, ob, mod, gain, gfin, w_mix, w_in, w_out, rows_per_cond):
    t = x.shape[0]
    tm = TOKEN_TILE
    row = pl.BlockSpec((tm, D_MODEL), lambda i: (i, 0))
    half = pl.BlockSpec((tm, A_W), lambda i: (i, 0))
    return pl.pallas_call(
        _post_kernel,
        grid=(t // tm,),
        in_specs=[row, half, half, _mod_spec(rows_per_cond, tm), _resident((1, D_MODEL)),
                  _resident((1, D_MODEL)), _resident((A_W + B_W, D_MODEL))] + _ffn_weight_specs(),
        out_specs=row,
        out_shape=jax.ShapeDtypeStruct((t, D_MODEL), F32),
        compiler_params=_cparams(("arbitrary",)),
        name="post",
    )(x, oa, ob, mod, gain, gfin, w_mix, w_in, w_in, w_out)


def _rope(x, cos, sin_signed):
    w = x.shape[-1]
    lane = lax.broadcasted_iota(jnp.int32, x.shape, 1)
    partner = jnp.where((lane & 31) < 16, pltpu.roll(x, w - 16, axis=1), pltpu.roll(x, 16, axis=1))
    return x * cos + partner * sin_signed


def _attn_core(q, k_ref, v_ref, lam, gain_ref, o_ref):
    tq = q.shape[0]
    lane = lax.broadcasted_iota(jnp.int32, (tq, DV_B), 1)
    qs = q * (DQK_B ** -0.5)
    for h in range(H_B):
        hs = slice(h * DV_B, (h + 1) * DV_B)
        qh = qs[:, hs]
        q2 = jnp.concatenate([jnp.where(lane < DQK_B, qh, 0.0), jnp.where(lane >= DQK_B, qh, 0.0)], axis=0)
        s = lax.dot_general(q2.astype(BF16), k_ref[:, hs], (((1,), (1,)), ((), ())),
                            preferred_element_type=F32)
        e = jnp.exp(s - jnp.max(s, axis=-1, keepdims=True))
        p = e / jnp.sum(e, axis=-1, keepdims=True)
        a = p[:tq] - lam * p[tq:]
        o = _dot(a.astype(BF16), v_ref[:, hs])
        o = o * lax.rsqrt(jnp.mean(o * o, axis=-1, keepdims=True) + EPS)
        o_ref[:, hs] = (o * gain_ref[...]) * (1.0 - LAM_INIT)


def _lambda(lam_ref):
    l = lam_ref[...]
    s1 = jnp.sum(l[0:1] * l[1:2], axis=-1, keepdims=True)
    s2 = jnp.sum(l[2:3] * l[3:4], axis=-1, keepdims=True)
    return jnp.exp(s1) - jnp.exp(s2) + LAM_INIT


def _attn_ctx_kernel(q_ref, k_ref, v_ref, lam_ref, gain_ref, o_ref, k_s, v_s):
    k_s[...] = k_ref[...].astype(BF16)
    v_s[...] = v_ref[...].astype(BF16)
    _attn_core(q_ref[...], k_s, v_s, _lambda(lam_ref), gain_ref, o_ref)


def _attn_ctx(q, k, v, lam_pack, gain):
    b, l, _ = q.shape
    blk = pl.BlockSpec((None, l, B_W), lambda i: (i, 0, 0))
    return pl.pallas_call(
        _attn_ctx_kernel,
        grid=(b,),
        in_specs=[blk, blk, blk, _resident((8, LANES)), _resident((1, DV_B))],
        out_specs=blk,
        out_shape=jax.ShapeDtypeStruct((b, l, B_W), F32),
        scratch_shapes=[pltpu.VMEM((l, B_W), BF16), pltpu.VMEM((l, B_W), BF16)],
        compiler_params=_cparams(("arbitrary",)),
        name="attn_ctx",
    )(q, k, v, lam_pack, gain)


def _attn_lat_kernel(q_ref, k_ref, v_ref, ck_ref, cv_ref, cosq_ref, sinq_ref, cos_ref, sin_ref,
                     lam_ref, gain_ref, o_ref, k_s, v_s, *, past):
    @pl.when(pl.program_id(1) == 0)
    def _():
        k_s[0:past, :] = ck_ref[...].astype(BF16)
        v_s[0:past, :] = cv_ref[...].astype(BF16)
        k_s[past:, :] = _rope(k_ref[...], cos_ref[...], sin_ref[...]).astype(BF16)
        v_s[past:, :] = v_ref[...].astype(BF16)

    q = _rope(q_ref[...], cosq_ref[...], sinq_ref[...])
    _attn_core(q, k_s, v_s, _lambda(lam_ref), gain_ref, o_ref)


def _attn_lat(q, k, v, ck, cv, cos, sin, lam_pack, gain):
    b, l, _ = q.shape
    past = ck.shape[1]
    tq = Q_TILE
    full = pl.BlockSpec((None, l, B_W), lambda i, j: (i, 0, 0))
    cache = pl.BlockSpec((None, past, B_W), lambda i, j: (i, 0, 0))
    qblk = pl.BlockSpec((None, tq, B_W), lambda i, j: (i, j, 0))
    tab_q = pl.BlockSpec((tq, B_W), lambda i, j: (j, 0))
    tab = pl.BlockSpec((l, B_W), lambda i, j: (0, 0))
    return pl.pallas_call(
        functools.partial(_attn_lat_kernel, past=past),
        grid=(b, l // tq),
        in_specs=[qblk, full, full, cache, cache, tab_q, tab_q, tab, tab,
                  pl.BlockSpec((8, LANES), lambda i, j: (0, 0)), pl.BlockSpec((1, DV_B), lambda i, j: (0, 0))],
        out_specs=qblk,
        out_shape=jax.ShapeDtypeStruct((b, l, B_W), F32),
        scratch_shapes=[pltpu.VMEM((past + l, B_W), BF16), pltpu.VMEM((past + l, B_W), BF16)],
        compiler_params=_cparams(("arbitrary", "arbitrary")),
        name="attn_lat",
    )(q, k, v, ck, cv, cos, sin, cos, sin, lam_pack, gain)


def _rope_tables(length):
    pairs = DQK_B // 4
    n_rows = length // GRID_W
    pos_row = jnp.repeat(jnp.arange(n_rows), GRID_W).astype(F32)
    pos_col = jnp.tile(jnp.arange(GRID_W), n_rows).astype(F32)
    inv = ROPE_THETA ** (-jnp.arange(pairs, dtype=F32) / pairs)
    ang_r = pos_row[:, None] * inv
    ang_c = pos_col[:, None] * inv
    cos = jnp.concatenate([jnp.cos(ang_r), jnp.cos(ang_r), jnp.cos(ang_c), jnp.cos(ang_c)], axis=-1)
    sin = jnp.concatenate([-jnp.sin(ang_r), jnp.sin(ang_r), -jnp.sin(ang_c), jnp.sin(ang_c)], axis=-1)
    reps = B_W // DQK_B
    return jnp.tile(cos, (1, reps)), jnp.tile(sin, (1, reps))


def _block_diag(x, mask):
    return jnp.where(mask, jnp.concatenate([x, x, x, x], axis=0), 0)


SOLVE_CHUNKS = 4
DELTA_ROWS = 1024


def _delta_kernel(*refs, nb, length, has_state):
    if has_state:
        (q_ref, k_ref, v_ref, z_ref, dec_ref, bet_ref, cw_ref, gpar_ref, dn_ref, s0_ref, o_ref, st_ref, *scr) = refs
    else:
        (q_ref, k_ref, v_ref, z_ref, dec_ref, bet_ref, cw_ref, gpar_ref, dn_ref, o_ref, st_ref, *scr) = refs
        s0_ref = None
    xpad, qs, ks, vs, gexp, bexp, u0_s, kg_s, w_s, qg_s, qkm_s, of_s, ob_s, s_s = scr
    L = length
    R = nb * L
    n_chunks = L // CHUNK
    grp = pl.program_id(1)
    W = GROUP_W
    PAD = 8

    r256 = lax.broadcasted_iota(jnp.int32, (W, W), 0)
    c256 = lax.broadcasted_iota(jnp.int32, (W, W), 1)
    bd_mask = (r256 >> 6) == (c256 >> 6)
    ones_bd = jnp.where(bd_mask, 1.0, 0.0).astype(BF16)
    ri = lax.broadcasted_iota(jnp.int32, (CHUNK, W), 0)
    cj = lax.broadcasted_iota(jnp.int32, (CHUNK, W), 1) & (CHUNK - 1)
    eye_p = ri == cj
    row_l = lax.broadcasted_iota(jnp.int32, (L, 1), 0)
    lane_g = lax.broadcasted_iota(jnp.int32, (L, LANES), 1)
    er = lax.broadcasted_iota(jnp.int32, (LANES, 2 * W), 0)
    ec = lax.broadcasted_iota(jnp.int32, (LANES, 2 * W), 1)
    expand = jnp.where(er == (ec >> 8) * H_A + grp * 4 + ((ec & (W - 1)) >> 6), 1.0, 0.0).astype(BF16)

    def bd(x):
        return _block_diag(x, bd_mask)

    xpad[0:PAD, :] = jnp.zeros((PAD, W), F32)
    xpad[PAD + L:PAD + L + PAD, :] = jnp.zeros((PAD, W), F32)

    def prologue(bi, carry):
        rows_b = pl.ds(pl.multiple_of(bi * L, L), L)
        w = cw_ref[...]

        def conv_slab(x_ref, slab):
            xpad[PAD:PAD + L, :] = x_ref[bi]
            acc = None
            for tap in range(CONV_K):
                term = xpad[PAD + tap - CONV_K // 2:PAD + tap - CONV_K // 2 + L, :] \
                    * w[tap:tap + 1, slab * W:(slab + 1) * W]
                acc = term if acc is None else acc + term
            return _silu(acc)

        def l2n(y):
            return y * lax.rsqrt(_dot_split(y * y, ones_bd) + EPS)

        qs[rows_b, :] = l2n(conv_slab(q_ref, 0)) * (DK_A ** -0.5)
        ks[rows_b, :] = l2n(conv_slab(k_ref, 1))
        vs[rows_b, :] = conv_slab(v_ref, 2)

        gpar = gpar_ref[...]
        x = dec_ref[bi] + gpar[1:2, :]
        softplus = jnp.maximum(x, 0.0) + jnp.log1p(jnp.exp(-jnp.abs(x)))
        g = jnp.where(lane_g < 2 * H_A, -jnp.exp(gpar[0:1, :]) * softplus, 0.0)
        beta = jax.nn.sigmoid(bet_ref[bi])
        pos = row_l & (CHUNK - 1)
        gcf = g
        gcb = g
        for s in (1, 2, 4, 8, 16, 32):
            gcf = gcf + jnp.where(pos >= s, pltpu.roll(gcf, s, axis=0), 0.0)
            gcb = gcb + jnp.where(pos < CHUNK - s, pltpu.roll(gcb, L - s, axis=0), 0.0)
        gsel = jnp.where(lane_g < H_A, gcf, gcb)
        ge_all = _dot_split(gsel, expand)
        be_all = _dot_split(beta, expand)
        for d in range(2):
            gexp[d, rows_b, :] = ge_all[:, d * W:(d + 1) * W]
            bexp[d, rows_b, :] = be_all[:, d * W:(d + 1) * W]
        return carry

    lax.fori_loop(0, nb, prologue, 0)

    def solve_body(j, carry):
        chains = []
        for k in range(SOLVE_CHUNKS):
            rows = pl.ds(pl.multiple_of((j * SOLVE_CHUNKS + k) * CHUNK, CHUNK), CHUNK)
            kc = ks[rows, :]
            qc = qs[rows, :]
            vc = vs[rows, :]
            kb = kc.astype(BF16)
            res = lax.dot_general(jnp.concatenate([qc.astype(BF16), kb], axis=0), bd(kb),
                                  (((1,), (1,)), ((), ())), preferred_element_type=F32)
            qk, kk = res[:CHUNK], res[CHUNK:]
            for d in range(2):
                ge = gexp[d, rows, :]
                be = bexp[d, rows, :]
                if d == 0:
                    incl, strict, gl = ri >= cj, ri > cj, ge[CHUNK - 1:CHUNK, :]
                else:
                    incl, strict, gl = ri <= cj, ri < cj, ge[0:1, :]
                g_row = jnp.sum(jnp.where(eye_p, ge, 0.0), axis=0, keepdims=True)
                dec_m = jnp.exp(jnp.where(incl, ge - g_row, -jnp.inf))
                nmat = jnp.where(strict, be * kk * dec_m, 0.0)
                eg = jnp.exp(ge)
                chains.append(dict(
                    d=d, rows=rows, nmat=nmat,
                    qkm=jnp.where(incl, qk * dec_m, 0.0).astype(BF16),
                    qg=(qc * eg).astype(BF16), kg=kc * jnp.exp(gl - ge),
                    rhs_u=(be * vc).astype(BF16), rhs_w=(be * eg * kc).astype(BF16),
                    t=jnp.where(eye_p, 1.0, 0.0) - jnp.where((ri >> 1) == (cj >> 1), nmat, 0.0)))

        for lb in range(1, 6):
            same = (ri >> (lb + 1)) == (cj >> (lb + 1))
            pair = (same & (((ri >> lb) & 1) == 1) & (((cj >> lb) & 1) == 0),
                    same & (((cj >> lb) & 1) == 1) & (((ri >> lb) & 1) == 0))
            tbs = [c["t"].astype(BF16) for c in chains]
            ys = [_dot(tb, bd(jnp.where(pair[c["d"]], c["nmat"], 0.0).astype(BF16))) for c, tb in zip(chains, tbs)]
            for c, tb, y in zip(chains, tbs, ys):
                c["t"] = c["t"] - _dot(y.astype(BF16), bd(tb))
        for c in chains:
            tb = c["t"].astype(BF16)
            c["u0"] = _dot(tb, bd(c["rhs_u"]))
            c["w"] = _dot(tb, bd(c["rhs_w"])).astype(BF16)
        for c in chains:
            d, rows = c["d"], c["rows"]
            u0_s[d, rows, :] = c["u0"]
            kg_s[d, rows, :] = c["kg"]
            w_s[d, rows, :] = c["w"]
            qg_s[d, rows, :] = c["qg"]
            qkm_s[d, rows, :] = c["qkm"]
        return carry

    lax.fori_loop(0, R // (SOLVE_CHUNKS * CHUNK), solve_body, 0)

    s_s[...] = jnp.zeros_like(s_s)
    if has_state:
        for bi in range(nb):
            for d in range(2):
                for h in range(4):
                    s_s[bi, d, h * DK_A:(h + 1) * DK_A, h * DK_A:(h + 1) * DK_A] = s0_ref[bi, d, h]

    def scan_body(i, carry):
        chains = []
        for bi in range(nb):
            for d in range(2):
                c = i if d == 0 else n_chunks - 1 - i
                r0 = pl.multiple_of(bi * L + c * CHUNK, CHUNK)
                rows = pl.ds(r0, CHUNK)
                if d == 0:
                    g_last = gexp[d, pl.ds(r0 + CHUNK - 8, 8), :][7:8]
                else:
                    g_last = gexp[d, pl.ds(r0, 8), :][0:1]
                chains.append(dict(
                    bi=bi, d=d, rows=rows, s=s_s[bi, d], u0=u0_s[d, rows, :], kg=kg_s[d, rows, :],
                    wq=jnp.concatenate([w_s[d, rows, :], qg_s[d, rows, :]], axis=0), qkm=qkm_s[d, rows, :],
                    decay=jnp.exp(g_last)))
        for c in chains:
            c["ws"] = _dot(c["wq"], c["s"].astype(BF16))
        for c in chains:
            c["ub"] = (c["u0"] - c["ws"][:CHUNK]).astype(BF16)
        for c in chains:
            c["o"] = c["ws"][CHUNK:] + _dot(c["qkm"], bd(c["ub"]))
            upd = _dot(c["kg"].T.astype(BF16), c["ub"])
            c["s_new"] = c["s"] * c["decay"] + jnp.where(bd_mask, upd, 0.0)
        for c in chains:
            s_s[c["bi"], c["d"]] = c["s_new"]
            if c["d"] == 0:
                of_s[c["rows"], :] = c["o"]
            else:
                ob_s[c["rows"], :] = c["o"]
        return carry

    lax.fori_loop(0, n_chunks, scan_body, 0)

    def epilogue(bi, carry):
        rows_b = pl.ds(pl.multiple_of(bi * L, L), L)
        o = of_s[rows_b, :] + ob_s[rows_b, :]
        ms = _dot_split(o * o, ones_bd) * (1.0 / DK_A)
        o = o * lax.rsqrt(ms + EPS) * dn_ref[...]
        o_ref[bi] = o * _silu(z_ref[bi])
        return carry

    lax.fori_loop(0, nb, epilogue, 0)
    for bi in range(nb):
        for d in range(2):
            for h in range(4):
                st_ref[bi, d, h] = s_s[bi, d, h * DK_A:(h + 1) * DK_A, h * DK_A:(h + 1) * DK_A]


def _delta(qkv, z, dec, bet, conv_w, gpar, dnorm, s0):
    b, l, _ = qkv.shape
    has_state = s0 is not None
    w = GROUP_W
    n_grp = A_W // w
    nb = DELTA_ROWS // l
    r = nb * l

    def slab(k):
        return pl.BlockSpec((nb, l, w), lambda i, g: (i, 0, k * n_grp + g))

    narrow = pl.BlockSpec((nb, l, LANES), lambda i, g: (i, 0, 0))
    st_spec = pl.BlockSpec((nb, 2, 4, DK_A, DK_A), lambda i, g: (i, 0, g, 0, 0))
    in_specs = [slab(0), slab(1), slab(2),
                pl.BlockSpec((nb, l, w), lambda i, g: (i, 0, g)),
                narrow, narrow,
                pl.BlockSpec((None, CONV_K, 3 * w), lambda i, g: (g, 0, 0)),
                pl.BlockSpec((8, LANES), lambda i, g: (0, 0)),
                pl.BlockSpec((1, w), lambda i, g: (0, 0))]
    args = [qkv, qkv, qkv, z, dec, bet, conv_w, gpar, dnorm]
    if has_state:
        in_specs.append(st_spec)
        args.append(s0)
    scratch = ([pltpu.VMEM((l + 16, w), F32)] + [pltpu.VMEM((r, w), F32)] * 3 + [pltpu.VMEM((2, r, w), F32)] * 4
               + [pltpu.VMEM((2, r, w), BF16)] * 3 + [pltpu.VMEM((r, w), F32)] * 2
               + [pltpu.VMEM((nb, 2, w, w), F32)])
    return pl.pallas_call(
        functools.partial(_delta_kernel, nb=nb, length=l, has_state=has_state),
        grid=(b // nb, n_grp),
        in_specs=in_specs,
        out_specs=[pl.BlockSpec((nb, l, w), lambda i, g: (i, 0, g)), st_spec],
        out_shape=[jax.ShapeDtypeStruct((b, l, A_W), F32),
                   jax.ShapeDtypeStruct((b, 2, H_A, DK_A, DK_A), F32)],
        scratch_shapes=scratch,
        compiler_params=_cparams(("arbitrary", "arbitrary")),
        name="delta_state" if has_state else "delta",
    )(*args)


def _pad_lanes(x, width):
    return jnp.pad(x, [(0, 0)] * (x.ndim - 1) + [(0, width - x.shape[-1])])


def kernel(x_prompt, x_sample, cache_diff_k, cache_diff_v, state_delta, c, c_ctx, w_ada, b_ada, norm_ffn1,
           w_ffn1_in, w_ffn1_out, norm_mix, w_in, conv_w, a_log, dt_bias, delta_norm, lambda_q1, lambda_k1,
           lambda_q2, lambda_k2, diff_norm, w_out, norm_ffn2, w_ffn2_in, w_ffn2_out, norm_final):
    bp, lp, _ = x_prompt.shape
    bs, ls, _ = x_sample.shape
    past = cache_diff_k.shape[2]

    w1_in = w_ffn1_in[0].astype(BF16)
    w1_out = w_ffn1_out[0].astype(BF16)
    w2_in = w_ffn2_in[0].astype(BF16)
    w2_out = w_ffn2_out[0].astype(BF16)
    w_mix_out = w_out[0].astype(BF16)
    wi = w_in[0]
    a_end = 4 * A_W
    dec_w = wi[:, a_end:a_end + 2 * H_A]
    bet_w = wi[:, a_end + 2 * H_A:a_end + 4 * H_A]
    rest = wi[:, a_end + 4 * H_A:]
    w_pack = jnp.concatenate([wi[:, :a_end], rest, _pad_lanes(dec_w, LANES), _pad_lanes(bet_w, LANES)],
                             axis=1).astype(BF16)
    g1 = norm_ffn1[0].reshape(1, D_MODEL)
    gm = norm_mix[0].reshape(1, D_MODEL)
    g2 = norm_ffn2[0].reshape(1, D_MODEL)
    gf = norm_final.reshape(1, D_MODEL)
    cw = conv_w[0].reshape(CONV_K, 3, 2, GROUP_W).transpose(2, 0, 1, 3).reshape(2, CONV_K, 3 * GROUP_W)
    gpar = jnp.zeros((8, LANES), F32)
    gpar = gpar.at[0, :2 * H_A].set(a_log[0].reshape(-1)).at[1, :2 * H_A].set(dt_bias[0].reshape(-1))
    dnorm = jnp.tile(delta_norm[0], 4).reshape(1, GROUP_W)
    lam_pack = jnp.zeros((8, LANES), F32)
    for r, v in enumerate((lambda_q1, lambda_k1, lambda_q2, lambda_k2)):
        lam_pack = lam_pack.at[r, :DQK_B].set(v[0])
    dgain = diff_norm[0].reshape(1, DV_B)
    cos, sin = _rope_tables(ls)

    cond = jnp.concatenate([c_ctx[None, :], c, jnp.zeros((8 - 1 - bs, D_MODEL), F32)], axis=0)
    mod = _adaln(cond, w_ada[0], b_ada[0]).reshape(8, N_MOD, D_MODEL)

    def trunk(x3, rows_per_cond, ctx):
        b, l, _ = x3.shape
        x = x3.reshape(b * l, D_MODEL)
        x1 = _ffn1(x, mod, g1, w1_in, w1_out, rows_per_cond)
        qkv, z, bq, bk, bv, dec, bet = _inproj(x1, mod, gm, w_pack, rows_per_cond)
        r3 = lambda a: a.reshape(b, l, a.shape[-1])
        s0 = None if ctx is None else ctx[2]
        oa, st = _delta(r3(qkv), r3(z), r3(dec), r3(bet), cw, gpar, dnorm, s0)
        if ctx is None:
            ob = _attn_ctx(r3(bq), r3(bk), r3(bv), lam_pack, dgain)
        else:
            ob = _attn_lat(r3(bq), r3(bk), r3(bv), ctx[0], ctx[1], cos, sin, lam_pack, dgain)
        y = _post(x1, oa.reshape(b * l, A_W), ob.reshape(b * l, B_W), mod, g2, gf, w_mix_out, w2_in, w2_out,
                  rows_per_cond)
        return y.reshape(b, l, D_MODEL), bk, bv, st

    y_prompt, k_ctx, v_ctx, s_ctx = trunk(x_prompt, None, None)
    ctx = (cache_diff_k[:, 0].reshape(bs, past, B_W), cache_diff_v[:, 0].reshape(bs, past, B_W), state_delta[:, 0])
    y_sample, _, _, _ = trunk(x_sample, ls, ctx)
    new_k = k_ctx.reshape(bp, 1, lp, H_B, 2, DQK_B)
    new_v = v_ctx.reshape(bp, 1, lp, H_B, DV_B)
    return (y_prompt, y_sample, new_k, new_v, s_ctx[:, None])
```

```python
import functools
import math

import jax
import jax.numpy as jnp
from jax import lax
from jax.experimental import pallas as pl
from jax.experimental.pallas import tpu as pltpu

F32 = jnp.float32
BF16 = jnp.bfloat16

D_MODEL = 1024
D_FF = 2816
N_MOD = 9
H_A = 8
DK_A = 64
A_W = H_A * DK_A
H_B = 4
DQK_B = 64
DV_B = 128
B_W = H_B * DV_B
CONV_K = 5
CHUNK = 64
GRID_W = 64
ROPE_THETA = 10000.0
EPS = 1e-6
LAM_INIT = 0.8 - 0.6 * math.exp(-0.3 * 0)

LANES = 128
GROUP_W = 4 * DK_A
IN_PACK_COLS = 3 * A_W + A_W + 3 * B_W + 2 * LANES
VMEM_LIMIT = 56 * 1024 * 1024

TOKEN_TILE = 512
Q_TILE = 256


def _cparams(sem):
    return pltpu.CompilerParams(dimension_semantics=sem, vmem_limit_bytes=VMEM_LIMIT)


def _resident(shape):
    nd = len(shape)
    return pl.BlockSpec(shape, lambda *_: (0,) * nd, pipeline_mode=pl.Buffered(1))


def _silu(x):
    return x * jax.nn.sigmoid(x)


def _mod_norm(x, gain, shift, scale):
    y = x * lax.rsqrt(jnp.mean(x * x, axis=-1, keepdims=True) + EPS)
    return (y * gain) * (1.0 + scale) + shift


def _dot(a, b):
    return jnp.dot(a, b, preferred_element_type=F32)


def _dot_split(a, b_bf16):
    hi = a.astype(BF16)
    lo = (a - hi.astype(F32)).astype(BF16)
    return _dot(hi, b_bf16) + _dot(lo, b_bf16)


def _adaln_kernel(c_ref, w_ref, b_ref, o_ref):
    s = _silu(c_ref[...])
    o_ref[...] = _dot(s.astype(BF16), w_ref[...].astype(BF16)) + b_ref[...]


def _adaln(cond, w_ada, b_ada):
    n = N_MOD * D_MODEL
    tn = n // 8
    return pl.pallas_call(
        _adaln_kernel,
        grid=(n // tn,),
        in_specs=[pl.BlockSpec((8, D_MODEL), lambda j: (0, 0)),
                  pl.BlockSpec((D_MODEL, tn), lambda j: (0, j)),
                  pl.BlockSpec((1, tn), lambda j: (0, j))],
        out_specs=pl.BlockSpec((8, tn), lambda j: (0, j)),
        out_shape=jax.ShapeDtypeStruct((8, n), F32),
        compiler_params=_cparams(("arbitrary",)),
        name="adaln",
    )(cond, w_ada, b_ada.reshape(1, n))


def _swiglu_update(x, mod_ref, k0, gain, wg_ref, wu_ref, wo_ref):
    h = _mod_norm(x, gain, mod_ref[k0:k0 + 1, :], mod_ref[k0 + 1:k0 + 2, :]).astype(BF16)
    half = D_FF // 2
    acc = None
    for j in range(2):
        sl = slice(j * half, (j + 1) * half)
        g = _dot(h, wg_ref[:, sl])
        u = _dot(h, wu_ref[:, sl])
        part = _dot((_silu(g) * u).astype(BF16), wo_ref[sl, :])
        acc = part if acc is None else acc + part
    return x + (0.5 * mod_ref[k0 + 2:k0 + 3, :]) * acc


def _ffn1_kernel(x_ref, mod_ref, gain_ref, wg_ref, wu_ref, wo_ref, o_ref):
    o_ref[...] = _swiglu_update(x_ref[...], mod_ref, 0, gain_ref[...], wg_ref, wu_ref, wo_ref)


def _mod_spec(rows_per_cond, tm):
    if rows_per_cond is None:
        return pl.BlockSpec((None, N_MOD, D_MODEL), lambda i: (0, 0, 0))
    return pl.BlockSpec((None, N_MOD, D_MODEL), lambda i: (1 + (i * tm) // rows_per_cond, 0, 0))


def _ffn_weight_specs():
    return [pl.BlockSpec((D_MODEL, D_FF), lambda i: (0, 0), pipeline_mode=pl.Buffered(1)),
            pl.BlockSpec((D_MODEL, D_FF), lambda i: (0, 1), pipeline_mode=pl.Buffered(1)),
            _resident((D_FF, D_MODEL))]


def _ffn1(x, mod, gain, w_in, w_out, rows_per_cond):
    t = x.shape[0]
    tm = TOKEN_TILE
    row = pl.BlockSpec((tm, D_MODEL), lambda i: (i, 0))
    return pl.pallas_call(
        _ffn1_kernel,
        grid=(t // tm,),
        in_specs=[row, _mod_spec(rows_per_cond, tm), _resident((1, D_MODEL))] + _ffn_weight_specs(),
        out_specs=row,
        out_shape=jax.ShapeDtypeStruct((t, D_MODEL), F32),
        compiler_params=_cparams(("arbitrary",)),
        name="ffn1",
    )(x, mod, gain, w_in, w_in, w_out)


_IN_SEGS = (("qkv", 0, 3 * A_W), ("z", 3 * A_W, A_W), ("bq", 4 * A_W, B_W), ("bk", 4 * A_W + B_W, B_W),
            ("bv", 4 * A_W + 2 * B_W, B_W), ("dec", 4 * A_W + 3 * B_W, LANES),
            ("bet", 4 * A_W + 3 * B_W + LANES, LANES))


def _inproj_kernel(x_ref, mod_ref, gain_ref, w_ref, *out_refs):
    h = _mod_norm(x_ref[...], gain_ref[...], mod_ref[3:4, :], mod_ref[4:5, :]).astype(BF16)
    for (_, start, width), o_ref in zip(_IN_SEGS, out_refs):
        o_ref[...] = _dot(h, w_ref[:, start:start + width])


def _inproj(x, mod, gain, w_pack, rows_per_cond):
    t = x.shape[0]
    tm = TOKEN_TILE
    row = pl.BlockSpec((tm, D_MODEL), lambda i: (i, 0))
    return pl.pallas_call(
        _inproj_kernel,
        grid=(t // tm,),
        in_specs=[row, _mod_spec(rows_per_cond, tm), _resident((1, D_MODEL)),
                  _resident((D_MODEL, IN_PACK_COLS))],
        out_specs=[pl.BlockSpec((tm, w), lambda i: (i, 0)) for _, _, w in _IN_SEGS],
        out_shape=[jax.ShapeDtypeStruct((t, w), F32) for _, _, w in _IN_SEGS],
        compiler_params=_cparams(("arbitrary",)),
        name="inproj",
    )(x, mod, gain, w_pack)


def _post_kernel(x_ref, oa_ref, ob_ref, mod_ref, gain_ref, gfin_ref, wmix_ref, wg_ref, wu_ref, wo_ref, o_ref):
    y = _dot(oa_ref[...].astype(BF16), wmix_ref[0:A_W, :]) + _dot(ob_ref[...].astype(BF16), wmix_ref[A_W:, :])
    x = x_ref[...] + mod_ref[5:6, :] * y
    x = _swiglu_update(x, mod_ref, 6, gain_ref[...], wg_ref, wu_ref, wo_ref)
    o_ref[...] = (x * lax.rsqrt(jnp.mean(x * x, axis=-1, keepdims=True) + EPS)) * gfin_ref[...]


def _post(x, oa, ob, mod, gain, gfin, w_mix, w_in, w_out, rows_per_cond):
    t = x.shape[0]
    tm = TOKEN_TILE
    row = pl.BlockSpec((tm, D_MODEL), lambda i: (i, 0))
    half = pl.BlockSpec((tm, A_W), lambda i: (i, 0))
    return pl.pallas_call(
        _post_kernel,
        grid=(t // tm,),
        in_specs=[row, half, half, _mod_spec(rows_per_cond, tm), _resident((1, D_MODEL)),
                  _resident((1, D_MODEL)), _resident((A_W + B_W, D_MODEL))] + _ffn_weight_specs(),
        out_specs=row,
        out_shape=jax.ShapeDtypeStruct((t, D_MODEL), F32),
        compiler_params=_cparams(("arbitrary",)),
        name="post",
    )(x, oa, ob, mod, gain, gfin, w_mix, w_in, w_in, w_out)


def _rope(x, cos, sin_signed):
    w = x.shape[-1]
    lane = lax.broadcasted_iota(jnp.int32, x.shape, 1)
    partner = jnp.where((lane & 31) < 16, pltpu.roll(x, w - 16, axis=1), pltpu.roll(x, 16, axis=1))
    return x * cos + partner * sin_signed


def _attn_core(q, k_ref, v_ref, lam, gain_ref, o_ref):
    tq = q.shape[0]
    lane = lax.broadcasted_iota(jnp.int32, (tq, DV_B), 1)
    qs = q * (DQK_B ** -0.5)
    for h in range(H_B):
        hs = slice(h * DV_B, (h + 1) * DV_B)
        qh = qs[:, hs]
        q2 = jnp.concatenate([jnp.where(lane < DQK_B, qh, 0.0), jnp.where(lane >= DQK_B, qh, 0.0)], axis=0)
        s = lax.dot_general(q2.astype(BF16), k_ref[:, hs], (((1,), (1,)), ((), ())),
                            preferred_element_type=F32)
        e = jnp.exp(s - jnp.max(s, axis=-1, keepdims=True))
        r = _dot(e.astype(BF16), v_ref[:, hs]) / jnp.sum(e, axis=-1, keepdims=True)
        o = r[:tq] - lam * r[tq:]
        o = o * lax.rsqrt(jnp.mean(o * o, axis=-1, keepdims=True) + EPS)
        o_ref[:, hs] = (o * gain_ref[...]) * (1.0 - LAM_INIT)


def _lambda(lam_ref):
    l = lam_ref[...]
    s1 = jnp.sum(l[0:1] * l[1:2], axis=-1, keepdims=True)
    s2 = jnp.sum(l[2:3] * l[3:4], axis=-1, keepdims=True)
    return jnp.exp(s1) - jnp.exp(s2) + LAM_INIT


def _attn_ctx_kernel(q_ref, k_ref, v_ref, lam_ref, gain_ref, o_ref, k_s, v_s):
    k_s[...] = k_ref[...].astype(BF16)
    v_s[...] = v_ref[...].astype(BF16)
    _attn_core(q_ref[...], k_s, v_s, _lambda(lam_ref), gain_ref, o_ref)


def _attn_ctx(q, k, v, lam_pack, gain):
    b, l, _ = q.shape
    blk = pl.BlockSpec((None, l, B_W), lambda i: (i, 0, 0))
    return pl.pallas_call(
        _attn_ctx_kernel,
        grid=(b,),
        in_specs=[blk, blk, blk, _resident((8, LANES)), _resident((1, DV_B))],
        out_specs=blk,
        out_shape=jax.ShapeDtypeStruct((b, l, B_W), F32),
        scratch_shapes=[pltpu.VMEM((l, B_W), BF16), pltpu.VMEM((l, B_W), BF16)],
        compiler_params=_cparams(("arbitrary",)),
        name="attn_ctx",
    )(q, k, v, lam_pack, gain)


def _attn_lat_kernel(q_ref, k_ref, v_ref, ck_ref, cv_ref, cosq_ref, sinq_ref, cos_ref, sin_ref,
                     lam_ref, gain_ref, o_ref, k_s, v_s, *, past):
    @pl.when(pl.program_id(1) == 0)
    def _():
        k_s[0:past, :] = ck_ref[...].astype(BF16)
        v_s[0:past, :] = cv_ref[...].astype(BF16)
        k_s[past:, :] = _rope(k_ref[...], cos_ref[...], sin_ref[...]).astype(BF16)
        v_s[past:, :] = v_ref[...].astype(BF16)

    q = _rope(q_ref[...], cosq_ref[...], sinq_ref[...])
    _attn_core(q, k_s, v_s, _lambda(lam_ref), gain_ref, o_ref)


def _attn_lat(q, k, v, ck, cv, cos, sin, lam_pack, gain):
    b, l, _ = q.shape
    past = ck.shape[1]
    tq = Q_TILE
    full = pl.BlockSpec((None, l, B_W), lambda i, j: (i, 0, 0))
    cache = pl.BlockSpec((None, past, B_W), lambda i, j: (i, 0, 0))
    qblk = pl.BlockSpec((None, tq, B_W), lambda i, j: (i, j, 0))
    tab_q = pl.BlockSpec((tq, B_W), lambda i, j: (j, 0))
    tab = pl.BlockSpec((l, B_W), lambda i, j: (0, 0))
    return pl.pallas_call(
        functools.partial(_attn_lat_kernel, past=past),
        grid=(b, l // tq),
        in_specs=[qblk, full, full, cache, cache, tab_q, tab_q, tab, tab,
                  pl.BlockSpec((8, LANES), lambda i, j: (0, 0)), pl.BlockSpec((1, DV_B), lambda i, j: (0, 0))],
        out_specs=qblk,
        out_shape=jax.ShapeDtypeStruct((b, l, B_W), F32),
        scratch_shapes=[pltpu.VMEM((past + l, B_W), BF16), pltpu.VMEM((past + l, B_W), BF16)],
        compiler_params=_cparams(("arbitrary", "arbitrary")),
        name="attn_lat",
    )(q, k, v, ck, cv, cos, sin, cos, sin, lam_pack, gain)


def _rope_tables(length):
    pairs = DQK_B // 4
    n_rows = length // GRID_W
    pos_row = jnp.repeat(jnp.arange(n_rows), GRID_W).astype(F32)
    pos_col = jnp.tile(jnp.arange(GRID_W), n_rows).astype(F32)
    inv = ROPE_THETA ** (-jnp.arange(pairs, dtype=F32) / pairs)
    ang_r = pos_row[:, None] * inv
    ang_c = pos_col[:, None] * inv
    cos = jnp.concatenate([jnp.cos(ang_r), jnp.cos(ang_r), jnp.cos(ang_c), jnp.cos(ang_c)], axis=-1)
    sin = jnp.concatenate([-jnp.sin(ang_r), jnp.sin(ang_r), -jnp.sin(ang_c), jnp.sin(ang_c)], axis=-1)
    reps = B_W // DQK_B
    return jnp.tile(cos, (1, reps)), jnp.tile(sin, (1, reps))


def _block_diag(x):
    lo, hi = x[:, :LANES], x[:, LANES:]
    first = lax.broadcasted_iota(jnp.int32, lo.shape, 1) < DK_A
    top = jnp.concatenate([jnp.where(first, lo, 0), jnp.where(first, 0, lo)], axis=0)
    bot = jnp.concatenate([jnp.where(first, hi, 0), jnp.where(first, 0, hi)], axis=0)
    zero = jnp.zeros_like(top)
    return jnp.concatenate([jnp.concatenate([top, zero], axis=1), jnp.concatenate([zero, bot], axis=1)], axis=0)


SOLVE_CHUNKS = 8
DELTA_ROWS = 1024


def _delta_kernel(*refs, nb, length, has_state):
    if has_state:
        (q_ref, k_ref, v_ref, z_ref, dec_ref, bet_ref, cw_ref, gpar_ref, dn_ref, s0_ref, o_ref, st_ref, *scr) = refs
    else:
        (q_ref, k_ref, v_ref, z_ref, dec_ref, bet_ref, cw_ref, gpar_ref, dn_ref, o_ref, st_ref, *scr) = refs
        s0_ref = None
    xpad, qs, ks, vs, gexp, bexp, u0_s, kg_s, w_s, qg_s, qkm_s, of_s, ob_s, s_s = scr
    L = length
    R = nb * L
    n_chunks = L // CHUNK
    grp = pl.program_id(1)
    W = GROUP_W
    PAD = 8

    r256 = lax.broadcasted_iota(jnp.int32, (W, W), 0)
    c256 = lax.broadcasted_iota(jnp.int32, (W, W), 1)
    bd_mask = (r256 >> 6) == (c256 >> 6)
    ones_bd = jnp.where(bd_mask, 1.0, 0.0).astype(BF16)
    ri = lax.broadcasted_iota(jnp.int32, (CHUNK, W), 0)
    cj = lax.broadcasted_iota(jnp.int32, (CHUNK, W), 1) & (CHUNK - 1)
    eye_p = ri == cj
    row_l = lax.broadcasted_iota(jnp.int32, (L, 1), 0)
    lane_g = lax.broadcasted_iota(jnp.int32, (L, LANES), 1)
    er = lax.broadcasted_iota(jnp.int32, (LANES, 2 * W), 0)
    ec = lax.broadcasted_iota(jnp.int32, (LANES, 2 * W), 1)
    expand = jnp.where(er == (ec >> 8) * H_A + grp * 4 + ((ec & (W - 1)) >> 6), 1.0, 0.0).astype(BF16)

    bd = _block_diag

    xpad[0:PAD, :] = jnp.zeros((PAD, W), F32)
    xpad[PAD + L:PAD + L + PAD, :] = jnp.zeros((PAD, W), F32)

    def prologue(bi, carry):
        rows_b = pl.ds(pl.multiple_of(bi * L, L), L)
        w = cw_ref[...]

        def conv_slab(x_ref, slab):
            xpad[PAD:PAD + L, :] = x_ref[bi]
            acc = None
            for tap in range(CONV_K):
                term = xpad[PAD + tap - CONV_K // 2:PAD + tap - CONV_K // 2 + L, :] \
                    * w[tap:tap + 1, slab * W:(slab + 1) * W]
                acc = term if acc is None else acc + term
            return _silu(acc)

        def l2n(y):
            return y * lax.rsqrt(_dot_split(y * y, ones_bd) + EPS)

        qs[rows_b, :] = l2n(conv_slab(q_ref, 0)) * (DK_A ** -0.5)
        ks[rows_b, :] = l2n(conv_slab(k_ref, 1))
        vs[rows_b, :] = conv_slab(v_ref, 2)

        gpar = gpar_ref[...]
        x = dec_ref[bi] + gpar[1:2, :]
        softplus = jnp.maximum(x, 0.0) + jnp.log1p(jnp.exp(-jnp.abs(x)))
        g = jnp.where(lane_g < 2 * H_A, -jnp.exp(gpar[0:1, :]) * softplus, 0.0)
        beta = jax.nn.sigmoid(bet_ref[bi])
        pos = row_l & (CHUNK - 1)
        gcf = g
        gcb = g
        for s in (1, 2, 4, 8, 16, 32):
            gcf = gcf + jnp.where(pos >= s, pltpu.roll(gcf, s, axis=0), 0.0)
            gcb = gcb + jnp.where(pos < CHUNK - s, pltpu.roll(gcb, L - s, axis=0), 0.0)
        gsel = jnp.where(lane_g < H_A, gcf, gcb)
        ge_all = _dot_split(gsel, expand)
        be_all = _dot_split(beta, expand)
        for d in range(2):
            gexp[d, rows_b, :] = ge_all[:, d * W:(d + 1) * W]
            bexp[d, rows_b, :] = be_all[:, d * W:(d + 1) * W]
        return carry

    lax.fori_loop(0, nb, prologue, 0)

    def solve_body(j, carry):
        chunks = []
        for k in range(SOLVE_CHUNKS):
            rows = pl.ds(pl.multiple_of((j * SOLVE_CHUNKS + k) * CHUNK, CHUNK), CHUNK)
            kc = ks[rows, :]
            qc = qs[rows, :]
            kb = kc.astype(BF16)
            res = lax.dot_general(jnp.concatenate([qc.astype(BF16), kb], axis=0), bd(kb),
                                  (((1,), (1,)), ((), ())), preferred_element_type=F32)
            qk, kk = res[:CHUNK], res[CHUNK:]
            ch = dict(rows=rows, bd_k=bd(kb), bd_v=bd(vs[rows, :].astype(BF16)), t=[], scale_u=[], scale_w=[])
            nsum = None
            for d in range(2):
                ge = gexp[d, rows, :]
                be = bexp[d, rows, :]
                if d == 0:
                    incl, strict, gl = ri >= cj, ri > cj, ge[CHUNK - 1:CHUNK, :]
                else:
                    incl, strict, gl = ri <= cj, ri < cj, ge[0:1, :]
                g_row = jnp.sum(jnp.where(eye_p, ge, 0.0), axis=0, keepdims=True)
                b_row = jnp.sum(jnp.where(eye_p, be, 0.0), axis=0, keepdims=True)
                dec_m = jnp.exp(jnp.where(incl, ge - g_row, -jnp.inf))
                nmat = jnp.where(strict, be * kk * dec_m, 0.0)
                nsum = nmat if nsum is None else nsum + nmat
                qkm_s[d, rows, :] = jnp.where(incl, qk * dec_m, 0.0).astype(BF16)
                qg_s[d, rows, :] = (qc * jnp.exp(ge)).astype(BF16)
                kg_s[d, rows, :] = kc * jnp.exp(gl - ge)
                ch["scale_u"].append(b_row)
                ch["scale_w"].append(b_row * jnp.exp(g_row))
                ch["t"].append(jnp.where(eye_p, 1.0, 0.0) - jnp.where((ri >> 1) == (cj >> 1), nmat, 0.0))
            ch["n_both"] = nsum.astype(BF16)
            chunks.append(ch)

        for lb in range(1, 6):
            same = (ri >> (lb + 1)) == (cj >> (lb + 1))
            pair = (same & (((ri >> lb) & 1) == 1) & (((cj >> lb) & 1) == 0),
                    same & (((cj >> lb) & 1) == 1) & (((ri >> lb) & 1) == 0))
            for ch in chunks:
                ch["tb"] = [t.astype(BF16) for t in ch["t"]]
                rhs = bd(jnp.where(pair[0] | pair[1], ch["n_both"], 0))
                ch["y"] = _dot(jnp.concatenate(ch["tb"], axis=0), rhs).astype(BF16)
            for ch in chunks:
                ys = (jnp.where(pair[0], ch["y"][:CHUNK], 0), jnp.where(pair[1], ch["y"][CHUNK:], 0))
                ch["t"] = [ch["t"][d] - _dot(ys[d], bd(ch["tb"][d])) for d in range(2)]
        for ch in chunks:
            tu = jnp.concatenate([(ch["t"][d] * ch["scale_u"][d]).astype(BF16) for d in range(2)], axis=0)
            tw = jnp.concatenate([(ch["t"][d] * ch["scale_w"][d]).astype(BF16) for d in range(2)], axis=0)
            u0 = _dot(tu, ch["bd_v"])
            w = _dot(tw, ch["bd_k"]).astype(BF16)
            for d in range(2):
                u0_s[d, ch["rows"], :] = u0[d * CHUNK:(d + 1) * CHUNK]
                w_s[d, ch["rows"], :] = w[d * CHUNK:(d + 1) * CHUNK]
        return carry

    lax.fori_loop(0, R // (SOLVE_CHUNKS * CHUNK), solve_body, 0)

    s_s[...] = jnp.zeros_like(s_s)
    if has_state:
        for bi in range(nb):
            for d in range(2):
                for h in range(4):
                    s_s[bi, d, h * DK_A:(h + 1) * DK_A, h * DK_A:(h + 1) * DK_A] = s0_ref[bi, d, h]

    def scan_body(i, carry):
        chains = []
        for bi in range(nb):
            for d in range(2):
                c = i if d == 0 else n_chunks - 1 - i
                r0 = pl.multiple_of(bi * L + c * CHUNK, CHUNK)
                rows = pl.ds(r0, CHUNK)
                if d == 0:
                    g_last = gexp[d, pl.ds(r0 + CHUNK - 8, 8), :][7:8]
                else:
                    g_last = gexp[d, pl.ds(r0, 8), :][0:1]
                chains.append(dict(
                    bi=bi, d=d, rows=rows, s=s_s[bi, d], u0=u0_s[d, rows, :], kg=kg_s[d, rows, :],
                    wq=jnp.concatenate([w_s[d, rows, :], qg_s[d, rows, :]], axis=0), qkm=qkm_s[d, rows, :],
                    decay=jnp.exp(g_last)))
        for c in chains:
            c["ws"] = _dot(c["wq"], c["s"].astype(BF16))
        for c in chains:
            c["ub"] = (c["u0"] - c["ws"][:CHUNK]).astype(BF16)
        for c in chains:
            c["o"] = c["ws"][CHUNK:] + _dot(c["qkm"], bd(c["ub"]))
            upd = _dot(c["kg"].T.astype(BF16), c["ub"])
            c["s_new"] = c["s"] * c["decay"] + jnp.where(bd_mask, upd, 0.0)
        for c in chains:
            s_s[c["bi"], c["d"]] = c["s_new"]
            if c["d"] == 0:
                of_s[c["rows"], :] = c["o"]
            else:
                ob_s[c["rows"], :] = c["o"]
        return carry

    lax.fori_loop(0, n_chunks, scan_body, 0)

    def epilogue(bi, carry):
        rows_b = pl.ds(pl.multiple_of(bi * L, L), L)
        o = of_s[rows_b, :] + ob_s[rows_b, :]
        ms = _dot_split(o * o, ones_bd) * (1.0 / DK_A)
        o = o * lax.rsqrt(ms + EPS) * dn_ref[...]
        o_ref[bi] = o * _silu(z_ref[bi])
        return carry

    lax.fori_loop(0, nb, epilogue, 0)
    for bi in range(nb):
        for d in range(2):
            for h in range(4):
                st_ref[bi, d, h] = s_s[bi, d, h * DK_A:(h + 1) * DK_A, h * DK_A:(h + 1) * DK_A]


def _delta(qkv, z, dec, bet, conv_w, gpar, dnorm, s0):
    b, l, _ = qkv.shape
    has_state = s0 is not None
    w = GROUP_W
    n_grp = A_W // w
    nb = DELTA_ROWS // l
    r = nb * l

    def slab(k):
        return pl.BlockSpec((nb, l, w), lambda i, g: (i, 0, k * n_grp + g))

    narrow = pl.BlockSpec((nb, l, LANES), lambda i, g: (i, 0, 0))
    st_spec = pl.BlockSpec((nb, 2, 4, DK_A, DK_A), lambda i, g: (i, 0, g, 0, 0))
    in_specs = [slab(0), slab(1), slab(2),
                pl.BlockSpec((nb, l, w), lambda i, g: (i, 0, g)),
                narrow, narrow,
                pl.BlockSpec((None, CONV_K, 3 * w), lambda i, g: (g, 0, 0)),
                pl.BlockSpec((8, LANES), lambda i, g: (0, 0)),
                pl.BlockSpec((1, w), lambda i, g: (0, 0))]
    args = [qkv, qkv, qkv, z, dec, bet, conv_w, gpar, dnorm]
    if has_state:
        in_specs.append(st_spec)
        args.append(s0)
    scratch = ([pltpu.VMEM((l + 16, w), F32)] + [pltpu.VMEM((r, w), F32)] * 3 + [pltpu.VMEM((2, r, w), F32)] * 4
               + [pltpu.VMEM((2, r, w), BF16)] * 3 + [pltpu.VMEM((r, w), F32)] * 2
               + [pltpu.VMEM((nb, 2, w, w), F32)])
    return pl.pallas_call(
        functools.partial(_delta_kernel, nb=nb, length=l, has_state=has_state),
        grid=(b // nb, n_grp),
        in_specs=in_specs,
        out_specs=[pl.BlockSpec((nb, l, w), lambda i, g: (i, 0, g)), st_spec],
        out_shape=[jax.ShapeDtypeStruct((b, l, A_W), F32),
                   jax.ShapeDtypeStruct((b, 2, H_A, DK_A, DK_A), F32)],
        scratch_shapes=scratch,
        compiler_params=_cparams(("arbitrary", "arbitrary")),
        name="delta_state" if has_state else "delta",
    )(*args)


def _pad_lanes(x, width):
    return jnp.pad(x, [(0, 0)] * (x.ndim - 1) + [(0, width - x.shape[-1])])


def kernel(x_prompt, x_sample, cache_diff_k, cache_diff_v, state_delta, c, c_ctx, w_ada, b_ada, norm_ffn1,
           w_ffn1_in, w_ffn1_out, norm_mix, w_in, conv_w, a_log, dt_bias, delta_norm, lambda_q1, lambda_k1,
           lambda_q2, lambda_k2, diff_norm, w_out, norm_ffn2, w_ffn2_in, w_ffn2_out, norm_final):
    bp, lp, _ = x_prompt.shape
    bs, ls, _ = x_sample.shape
    past = cache_diff_k.shape[2]

    w1_in = w_ffn1_in[0].astype(BF16)
    w1_out = w_ffn1_out[0].astype(BF16)
    w2_in = w_ffn2_in[0].astype(BF16)
    w2_out = w_ffn2_out[0].astype(BF16)
    w_mix_out = w_out[0].astype(BF16)
    wi = w_in[0]
    a_end = 4 * A_W
    dec_w = wi[:, a_end:a_end + 2 * H_A]
    bet_w = wi[:, a_end + 2 * H_A:a_end + 4 * H_A]
    rest = wi[:, a_end + 4 * H_A:]
    w_pack = jnp.concatenate([wi[:, :a_end], rest, _pad_lanes(dec_w, LANES), _pad_lanes(bet_w, LANES)],
                             axis=1).astype(BF16)
    g1 = norm_ffn1[0].reshape(1, D_MODEL)
    gm = norm_mix[0].reshape(1, D_MODEL)
    g2 = norm_ffn2[0].reshape(1, D_MODEL)
    gf = norm_final.reshape(1, D_MODEL)
    cw = conv_w[0].reshape(CONV_K, 3, 2, GROUP_W).transpose(2, 0, 1, 3).reshape(2, CONV_K, 3 * GROUP_W)
    gpar = jnp.zeros((8, LANES), F32)
    gpar = gpar.at[0, :2 * H_A].set(a_log[0].reshape(-1)).at[1, :2 * H_A].set(dt_bias[0].reshape(-1))
    dnorm = jnp.tile(delta_norm[0], 4).reshape(1, GROUP_W)
    lam_pack = jnp.zeros((8, LANES), F32)
    for r, v in enumerate((lambda_q1, lambda_k1, lambda_q2, lambda_k2)):
        lam_pack = lam_pack.at[r, :DQK_B].set(v[0])
    dgain = diff_norm[0].reshape(1, DV_B)
    cos, sin = _rope_tables(ls)

    cond = jnp.concatenate([c_ctx[None, :], c, jnp.zeros((8 - 1 - bs, D_MODEL), F32)], axis=0)
    mod = _adaln(cond, w_ada[0], b_ada[0]).reshape(8, N_MOD, D_MODEL)

    def trunk(x3, rows_per_cond, ctx):
        b, l, _ = x3.shape
        x = x3.reshape(b * l, D_MODEL)
        x1 = _ffn1(x, mod, g1, w1_in, w1_out, rows_per_cond)
        qkv, z, bq, bk, bv, dec, bet = _inproj(x1, mod, gm, w_pack, rows_per_cond)
        r3 = lambda a: a.reshape(b, l, a.shape[-1])
        s0 = None if ctx is None else ctx[2]
        oa, st = _delta(r3(qkv), r3(z), r3(dec), r3(bet), cw, gpar, dnorm, s0)
        if ctx is None:
            ob = _attn_ctx(r3(bq), r3(bk), r3(bv), lam_pack, dgain)
        else:
            ob = _attn_lat(r3(bq), r3(bk), r3(bv), ctx[0], ctx[1], cos, sin, lam_pack, dgain)
        y = _post(x1, oa.reshape(b * l, A_W), ob.reshape(b * l, B_W), mod, g2, gf, w_mix_out, w2_in, w2_out,
                  rows_per_cond)
        return y.reshape(b, l, D_MODEL), bk, bv, st

    y_prompt, k_ctx, v_ctx, s_ctx = trunk(x_prompt, None, None)
    ctx = (cache_diff_k[:, 0].reshape(bs, past, B_W), cache_diff_v[:, 0].reshape(bs, past, B_W), state_delta[:, 0])
    y_sample, _, _, _ = trunk(x_sample, ls, ctx)
    new_k = k_ctx.reshape(bp, 1, lp, H_B, 2, DQK_B)
    new_v = v_ctx.reshape(bp, 1, lp, H_B, DV_B)
    return (y_prompt, y_sample, new_k, new_v, s_ctx[:, None])
```

```python
import functools
import math

import jax
import jax.numpy as jnp
from jax import lax
from jax.experimental import pallas as pl
from jax.experimental.pallas import tpu as pltpu

F32 = jnp.float32
BF16 = jnp.bfloat16

D_MODEL = 1024
D_FF = 2816
N_MOD = 9
H_A = 8
DK_A = 64
A_W = H_A * DK_A
H_B = 4
DQK_B = 64
DV_B = 128
B_W = H_B * DV_B
CONV_K = 5
CHUNK = 64
GRID_W = 64
ROPE_THETA = 10000.0
EPS = 1e-6
LAM_INIT = 0.8 - 0.6 * math.exp(-0.3 * 0)

LANES = 128
GROUP_W = 4 * DK_A
IN_PACK_COLS = 3 * A_W + A_W + 3 * B_W + 2 * LANES
VMEM_LIMIT = 56 * 1024 * 1024

TOKEN_TILE = 512
Q_TILE = 256


def _cparams(sem):
    return pltpu.CompilerParams(dimension_semantics=sem, vmem_limit_bytes=VMEM_LIMIT)


def _resident(shape):
    nd = len(shape)
    return pl.BlockSpec(shape, lambda *_: (0,) * nd, pipeline_mode=pl.Buffered(1))


def _silu(x):
    return x * jax.nn.sigmoid(x)


def _mod_norm(x, gain, shift, scale):
    y = x * lax.rsqrt(jnp.mean(x * x, axis=-1, keepdims=True) + EPS)
    return (y * gain) * (1.0 + scale) + shift


def _dot(a, b):
    return jnp.dot(a, b, preferred_element_type=F32)


def _dot_split(a, b_bf16):
    hi = a.astype(BF16)
    lo = (a - hi.astype(F32)).astype(BF16)
    return _dot(hi, b_bf16) + _dot(lo, b_bf16)


def _adaln_kernel(c_ref, w_ref, b_ref, o_ref):
    s = _silu(c_ref[...])
    o_ref[...] = _dot(s.astype(BF16), w_ref[...].astype(BF16)) + b_ref[...]


def _adaln(cond, w_ada, b_ada):
    n = N_MOD * D_MODEL
    tn = n // 8
    return pl.pallas_call(
        _adaln_kernel,
        grid=(n // tn,),
        in_specs=[pl.BlockSpec((8, D_MODEL), lambda j: (0, 0)),
                  pl.BlockSpec((D_MODEL, tn), lambda j: (0, j)),
                  pl.BlockSpec((1, tn), lambda j: (0, j))],
        out_specs=pl.BlockSpec((8, tn), lambda j: (0, j)),
        out_shape=jax.ShapeDtypeStruct((8, n), F32),
        compiler_params=_cparams(("arbitrary",)),
        name="adaln",
    )(cond, w_ada, b_ada.reshape(1, n))


def _swiglu_update(x, mod_ref, k0, gain, wg_ref, wu_ref, wo_ref):
    h = _mod_norm(x, gain, mod_ref[k0:k0 + 1, :], mod_ref[k0 + 1:k0 + 2, :]).astype(BF16)
    half = D_FF // 2
    acc = None
    for j in range(2):
        sl = slice(j * half, (j + 1) * half)
        g = _dot(h, wg_ref[:, sl])
        u = _dot(h, wu_ref[:, sl])
        part = _dot((_silu(g) * u).astype(BF16), wo_ref[sl, :])
        acc = part if acc is None else acc + part
    return x + (0.5 * mod_ref[k0 + 2:k0 + 3, :]) * acc


def _ffn1_kernel(x_ref, mod_ref, gain_ref, wg_ref, wu_ref, wo_ref, o_ref):
    o_ref[...] = _swiglu_update(x_ref[...], mod_ref, 0, gain_ref[...], wg_ref, wu_ref, wo_ref)


def _mod_spec(rows_per_cond, tm):
    if rows_per_cond is None:
        return pl.BlockSpec((None, N_MOD, D_MODEL), lambda i: (0, 0, 0))
    return pl.BlockSpec((None, N_MOD, D_MODEL), lambda i: (1 + (i * tm) // rows_per_cond, 0, 0))


def _ffn_weight_specs():
    return [pl.BlockSpec((D_MODEL, D_FF), lambda i: (0, 0), pipeline_mode=pl.Buffered(1)),
            pl.BlockSpec((D_MODEL, D_FF), lambda i: (0, 1), pipeline_mode=pl.Buffered(1)),
            _resident((D_FF, D_MODEL))]


def _ffn1(x, mod, gain, w_in, w_out, rows_per_cond):
    t = x.shape[0]
    tm = TOKEN_TILE
    row = pl.BlockSpec((tm, D_MODEL), lambda i: (i, 0))
    return pl.pallas_call(
        _ffn1_kernel,
        grid=(t // tm,),
        in_specs=[row, _mod_spec(rows_per_cond, tm), _resident((1, D_MODEL))] + _ffn_weight_specs(),
        out_specs=row,
        out_shape=jax.ShapeDtypeStruct((t, D_MODEL), F32),
        compiler_params=_cparams(("arbitrary",)),
        name="ffn1",
    )(x, mod, gain, w_in, w_in, w_out)


_IN_SEGS = (("qkv", 0, 3 * A_W), ("z", 3 * A_W, A_W), ("bq", 4 * A_W, B_W), ("bk", 4 * A_W + B_W, B_W),
            ("bv", 4 * A_W + 2 * B_W, B_W), ("dec", 4 * A_W + 3 * B_W, LANES),
            ("bet", 4 * A_W + 3 * B_W + LANES, LANES))


HALO = 8
CONV_SLAB = 256


def _inproj_kernel(*refs, tm, seq_len, emit_maps):
    x_ref, xp_ref, xn_ref, mod_ref, gain_ref, w_ref, cw_ref, gpar_ref = refs[:8]
    qkv_ref, z_ref, bq_ref, bk_ref, bv_ref, gcs_ref, beta_ref = refs[8:15]
    xpad = refs[-1]
    shift, scale = mod_ref[3:4, :], mod_ref[4:5, :]
    h = _mod_norm(x_ref[...], gain_ref[...], shift, scale).astype(BF16)
    segs = dict((name, (start, width)) for name, start, width in _IN_SEGS)

    def proj(name):
        start, width = segs[name]
        return _dot(h, w_ref[:, start:start + width])

    seg = min(tm, seq_len)
    n_seg = tm // seg
    span = seg + 2 * HALO
    qkv_w = 3 * A_W
    if seq_len > tm:
        tiles = seq_len // tm
        pos = pl.program_id(0) % tiles
        hh = _mod_norm(jnp.concatenate([xp_ref[...], xn_ref[...]], axis=0), gain_ref[...], shift, scale)
        halo = _dot(hh.astype(BF16), w_ref[:, 0:qkv_w])
        before = jnp.where(pos > 0, halo[:HALO], 0.0)
        after = jnp.where(pos < tiles - 1, halo[HALO:], 0.0)
    else:
        before = after = jnp.zeros((HALO, qkv_w), F32)
    for s in range(n_seg):
        xpad[s * span:s * span + HALO, :] = before
        xpad[s * span + HALO + seg:(s + 1) * span, :] = after

    units = [(s, c) for s in range(n_seg) for c in range(qkv_w // CONV_SLAB)]

    def project(s, c):
        cs = slice(c * CONV_SLAB, (c + 1) * CONV_SLAB)
        xpad[s * span + HALO:s * span + HALO + seg, cs] = _dot(h[s * seg:(s + 1) * seg], w_ref[:, cs])

    def convolve(s, c):
        cs = slice(c * CONV_SLAB, (c + 1) * CONV_SLAB)
        acc = None
        for tap in range(CONV_K):
            lo = s * span + HALO + tap - CONV_K // 2
            term = xpad[lo:lo + seg, cs] * cw_ref[tap:tap + 1, cs]
            acc = term if acc is None else acc + term
        qkv_ref[s * seg:(s + 1) * seg, cs] = _silu(acc)

    z_ref[...] = proj("z")
    bq_ref[...] = proj("bq")
    zk = proj("bk")
    zv = proj("bv")
    bk_ref[...] = zk
    bv_ref[...] = zv
    if emit_maps:
        bk8_ref, bv4_ref = refs[15:17]
        for r in range(2 * H_B):
            bk8_ref[:, r, :] = zk[:, r * DQK_B:(r + 1) * DQK_B]
        for r in range(H_B):
            bv4_ref[:, r, :] = zv[:, r * DV_B:(r + 1) * DV_B]
    for unit in units:
        project(*unit)
        convolve(*unit)

    gpar = gpar_ref[...]
    xg = proj("dec") + gpar[1:2, :]
    softplus = jnp.maximum(xg, 0.0) + jnp.log1p(jnp.exp(-jnp.abs(xg)))
    lane = lax.broadcasted_iota(jnp.int32, (tm, LANES), 1)
    g = jnp.where(lane < 2 * H_A, -jnp.exp(gpar[0:1, :]) * softplus, 0.0)
    beta_ref[...] = jax.nn.sigmoid(proj("bet"))
    pos_c = lax.broadcasted_iota(jnp.int32, (tm, 1), 0) & (CHUNK - 1)
    gcf = g
    gcb = g
    for sft in (1, 2, 4, 8, 16, 32):
        gcf = gcf + jnp.where(pos_c >= sft, pltpu.roll(gcf, sft, axis=0), 0.0)
        gcb = gcb + jnp.where(pos_c < CHUNK - sft, pltpu.roll(gcb, tm - sft, axis=0), 0.0)
    gcs_ref[...] = jnp.where(lane < H_A, gcf, gcb)


def _inproj(x, mod, gain, w_pack, conv_w, gpar, rows_per_cond, seq_len, emit_maps):
    t = x.shape[0]
    tm = TOKEN_TILE
    hb = tm // HALO
    row = pl.BlockSpec((tm, D_MODEL), lambda i: (i, 0))
    prev = pl.BlockSpec((HALO, D_MODEL), lambda i: (jnp.maximum(i * hb - 1, 0), 0))
    nxt = pl.BlockSpec((HALO, D_MODEL), lambda i: (jnp.minimum((i + 1) * hb, t // HALO - 1), 0))
    widths = [w for _, _, w in _IN_SEGS]
    out_specs = [pl.BlockSpec((tm, w), lambda i: (i, 0)) for w in widths]
    out_shape = [jax.ShapeDtypeStruct((t, w), F32) for w in widths]
    if emit_maps:
        out_specs += [pl.BlockSpec((tm, 2 * H_B, DQK_B), lambda i: (i, 0, 0)),
                      pl.BlockSpec((tm, H_B, DV_B), lambda i: (i, 0, 0))]
        out_shape += [jax.ShapeDtypeStruct((t, 2 * H_B, DQK_B), F32), jax.ShapeDtypeStruct((t, H_B, DV_B), F32)]
    return pl.pallas_call(
        functools.partial(_inproj_kernel, tm=tm, seq_len=seq_len, emit_maps=emit_maps),
        grid=(t // tm,),
        in_specs=[row, prev, nxt, _mod_spec(rows_per_cond, tm), _resident((1, D_MODEL)),
                  _resident((D_MODEL, IN_PACK_COLS)), _resident((CONV_K, 3 * A_W)), _resident((8, LANES))],
        out_specs=out_specs,
        out_shape=out_shape,
        scratch_shapes=[pltpu.VMEM((tm + 2 * HALO * max(1, tm // seq_len), 3 * A_W), F32)],
        compiler_params=_cparams(("arbitrary",)),
        name="inproj",
    )(x, x, x, mod, gain, w_pack, conv_w, gpar)


def _post_kernel(x_ref, oa_ref, ob_ref, mod_ref, gain_ref, gfin_ref, wmix_ref, wg_ref, wu_ref, wo_ref, o_ref):
    y = _dot(oa_ref[...].astype(BF16), wmix_ref[0:A_W, :]) + _dot(ob_ref[...].astype(BF16), wmix_ref[A_W:, :])
    x = x_ref[...] + mod_ref[5:6, :] * y
    x = _swiglu_update(x, mod_ref, 6, gain_ref[...], wg_ref, wu_ref, wo_ref)
    o_ref[...] = (x * lax.rsqrt(jnp.mean(x * x, axis=-1, keepdims=True) + EPS)) * gfin_ref[...]


def _post(x, oa, ob, mod, gain, gfin, w_mix, w_in, w_out, rows_per_cond):
    t = x.shape[0]
    tm = TOKEN_TILE
    row = pl.BlockSpec((tm, D_MODEL), lambda i: (i, 0))
    half = pl.BlockSpec((tm, A_W), lambda i: (i, 0))
    return pl.pallas_call(
        _post_kernel,
        grid=(t // tm,),
        in_specs=[row, half, half, _mod_spec(rows_per_cond, tm), _resident((1, D_MODEL)),
                  _resident((1, D_MODEL)), _resident((A_W + B_W, D_MODEL))] + _ffn_weight_specs(),
        out_specs=row,
        out_shape=jax.ShapeDtypeStruct((t, D_MODEL), F32),
        compiler_params=_cparams(("arbitrary",)),
        name="post",
    )(x, oa, ob, mod, gain, gfin, w_mix, w_in, w_in, w_out)


def _rope(x, cos, sin_signed):
    w = x.shape[-1]
    lane = lax.broadcasted_iota(jnp.int32, x.shape, 1)
    partner = jnp.where((lane & 31) < 16, pltpu.roll(x, w - 16, axis=1), pltpu.roll(x, 16, axis=1))
    return x * cos + partner * sin_signed


def _attn_core(q, k_ref, v_ref, lam, gain_ref, o_ref):
    tq = q.shape[0]
    lane = lax.broadcasted_iota(jnp.int32, (tq, DV_B), 1)
    qs = q * (DQK_B ** -0.5)
    for h in range(H_B):
        hs = slice(h * DV_B, (h + 1) * DV_B)
        qh = qs[:, hs]
        q2 = jnp.concatenate([jnp.where(lane < DQK_B, qh, 0.0), jnp.where(lane >= DQK_B, qh, 0.0)], axis=0)
        s = lax.dot_general(q2.astype(BF16), k_ref[:, hs], (((1,), (1,)), ((), ())),
                            preferred_element_type=F32)
        e = jnp.exp(s - jnp.max(s, axis=-1, keepdims=True))
        r = _dot(e.astype(BF16), v_ref[:, hs]) / jnp.sum(e, axis=-1, keepdims=True)
        o = r[:tq] - lam * r[tq:]
        o = o * lax.rsqrt(jnp.mean(o * o, axis=-1, keepdims=True) + EPS)
        o_ref[:, hs] = (o * gain_ref[...]) * (1.0 - LAM_INIT)


def _lambda(lam_ref):
    l = lam_ref[...]
    s1 = jnp.sum(l[0:1] * l[1:2], axis=-1, keepdims=True)
    s2 = jnp.sum(l[2:3] * l[3:4], axis=-1, keepdims=True)
    return jnp.exp(s1) - jnp.exp(s2) + LAM_INIT


CTX_SEQS = 4


def _attn_ctx_kernel(q_ref, k_ref, v_ref, lam_ref, gain_ref, o_ref, k_s, v_s):
    lam = _lambda(lam_ref)

    def one_sequence(bi, carry):
        k_s[...] = k_ref[bi].astype(BF16)
        v_s[...] = v_ref[bi].astype(BF16)
        _attn_core(q_ref[bi], k_s, v_s, lam, gain_ref, o_ref.at[bi])
        return carry

    lax.fori_loop(0, CTX_SEQS, one_sequence, 0)


def _attn_ctx(q, k, v, lam_pack, gain):
    b, l, _ = q.shape
    blk = pl.BlockSpec((CTX_SEQS, l, B_W), lambda i: (i, 0, 0))
    return pl.pallas_call(
        _attn_ctx_kernel,
        grid=(b // CTX_SEQS,),
        in_specs=[blk, blk, blk, _resident((8, LANES)), _resident((1, DV_B))],
        out_specs=blk,
        out_shape=jax.ShapeDtypeStruct((b, l, B_W), F32),
        scratch_shapes=[pltpu.VMEM((l, B_W), BF16), pltpu.VMEM((l, B_W), BF16)],
        compiler_params=_cparams(("arbitrary",)),
        name="attn_ctx",
    )(q, k, v, lam_pack, gain)


def _attn_lat_kernel(q_ref, k_ref, v_ref, ck_ref, cv_ref, cosq_ref, sinq_ref, cos_ref, sin_ref,
                     lam_ref, gain_ref, o_ref, k_s, v_s, *, past):
    @pl.when(pl.program_id(1) == 0)
    def _():
        k_s[0:past, :] = ck_ref[...].astype(BF16)
        v_s[0:past, :] = cv_ref[...].astype(BF16)
        k_s[past:, :] = _rope(k_ref[...], cos_ref[...], sin_ref[...]).astype(BF16)
        v_s[past:, :] = v_ref[...].astype(BF16)

    q = _rope(q_ref[...], cosq_ref[...], sinq_ref[...])
    _attn_core(q, k_s, v_s, _lambda(lam_ref), gain_ref, o_ref)


def _attn_lat(q, k, v, ck, cv, cos, sin, lam_pack, gain):
    b, l, _ = q.shape
    past = ck.shape[1]
    tq = Q_TILE
    full = pl.BlockSpec((None, l, B_W), lambda i, j: (i, 0, 0))
    cache = pl.BlockSpec((None, past, B_W), lambda i, j: (i, 0, 0))
    qblk = pl.BlockSpec((None, tq, B_W), lambda i, j: (i, j, 0))
    tab_q = pl.BlockSpec((tq, B_W), lambda i, j: (j, 0))
    tab = pl.BlockSpec((l, B_W), lambda i, j: (0, 0))
    return pl.pallas_call(
        functools.partial(_attn_lat_kernel, past=past),
        grid=(b, l // tq),
        in_specs=[qblk, full, full, cache, cache, tab_q, tab_q, tab, tab,
                  pl.BlockSpec((8, LANES), lambda i, j: (0, 0)), pl.BlockSpec((1, DV_B), lambda i, j: (0, 0))],
        out_specs=qblk,
        out_shape=jax.ShapeDtypeStruct((b, l, B_W), F32),
        scratch_shapes=[pltpu.VMEM((past + l, B_W), BF16), pltpu.VMEM((past + l, B_W), BF16)],
        compiler_params=_cparams(("arbitrary", "arbitrary")),
        name="attn_lat",
    )(q, k, v, ck, cv, cos, sin, cos, sin, lam_pack, gain)


def _rope_tables(length):
    pairs = DQK_B // 4
    n_rows = length // GRID_W
    pos_row = jnp.repeat(jnp.arange(n_rows), GRID_W).astype(F32)
    pos_col = jnp.tile(jnp.arange(GRID_W), n_rows).astype(F32)
    inv = ROPE_THETA ** (-jnp.arange(pairs, dtype=F32) / pairs)
    ang_r = pos_row[:, None] * inv
    ang_c = pos_col[:, None] * inv
    cos = jnp.concatenate([jnp.cos(ang_r), jnp.cos(ang_r), jnp.cos(ang_c), jnp.cos(ang_c)], axis=-1)
    sin = jnp.concatenate([-jnp.sin(ang_r), jnp.sin(ang_r), -jnp.sin(ang_c), jnp.sin(ang_c)], axis=-1)
    reps = B_W // DQK_B
    return jnp.tile(cos, (1, reps)), jnp.tile(sin, (1, reps))


def _block_diag(x):
    lo, hi = x[:, :LANES], x[:, LANES:]
    first = lax.broadcasted_iota(jnp.int32, lo.shape, 1) < DK_A
    top = jnp.concatenate([jnp.where(first, lo, 0), jnp.where(first, 0, lo)], axis=0)
    bot = jnp.concatenate([jnp.where(first, hi, 0), jnp.where(first, 0, hi)], axis=0)
    zero = jnp.zeros_like(top)
    return jnp.concatenate([jnp.concatenate([top, zero], axis=1), jnp.concatenate([zero, bot], axis=1)], axis=0)


SOLVE_CHUNKS = 8
DELTA_ROWS = 1024


def _delta_kernel(*refs, nb, length, has_state):
    if has_state:
        (q_ref, k_ref, v_ref, z_ref, gcs_ref, beta_ref, dn_ref, s0_ref, o_ref, st_ref, *scr) = refs
    else:
        (q_ref, k_ref, v_ref, z_ref, gcs_ref, beta_ref, dn_ref, o_ref, st_ref, *scr) = refs
        s0_ref = None
    qs, ks, vs, gexp, bexp, u0_s, kg_s, w_s, qg_s, qkm_s, of_s, ob_s, s_s = scr
    L = length
    R = nb * L
    n_chunks = L // CHUNK
    grp = pl.program_id(1)
    W = GROUP_W

    r256 = lax.broadcasted_iota(jnp.int32, (W, W), 0)
    c256 = lax.broadcasted_iota(jnp.int32, (W, W), 1)
    bd_mask = (r256 >> 6) == (c256 >> 6)
    ones_bd = jnp.where(bd_mask, 1.0, 0.0).astype(BF16)
    ri = lax.broadcasted_iota(jnp.int32, (CHUNK, W), 0)
    cj = lax.broadcasted_iota(jnp.int32, (CHUNK, W), 1) & (CHUNK - 1)
    eye_p = ri == cj
    er = lax.broadcasted_iota(jnp.int32, (LANES, 2 * W), 0)
    ec = lax.broadcasted_iota(jnp.int32, (LANES, 2 * W), 1)
    expand = jnp.where(er == (ec >> 8) * H_A + grp * 4 + ((ec & (W - 1)) >> 6), 1.0, 0.0).astype(BF16)

    bd = _block_diag

    def prologue(bi, carry):
        rows_b = pl.ds(pl.multiple_of(bi * L, L), L)

        def l2n(y):
            return y * lax.rsqrt(_dot_split(y * y, ones_bd) + EPS)

        qs[rows_b, :] = l2n(q_ref[bi]) * (DK_A ** -0.5)
        ks[rows_b, :] = l2n(k_ref[bi])
        vs[rows_b, :] = v_ref[bi]
        ge_all = _dot_split(gcs_ref[bi], expand)
        be_all = _dot_split(beta_ref[bi], expand)
        for d in range(2):
            gexp[d, rows_b, :] = ge_all[:, d * W:(d + 1) * W]
            bexp[d, rows_b, :] = be_all[:, d * W:(d + 1) * W]
        return carry

    lax.fori_loop(0, nb, prologue, 0)

    def solve_body(j, carry):
        chunks = []
        for k in range(SOLVE_CHUNKS):
            rows = pl.ds(pl.multiple_of((j * SOLVE_CHUNKS + k) * CHUNK, CHUNK), CHUNK)
            kc = ks[rows, :]
            qc = qs[rows, :]
            kb = kc.astype(BF16)
            res = lax.dot_general(jnp.concatenate([qc.astype(BF16), kb], axis=0), bd(kb),
                                  (((1,), (1,)), ((), ())), preferred_element_type=F32)
            qk, kk = res[:CHUNK], res[CHUNK:]
            ch = dict(rows=rows, bd_k=bd(kb), bd_v=bd(vs[rows, :].astype(BF16)), t=[], scale_u=[], scale_w=[])
            nsum = None
            for d in range(2):
                ge = gexp[d, rows, :]
                be = bexp[d, rows, :]
                if d == 0:
                    incl, strict, gl = ri >= cj, ri > cj, ge[CHUNK - 1:CHUNK, :]
                else:
                    incl, strict, gl = ri <= cj, ri < cj, ge[0:1, :]
                g_row = jnp.sum(jnp.where(eye_p, ge, 0.0), axis=0, keepdims=True)
                b_row = jnp.sum(jnp.where(eye_p, be, 0.0), axis=0, keepdims=True)
                dec_m = jnp.exp(jnp.where(incl, ge - g_row, -jnp.inf))
                nmat = jnp.where(strict, be * kk * dec_m, 0.0)
                nsum = nmat if nsum is None else nsum + nmat
                qkm_s[d, rows, :] = jnp.where(incl, qk * dec_m, 0.0).astype(BF16)
                qg_s[d, rows, :] = (qc * jnp.exp(ge)).astype(BF16)
                kg_s[d, rows, :] = kc * jnp.exp(gl - ge)
                ch["scale_u"].append(b_row)
                ch["scale_w"].append(b_row * jnp.exp(g_row))
                ch["t"].append(jnp.where(eye_p, 1.0, 0.0) - jnp.where((ri >> 1) == (cj >> 1), nmat, 0.0))
            ch["n_both"] = nsum.astype(BF16)
            chunks.append(ch)

        for lb in range(1, 6):
            same = (ri >> (lb + 1)) == (cj >> (lb + 1))
            pair = (same & (((ri >> lb) & 1) == 1) & (((cj >> lb) & 1) == 0),
                    same & (((cj >> lb) & 1) == 1) & (((ri >> lb) & 1) == 0))
            for ch in chunks:
                ch["tb"] = [t.astype(BF16) for t in ch["t"]]
                rhs = bd(jnp.where(pair[0] | pair[1], ch["n_both"], 0))
                ch["y"] = _dot(jnp.concatenate(ch["tb"], axis=0), rhs).astype(BF16)
            for ch in chunks:
                ys = (jnp.where(pair[0], ch["y"][:CHUNK], 0), jnp.where(pair[1], ch["y"][CHUNK:], 0))
                ch["t"] = [ch["t"][d] - _dot(ys[d], bd(ch["tb"][d])) for d in range(2)]
        for ch in chunks:
            tu = jnp.concatenate([(ch["t"][d] * ch["scale_u"][d]).astype(BF16) for d in range(2)], axis=0)
            tw = jnp.concatenate([(ch["t"][d] * ch["scale_w"][d]).astype(BF16) for d in range(2)], axis=0)
            u0 = _dot(tu, ch["bd_v"])
            w = _dot(tw, ch["bd_k"]).astype(BF16)
            for d in range(2):
                u0_s[d, ch["rows"], :] = u0[d * CHUNK:(d + 1) * CHUNK]
                w_s[d, ch["rows"], :] = w[d * CHUNK:(d + 1) * CHUNK]
        return carry

    lax.fori_loop(0, R // (SOLVE_CHUNKS * CHUNK), solve_body, 0)

    s_s[...] = jnp.zeros_like(s_s)
    if has_state:
        for bi in range(nb):
            for d in range(2):
                for h in range(4):
                    s_s[bi, d, h * DK_A:(h + 1) * DK_A, h * DK_A:(h + 1) * DK_A] = s0_ref[bi, d, h]

    def scan_body(i, carry):
        chains = []
        for bi in range(nb):
            for d in range(2):
                c = i if d == 0 else n_chunks - 1 - i
                r0 = pl.multiple_of(bi * L + c * CHUNK, CHUNK)
                rows = pl.ds(r0, CHUNK)
                if d == 0:
                    g_last = gexp[d, pl.ds(r0 + CHUNK - 8, 8), :][7:8]
                else:
                    g_last = gexp[d, pl.ds(r0, 8), :][0:1]
                chains.append(dict(
                    bi=bi, d=d, rows=rows, s=s_s[bi, d], u0=u0_s[d, rows, :], kg=kg_s[d, rows, :],
                    wq=jnp.concatenate([w_s[d, rows, :], qg_s[d, rows, :]], axis=0), qkm=qkm_s[d, rows, :],
                    decay=jnp.exp(g_last)))
        for c in chains:
            c["ws"] = _dot(c["wq"], c["s"].astype(BF16))
        for c in chains:
            c["ub"] = (c["u0"] - c["ws"][:CHUNK]).astype(BF16)
        for c in chains:
            c["o"] = c["ws"][CHUNK:] + _dot(c["qkm"], bd(c["ub"]))
            upd = _dot(c["kg"].T.astype(BF16), c["ub"])
            c["s_new"] = c["s"] * c["decay"] + jnp.where(bd_mask, upd, 0.0)
        for c in chains:
            s_s[c["bi"], c["d"]] = c["s_new"]
            if c["d"] == 0:
                of_s[c["rows"], :] = c["o"]
            else:
                ob_s[c["rows"], :] = c["o"]
        return carry

    lax.fori_loop(0, n_chunks, scan_body, 0)

    def epilogue(bi, carry):
        rows_b = pl.ds(pl.multiple_of(bi * L, L), L)
        o = of_s[rows_b, :] + ob_s[rows_b, :]
        ms = _dot_split(o * o, ones_bd) * (1.0 / DK_A)
        o = o * lax.rsqrt(ms + EPS) * dn_ref[...]
        o_ref[bi] = o * _silu(z_ref[bi])
        return carry

    lax.fori_loop(0, nb, epilogue, 0)
    for bi in range(nb):
        for d in range(2):
            for h in range(4):
                st_ref[bi, d, h] = s_s[bi, d, h * DK_A:(h + 1) * DK_A, h * DK_A:(h + 1) * DK_A]


def _delta(qkv, z, gcs, beta, dnorm, s0):
    b, l, _ = qkv.shape
    has_state = s0 is not None
    w = GROUP_W
    n_grp = A_W // w
    nb = DELTA_ROWS // l
    r = nb * l

    def slab(k):
        return pl.BlockSpec((nb, l, w), lambda i, g: (i, 0, k * n_grp + g))

    narrow = pl.BlockSpec((nb, l, LANES), lambda i, g: (i, 0, 0))
    st_spec = pl.BlockSpec((nb, 2, 4, DK_A, DK_A), lambda i, g: (i, 0, g, 0, 0))
    in_specs = [slab(0), slab(1), slab(2),
                pl.BlockSpec((nb, l, w), lambda i, g: (i, 0, g)),
                narrow, narrow,
                pl.BlockSpec((1, w), lambda i, g: (0, 0))]
    args = [qkv, qkv, qkv, z, gcs, beta, dnorm]
    if has_state:
        in_specs.append(st_spec)
        args.append(s0)
    scratch = ([pltpu.VMEM((r, w), F32)] * 3 + [pltpu.VMEM((2, r, w), F32)] * 4
               + [pltpu.VMEM((2, r, w), BF16)] * 3 + [pltpu.VMEM((r, w), F32)] * 2
               + [pltpu.VMEM((nb, 2, w, w), F32)])
    return pl.pallas_call(
        functools.partial(_delta_kernel, nb=nb, length=l, has_state=has_state),
        grid=(b // nb, n_grp),
        in_specs=in_specs,
        out_specs=[pl.BlockSpec((nb, l, w), lambda i, g: (i, 0, g)), st_spec],
        out_shape=[jax.ShapeDtypeStruct((b, l, A_W), F32),
                   jax.ShapeDtypeStruct((b, 2, H_A, DK_A, DK_A), F32)],
        scratch_shapes=scratch,
        compiler_params=_cparams(("arbitrary", "arbitrary")),
        name="delta_state" if has_state else "delta",
    )(*args)


def _pad_lanes(x, width):
    return jnp.pad(x, [(0, 0)] * (x.ndim - 1) + [(0, width - x.shape[-1])])


def kernel(x_prompt, x_sample, cache_diff_k, cache_diff_v, state_delta, c, c_ctx, w_ada, b_ada, norm_ffn1,
           w_ffn1_in, w_ffn1_out, norm_mix, w_in, conv_w, a_log, dt_bias, delta_norm, lambda_q1, lambda_k1,
           lambda_q2, lambda_k2, diff_norm, w_out, norm_ffn2, w_ffn2_in, w_ffn2_out, norm_final):
    bp, lp, _ = x_prompt.shape
    bs, ls, _ = x_sample.shape
    past = cache_diff_k.shape[2]

    w1_in = w_ffn1_in[0].astype(BF16)
    w1_out = w_ffn1_out[0].astype(BF16)
    w2_in = w_ffn2_in[0].astype(BF16)
    w2_out = w_ffn2_out[0].astype(BF16)
    w_mix_out = w_out[0].astype(BF16)
    wi = w_in[0]
    a_end = 4 * A_W
    dec_w = wi[:, a_end:a_end + 2 * H_A]
    bet_w = wi[:, a_end + 2 * H_A:a_end + 4 * H_A]
    rest = wi[:, a_end + 4 * H_A:]
    w_pack = jnp.concatenate([wi[:, :a_end], rest, _pad_lanes(dec_w, LANES), _pad_lanes(bet_w, LANES)],
                             axis=1).astype(BF16)
    g1 = norm_ffn1[0].reshape(1, D_MODEL)
    gm = norm_mix[0].reshape(1, D_MODEL)
    g2 = norm_ffn2[0].reshape(1, D_MODEL)
    gf = norm_final.reshape(1, D_MODEL)
    gpar = jnp.zeros((8, LANES), F32)
    gpar = gpar.at[0, :2 * H_A].set(a_log[0].reshape(-1)).at[1, :2 * H_A].set(dt_bias[0].reshape(-1))
    dnorm = jnp.tile(delta_norm[0], 4).reshape(1, GROUP_W)
    lam_pack = jnp.zeros((8, LANES), F32)
    for r, v in enumerate((lambda_q1, lambda_k1, lambda_q2, lambda_k2)):
        lam_pack = lam_pack.at[r, :DQK_B].set(v[0])
    dgain = diff_norm[0].reshape(1, DV_B)
    cos, sin = _rope_tables(ls)

    cond = jnp.concatenate([c_ctx[None, :], c, jnp.zeros((8 - 1 - bs, D_MODEL), F32)], axis=0)
    mod = _adaln(cond, w_ada[0], b_ada[0]).reshape(8, N_MOD, D_MODEL)

    def trunk(x3, rows_per_cond, ctx):
        b, l, _ = x3.shape
        x = x3.reshape(b * l, D_MODEL)
        x1 = _ffn1(x, mod, g1, w1_in, w1_out, rows_per_cond)
        outs = _inproj(x1, mod, gm, w_pack, conv_w[0], gpar, rows_per_cond, l, ctx is None)
        qkv, z, bq, bk, bv, gcs, beta = outs[:7]
        r3 = lambda a: a.reshape(b, l, a.shape[-1])
        s0 = None if ctx is None else ctx[2]
        oa, st = _delta(r3(qkv), r3(z), r3(gcs), r3(beta), dnorm, s0)
        if ctx is None:
            ob = _attn_ctx(r3(bq), r3(bk), r3(bv), lam_pack, dgain)
        else:
            ob = _attn_lat(r3(bq), r3(bk), r3(bv), ctx[0], ctx[1], cos, sin, lam_pack, dgain)
        y = _post(x1, oa.reshape(b * l, A_W), ob.reshape(b * l, B_W), mod, g2, gf, w_mix_out, w2_in, w2_out,
                  rows_per_cond)
        return y.reshape(b, l, D_MODEL), outs[7:], st

    y_prompt, (k_maps, v_heads), s_ctx = trunk(x_prompt, None, None)
    ctx = (cache_diff_k[:, 0].reshape(bs, past, B_W), cache_diff_v[:, 0].reshape(bs, past, B_W), state_delta[:, 0])
    y_sample, _, _ = trunk(x_sample, ls, ctx)
    new_k = k_maps.reshape(bp, 1, lp, H_B, 2, DQK_B)
    new_v = v_heads.reshape(bp, 1, lp, H_B, DV_B)
    return (y_prompt, y_sample, new_k, new_v, s_ctx[:, None])
```

```python
import functools
import math

import jax
import jax.numpy as jnp
from jax import lax
from jax.experimental import pallas as pl
from jax.experimental.pallas import tpu as pltpu

F32 = jnp.float32
BF16 = jnp.bfloat16

D_MODEL = 1024
D_FF = 2816
N_MOD = 9
H_A = 8
DK_A = 64
A_W = H_A * DK_A
H_B = 4
DQK_B = 64
DV_B = 128
B_W = H_B * DV_B
CONV_K = 5
CHUNK = 64
GRID_W = 64
ROPE_THETA = 10000.0
EPS = 1e-6
LAM_INIT = 0.8 - 0.6 * math.exp(-0.3 * 0)

LANES = 128
GROUP_W = 4 * DK_A
IN_PACK_COLS = 3 * A_W + A_W + 3 * B_W + 2 * LANES
VMEM_LIMIT = 56 * 1024 * 1024

TOKEN_TILE = 512
Q_TILE = 256


def _cparams(sem):
    return pltpu.CompilerParams(dimension_semantics=sem, vmem_limit_bytes=VMEM_LIMIT)


def _resident(shape):
    nd = len(shape)
    return pl.BlockSpec(shape, lambda *_: (0,) * nd, pipeline_mode=pl.Buffered(1))


def _silu(x):
    half = 0.5 * x
    return half * jnp.tanh(half) + half


def _mod_norm(x, gain, shift, scale):
    y = x * lax.rsqrt(jnp.mean(x * x, axis=-1, keepdims=True) + EPS)
    return (y * gain) * (1.0 + scale) + shift


def _dot(a, b):
    return jnp.dot(a, b, preferred_element_type=F32)


def _dot_split(a, b_bf16):
    hi = a.astype(BF16)
    lo = (a - hi.astype(F32)).astype(BF16)
    return _dot(hi, b_bf16) + _dot(lo, b_bf16)


def _adaln_kernel(c_ref, w_ref, b_ref, o_ref):
    s = _silu(c_ref[...])
    o_ref[...] = _dot(s.astype(BF16), w_ref[...].astype(BF16)) + b_ref[...]


def _adaln(cond, w_ada, b_ada):
    n = N_MOD * D_MODEL
    tn = n // 8
    return pl.pallas_call(
        _adaln_kernel,
        grid=(n // tn,),
        in_specs=[pl.BlockSpec((8, D_MODEL), lambda j: (0, 0)),
                  pl.BlockSpec((D_MODEL, tn), lambda j: (0, j)),
                  pl.BlockSpec((1, tn), lambda j: (0, j))],
        out_specs=pl.BlockSpec((8, tn), lambda j: (0, j)),
        out_shape=jax.ShapeDtypeStruct((8, n), F32),
        compiler_params=_cparams(("arbitrary",)),
        name="adaln",
    )(cond, w_ada, b_ada.reshape(1, n))


def _swiglu_update(x, mod_ref, k0, gain, wg_ref, wu_ref, wo_ref):
    h = _mod_norm(x, gain, mod_ref[k0:k0 + 1, :], mod_ref[k0 + 1:k0 + 2, :]).astype(BF16)
    half = D_FF // 2
    acc = None
    for j in range(2):
        sl = slice(j * half, (j + 1) * half)
        g = _dot(h, wg_ref[:, sl])
        u = _dot(h, wu_ref[:, sl])
        part = _dot((_silu(g) * u).astype(BF16), wo_ref[sl, :])
        acc = part if acc is None else acc + part
    return x + (0.5 * mod_ref[k0 + 2:k0 + 3, :]) * acc


def _ffn1_kernel(x_ref, mod_ref, gain_ref, wg_ref, wu_ref, wo_ref, o_ref):
    o_ref[...] = _swiglu_update(x_ref[...], mod_ref, 0, gain_ref[...], wg_ref, wu_ref, wo_ref)


def _mod_spec(rows_per_cond, tm):
    if rows_per_cond is None:
        return pl.BlockSpec((None, N_MOD, D_MODEL), lambda i: (0, 0, 0))
    return pl.BlockSpec((None, N_MOD, D_MODEL), lambda i: (1 + (i * tm) // rows_per_cond, 0, 0))


def _ffn_weight_specs():
    return [pl.BlockSpec((D_MODEL, D_FF), lambda i: (0, 0), pipeline_mode=pl.Buffered(1)),
            pl.BlockSpec((D_MODEL, D_FF), lambda i: (0, 1), pipeline_mode=pl.Buffered(1)),
            _resident((D_FF, D_MODEL))]


def _ffn1(x, mod, gain, w_in, w_out, rows_per_cond):
    t = x.shape[0]
    tm = TOKEN_TILE
    row = pl.BlockSpec((tm, D_MODEL), lambda i: (i, 0))
    return pl.pallas_call(
        _ffn1_kernel,
        grid=(t // tm,),
        in_specs=[row, _mod_spec(rows_per_cond, tm), _resident((1, D_MODEL))] + _ffn_weight_specs(),
        out_specs=row,
        out_shape=jax.ShapeDtypeStruct((t, D_MODEL), F32),
        compiler_params=_cparams(("arbitrary",)),
        name="ffn1",
    )(x, mod, gain, w_in, w_in, w_out)


_IN_SEGS = (("qkv", 0, 3 * A_W), ("z", 3 * A_W, A_W), ("bq", 4 * A_W, B_W), ("bk", 4 * A_W + B_W, B_W),
            ("bv", 4 * A_W + 2 * B_W, B_W), ("dec", 4 * A_W + 3 * B_W, LANES),
            ("bet", 4 * A_W + 3 * B_W + LANES, LANES))


HALO = 8
CONV_SLAB = 256


def _inproj_kernel(*refs, tm, seq_len):
    x_ref, xp_ref, xn_ref, mod_ref, gain_ref, w_ref, cw_ref, gpar_ref = refs[:8]
    qkv_ref, z_ref, bq_ref, bk_ref, bv_ref, gcs_ref, beta_ref = refs[8:15]
    xpad = refs[-1]
    shift, scale = mod_ref[3:4, :], mod_ref[4:5, :]
    hf = _mod_norm(x_ref[...], gain_ref[...], shift, scale)
    h = hf.astype(BF16)
    segs = dict((name, (start, width)) for name, start, width in _IN_SEGS)

    def proj(name):
        start, width = segs[name]
        return _dot(h, w_ref[:, start:start + width])

    seg = min(tm, seq_len)
    n_seg = tm // seg
    span = seg + 2 * HALO
    qkv_w = 3 * A_W
    units = [(s, c) for s in range(n_seg) for c in range(qkv_w // CONV_SLAB)]
    if seq_len > tm:
        tiles = seq_len // tm
        pos = pl.program_id(0) % tiles
        h_prev = _mod_norm(xp_ref[...], gain_ref[...], shift, scale) * jnp.where(pos > 0, 1.0, 0.0)
        h_next = _mod_norm(xn_ref[...], gain_ref[...], shift, scale) * jnp.where(pos < tiles - 1, 1.0, 0.0)
        h_ext = jnp.concatenate([h_prev, hf, h_next], axis=0).astype(BF16)

        def project(s, c):
            cs = slice(c * CONV_SLAB, (c + 1) * CONV_SLAB)
            xpad[:, cs] = _dot(h_ext, w_ref[:, cs])
    else:
        for s in range(n_seg):
            xpad[s * span:s * span + HALO, :] = jnp.zeros((HALO, qkv_w), F32)
            xpad[s * span + HALO + seg:(s + 1) * span, :] = jnp.zeros((HALO, qkv_w), F32)

        def project(s, c):
            cs = slice(c * CONV_SLAB, (c + 1) * CONV_SLAB)
            xpad[s * span + HALO:s * span + HALO + seg, cs] = _dot(h[s * seg:(s + 1) * seg], w_ref[:, cs])

    def convolve(s, c):
        cs = slice(c * CONV_SLAB, (c + 1) * CONV_SLAB)
        acc = None
        for tap in range(CONV_K):
            lo = s * span + HALO + tap - CONV_K // 2
            term = xpad[lo:lo + seg, cs] * cw_ref[tap:tap + 1, cs]
            acc = term if acc is None else acc + term
        qkv_ref[s * seg:(s + 1) * seg, cs] = _silu(acc)

    z_ref[...] = proj("z")
    bq_ref[...] = proj("bq")
    bk_ref[...] = proj("bk")
    bv_ref[...] = proj("bv")
    for unit in units:
        project(*unit)
        convolve(*unit)

    gpar = gpar_ref[...]
    xg = proj("dec") + gpar[1:2, :]
    softplus = jnp.maximum(xg, 0.0) + jnp.log1p(jnp.exp(-jnp.abs(xg)))
    lane = lax.broadcasted_iota(jnp.int32, (tm, LANES), 1)
    g = jnp.where(lane < 2 * H_A, -jnp.exp(gpar[0:1, :]) * softplus, 0.0)
    beta_ref[...] = jax.nn.sigmoid(proj("bet"))
    pos_c = lax.broadcasted_iota(jnp.int32, (tm, 1), 0) & (CHUNK - 1)
    gcf = g
    gcb = g
    for sft in (1, 2, 4, 8, 16, 32):
        gcf = gcf + jnp.where(pos_c >= sft, pltpu.roll(gcf, sft, axis=0), 0.0)
        gcb = gcb + jnp.where(pos_c < CHUNK - sft, pltpu.roll(gcb, tm - sft, axis=0), 0.0)
    gcs_ref[...] = jnp.where(lane < H_A, gcf, gcb)


def _inproj(x, mod, gain, w_pack, conv_w, gpar, rows_per_cond, seq_len):
    t = x.shape[0]
    tm = TOKEN_TILE
    hb = tm // HALO
    row = pl.BlockSpec((tm, D_MODEL), lambda i: (i, 0))
    prev = pl.BlockSpec((HALO, D_MODEL), lambda i: (jnp.maximum(i * hb - 1, 0), 0))
    nxt = pl.BlockSpec((HALO, D_MODEL), lambda i: (jnp.minimum((i + 1) * hb, t // HALO - 1), 0))
    widths = [w for _, _, w in _IN_SEGS]
    out_specs = [pl.BlockSpec((tm, w), lambda i: (i, 0)) for w in widths]
    out_shape = [jax.ShapeDtypeStruct((t, w), F32) for w in widths]
    return pl.pallas_call(
        functools.partial(_inproj_kernel, tm=tm, seq_len=seq_len),
        grid=(t // tm,),
        in_specs=[row, prev, nxt, _mod_spec(rows_per_cond, tm), _resident((1, D_MODEL)),
                  _resident((D_MODEL, IN_PACK_COLS)), _resident((CONV_K, 3 * A_W)), _resident((8, LANES))],
        out_specs=out_specs,
        out_shape=out_shape,
        scratch_shapes=[pltpu.VMEM((tm + 2 * HALO * max(1, tm // seq_len), 3 * A_W), F32)],
        compiler_params=_cparams(("arbitrary",)),
        name="inproj",
    )(x, x, x, mod, gain, w_pack, conv_w, gpar)


def _post_kernel(*refs, emit_maps):
    x_ref, oa_ref, ob_ref, mod_ref, gain_ref, gfin_ref, wmix_ref, wg_ref, wu_ref, wo_ref = refs[:10]
    if emit_maps:
        bk_ref, bv_ref, o_ref, bk8_ref, bv4_ref = refs[10:]
        for r in range(2 * H_B):
            bk8_ref[:, r, :] = bk_ref[:, r * DQK_B:(r + 1) * DQK_B]
        for r in range(H_B):
            bv4_ref[:, r, :] = bv_ref[:, r * DV_B:(r + 1) * DV_B]
    else:
        o_ref = refs[10]
    y = _dot(oa_ref[...].astype(BF16), wmix_ref[0:A_W, :]) + _dot(ob_ref[...].astype(BF16), wmix_ref[A_W:, :])
    x = x_ref[...] + mod_ref[5:6, :] * y
    x = _swiglu_update(x, mod_ref, 6, gain_ref[...], wg_ref, wu_ref, wo_ref)
    o_ref[...] = (x * lax.rsqrt(jnp.mean(x * x, axis=-1, keepdims=True) + EPS)) * gfin_ref[...]


def _post(x, oa, ob, mod, gain, gfin, w_mix, w_in, w_out, rows_per_cond, kv=None):
    t = x.shape[0]
    tm = TOKEN_TILE
    row = pl.BlockSpec((tm, D_MODEL), lambda i: (i, 0))
    half = pl.BlockSpec((tm, A_W), lambda i: (i, 0))
    in_specs = [row, half, half, _mod_spec(rows_per_cond, tm), _resident((1, D_MODEL)),
                _resident((1, D_MODEL)), _resident((A_W + B_W, D_MODEL))] + _ffn_weight_specs()
    args = [x, oa, ob, mod, gain, gfin, w_mix, w_in, w_in, w_out]
    out_specs = [row]
    out_shape = [jax.ShapeDtypeStruct((t, D_MODEL), F32)]
    if kv is not None:
        in_specs += [half, half]
        args += list(kv)
        out_specs += [pl.BlockSpec((tm, 2 * H_B, DQK_B), lambda i: (i, 0, 0)),
                      pl.BlockSpec((tm, H_B, DV_B), lambda i: (i, 0, 0))]
        out_shape += [jax.ShapeDtypeStruct((t, 2 * H_B, DQK_B), F32), jax.ShapeDtypeStruct((t, H_B, DV_B), F32)]
    return pl.pallas_call(
        functools.partial(_post_kernel, emit_maps=kv is not None),
        grid=(t // tm,),
        in_specs=in_specs,
        out_specs=out_specs,
        out_shape=out_shape,
        compiler_params=_cparams(("arbitrary",)),
        name="post",
    )(*args)


def _rope(x, cos, sin_signed):
    w = x.shape[-1]
    lane = lax.broadcasted_iota(jnp.int32, x.shape, 1)
    partner = jnp.where((lane & 31) < 16, pltpu.roll(x, w - 16, axis=1), pltpu.roll(x, 16, axis=1))
    return x * cos + partner * sin_signed


def _attn_core(q, k_ref, v_ref, lam, gain_ref, o_ref):
    tq = q.shape[0]
    lane = lax.broadcasted_iota(jnp.int32, (tq, DV_B), 1)
    qs = q * (DQK_B ** -0.5)
    heads = [slice(h * DV_B, (h + 1) * DV_B) for h in range(H_B)]
    scores = []
    for hs in heads:
        qh = qs[:, hs]
        q2 = jnp.concatenate([jnp.where(lane < DQK_B, qh, 0.0), jnp.where(lane >= DQK_B, qh, 0.0)], axis=0)
        scores.append(lax.dot_general(q2.astype(BF16), k_ref[:, hs], (((1,), (1,)), ((), ())),
                                      preferred_element_type=F32))
    for hs, s in zip(heads, scores):
        e = jnp.exp(s - jnp.max(s, axis=-1, keepdims=True))
        r = _dot(e.astype(BF16), v_ref[:, hs]) / jnp.sum(e, axis=-1, keepdims=True)
        o = r[:tq] - lam * r[tq:]
        o = o * lax.rsqrt(jnp.mean(o * o, axis=-1, keepdims=True) + EPS)
        o_ref[:, hs] = (o * gain_ref[...]) * (1.0 - LAM_INIT)


def _lambda(lam_ref):
    l = lam_ref[...]
    s1 = jnp.sum(l[0:1] * l[1:2], axis=-1, keepdims=True)
    s2 = jnp.sum(l[2:3] * l[3:4], axis=-1, keepdims=True)
    return jnp.exp(s1) - jnp.exp(s2) + LAM_INIT


CTX_SEQS = 4


def _attn_ctx_kernel(q_ref, k_ref, v_ref, lam_ref, gain_ref, o_ref, k_s, v_s):
    lam = _lambda(lam_ref)

    def one_sequence(bi, carry):
        k_s[...] = k_ref[bi].astype(BF16)
        v_s[...] = v_ref[bi].astype(BF16)
        _attn_core(q_ref[bi], k_s, v_s, lam, gain_ref, o_ref.at[bi])
        return carry

    lax.fori_loop(0, CTX_SEQS, one_sequence, 0)


def _attn_ctx(q, k, v, lam_pack, gain):
    b, l, _ = q.shape
    blk = pl.BlockSpec((CTX_SEQS, l, B_W), lambda i: (i, 0, 0))
    return pl.pallas_call(
        _attn_ctx_kernel,
        grid=(b // CTX_SEQS,),
        in_specs=[blk, blk, blk, _resident((8, LANES)), _resident((1, DV_B))],
        out_specs=blk,
        out_shape=jax.ShapeDtypeStruct((b, l, B_W), F32),
        scratch_shapes=[pltpu.VMEM((l, B_W), BF16), pltpu.VMEM((l, B_W), BF16)],
        compiler_params=_cparams(("arbitrary",)),
        name="attn_ctx",
    )(q, k, v, lam_pack, gain)


def _attn_lat_kernel(q_ref, k_ref, v_ref, ck_ref, cv_ref, cosq_ref, sinq_ref, cos_ref, sin_ref,
                     lam_ref, gain_ref, o_ref, k_s, v_s, *, past):
    @pl.when(pl.program_id(1) == 0)
    def _():
        k_s[0:past, :] = ck_ref[...].astype(BF16)
        v_s[0:past, :] = cv_ref[...].astype(BF16)
        k_s[past:, :] = _rope(k_ref[...], cos_ref[...], sin_ref[...]).astype(BF16)
        v_s[past:, :] = v_ref[...].astype(BF16)

    q = _rope(q_ref[...], cosq_ref[...], sinq_ref[...])
    _attn_core(q, k_s, v_s, _lambda(lam_ref), gain_ref, o_ref)


def _attn_lat(q, k, v, ck, cv, cos, sin, lam_pack, gain):
    b, l, _ = q.shape
    past = ck.shape[1]
    tq = Q_TILE
    full = pl.BlockSpec((None, l, B_W), lambda i, j: (i, 0, 0))
    cache = pl.BlockSpec((None, past, B_W), lambda i, j: (i, 0, 0))
    qblk = pl.BlockSpec((None, tq, B_W), lambda i, j: (i, j, 0))
    tab_q = pl.BlockSpec((tq, B_W), lambda i, j: (j, 0))
    tab = pl.BlockSpec((l, B_W), lambda i, j: (0, 0))
    return pl.pallas_call(
        functools.partial(_attn_lat_kernel, past=past),
        grid=(b, l // tq),
        in_specs=[qblk, full, full, cache, cache, tab_q, tab_q, tab, tab,
                  pl.BlockSpec((8, LANES), lambda i, j: (0, 0)), pl.BlockSpec((1, DV_B), lambda i, j: (0, 0))],
        out_specs=qblk,
        out_shape=jax.ShapeDtypeStruct((b, l, B_W), F32),
        scratch_shapes=[pltpu.VMEM((past + l, B_W), BF16), pltpu.VMEM((past + l, B_W), BF16)],
        compiler_params=_cparams(("arbitrary", "arbitrary")),
        name="attn_lat",
    )(q, k, v, ck, cv, cos, sin, cos, sin, lam_pack, gain)


def _rope_tables(length):
    pairs = DQK_B // 4
    n_rows = length // GRID_W
    pos_row = jnp.repeat(jnp.arange(n_rows), GRID_W).astype(F32)
    pos_col = jnp.tile(jnp.arange(GRID_W), n_rows).astype(F32)
    inv = ROPE_THETA ** (-jnp.arange(pairs, dtype=F32) / pairs)
    ang_r = pos_row[:, None] * inv
    ang_c = pos_col[:, None] * inv
    cos = jnp.concatenate([jnp.cos(ang_r), jnp.cos(ang_r), jnp.cos(ang_c), jnp.cos(ang_c)], axis=-1)
    sin = jnp.concatenate([-jnp.sin(ang_r), jnp.sin(ang_r), -jnp.sin(ang_c), jnp.sin(ang_c)], axis=-1)
    reps = B_W // DQK_B
    return jnp.tile(cos, (1, reps)), jnp.tile(sin, (1, reps))


def _block_diag(x):
    lo, hi = x[:, :LANES], x[:, LANES:]
    first = lax.broadcasted_iota(jnp.int32, lo.shape, 1) < DK_A
    top = jnp.concatenate([jnp.where(first, lo, 0), jnp.where(first, 0, lo)], axis=0)
    bot = jnp.concatenate([jnp.where(first, hi, 0), jnp.where(first, 0, hi)], axis=0)
    zero = jnp.zeros_like(top)
    return jnp.concatenate([jnp.concatenate([top, zero], axis=1), jnp.concatenate([zero, bot], axis=1)], axis=0)


SOLVE_CHUNKS = 8
DELTA_ROWS = 1024


def _delta_kernel(*refs, nb, length, has_state):
    if has_state:
        (q_ref, k_ref, v_ref, z_ref, gcs_ref, beta_ref, dn_ref, s0_ref, o_ref, st_ref, *scr) = refs
    else:
        (q_ref, k_ref, v_ref, z_ref, gcs_ref, beta_ref, dn_ref, o_ref, st_ref, *scr) = refs
        s0_ref = None
    qs, ks, vs, gexp, bexp, u0_s, kg_s, w_s, qg_s, qkm_s, of_s, ob_s, s_s = scr
    L = length
    R = nb * L
    n_chunks = L // CHUNK
    grp = pl.program_id(1)
    W = GROUP_W

    r256 = lax.broadcasted_iota(jnp.int32, (W, W), 0)
    c256 = lax.broadcasted_iota(jnp.int32, (W, W), 1)
    bd_mask = (r256 >> 6) == (c256 >> 6)
    ones_bd = jnp.where(bd_mask, 1.0, 0.0).astype(BF16)
    ri = lax.broadcasted_iota(jnp.int32, (CHUNK, W), 0)
    cj = lax.broadcasted_iota(jnp.int32, (CHUNK, W), 1) & (CHUNK - 1)
    eye_p = ri == cj
    er = lax.broadcasted_iota(jnp.int32, (LANES, 2 * W), 0)
    ec = lax.broadcasted_iota(jnp.int32, (LANES, 2 * W), 1)
    expand = jnp.where(er == (ec >> 8) * H_A + grp * 4 + ((ec & (W - 1)) >> 6), 1.0, 0.0).astype(BF16)

    bd = _block_diag

    def prologue(bi, carry):
        rows_b = pl.ds(pl.multiple_of(bi * L, L), L)

        def l2n(y):
            return y * lax.rsqrt(_dot_split(y * y, ones_bd) + EPS)

        qs[rows_b, :] = l2n(q_ref[bi]) * (DK_A ** -0.5)
        ks[rows_b, :] = l2n(k_ref[bi])
        vs[rows_b, :] = v_ref[bi]
        ge_all = _dot_split(gcs_ref[bi], expand)
        be_all = _dot_split(beta_ref[bi], expand)
        for d in range(2):
            gexp[d, rows_b, :] = ge_all[:, d * W:(d + 1) * W]
            bexp[d, rows_b, :] = be_all[:, d * W:(d + 1) * W]
        return carry

    lax.fori_loop(0, nb, prologue, 0)

    def solve_body(j, carry):
        chunks = []
        for k in range(SOLVE_CHUNKS):
            rows = pl.ds(pl.multiple_of((j * SOLVE_CHUNKS + k) * CHUNK, CHUNK), CHUNK)
            kc = ks[rows, :]
            qc = qs[rows, :]
            kb = kc.astype(BF16)
            res = lax.dot_general(jnp.concatenate([qc.astype(BF16), kb], axis=0), bd(kb),
                                  (((1,), (1,)), ((), ())), preferred_element_type=F32)
            qk, kk = res[:CHUNK], res[CHUNK:]
            ch = dict(rows=rows, bd_k=bd(kb), bd_v=bd(vs[rows, :].astype(BF16)), t=[], scale_u=[], scale_w=[])
            nsum = None
            for d in range(2):
                ge = gexp[d, rows, :]
                be = bexp[d, rows, :]
                if d == 0:
                    incl, strict, gl = ri >= cj, ri > cj, ge[CHUNK - 1:CHUNK, :]
                else:
                    incl, strict, gl = ri <= cj, ri < cj, ge[0:1, :]
                g_row = jnp.sum(jnp.where(eye_p, ge, 0.0), axis=0, keepdims=True)
                b_row = jnp.sum(jnp.where(eye_p, be, 0.0), axis=0, keepdims=True)
                dec_m = jnp.exp(jnp.where(incl, ge - g_row, -jnp.inf))
                nmat = jnp.where(strict, be * kk * dec_m, 0.0)
                nsum = nmat if nsum is None else nsum + nmat
                qkm_s[d, rows, :] = jnp.where(incl, qk * dec_m, 0.0).astype(BF16)
                qg_s[d, rows, :] = (qc * jnp.exp(ge)).astype(BF16)
                kg_s[d, rows, :] = kc * jnp.exp(gl - ge)
                ch["scale_u"].append(b_row)
                ch["scale_w"].append(b_row * jnp.exp(g_row))
                ch["t"].append(jnp.where(eye_p, 1.0, 0.0) - jnp.where((ri >> 1) == (cj >> 1), nmat, 0.0))
            ch["n_both"] = nsum.astype(BF16)
            chunks.append(ch)

        for lb in range(1, 6):
            same = (ri >> (lb + 1)) == (cj >> (lb + 1))
            pair = (same & (((ri >> lb) & 1) == 1) & (((cj >> lb) & 1) == 0),
                    same & (((cj >> lb) & 1) == 1) & (((ri >> lb) & 1) == 0))
            for ch in chunks:
                ch["tb"] = [t.astype(BF16) for t in ch["t"]]
                rhs = bd(jnp.where(pair[0] | pair[1], ch["n_both"], 0))
                ch["y"] = _dot(jnp.concatenate(ch["tb"], axis=0), rhs).astype(BF16)
            for ch in chunks:
                ys = (jnp.where(pair[0], ch["y"][:CHUNK], 0), jnp.where(pair[1], ch["y"][CHUNK:], 0))
                ch["t"] = [ch["t"][d] - _dot(ys[d], bd(ch["tb"][d])) for d in range(2)]
        for ch in chunks:
            tu = jnp.concatenate([(ch["t"][d] * ch["scale_u"][d]).astype(BF16) for d in range(2)], axis=0)
            tw = jnp.concatenate([(ch["t"][d] * ch["scale_w"][d]).astype(BF16) for d in range(2)], axis=0)
            u0 = _dot(tu, ch["bd_v"])
            w = _dot(tw, ch["bd_k"]).astype(BF16)
            for d in range(2):
                u0_s[d, ch["rows"], :] = u0[d * CHUNK:(d + 1) * CHUNK]
                w_s[d, ch["rows"], :] = w[d * CHUNK:(d + 1) * CHUNK]
        return carry

    lax.fori_loop(0, R // (SOLVE_CHUNKS * CHUNK), solve_body, 0)

    s_s[...] = jnp.zeros_like(s_s)
    if has_state:
        for bi in range(nb):
            for d in range(2):
                for h in range(4):
                    s_s[bi, d, h * DK_A:(h + 1) * DK_A, h * DK_A:(h + 1) * DK_A] = s0_ref[bi, d, h]

    def scan_body(i, carry):
        chains = []
        for bi in range(nb):
            for d in range(2):
                c = i if d == 0 else n_chunks - 1 - i
                r0 = pl.multiple_of(bi * L + c * CHUNK, CHUNK)
                rows = pl.ds(r0, CHUNK)
                if d == 0:
                    g_last = gexp[d, pl.ds(r0 + CHUNK - 8, 8), :][7:8]
                else:
                    g_last = gexp[d, pl.ds(r0, 8), :][0:1]
                chains.append(dict(
                    bi=bi, d=d, rows=rows, s=s_s[bi, d], u0=u0_s[d, rows, :], kg=kg_s[d, rows, :],
                    wq=jnp.concatenate([w_s[d, rows, :], qg_s[d, rows, :]], axis=0), qkm=qkm_s[d, rows, :],
                    decay=jnp.exp(g_last)))
        for c in chains:
            c["ws"] = _dot(c["wq"], c["s"].astype(BF16))
        for c in chains:
            c["ub"] = (c["u0"] - c["ws"][:CHUNK]).astype(BF16)
        for c in chains:
            c["o"] = c["ws"][CHUNK:] + _dot(c["qkm"], bd(c["ub"]))
            upd = _dot(c["kg"].T.astype(BF16), c["ub"])
            c["s_new"] = c["s"] * c["decay"] + jnp.where(bd_mask, upd, 0.0)
        for c in chains:
            s_s[c["bi"], c["d"]] = c["s_new"]
            if c["d"] == 0:
                of_s[c["rows"], :] = c["o"]
            else:
                ob_s[c["rows"], :] = c["o"]
        return carry

    lax.fori_loop(0, n_chunks, scan_body, 0)

    def epilogue(bi, carry):
        rows_b = pl.ds(pl.multiple_of(bi * L, L), L)
        o = of_s[rows_b, :] + ob_s[rows_b, :]
        ms = _dot_split(o * o, ones_bd) * (1.0 / DK_A)
        o = o * lax.rsqrt(ms + EPS) * dn_ref[...]
        o_ref[bi] = o * _silu(z_ref[bi])
        return carry

    lax.fori_loop(0, nb, epilogue, 0)
    for bi in range(nb):
        for d in range(2):
            for h in range(4):
                st_ref[bi, d, h] = s_s[bi, d, h * DK_A:(h + 1) * DK_A, h * DK_A:(h + 1) * DK_A]


def _delta(qkv, z, gcs, beta, dnorm, s0):
    b, l, _ = qkv.shape
    has_state = s0 is not None
    w = GROUP_W
    n_grp = A_W // w
    nb = DELTA_ROWS // l
    r = nb * l

    def slab(k):
        return pl.BlockSpec((nb, l, w), lambda i, g: (i, 0, k * n_grp + g))

    narrow = pl.BlockSpec((nb, l, LANES), lambda i, g: (i, 0, 0))
    st_spec = pl.BlockSpec((nb, 2, 4, DK_A, DK_A), lambda i, g: (i, 0, g, 0, 0))
    in_specs = [slab(0), slab(1), slab(2),
                pl.BlockSpec((nb, l, w), lambda i, g: (i, 0, g)),
                narrow, narrow,
                pl.BlockSpec((1, w), lambda i, g: (0, 0))]
    args = [qkv, qkv, qkv, z, gcs, beta, dnorm]
    if has_state:
        in_specs.append(st_spec)
        args.append(s0)
    scratch = ([pltpu.VMEM((r, w), F32)] * 3 + [pltpu.VMEM((2, r, w), F32)] * 4
               + [pltpu.VMEM((2, r, w), BF16)] * 3 + [pltpu.VMEM((r, w), F32)] * 2
               + [pltpu.VMEM((nb, 2, w, w), F32)])
    return pl.pallas_call(
        functools.partial(_delta_kernel, nb=nb, length=l, has_state=has_state),
        grid=(b // nb, n_grp),
        in_specs=in_specs,
        out_specs=[pl.BlockSpec((nb, l, w), lambda i, g: (i, 0, g)), st_spec],
        out_shape=[jax.ShapeDtypeStruct((b, l, A_W), F32),
                   jax.ShapeDtypeStruct((b, 2, H_A, DK_A, DK_A), F32)],
        scratch_shapes=scratch,
        compiler_params=_cparams(("arbitrary", "arbitrary")),
        name="delta_state" if has_state else "delta",
    )(*args)


def _pad_lanes(x, width):
    return jnp.pad(x, [(0, 0)] * (x.ndim - 1) + [(0, width - x.shape[-1])])


def kernel(x_prompt, x_sample, cache_diff_k, cache_diff_v, state_delta, c, c_ctx, w_ada, b_ada, norm_ffn1,
           w_ffn1_in, w_ffn1_out, norm_mix, w_in, conv_w, a_log, dt_bias, delta_norm, lambda_q1, lambda_k1,
           lambda_q2, lambda_k2, diff_norm, w_out, norm_ffn2, w_ffn2_in, w_ffn2_out, norm_final):
    bp, lp, _ = x_prompt.shape
    bs, ls, _ = x_sample.shape
    past = cache_diff_k.shape[2]

    w1_in = w_ffn1_in[0].astype(BF16)
    w1_out = w_ffn1_out[0].astype(BF16)
    w2_in = w_ffn2_in[0].astype(BF16)
    w2_out = w_ffn2_out[0].astype(BF16)
    w_mix_out = w_out[0].astype(BF16)
    wi = w_in[0]
    a_end = 4 * A_W
    dec_w = wi[:, a_end:a_end + 2 * H_A]
    bet_w = wi[:, a_end + 2 * H_A:a_end + 4 * H_A]
    rest = wi[:, a_end + 4 * H_A:]
    w_pack = jnp.concatenate([wi[:, :a_end], rest, _pad_lanes(dec_w, LANES), _pad_lanes(bet_w, LANES)],
                             axis=1).astype(BF16)
    g1 = norm_ffn1[0].reshape(1, D_MODEL)
    gm = norm_mix[0].reshape(1, D_MODEL)
    g2 = norm_ffn2[0].reshape(1, D_MODEL)
    gf = norm_final.reshape(1, D_MODEL)
    gpar = jnp.zeros((8, LANES), F32)
    gpar = gpar.at[0, :2 * H_A].set(a_log[0].reshape(-1)).at[1, :2 * H_A].set(dt_bias[0].reshape(-1))
    dnorm = jnp.tile(delta_norm[0], 4).reshape(1, GROUP_W)
    lam_pack = jnp.zeros((8, LANES), F32)
    for r, v in enumerate((lambda_q1, lambda_k1, lambda_q2, lambda_k2)):
        lam_pack = lam_pack.at[r, :DQK_B].set(v[0])
    dgain = diff_norm[0].reshape(1, DV_B)
    cos, sin = _rope_tables(ls)

    cond = jnp.concatenate([c_ctx[None, :], c, jnp.zeros((8 - 1 - bs, D_MODEL), F32)], axis=0)
    mod = _adaln(cond, w_ada[0], b_ada[0]).reshape(8, N_MOD, D_MODEL)

    def trunk(x3, rows_per_cond, ctx):
        b, l, _ = x3.shape
        x = x3.reshape(b * l, D_MODEL)
        x1 = _ffn1(x, mod, g1, w1_in, w1_out, rows_per_cond)
        qkv, z, bq, bk, bv, gcs, beta = _inproj(x1, mod, gm, w_pack, conv_w[0], gpar, rows_per_cond, l)
        r3 = lambda a: a.reshape(b, l, a.shape[-1])
        s0 = None if ctx is None else ctx[2]
        oa, st = _delta(r3(qkv), r3(z), r3(gcs), r3(beta), dnorm, s0)
        if ctx is None:
            ob = _attn_ctx(r3(bq), r3(bk), r3(bv), lam_pack, dgain)
        else:
            ob = _attn_lat(r3(bq), r3(bk), r3(bv), ctx[0], ctx[1], cos, sin, lam_pack, dgain)
        outs = _post(x1, oa.reshape(b * l, A_W), ob.reshape(b * l, B_W), mod, g2, gf, w_mix_out, w2_in, w2_out,
                     rows_per_cond, kv=(bk, bv) if ctx is None else None)
        return outs[0].reshape(b, l, D_MODEL), outs[1:], st

    y_prompt, (k_maps, v_heads), s_ctx = trunk(x_prompt, None, None)
    ctx = (cache_diff_k[:, 0].reshape(bs, past, B_W), cache_diff_v[:, 0].reshape(bs, past, B_W), state_delta[:, 0])
    y_sample, _, _ = trunk(x_sample, ls, ctx)
    new_k = k_maps.reshape(bp, 1, lp, H_B, 2, DQK_B)
    new_v = v_heads.reshape(bp, 1, lp, H_B, DV_B)
    return (y_prompt, y_sample, new_k, new_v, s_ctx[:, None])
```

```python
import functools
import math

import jax
import jax.numpy as jnp
from jax import lax
from jax.experimental import pallas as pl
from jax.experimental.pallas import tpu as pltpu

F32 = jnp.float32
BF16 = jnp.bfloat16

D_MODEL = 1024
D_FF = 2816
N_MOD = 9
H_A = 8
DK_A = 64
A_W = H_A * DK_A
H_B = 4
DQK_B = 64
DV_B = 128
B_W = H_B * DV_B
CONV_K = 5
CHUNK = 64
GRID_W = 64
ROPE_THETA = 10000.0
EPS = 1e-6
LAM_INIT = 0.8 - 0.6 * math.exp(-0.3 * 0)

LANES = 128
GROUP_W = 4 * DK_A
IN_PACK_COLS = 3 * A_W + A_W + 3 * B_W + 2 * LANES
VMEM_LIMIT = 56 * 1024 * 1024

MXU_W = 256
TOKEN_TILE = 512
FF_CHUNKS = ((0, 6 * MXU_W), (6 * MXU_W, D_FF))
Q_TILE = 256


def _cparams(sem):
    return pltpu.CompilerParams(dimension_semantics=sem, vmem_limit_bytes=VMEM_LIMIT)


def _resident(shape):
    nd = len(shape)
    return pl.BlockSpec(shape, lambda *_: (0,) * nd, pipeline_mode=pl.Buffered(1))


def _silu(x):
    half = 0.5 * x
    return half * jnp.tanh(half) + half


def _mod_norm(x, gain, shift, scale):
    y = x * lax.rsqrt(jnp.mean(x * x, axis=-1, keepdims=True) + EPS)
    return (y * gain) * (1.0 + scale) + shift


def _dot(a, b):
    return jnp.dot(a, b, preferred_element_type=F32)


def _dot_split(a, b_bf16):
    hi = a.astype(BF16)
    lo = (a - hi.astype(F32)).astype(BF16)
    return _dot(hi, b_bf16) + _dot(lo, b_bf16)


def _adaln_kernel(c_ref, w_ref, b_ref, o_ref):
    s = _silu(c_ref[...])
    o_ref[...] = _dot(s.astype(BF16), w_ref[...].astype(BF16)) + b_ref[...]


def _adaln(cond, w_ada, b_ada):
    n = N_MOD * D_MODEL
    tn = n // 4
    return pl.pallas_call(
        _adaln_kernel,
        grid=(n // tn,),
        in_specs=[pl.BlockSpec((8, D_MODEL), lambda j: (0, 0)),
                  pl.BlockSpec((D_MODEL, tn), lambda j: (0, j)),
                  pl.BlockSpec((1, tn), lambda j: (0, j))],
        out_specs=pl.BlockSpec((8, tn), lambda j: (0, j)),
        out_shape=jax.ShapeDtypeStruct((8, n), F32),
        compiler_params=_cparams(("arbitrary",)),
        name="adaln",
    )(cond, w_ada, b_ada.reshape(1, n))


def _swiglu_update(x, mod_ref, k0, gain, wg_ref, wu_ref, wo_ref):
    h = _mod_norm(x, gain, mod_ref[k0:k0 + 1, :], mod_ref[k0 + 1:k0 + 2, :]).astype(BF16)
    acc = None
    for lo, hi in FF_CHUNKS:
        sl = slice(lo, hi)
        g = _dot(h, wg_ref[:, sl])
        u = _dot(h, wu_ref[:, sl])
        part = _dot((_silu(g) * u).astype(BF16), wo_ref[sl, :])
        acc = part if acc is None else acc + part
    return x + (0.5 * mod_ref[k0 + 2:k0 + 3, :]) * acc


def _ffn1_kernel(x_ref, mod_ref, gain_ref, wg_ref, wu_ref, wo_ref, o_ref):
    o_ref[...] = _swiglu_update(x_ref[...], mod_ref, 0, gain_ref[...], wg_ref, wu_ref, wo_ref)


def _mod_spec(rows_per_cond, tm):
    if rows_per_cond is None:
        return pl.BlockSpec((None, N_MOD, D_MODEL), lambda i: (0, 0, 0))
    return pl.BlockSpec((None, N_MOD, D_MODEL), lambda i: (1 + (i * tm) // rows_per_cond, 0, 0))


def _ffn_weight_specs():
    return [pl.BlockSpec((D_MODEL, D_FF), lambda i: (0, 0), pipeline_mode=pl.Buffered(1)),
            pl.BlockSpec((D_MODEL, D_FF), lambda i: (0, 1), pipeline_mode=pl.Buffered(1)),
            _resident((D_FF, D_MODEL))]


def _ffn1(x, mod, gain, w_in, w_out, rows_per_cond):
    t = x.shape[0]
    tm = TOKEN_TILE
    row = pl.BlockSpec((tm, D_MODEL), lambda i: (i, 0))
    return pl.pallas_call(
        _ffn1_kernel,
        grid=(t // tm,),
        in_specs=[row, _mod_spec(rows_per_cond, tm), _resident((1, D_MODEL))] + _ffn_weight_specs(),
        out_specs=row,
        out_shape=jax.ShapeDtypeStruct((t, D_MODEL), F32),
        compiler_params=_cparams(("arbitrary",)),
        name="ffn1",
    )(x, mod, gain, w_in, w_in, w_out)


_IN_SEGS = (("qkv", 0, 3 * A_W), ("z", 3 * A_W, A_W), ("bq", 4 * A_W, B_W), ("bk", 4 * A_W + B_W, B_W),
            ("bv", 4 * A_W + 2 * B_W, B_W), ("dec", 4 * A_W + 3 * B_W, LANES),
            ("bet", 4 * A_W + 3 * B_W + LANES, LANES))


HALO = 8
CONV_SLAB = 256


def _inproj_kernel(*refs, tm, seq_len):
    x_ref, xp_ref, xn_ref, mod_ref, gain_ref, w_ref, cw_ref, gpar_ref = refs[:8]
    qkv_ref, z_ref, bq_ref, bk_ref, bv_ref, gcs_ref, beta_ref = refs[8:15]
    xpad = refs[-1]
    shift, scale = mod_ref[3:4, :], mod_ref[4:5, :]
    hf = _mod_norm(x_ref[...], gain_ref[...], shift, scale)
    h = hf.astype(BF16)
    segs = dict((name, (start, width)) for name, start, width in _IN_SEGS)

    def proj(name):
        start, width = segs[name]
        return _dot(h, w_ref[:, start:start + width])

    seg = min(tm, seq_len)
    n_seg = tm // seg
    span = seg + 2 * HALO
    qkv_w = 3 * A_W
    units = [(s, c) for s in range(n_seg) for c in range(qkv_w // CONV_SLAB)]
    if seq_len > tm:
        tiles = seq_len // tm
        pos = pl.program_id(0) % tiles
        h_prev = _mod_norm(xp_ref[...], gain_ref[...], shift, scale) * jnp.where(pos > 0, 1.0, 0.0)
        h_next = _mod_norm(xn_ref[...], gain_ref[...], shift, scale) * jnp.where(pos < tiles - 1, 1.0, 0.0)
        h_ext = jnp.concatenate([h_prev, hf, h_next], axis=0).astype(BF16)

        def project(s, c):
            cs = slice(c * CONV_SLAB, (c + 1) * CONV_SLAB)
            xpad[:, cs] = _dot(h_ext, w_ref[:, cs])
    else:
        for s in range(n_seg):
            xpad[s * span:s * span + HALO, :] = jnp.zeros((HALO, qkv_w), F32)
            xpad[s * span + HALO + seg:(s + 1) * span, :] = jnp.zeros((HALO, qkv_w), F32)

        def project(s, c):
            cs = slice(c * CONV_SLAB, (c + 1) * CONV_SLAB)
            xpad[s * span + HALO:s * span + HALO + seg, cs] = _dot(h[s * seg:(s + 1) * seg], w_ref[:, cs])

    def convolve(s, c):
        cs = slice(c * CONV_SLAB, (c + 1) * CONV_SLAB)
        acc = None
        for tap in range(CONV_K):
            lo = s * span + HALO + tap - CONV_K // 2
            term = xpad[lo:lo + seg, cs] * cw_ref[tap:tap + 1, cs]
            acc = term if acc is None else acc + term
        qkv_ref[s * seg:(s + 1) * seg, cs] = _silu(acc)

    z_ref[...] = proj("z")
    bq_ref[...] = proj("bq")
    bk_ref[...] = proj("bk")
    bv_ref[...] = proj("bv")
    for unit in units:
        project(*unit)
        convolve(*unit)

    gpar = gpar_ref[...]
    dec_start = segs["dec"][0]
    gate_pre = _dot(h, w_ref[:, dec_start:dec_start + 2 * LANES])
    xg = gate_pre[:, :LANES] + gpar[1:2, :]
    softplus = jnp.maximum(xg, 0.0) + jnp.log1p(jnp.exp(-jnp.abs(xg)))
    lane = lax.broadcasted_iota(jnp.int32, (tm, LANES), 1)
    g = jnp.where(lane < 2 * H_A, -jnp.exp(gpar[0:1, :]) * softplus, 0.0)
    beta_ref[...] = jax.nn.sigmoid(gate_pre[:, LANES:])
    pos_c = lax.broadcasted_iota(jnp.int32, (tm, 1), 0) & (CHUNK - 1)
    gcf = g
    gcb = g
    for sft in (1, 2, 4, 8, 16, 32):
        gcf = gcf + jnp.where(pos_c >= sft, pltpu.roll(gcf, sft, axis=0), 0.0)
        gcb = gcb + jnp.where(pos_c < CHUNK - sft, pltpu.roll(gcb, tm - sft, axis=0), 0.0)
    gcs_ref[...] = jnp.where(lane < H_A, gcf, gcb)


def _inproj(x, mod, gain, w_pack, conv_w, gpar, rows_per_cond, seq_len):
    t = x.shape[0]
    tm = TOKEN_TILE
    hb = tm // HALO
    row = pl.BlockSpec((tm, D_MODEL), lambda i: (i, 0))
    prev = pl.BlockSpec((HALO, D_MODEL), lambda i: (jnp.maximum(i * hb - 1, 0), 0))
    nxt = pl.BlockSpec((HALO, D_MODEL), lambda i: (jnp.minimum((i + 1) * hb, t // HALO - 1), 0))
    widths = [w for _, _, w in _IN_SEGS]
    out_specs = [pl.BlockSpec((tm, w), lambda i: (i, 0)) for w in widths]
    out_shape = [jax.ShapeDtypeStruct((t, w), F32) for w in widths]
    return pl.pallas_call(
        functools.partial(_inproj_kernel, tm=tm, seq_len=seq_len),
        grid=(t // tm,),
        in_specs=[row, prev, nxt, _mod_spec(rows_per_cond, tm), _resident((1, D_MODEL)),
                  _resident((D_MODEL, IN_PACK_COLS)), _resident((CONV_K, 3 * A_W)), _resident((8, LANES))],
        out_specs=out_specs,
        out_shape=out_shape,
        scratch_shapes=[pltpu.VMEM((tm + 2 * HALO * max(1, tm // seq_len), 3 * A_W), F32)],
        compiler_params=_cparams(("arbitrary",)),
        name="inproj",
    )(x, x, x, mod, gain, w_pack, conv_w, gpar)


def _post_kernel(*refs, emit_maps):
    x_ref, oa_ref, ob_ref, mod_ref, gain_ref, gfin_ref, wmix_ref, wg_ref, wu_ref, wo_ref = refs[:10]
    if emit_maps:
        bk_ref, bv_ref, o_ref, bk8_ref, bv4_ref = refs[10:]
        for r in range(2 * H_B):
            bk8_ref[:, r, :] = bk_ref[:, r * DQK_B:(r + 1) * DQK_B]
        for r in range(H_B):
            bv4_ref[:, r, :] = bv_ref[:, r * DV_B:(r + 1) * DV_B]
    else:
        o_ref = refs[10]
    y = _dot(oa_ref[...].astype(BF16), wmix_ref[0:A_W, :]) + _dot(ob_ref[...].astype(BF16), wmix_ref[A_W:, :])
    x = x_ref[...] + mod_ref[5:6, :] * y
    x = _swiglu_update(x, mod_ref, 6, gain_ref[...], wg_ref, wu_ref, wo_ref)
    o_ref[...] = (x * lax.rsqrt(jnp.mean(x * x, axis=-1, keepdims=True) + EPS)) * gfin_ref[...]


def _post(x, oa, ob, mod, gain, gfin, w_mix, w_in, w_out, rows_per_cond, kv=None):
    t = x.shape[0]
    tm = TOKEN_TILE
    row = pl.BlockSpec((tm, D_MODEL), lambda i: (i, 0))
    half = pl.BlockSpec((tm, A_W), lambda i: (i, 0))
    in_specs = [row, half, half, _mod_spec(rows_per_cond, tm), _resident((1, D_MODEL)),
                _resident((1, D_MODEL)), _resident((A_W + B_W, D_MODEL))] + _ffn_weight_specs()
    args = [x, oa, ob, mod, gain, gfin, w_mix, w_in, w_in, w_out]
    out_specs = [row]
    out_shape = [jax.ShapeDtypeStruct((t, D_MODEL), F32)]
    if kv is not None:
        in_specs += [half, half]
        args += list(kv)
        out_specs += [pl.BlockSpec((tm, 2 * H_B, DQK_B), lambda i: (i, 0, 0)),
                      pl.BlockSpec((tm, H_B, DV_B), lambda i: (i, 0, 0))]
        out_shape += [jax.ShapeDtypeStruct((t, 2 * H_B, DQK_B), F32), jax.ShapeDtypeStruct((t, H_B, DV_B), F32)]
    return pl.pallas_call(
        functools.partial(_post_kernel, emit_maps=kv is not None),
        grid=(t // tm,),
        in_specs=in_specs,
        out_specs=out_specs,
        out_shape=out_shape,
        compiler_params=_cparams(("arbitrary",)),
        name="post",
    )(*args)


def _rope(x, cos, sin_signed):
    w = x.shape[-1]
    lane = lax.broadcasted_iota(jnp.int32, x.shape, 1)
    partner = jnp.where((lane & 31) < 16, pltpu.roll(x, w - 16, axis=1), pltpu.roll(x, 16, axis=1))
    return x * cos + partner * sin_signed


def _attn_core(q, k_ref, v_ref, lam, gain_ref, o_ref):
    tq = q.shape[0]
    lane = lax.broadcasted_iota(jnp.int32, (tq, DV_B), 1)
    qs = q * (DQK_B ** -0.5)
    heads = [slice(h * DV_B, (h + 1) * DV_B) for h in range(H_B)]
    scores = []
    for hs in heads:
        qh = qs[:, hs]
        q2 = jnp.concatenate([jnp.where(lane < DQK_B, qh, 0.0), jnp.where(lane >= DQK_B, qh, 0.0)], axis=0)
        scores.append(lax.dot_general(q2.astype(BF16), k_ref[:, hs], (((1,), (1,)), ((), ())),
                                      preferred_element_type=F32))
    for hs, s in zip(heads, scores):
        e = jnp.exp(s - jnp.max(s, axis=-1, keepdims=True))
        r = _dot(e.astype(BF16), v_ref[:, hs]) / jnp.sum(e, axis=-1, keepdims=True)
        o = r[:tq] - lam * r[tq:]
        o = o * lax.rsqrt(jnp.mean(o * o, axis=-1, keepdims=True) + EPS)
        o_ref[:, hs] = (o * gain_ref[...]) * (1.0 - LAM_INIT)


def _lambda(lam_ref):
    l = lam_ref[...]
    s1 = jnp.sum(l[0:1] * l[1:2], axis=-1, keepdims=True)
    s2 = jnp.sum(l[2:3] * l[3:4], axis=-1, keepdims=True)
    return jnp.exp(s1) - jnp.exp(s2) + LAM_INIT


CTX_SEQS = 4


def _attn_ctx_kernel(q_ref, k_ref, v_ref, lam_ref, gain_ref, o_ref, k_s, v_s):
    lam = _lambda(lam_ref)

    def one_sequence(bi, carry):
        k_s[...] = k_ref[bi].astype(BF16)
        v_s[...] = v_ref[bi].astype(BF16)
        _attn_core(q_ref[bi], k_s, v_s, lam, gain_ref, o_ref.at[bi])
        return carry

    lax.fori_loop(0, CTX_SEQS, one_sequence, 0)


def _attn_ctx(q, k, v, lam_pack, gain):
    b, l, _ = q.shape
    blk = pl.BlockSpec((CTX_SEQS, l, B_W), lambda i: (i, 0, 0))
    return pl.pallas_call(
        _attn_ctx_kernel,
        grid=(b // CTX_SEQS,),
        in_specs=[blk, blk, blk, _resident((8, LANES)), _resident((1, DV_B))],
        out_specs=blk,
        out_shape=jax.ShapeDtypeStruct((b, l, B_W), F32),
        scratch_shapes=[pltpu.VMEM((l, B_W), BF16), pltpu.VMEM((l, B_W), BF16)],
        compiler_params=_cparams(("arbitrary",)),
        name="attn_ctx",
    )(q, k, v, lam_pack, gain)


def _attn_lat_kernel(q_ref, k_ref, v_ref, ck_ref, cv_ref, cosq_ref, sinq_ref, cos_ref, sin_ref,
                     lam_ref, gain_ref, o_ref, k_s, v_s, *, past):
    @pl.when(pl.program_id(1) == 0)
    def _():
        k_s[0:past, :] = ck_ref[...].astype(BF16)
        v_s[0:past, :] = cv_ref[...].astype(BF16)
        k_s[past:, :] = _rope(k_ref[...], cos_ref[...], sin_ref[...]).astype(BF16)
        v_s[past:, :] = v_ref[...].astype(BF16)

    q = _rope(q_ref[...], cosq_ref[...], sinq_ref[...])
    _attn_core(q, k_s, v_s, _lambda(lam_ref), gain_ref, o_ref)


def _attn_lat(q, k, v, ck, cv, cos, sin, lam_pack, gain):
    b, l, _ = q.shape
    past = ck.shape[1]
    tq = Q_TILE
    full = pl.BlockSpec((None, l, B_W), lambda i, j: (i, 0, 0))
    cache = pl.BlockSpec((None, past, B_W), lambda i, j: (i, 0, 0))
    qblk = pl.BlockSpec((None, tq, B_W), lambda i, j: (i, j, 0))
    tab_q = pl.BlockSpec((tq, B_W), lambda i, j: (j, 0))
    tab = pl.BlockSpec((l, B_W), lambda i, j: (0, 0))
    return pl.pallas_call(
        functools.partial(_attn_lat_kernel, past=past),
        grid=(b, l // tq),
        in_specs=[qblk, full, full, cache, cache, tab_q, tab_q, tab, tab,
                  pl.BlockSpec((8, LANES), lambda i, j: (0, 0)), pl.BlockSpec((1, DV_B), lambda i, j: (0, 0))],
        out_specs=qblk,
        out_shape=jax.ShapeDtypeStruct((b, l, B_W), F32),
        scratch_shapes=[pltpu.VMEM((past + l, B_W), BF16), pltpu.VMEM((past + l, B_W), BF16)],
        compiler_params=_cparams(("arbitrary", "arbitrary")),
        name="attn_lat",
    )(q, k, v, ck, cv, cos, sin, cos, sin, lam_pack, gain)


def _rope_tables(length):
    pairs = DQK_B // 4
    n_rows = length // GRID_W
    pos_row = jnp.repeat(jnp.arange(n_rows), GRID_W).astype(F32)
    pos_col = jnp.tile(jnp.arange(GRID_W), n_rows).astype(F32)
    inv = ROPE_THETA ** (-jnp.arange(pairs, dtype=F32) / pairs)
    ang_r = pos_row[:, None] * inv
    ang_c = pos_col[:, None] * inv
    cos = jnp.concatenate([jnp.cos(ang_r), jnp.cos(ang_r), jnp.cos(ang_c), jnp.cos(ang_c)], axis=-1)
    sin = jnp.concatenate([-jnp.sin(ang_r), jnp.sin(ang_r), -jnp.sin(ang_c), jnp.sin(ang_c)], axis=-1)
    reps = B_W // DQK_B
    return jnp.tile(cos, (1, reps)), jnp.tile(sin, (1, reps))


def _block_diag(x):
    lo, hi = x[:, :LANES], x[:, LANES:]
    first = lax.broadcasted_iota(jnp.int32, lo.shape, 1) < DK_A
    top = jnp.concatenate([jnp.where(first, lo, 0), jnp.where(first, 0, lo)], axis=0)
    bot = jnp.concatenate([jnp.where(first, hi, 0), jnp.where(first, 0, hi)], axis=0)
    zero = jnp.zeros_like(top)
    return jnp.concatenate([jnp.concatenate([top, zero], axis=1), jnp.concatenate([zero, bot], axis=1)], axis=0)


SOLVE_CHUNKS = 8
DELTA_ROWS = 1024


def _delta_kernel(*refs, nb, length, has_state):
    if has_state:
        (q_ref, k_ref, v_ref, z_ref, gcs_ref, beta_ref, dn_ref, s0_ref, o_ref, st_ref, *scr) = refs
    else:
        (q_ref, k_ref, v_ref, z_ref, gcs_ref, beta_ref, dn_ref, o_ref, st_ref, *scr) = refs
        s0_ref = None
    qs, ks, vs, gexp, bexp, u0_s, kg_s, w_s, qg_s, qkm_s, of_s, ob_s, s_s = scr
    L = length
    R = nb * L
    n_chunks = L // CHUNK
    grp = pl.program_id(1)
    W = GROUP_W

    r256 = lax.broadcasted_iota(jnp.int32, (W, W), 0)
    c256 = lax.broadcasted_iota(jnp.int32, (W, W), 1)
    bd_mask = (r256 >> 6) == (c256 >> 6)
    ones_bd = jnp.where(bd_mask, 1.0, 0.0).astype(BF16)
    ri = lax.broadcasted_iota(jnp.int32, (CHUNK, W), 0)
    cj = lax.broadcasted_iota(jnp.int32, (CHUNK, W), 1) & (CHUNK - 1)
    eye_p = ri == cj
    er = lax.broadcasted_iota(jnp.int32, (LANES, 2 * W), 0)
    ec = lax.broadcasted_iota(jnp.int32, (LANES, 2 * W), 1)
    expand = jnp.where(er == (ec >> 8) * H_A + grp * 4 + ((ec & (W - 1)) >> 6), 1.0, 0.0).astype(BF16)

    bd = _block_diag

    def prologue(bi, carry):
        rows_b = pl.ds(pl.multiple_of(bi * L, L), L)

        def l2n(y):
            return y * lax.rsqrt(_dot_split(y * y, ones_bd) + EPS)

        qs[rows_b, :] = l2n(q_ref[bi]) * (DK_A ** -0.5)
        ks[rows_b, :] = l2n(k_ref[bi])
        vs[rows_b, :] = v_ref[bi]
        ge_all = _dot_split(gcs_ref[bi], expand)
        be_all = _dot_split(beta_ref[bi], expand)
        for d in range(2):
            gexp[d, rows_b, :] = ge_all[:, d * W:(d + 1) * W]
            bexp[d, rows_b, :] = be_all[:, d * W:(d + 1) * W]
        return carry

    lax.fori_loop(0, nb, prologue, 0)

    def solve_body(j, carry):
        chunks = []
        for k in range(SOLVE_CHUNKS):
            rows = pl.ds(pl.multiple_of((j * SOLVE_CHUNKS + k) * CHUNK, CHUNK), CHUNK)
            kc = ks[rows, :]
            qc = qs[rows, :]
            kb = kc.astype(BF16)
            res = lax.dot_general(jnp.concatenate([qc.astype(BF16), kb], axis=0), bd(kb),
                                  (((1,), (1,)), ((), ())), preferred_element_type=F32)
            qk, kk = res[:CHUNK], res[CHUNK:]
            ch = dict(rows=rows, bd_k=bd(kb), bd_v=bd(vs[rows, :].astype(BF16)), t=[], scale_u=[], scale_w=[])
            nsum = None
            for d in range(2):
                ge = gexp[d, rows, :]
                be = bexp[d, rows, :]
                if d == 0:
                    incl, strict, gl = ri >= cj, ri > cj, ge[CHUNK - 1:CHUNK, :]
                else:
                    incl, strict, gl = ri <= cj, ri < cj, ge[0:1, :]
                g_row = jnp.sum(jnp.where(eye_p, ge, 0.0), axis=0, keepdims=True)
                b_row = jnp.sum(jnp.where(eye_p, be, 0.0), axis=0, keepdims=True)
                dec_m = jnp.exp(jnp.where(incl, ge - g_row, -jnp.inf))
                nmat = jnp.where(strict, be * kk * dec_m, 0.0)
                nsum = nmat if nsum is None else nsum + nmat
                qkm_s[d, rows, :] = jnp.where(incl, qk * dec_m, 0.0).astype(BF16)
                qg_s[d, rows, :] = (qc * jnp.exp(ge)).astype(BF16)
                kg_s[d, rows, :] = kc * jnp.exp(gl - ge)
                ch["scale_u"].append(b_row)
                ch["scale_w"].append(b_row * jnp.exp(g_row))
                ch["t"].append(jnp.where(eye_p, 1.0, 0.0) - jnp.where((ri >> 1) == (cj >> 1), nmat, 0.0))
            ch["n_both"] = nsum.astype(BF16)
            chunks.append(ch)

        for lb in range(1, 6):
            same = (ri >> (lb + 1)) == (cj >> (lb + 1))
            pair = (same & (((ri >> lb) & 1) == 1) & (((cj >> lb) & 1) == 0),
                    same & (((cj >> lb) & 1) == 1) & (((ri >> lb) & 1) == 0))
            for ch in chunks:
                ch["tb"] = [t.astype(BF16) for t in ch["t"]]
                rhs = bd(jnp.where(pair[0] | pair[1], ch["n_both"], 0))
                ch["y"] = _dot(jnp.concatenate(ch["tb"], axis=0), rhs).astype(BF16)
            for ch in chunks:
                ys = (jnp.where(pair[0], ch["y"][:CHUNK], 0), jnp.where(pair[1], ch["y"][CHUNK:], 0))
                ch["t"] = [ch["t"][d] - _dot(ys[d], bd(ch["tb"][d])) for d in range(2)]
        for ch in chunks:
            tu = jnp.concatenate([(ch["t"][d] * ch["scale_u"][d]).astype(BF16) for d in range(2)], axis=0)
            tw = jnp.concatenate([(ch["t"][d] * ch["scale_w"][d]).astype(BF16) for d in range(2)], axis=0)
            u0 = _dot(tu, ch["bd_v"])
            w = _dot(tw, ch["bd_k"]).astype(BF16)
            for d in range(2):
                u0_s[d, ch["rows"], :] = u0[d * CHUNK:(d + 1) * CHUNK]
                w_s[d, ch["rows"], :] = w[d * CHUNK:(d + 1) * CHUNK]
        return carry

    lax.fori_loop(0, R // (SOLVE_CHUNKS * CHUNK), solve_body, 0)

    s_s[...] = jnp.zeros_like(s_s)
    if has_state:
        for bi in range(nb):
            for d in range(2):
                for h in range(4):
                    s_s[bi, d, h * DK_A:(h + 1) * DK_A, h * DK_A:(h + 1) * DK_A] = s0_ref[bi, d, h]

    def scan_body(i, carry):
        chains = []
        for bi in range(nb):
            for d in range(2):
                c = i if d == 0 else n_chunks - 1 - i
                r0 = pl.multiple_of(bi * L + c * CHUNK, CHUNK)
                rows = pl.ds(r0, CHUNK)
                if d == 0:
                    g_last = gexp[d, pl.ds(r0 + CHUNK - 8, 8), :][7:8]
                else:
                    g_last = gexp[d, pl.ds(r0, 8), :][0:1]
                chains.append(dict(
                    bi=bi, d=d, rows=rows, s=s_s[bi, d], u0=u0_s[d, rows, :], kg=kg_s[d, rows, :],
                    wq=jnp.concatenate([w_s[d, rows, :], qg_s[d, rows, :]], axis=0), qkm=qkm_s[d, rows, :],
                    decay=jnp.exp(g_last)))
        for c in chains:
            c["ws"] = _dot(c["wq"], c["s"].astype(BF16))
        for c in chains:
            c["ub"] = (c["u0"] - c["ws"][:CHUNK]).astype(BF16)
        for c in chains:
            c["o"] = c["ws"][CHUNK:] + _dot(c["qkm"], bd(c["ub"]))
            upd = _dot(c["kg"].T.astype(BF16), c["ub"])
            c["s_new"] = c["s"] * c["decay"] + jnp.where(bd_mask, upd, 0.0)
        for c in chains:
            s_s[c["bi"], c["d"]] = c["s_new"]
            if c["d"] == 0:
                of_s[c["rows"], :] = c["o"]
            else:
                ob_s[c["rows"], :] = c["o"]
        return carry

    lax.fori_loop(0, n_chunks, scan_body, 0)

    def epilogue(bi, carry):
        rows_b = pl.ds(pl.multiple_of(bi * L, L), L)
        o = of_s[rows_b, :] + ob_s[rows_b, :]
        ms = _dot_split(o * o, ones_bd) * (1.0 / DK_A)
        o = o * lax.rsqrt(ms + EPS) * dn_ref[...]
        o_ref[bi] = o * _silu(z_ref[bi])
        return carry

    lax.fori_loop(0, nb, epilogue, 0)
    for bi in range(nb):
        for d in range(2):
            for h in range(4):
                st_ref[bi, d, h] = s_s[bi, d, h * DK_A:(h + 1) * DK_A, h * DK_A:(h + 1) * DK_A]


def _delta(qkv, z, gcs, beta, dnorm, s0):
    b, l, _ = qkv.shape
    has_state = s0 is not None
    w = GROUP_W
    n_grp = A_W // w
    nb = DELTA_ROWS // l
    r = nb * l

    def slab(k):
        return pl.BlockSpec((nb, l, w), lambda i, g: (i, 0, k * n_grp + g))

    narrow = pl.BlockSpec((nb, l, LANES), lambda i, g: (i, 0, 0))
    st_spec = pl.BlockSpec((nb, 2, 4, DK_A, DK_A), lambda i, g: (i, 0, g, 0, 0))
    in_specs = [slab(0), slab(1), slab(2),
                pl.BlockSpec((nb, l, w), lambda i, g: (i, 0, g)),
                narrow, narrow,
                pl.BlockSpec((1, w), lambda i, g: (0, 0))]
    args = [qkv, qkv, qkv, z, gcs, beta, dnorm]
    if has_state:
        in_specs.append(st_spec)
        args.append(s0)
    scratch = ([pltpu.VMEM((r, w), F32)] * 3 + [pltpu.VMEM((2, r, w), F32)] * 4
               + [pltpu.VMEM((2, r, w), BF16)] * 3 + [pltpu.VMEM((r, w), F32)] * 2
               + [pltpu.VMEM((nb, 2, w, w), F32)])
    return pl.pallas_call(
        functools.partial(_delta_kernel, nb=nb, length=l, has_state=has_state),
        grid=(b // nb, n_grp),
        in_specs=in_specs,
        out_specs=[pl.BlockSpec((nb, l, w), lambda i, g: (i, 0, g)), st_spec],
        out_shape=[jax.ShapeDtypeStruct((b, l, A_W), F32),
                   jax.ShapeDtypeStruct((b, 2, H_A, DK_A, DK_A), F32)],
        scratch_shapes=scratch,
        compiler_params=_cparams(("arbitrary", "arbitrary")),
        name="delta_state" if has_state else "delta",
    )(*args)


def _pad_lanes(x, width):
    return jnp.pad(x, [(0, 0)] * (x.ndim - 1) + [(0, width - x.shape[-1])])


def kernel(x_prompt, x_sample, cache_diff_k, cache_diff_v, state_delta, c, c_ctx, w_ada, b_ada, norm_ffn1,
           w_ffn1_in, w_ffn1_out, norm_mix, w_in, conv_w, a_log, dt_bias, delta_norm, lambda_q1, lambda_k1,
           lambda_q2, lambda_k2, diff_norm, w_out, norm_ffn2, w_ffn2_in, w_ffn2_out, norm_final):
    bp, lp, _ = x_prompt.shape
    bs, ls, _ = x_sample.shape
    past = cache_diff_k.shape[2]

    w1_in = w_ffn1_in[0].astype(BF16)
    w1_out = w_ffn1_out[0].astype(BF16)
    w2_in = w_ffn2_in[0].astype(BF16)
    w2_out = w_ffn2_out[0].astype(BF16)
    w_mix_out = w_out[0].astype(BF16)
    wi = w_in[0]
    a_end = 4 * A_W
    dec_w = wi[:, a_end:a_end + 2 * H_A]
    bet_w = wi[:, a_end + 2 * H_A:a_end + 4 * H_A]
    rest = wi[:, a_end + 4 * H_A:]
    w_pack = jnp.concatenate([wi[:, :a_end], rest, _pad_lanes(dec_w, LANES), _pad_lanes(bet_w, LANES)],
                             axis=1).astype(BF16)
    g1 = norm_ffn1[0].reshape(1, D_MODEL)
    gm = norm_mix[0].reshape(1, D_MODEL)
    g2 = norm_ffn2[0].reshape(1, D_MODEL)
    gf = norm_final.reshape(1, D_MODEL)
    gpar = jnp.zeros((8, LANES), F32)
    gpar = gpar.at[0, :2 * H_A].set(a_log[0].reshape(-1)).at[1, :2 * H_A].set(dt_bias[0].reshape(-1))
    dnorm = jnp.tile(delta_norm[0], 4).reshape(1, GROUP_W)
    lam_pack = jnp.zeros((8, LANES), F32)
    for r, v in enumerate((lambda_q1, lambda_k1, lambda_q2, lambda_k2)):
        lam_pack = lam_pack.at[r, :DQK_B].set(v[0])
    dgain = diff_norm[0].reshape(1, DV_B)
    cos, sin = _rope_tables(ls)

    cond = jnp.concatenate([c_ctx[None, :], c, jnp.zeros((8 - 1 - bs, D_MODEL), F32)], axis=0)
    mod = _adaln(cond, w_ada[0], b_ada[0]).reshape(8, N_MOD, D_MODEL)

    def trunk(x3, rows_per_cond, ctx):
        b, l, _ = x3.shape
        x = x3.reshape(b * l, D_MODEL)
        x1 = _ffn1(x, mod, g1, w1_in, w1_out, rows_per_cond)
        qkv, z, bq, bk, bv, gcs, beta = _inproj(x1, mod, gm, w_pack, conv_w[0], gpar, rows_per_cond, l)
        r3 = lambda a: a.reshape(b, l, a.shape[-1])
        s0 = None if ctx is None else ctx[2]
        oa, st = _delta(r3(qkv), r3(z), r3(gcs), r3(beta), dnorm, s0)
        if ctx is None:
            ob = _attn_ctx(r3(bq), r3(bk), r3(bv), lam_pack, dgain)
        else:
            ob = _attn_lat(r3(bq), r3(bk), r3(bv), ctx[0], ctx[1], cos, sin, lam_pack, dgain)
        outs = _post(x1, oa.reshape(b * l, A_W), ob.reshape(b * l, B_W), mod, g2, gf, w_mix_out, w2_in, w2_out,
                     rows_per_cond, kv=(bk, bv) if ctx is None else None)
        return outs[0].reshape(b, l, D_MODEL), outs[1:], st

    y_prompt, (k_maps, v_heads), s_ctx = trunk(x_prompt, None, None)
    ctx = (cache_diff_k[:, 0].reshape(bs, past, B_W), cache_diff_v[:, 0].reshape(bs, past, B_W), state_delta[:, 0])
    y_sample, _, _ = trunk(x_sample, ls, ctx)
    new_k = k_maps.reshape(bp, 1, lp, H_B, 2, DQK_B)
    new_v = v_heads.reshape(bp, 1, lp, H_B, DV_B)
    return (y_prompt, y_sample, new_k, new_v, s_ctx[:, None])
```

```python
import functools
import math

import jax
import jax.numpy as jnp
from jax import lax
from jax.experimental import pallas as pl
from jax.experimental.pallas import tpu as pltpu

F32 = jnp.float32
BF16 = jnp.bfloat16

D_MODEL = 1024
D_FF = 2816
N_MOD = 9
H_A = 8
DK_A = 64
A_W = H_A * DK_A
H_B = 4
DQK_B = 64
DV_B = 128
B_W = H_B * DV_B
CONV_K = 5
CHUNK = 64
GRID_W = 64
ROPE_THETA = 10000.0
EPS = 1e-6
LAM_INIT = 0.8 - 0.6 * math.exp(-0.3 * 0)

LANES = 128
GROUP_W = 4 * DK_A
IN_PACK_COLS = 3 * A_W + A_W + 3 * B_W + 2 * LANES
VMEM_LIMIT = 56 * 1024 * 1024

MXU_W = 256
TOKEN_TILE = 512
FF_CHUNKS = ((0, 6 * MXU_W), (6 * MXU_W, D_FF))
Q_TILE = 256


def _cparams(sem):
    return pltpu.CompilerParams(dimension_semantics=sem, vmem_limit_bytes=VMEM_LIMIT)


def _resident(shape):
    nd = len(shape)
    return pl.BlockSpec(shape, lambda *_: (0,) * nd, pipeline_mode=pl.Buffered(1))


def _silu(x):
    half = 0.5 * x
    return half * jnp.tanh(half) + half


def _mod_norm(x, gain, shift, scale):
    y = x * lax.rsqrt(jnp.mean(x * x, axis=-1, keepdims=True) + EPS)
    return (y * gain) * (1.0 + scale) + shift


def _dot(a, b):
    return jnp.dot(a, b, preferred_element_type=F32)


def _dot_split(a, b_bf16):
    hi = a.astype(BF16)
    lo = (a - hi.astype(F32)).astype(BF16)
    return _dot(hi, b_bf16) + _dot(lo, b_bf16)


def _adaln_kernel(c_ref, w_ref, b_ref, o_ref):
    s = _silu(c_ref[...])
    o_ref[...] = _dot(s.astype(BF16), w_ref[...].astype(BF16)) + b_ref[...]


def _adaln(cond, w_ada, b_ada):
    n = N_MOD * D_MODEL
    tn = n // 4
    return pl.pallas_call(
        _adaln_kernel,
        grid=(n // tn,),
        in_specs=[pl.BlockSpec((8, D_MODEL), lambda j: (0, 0)),
                  pl.BlockSpec((D_MODEL, tn), lambda j: (0, j)),
                  pl.BlockSpec((1, tn), lambda j: (0, j))],
        out_specs=pl.BlockSpec((8, tn), lambda j: (0, j)),
        out_shape=jax.ShapeDtypeStruct((8, n), F32),
        compiler_params=_cparams(("arbitrary",)),
        name="adaln",
    )(cond, w_ada, b_ada.reshape(1, n))


def _swiglu_update(x, mod_ref, k0, gain, w_in_ref, w_out_ref):
    h = _mod_norm(x, gain, mod_ref[k0:k0 + 1, :], mod_ref[k0 + 1:k0 + 2, :]).astype(BF16)
    acc = None
    for lo, hi in FF_CHUNKS:
        g = _dot(h, w_in_ref[:, lo:hi])
        u = _dot(h, w_in_ref[:, D_FF + lo:D_FF + hi])
        part = _dot((_silu(g) * u).astype(BF16), w_out_ref[lo:hi, :])
        acc = part if acc is None else acc + part
    return x + (0.5 * mod_ref[k0 + 2:k0 + 3, :]) * acc


W_CHUNKS = 11
W_IN_CHUNK = 2 * D_FF // W_CHUNKS
W_OUT_CHUNK = D_FF // W_CHUNKS


def _stage_ffn_weights(step, w_in_ref, w_out_ref, w_in_bf_ref, w_out_bf_ref, w_in_s, w_out_s):
    for c in range(W_CHUNKS):
        @pl.when(step == c)
        def _():
            a = w_in_ref[...].astype(BF16)
            b = w_out_ref[...].astype(BF16)
            w_in_bf_ref[...] = a
            w_out_bf_ref[...] = b
            w_in_s[:, c * W_IN_CHUNK:(c + 1) * W_IN_CHUNK] = a
            w_out_s[c * W_OUT_CHUNK:(c + 1) * W_OUT_CHUNK, :] = b


def _ffn_weight_plumbing(cast):
    if not cast:
        return [_resident((D_MODEL, 2 * D_FF)), _resident((D_FF, D_MODEL))], [], [], []
    last = W_CHUNKS - 1
    w_in_chunk = pl.BlockSpec((D_MODEL, W_IN_CHUNK), lambda s: (0, jnp.minimum(s, last)))
    w_out_chunk = pl.BlockSpec((W_OUT_CHUNK, D_MODEL), lambda s: (jnp.minimum(s, last), 0))
    out_shape = [jax.ShapeDtypeStruct((D_MODEL, 2 * D_FF), BF16), jax.ShapeDtypeStruct((D_FF, D_MODEL), BF16)]
    scratch = [pltpu.VMEM((D_MODEL, 2 * D_FF), BF16), pltpu.VMEM((D_FF, D_MODEL), BF16)]
    return [w_in_chunk, w_out_chunk], [w_in_chunk, w_out_chunk], out_shape, scratch


def _tile_of(cast):
    off = W_CHUNKS if cast else 0
    return lambda s: jnp.maximum(s - off, 0)


def _mod_spec(rows_per_cond, tm, tile):
    if rows_per_cond is None:
        return pl.BlockSpec((None, N_MOD, D_MODEL), lambda s: (0, 0, 0))
    return pl.BlockSpec((None, N_MOD, D_MODEL), lambda s: (1 + (tile(s) * tm) // rows_per_cond, 0, 0))


def _ffn1_kernel(*refs, cast):
    x_ref, mod_ref, gain_ref, w_in_ref, w_out_ref, o_ref = refs[:6]
    if cast:
        step = pl.program_id(0)
        w_in_s, w_out_s = refs[8:10]
        _stage_ffn_weights(step, w_in_ref, w_out_ref, refs[6], refs[7], w_in_s, w_out_s)

        @pl.when(step >= W_CHUNKS)
        def _():
            o_ref[...] = _swiglu_update(x_ref[...], mod_ref, 0, gain_ref[...], w_in_s, w_out_s)
    else:
        o_ref[...] = _swiglu_update(x_ref[...], mod_ref, 0, gain_ref[...], w_in_ref, w_out_ref)


def _ffn1(x, mod, gain, w_in, w_out, rows_per_cond):
    t = x.shape[0]
    tm = TOKEN_TILE
    cast = w_in.dtype == F32
    tile = _tile_of(cast)
    row = pl.BlockSpec((tm, D_MODEL), lambda s: (tile(s), 0))
    w_in_specs, w_out_specs, w_shapes, w_scratch = _ffn_weight_plumbing(cast)
    outs = pl.pallas_call(
        functools.partial(_ffn1_kernel, cast=cast),
        grid=(t // tm + (W_CHUNKS if cast else 0),),
        in_specs=[row, _mod_spec(rows_per_cond, tm, tile), _resident((1, D_MODEL))] + w_in_specs,
        out_specs=[row] + w_out_specs,
        out_shape=[jax.ShapeDtypeStruct((t, D_MODEL), F32)] + w_shapes,
        scratch_shapes=w_scratch,
        compiler_params=_cparams(("arbitrary",)),
        name="ffn1",
    )(x, mod, gain, w_in, w_out)
    return (outs[0], outs[1], outs[2]) if cast else (outs[0], w_in, w_out)


_IN_SEGS = (("qkv", 0, 3 * A_W), ("z", 3 * A_W, A_W), ("bq", 4 * A_W, B_W), ("bk", 4 * A_W + B_W, B_W),
            ("bv", 4 * A_W + 2 * B_W, B_W), ("dec", 4 * A_W + 3 * B_W, LANES),
            ("bet", 4 * A_W + 3 * B_W + LANES, LANES))


HALO = 8
CONV_SLAB = 256


def _inproj_kernel(*refs, tm, seq_len, emit_maps):
    x_ref, xp_ref, xn_ref, mod_ref, gain_ref, w_ref, cw_ref, gpar_ref = refs[:8]
    qkv_ref, z_ref, bq_ref, bk_ref, bv_ref, gcs_ref, beta_ref = refs[8:15]
    xpad = refs[-1]
    shift, scale = mod_ref[3:4, :], mod_ref[4:5, :]
    hf = _mod_norm(x_ref[...], gain_ref[...], shift, scale)
    h = hf.astype(BF16)
    segs = dict((name, (start, width)) for name, start, width in _IN_SEGS)

    def proj(name):
        start, width = segs[name]
        return _dot(h, w_ref[:, start:start + width])

    seg = min(tm, seq_len)
    n_seg = tm // seg
    span = seg + 2 * HALO
    qkv_w = 3 * A_W
    units = [(s, c) for s in range(n_seg) for c in range(qkv_w // CONV_SLAB)]
    if seq_len > tm:
        tiles = seq_len // tm
        pos = pl.program_id(0) % tiles
        h_prev = _mod_norm(xp_ref[...], gain_ref[...], shift, scale) * jnp.where(pos > 0, 1.0, 0.0)
        h_next = _mod_norm(xn_ref[...], gain_ref[...], shift, scale) * jnp.where(pos < tiles - 1, 1.0, 0.0)
        h_ext = jnp.concatenate([h_prev, hf, h_next], axis=0).astype(BF16)

        def project(s, c):
            cs = slice(c * CONV_SLAB, (c + 1) * CONV_SLAB)
            xpad[:, cs] = _dot(h_ext, w_ref[:, cs])
    else:
        for s in range(n_seg):
            xpad[s * span:s * span + HALO, :] = jnp.zeros((HALO, qkv_w), F32)
            xpad[s * span + HALO + seg:(s + 1) * span, :] = jnp.zeros((HALO, qkv_w), F32)

        def project(s, c):
            cs = slice(c * CONV_SLAB, (c + 1) * CONV_SLAB)
            xpad[s * span + HALO:s * span + HALO + seg, cs] = _dot(h[s * seg:(s + 1) * seg], w_ref[:, cs])

    def convolve(s, c):
        cs = slice(c * CONV_SLAB, (c + 1) * CONV_SLAB)
        acc = None
        for tap in range(CONV_K):
            lo = s * span + HALO + tap - CONV_K // 2
            term = xpad[lo:lo + seg, cs] * cw_ref[tap:tap + 1, cs]
            acc = term if acc is None else acc + term
        qkv_ref[s * seg:(s + 1) * seg, cs] = _silu(acc)

    z_ref[...] = proj("z")
    bq_ref[...] = proj("bq")
    zk = proj("bk")
    zv = proj("bv")
    bk_ref[...] = zk
    bv_ref[...] = zv
    if emit_maps:
        bk8_ref, bv4_ref = refs[15:17]
        for r in range(2 * H_B):
            bk8_ref[:, r, :] = zk[:, r * DQK_B:(r + 1) * DQK_B]
        for r in range(H_B):
            bv4_ref[:, r, :] = zv[:, r * DV_B:(r + 1) * DV_B]
    for unit in units:
        project(*unit)
        convolve(*unit)

    gpar = gpar_ref[...]
    dec_start = segs["dec"][0]
    gate_pre = _dot(h, w_ref[:, dec_start:dec_start + 2 * LANES])
    xg = gate_pre[:, :LANES] + gpar[1:2, :]
    softplus = jnp.maximum(xg, 0.0) + jnp.log1p(jnp.exp(-jnp.abs(xg)))
    lane = lax.broadcasted_iota(jnp.int32, (tm, LANES), 1)
    g = jnp.where(lane < 2 * H_A, -jnp.exp(gpar[0:1, :]) * softplus, 0.0)
    beta_ref[...] = jax.nn.sigmoid(gate_pre[:, LANES:])
    pos_c = lax.broadcasted_iota(jnp.int32, (tm, 1), 0) & (CHUNK - 1)
    gcf = g
    gcb = g
    for sft in (1, 2, 4, 8, 16, 32):
        gcf = gcf + jnp.where(pos_c >= sft, pltpu.roll(gcf, sft, axis=0), 0.0)
        gcb = gcb + jnp.where(pos_c < CHUNK - sft, pltpu.roll(gcb, tm - sft, axis=0), 0.0)
    gcs_ref[...] = jnp.where(lane < H_A, gcf, gcb)


def _inproj(x, mod, gain, w_pack, conv_w, gpar, rows_per_cond, seq_len, emit_maps):
    t = x.shape[0]
    tm = TOKEN_TILE
    hb = tm // HALO
    row = pl.BlockSpec((tm, D_MODEL), lambda i: (i, 0))
    prev = pl.BlockSpec((HALO, D_MODEL), lambda i: (jnp.maximum(i * hb - 1, 0), 0))
    nxt = pl.BlockSpec((HALO, D_MODEL), lambda i: (jnp.minimum((i + 1) * hb, t // HALO - 1), 0))
    widths = [w for _, _, w in _IN_SEGS]
    out_specs = [pl.BlockSpec((tm, w), lambda i: (i, 0)) for w in widths]
    out_shape = [jax.ShapeDtypeStruct((t, w), F32) for w in widths]
    if emit_maps:
        out_specs += [pl.BlockSpec((tm, 2 * H_B, DQK_B), lambda i: (i, 0, 0)),
                      pl.BlockSpec((tm, H_B, DV_B), lambda i: (i, 0, 0))]
        out_shape += [jax.ShapeDtypeStruct((t, 2 * H_B, DQK_B), F32), jax.ShapeDtypeStruct((t, H_B, DV_B), F32)]
    return pl.pallas_call(
        functools.partial(_inproj_kernel, tm=tm, seq_len=seq_len, emit_maps=emit_maps),
        grid=(t // tm,),
        in_specs=[row, prev, nxt, _mod_spec(rows_per_cond, tm, lambda s: s), _resident((1, D_MODEL)),
                  _resident((D_MODEL, IN_PACK_COLS)), _resident((CONV_K, 3 * A_W)), _resident((8, LANES))],
        out_specs=out_specs,
        out_shape=out_shape,
        scratch_shapes=[pltpu.VMEM((tm + 2 * HALO * max(1, tm // seq_len), 3 * A_W), F32)],
        compiler_params=_cparams(("arbitrary",)),
        name="inproj",
    )(x, x, x, mod, gain, w_pack, conv_w, gpar)


def _post_kernel(*refs, cast):
    x_ref, oa_ref, ob_ref, mod_ref, gain_ref, gfin_ref, wmix_ref, w_in_ref, w_out_ref, o_ref = refs[:10]

    def tokens(w_in, w_out):
        y = _dot(oa_ref[...].astype(BF16), wmix_ref[0:A_W, :]) + _dot(ob_ref[...].astype(BF16), wmix_ref[A_W:, :])
        x = x_ref[...] + mod_ref[5:6, :] * y
        x = _swiglu_update(x, mod_ref, 6, gain_ref[...], w_in, w_out)
        o_ref[...] = (x * lax.rsqrt(jnp.mean(x * x, axis=-1, keepdims=True) + EPS)) * gfin_ref[...]

    if cast:
        step = pl.program_id(0)
        w_in_s, w_out_s = refs[12:14]
        _stage_ffn_weights(step, w_in_ref, w_out_ref, refs[10], refs[11], w_in_s, w_out_s)
        pl.when(step >= W_CHUNKS)(lambda: tokens(w_in_s, w_out_s))
    else:
        tokens(w_in_ref, w_out_ref)


def _post(x, oa, ob, mod, gain, gfin, w_mix, w_in, w_out, rows_per_cond):
    t = x.shape[0]
    tm = TOKEN_TILE
    cast = w_in.dtype == F32
    tile = _tile_of(cast)
    row = pl.BlockSpec((tm, D_MODEL), lambda s: (tile(s), 0))
    half = pl.BlockSpec((tm, A_W), lambda s: (tile(s), 0))
    w_in_specs, w_out_specs, w_shapes, w_scratch = _ffn_weight_plumbing(cast)
    outs = pl.pallas_call(
        functools.partial(_post_kernel, cast=cast),
        grid=(t // tm + (W_CHUNKS if cast else 0),),
        in_specs=[row, half, half, _mod_spec(rows_per_cond, tm, tile), _resident((1, D_MODEL)),
                  _resident((1, D_MODEL)), _resident((A_W + B_W, D_MODEL))] + w_in_specs,
        out_specs=[row] + w_out_specs,
        out_shape=[jax.ShapeDtypeStruct((t, D_MODEL), F32)] + w_shapes,
        scratch_shapes=w_scratch,
        compiler_params=_cparams(("arbitrary",)),
        name="post",
    )(x, oa, ob, mod, gain, gfin, w_mix, w_in, w_out)
    return (outs[0], outs[1], outs[2]) if cast else (outs[0], w_in, w_out)


def _rope(x, cos, sin_signed):
    w = x.shape[-1]
    lane = lax.broadcasted_iota(jnp.int32, x.shape, 1)
    partner = jnp.where((lane & 31) < 16, pltpu.roll(x, w - 16, axis=1), pltpu.roll(x, 16, axis=1))
    return x * cos + partner * sin_signed


def _store_values(v_s, rows, v):
    n = v.shape[0]
    for h in range(H_B):
        v_s[rows, 2 * h * DV_B:(2 * h + 1) * DV_B] = v[:, h * DV_B:(h + 1) * DV_B].astype(BF16)
        v_s[rows, (2 * h + 1) * DV_B:(2 * h + 2) * DV_B] = jnp.ones((n, DV_B), BF16)


def _attn_core(q, k_ref, v_ref, lam, gain_ref, o_ref):
    tq = q.shape[0]
    lane = lax.broadcasted_iota(jnp.int32, (tq, DV_B), 1)
    qs = q * (DQK_B ** -0.5)
    heads = [slice(h * DV_B, (h + 1) * DV_B) for h in range(H_B)]
    scores = []
    for hs in heads:
        qh = qs[:, hs]
        q2 = jnp.concatenate([jnp.where(lane < DQK_B, qh, 0.0), jnp.where(lane >= DQK_B, qh, 0.0)], axis=0)
        scores.append(lax.dot_general(q2.astype(BF16), k_ref[:, hs], (((1,), (1,)), ((), ())),
                                      preferred_element_type=F32))
    for h, (hs, s) in enumerate(zip(heads, scores)):
        e = jnp.exp(s - jnp.max(s, axis=-1, keepdims=True))
        rs = _dot(e.astype(BF16), v_ref[:, 2 * h * DV_B:(2 * h + 2) * DV_B])
        r = rs[:, :DV_B] / rs[:, DV_B:]
        o = r[:tq] - lam * r[tq:]
        o = o * lax.rsqrt(jnp.mean(o * o, axis=-1, keepdims=True) + EPS)
        o_ref[:, hs] = (o * gain_ref[...]) * (1.0 - LAM_INIT)


def _lambda(lam_ref):
    l = lam_ref[...]
    s1 = jnp.sum(l[0:1] * l[1:2], axis=-1, keepdims=True)
    s2 = jnp.sum(l[2:3] * l[3:4], axis=-1, keepdims=True)
    return jnp.exp(s1) - jnp.exp(s2) + LAM_INIT


CTX_SEQS = 4


def _attn_ctx_kernel(q_ref, k_ref, v_ref, lam_ref, gain_ref, o_ref, k_s, v_s):
    lam = _lambda(lam_ref)

    def one_sequence(bi, carry):
        k_s[...] = k_ref[bi].astype(BF16)
        _store_values(v_s, slice(None), v_ref[bi])
        _attn_core(q_ref[bi], k_s, v_s, lam, gain_ref, o_ref.at[bi])
        return carry

    lax.fori_loop(0, CTX_SEQS, one_sequence, 0)


def _attn_ctx(q, k, v, lam_pack, gain):
    b, l, _ = q.shape
    blk = pl.BlockSpec((CTX_SEQS, l, B_W), lambda i: (i, 0, 0))
    return pl.pallas_call(
        _attn_ctx_kernel,
        grid=(b // CTX_SEQS,),
        in_specs=[blk, blk, blk, _resident((8, LANES)), _resident((1, DV_B))],
        out_specs=blk,
        out_shape=jax.ShapeDtypeStruct((b, l, B_W), F32),
        scratch_shapes=[pltpu.VMEM((l, B_W), BF16), pltpu.VMEM((l, 2 * B_W), BF16)],
        compiler_params=_cparams(("arbitrary",)),
        name="attn_ctx",
    )(q, k, v, lam_pack, gain)


def _attn_lat_kernel(q_ref, k_ref, v_ref, ck_ref, cv_ref, cosq_ref, sinq_ref, cos_ref, sin_ref,
                     lam_ref, gain_ref, o_ref, k_s, v_s, *, past):
    @pl.when(pl.program_id(1) == 0)
    def _():
        k_s[0:past, :] = ck_ref[...].astype(BF16)
        _store_values(v_s, slice(0, past), cv_ref[...])
        k_s[past:, :] = _rope(k_ref[...], cos_ref[...], sin_ref[...]).astype(BF16)
        _store_values(v_s, slice(past, None), v_ref[...])

    q = _rope(q_ref[...], cosq_ref[...], sinq_ref[...])
    _attn_core(q, k_s, v_s, _lambda(lam_ref), gain_ref, o_ref)


def _attn_lat(q, k, v, ck, cv, cos, sin, lam_pack, gain):
    b, l, _ = q.shape
    past = ck.shape[1]
    tq = Q_TILE
    full = pl.BlockSpec((None, l, B_W), lambda i, j: (i, 0, 0))
    cache = pl.BlockSpec((None, past, B_W), lambda i, j: (i, 0, 0))
    qblk = pl.BlockSpec((None, tq, B_W), lambda i, j: (i, j, 0))
    tab_q = pl.BlockSpec((tq, B_W), lambda i, j: (j, 0))
    tab = pl.BlockSpec((l, B_W), lambda i, j: (0, 0))
    return pl.pallas_call(
        functools.partial(_attn_lat_kernel, past=past),
        grid=(b, l // tq),
        in_specs=[qblk, full, full, cache, cache, tab_q, tab_q, tab, tab,
                  pl.BlockSpec((8, LANES), lambda i, j: (0, 0)), pl.BlockSpec((1, DV_B), lambda i, j: (0, 0))],
        out_specs=qblk,
        out_shape=jax.ShapeDtypeStruct((b, l, B_W), F32),
        scratch_shapes=[pltpu.VMEM((past + l, B_W), BF16), pltpu.VMEM((past + l, 2 * B_W), BF16)],
        compiler_params=_cparams(("arbitrary", "arbitrary")),
        name="attn_lat",
    )(q, k, v, ck, cv, cos, sin, cos, sin, lam_pack, gain)


def _rope_tables(length):
    pairs = DQK_B // 4
    n_rows = length // GRID_W
    pos_row = jnp.repeat(jnp.arange(n_rows), GRID_W).astype(F32)
    pos_col = jnp.tile(jnp.arange(GRID_W), n_rows).astype(F32)
    inv = ROPE_THETA ** (-jnp.arange(pairs, dtype=F32) / pairs)
    ang_r = pos_row[:, None] * inv
    ang_c = pos_col[:, None] * inv
    cos = jnp.concatenate([jnp.cos(ang_r), jnp.cos(ang_r), jnp.cos(ang_c), jnp.cos(ang_c)], axis=-1)
    sin = jnp.concatenate([-jnp.sin(ang_r), jnp.sin(ang_r), -jnp.sin(ang_c), jnp.sin(ang_c)], axis=-1)
    reps = B_W // DQK_B
    return jnp.tile(cos, (1, reps)), jnp.tile(sin, (1, reps))


def _block_diag(x):
    lo, hi = x[:, :LANES], x[:, LANES:]
    first = lax.broadcasted_iota(jnp.int32, lo.shape, 1) < DK_A
    top = jnp.concatenate([jnp.where(first, lo, 0), jnp.where(first, 0, lo)], axis=0)
    bot = jnp.concatenate([jnp.where(first, hi, 0), jnp.where(first, 0, hi)], axis=0)
    zero = jnp.zeros_like(top)
    return jnp.concatenate([jnp.concatenate([top, zero], axis=1), jnp.concatenate([zero, bot], axis=1)], axis=0)


SOLVE_CHUNKS = 8
DELTA_ROWS = 1024


def _delta_kernel(*refs, nb, length, has_state):
    if has_state:
        (q_ref, k_ref, v_ref, z_ref, gcs_ref, beta_ref, dn_ref, s0_ref, o_ref, st_ref, *scr) = refs
    else:
        (q_ref, k_ref, v_ref, z_ref, gcs_ref, beta_ref, dn_ref, o_ref, st_ref, *scr) = refs
        s0_ref = None
    qs, ks, vs, gexp, bexp, u0_s, kg_s, w_s, qg_s, qkm_s, of_s, ob_s, s_s = scr
    L = length
    R = nb * L
    n_chunks = L // CHUNK
    grp = pl.program_id(1)
    W = GROUP_W

    r256 = lax.broadcasted_iota(jnp.int32, (W, W), 0)
    c256 = lax.broadcasted_iota(jnp.int32, (W, W), 1)
    bd_mask = (r256 >> 6) == (c256 >> 6)
    ones_bd = jnp.where(bd_mask, 1.0, 0.0).astype(BF16)
    ri = lax.broadcasted_iota(jnp.int32, (CHUNK, W), 0)
    cj = lax.broadcasted_iota(jnp.int32, (CHUNK, W), 1) & (CHUNK - 1)
    eye_p = ri == cj
    er = lax.broadcasted_iota(jnp.int32, (LANES, 2 * W), 0)
    ec = lax.broadcasted_iota(jnp.int32, (LANES, 2 * W), 1)
    expand = jnp.where(er == (ec >> 8) * H_A + grp * 4 + ((ec & (W - 1)) >> 6), 1.0, 0.0).astype(BF16)

    bd = _block_diag

    def prologue(bi, carry):
        rows_b = pl.ds(pl.multiple_of(bi * L, L), L)

        def l2n(y):
            return y * lax.rsqrt(_dot_split(y * y, ones_bd) + EPS)

        qs[rows_b, :] = l2n(q_ref[bi]) * (DK_A ** -0.5)
        ks[rows_b, :] = l2n(k_ref[bi])
        vs[rows_b, :] = v_ref[bi]
        ge_all = _dot_split(gcs_ref[bi], expand)
        be_all = _dot_split(beta_ref[bi], expand)
        for d in range(2):
            gexp[d, rows_b, :] = ge_all[:, d * W:(d + 1) * W]
            bexp[d, rows_b, :] = be_all[:, d * W:(d + 1) * W]
        return carry

    lax.fori_loop(0, nb, prologue, 0)

    def solve_body(j, carry):
        chunks = []
        for k in range(SOLVE_CHUNKS):
            rows = pl.ds(pl.multiple_of((j * SOLVE_CHUNKS + k) * CHUNK, CHUNK), CHUNK)
            kc = ks[rows, :]
            qc = qs[rows, :]
            kb = kc.astype(BF16)
            res = lax.dot_general(jnp.concatenate([qc.astype(BF16), kb], axis=0), bd(kb),
                                  (((1,), (1,)), ((), ())), preferred_element_type=F32)
            qk, kk = res[:CHUNK], res[CHUNK:]
            ch = dict(rows=rows, bd_k=bd(kb), bd_v=bd(vs[rows, :].astype(BF16)), t=[], scale_u=[], scale_w=[])
            nsum = None
            for d in range(2):
                ge = gexp[d, rows, :]
                be = bexp[d, rows, :]
                if d == 0:
                    incl, strict, gl = ri >= cj, ri > cj, ge[CHUNK - 1:CHUNK, :]
                else:
                    incl, strict, gl = ri <= cj, ri < cj, ge[0:1, :]
                g_row = jnp.sum(jnp.where(eye_p, ge, 0.0), axis=0, keepdims=True)
                b_row = jnp.sum(jnp.where(eye_p, be, 0.0), axis=0, keepdims=True)
                dec_m = jnp.exp(jnp.where(incl, ge - g_row, -jnp.inf))
                nmat = jnp.where(strict, be * kk * dec_m, 0.0)
                nsum = nmat if nsum is None else nsum + nmat
                qkm_s[d, rows, :] = jnp.where(incl, qk * dec_m, 0.0).astype(BF16)
                qg_s[d, rows, :] = (qc * jnp.exp(ge)).astype(BF16)
                kg_s[d, rows, :] = kc * jnp.exp(gl - ge)
                ch["scale_u"].append(b_row)
                ch["scale_w"].append(b_row * jnp.exp(g_row))
                ch["t"].append(jnp.where(eye_p, 1.0, 0.0) - jnp.where((ri >> 1) == (cj >> 1), nmat, 0.0))
            ch["n_both"] = nsum.astype(BF16)
            chunks.append(ch)

        for lb in range(1, 6):
            same = (ri >> (lb + 1)) == (cj >> (lb + 1))
            pair = (same & (((ri >> lb) & 1) == 1) & (((cj >> lb) & 1) == 0),
                    same & (((cj >> lb) & 1) == 1) & (((ri >> lb) & 1) == 0))
            for ch in chunks:
                ch["tb"] = [t.astype(BF16) for t in ch["t"]]
                rhs = bd(jnp.where(pair[0] | pair[1], ch["n_both"], 0))
                ch["y"] = _dot(jnp.concatenate(ch["tb"], axis=0), rhs).astype(BF16)
            for ch in chunks:
                ys = (jnp.where(pair[0], ch["y"][:CHUNK], 0), jnp.where(pair[1], ch["y"][CHUNK:], 0))
                ch["t"] = [ch["t"][d] - _dot(ys[d], bd(ch["tb"][d])) for d in range(2)]
        for ch in chunks:
            tu = jnp.concatenate([(ch["t"][d] * ch["scale_u"][d]).astype(BF16) for d in range(2)], axis=0)
            tw = jnp.concatenate([(ch["t"][d] * ch["scale_w"][d]).astype(BF16) for d in range(2)], axis=0)
            u0 = _dot(tu, ch["bd_v"])
            w = _dot(tw, ch["bd_k"]).astype(BF16)
            for d in range(2):
                u0_s[d, ch["rows"], :] = u0[d * CHUNK:(d + 1) * CHUNK]
                w_s[d, ch["rows"], :] = w[d * CHUNK:(d + 1) * CHUNK]
        return carry

    lax.fori_loop(0, R // (SOLVE_CHUNKS * CHUNK), solve_body, 0)

    s_s[...] = jnp.zeros_like(s_s)
    if has_state:
        for bi in range(nb):
            for d in range(2):
                for h in range(4):
                    s_s[bi, d, h * DK_A:(h + 1) * DK_A, h * DK_A:(h + 1) * DK_A] = s0_ref[bi, d, h]

    def scan_body(i, carry):
        chains = []
        for bi in range(nb):
            for d in range(2):
                c = i if d == 0 else n_chunks - 1 - i
                r0 = pl.multiple_of(bi * L + c * CHUNK, CHUNK)
                rows = pl.ds(r0, CHUNK)
                if d == 0:
                    g_last = gexp[d, pl.ds(r0 + CHUNK - 8, 8), :][7:8]
                else:
                    g_last = gexp[d, pl.ds(r0, 8), :][0:1]
                chains.append(dict(
                    bi=bi, d=d, rows=rows, s=s_s[bi, d], u0=u0_s[d, rows, :], kg=kg_s[d, rows, :],
                    wq=jnp.concatenate([w_s[d, rows, :], qg_s[d, rows, :]], axis=0), qkm=qkm_s[d, rows, :],
                    decay=jnp.exp(g_last)))
        for c in chains:
            c["ws"] = _dot(c["wq"], c["s"].astype(BF16))
        for c in chains:
            c["ub"] = (c["u0"] - c["ws"][:CHUNK]).astype(BF16)
        for c in chains:
            c["o"] = c["ws"][CHUNK:] + _dot(c["qkm"], bd(c["ub"]))
            upd = _dot(c["kg"].T.astype(BF16), c["ub"])
            c["s_new"] = c["s"] * c["decay"] + jnp.where(bd_mask, upd, 0.0)
        for c in chains:
            s_s[c["bi"], c["d"]] = c["s_new"]
            if c["d"] == 0:
                of_s[c["rows"], :] = c["o"]
            else:
                ob_s[c["rows"], :] = c["o"]
        return carry

    lax.fori_loop(0, n_chunks, scan_body, 0)

    def epilogue(bi, carry):
        rows_b = pl.ds(pl.multiple_of(bi * L, L), L)
        o = of_s[rows_b, :] + ob_s[rows_b, :]
        ms = _dot_split(o * o, ones_bd) * (1.0 / DK_A)
        o = o * lax.rsqrt(ms + EPS) * dn_ref[...]
        o_ref[bi] = o * _silu(z_ref[bi])
        return carry

    lax.fori_loop(0, nb, epilogue, 0)
    for bi in range(nb):
        for d in range(2):
            for h in range(4):
                st_ref[bi, d, h] = s_s[bi, d, h * DK_A:(h + 1) * DK_A, h * DK_A:(h + 1) * DK_A]


def _delta(qkv, z, gcs, beta, dnorm, s0):
    b, l, _ = qkv.shape
    has_state = s0 is not None
    w = GROUP_W
    n_grp = A_W // w
    nb = DELTA_ROWS // l
    r = nb * l

    def slab(k):
        return pl.BlockSpec((nb, l, w), lambda i, g: (i, 0, k * n_grp + g))

    narrow = pl.BlockSpec((nb, l, LANES), lambda i, g: (i, 0, 0))
    st_spec = pl.BlockSpec((nb, 2, 4, DK_A, DK_A), lambda i, g: (i, 0, g, 0, 0))
    in_specs = [slab(0), slab(1), slab(2),
                pl.BlockSpec((nb, l, w), lambda i, g: (i, 0, g)),
                narrow, narrow,
                pl.BlockSpec((1, w), lambda i, g: (0, 0))]
    args = [qkv, qkv, qkv, z, gcs, beta, dnorm]
    if has_state:
        in_specs.append(st_spec)
        args.append(s0)
    scratch = ([pltpu.VMEM((r, w), F32)] * 3 + [pltpu.VMEM((2, r, w), F32)] * 4
               + [pltpu.VMEM((2, r, w), BF16)] * 3 + [pltpu.VMEM((r, w), F32)] * 2
               + [pltpu.VMEM((nb, 2, w, w), F32)])
    return pl.pallas_call(
        functools.partial(_delta_kernel, nb=nb, length=l, has_state=has_state),
        grid=(b // nb, n_grp),
        in_specs=in_specs,
        out_specs=[pl.BlockSpec((nb, l, w), lambda i, g: (i, 0, g)), st_spec],
        out_shape=[jax.ShapeDtypeStruct((b, l, A_W), F32),
                   jax.ShapeDtypeStruct((b, 2, H_A, DK_A, DK_A), F32)],
        scratch_shapes=scratch,
        compiler_params=_cparams(("arbitrary", "arbitrary")),
        name="delta_state" if has_state else "delta",
    )(*args)


def _pad_lanes(x, width):
    return jnp.pad(x, [(0, 0)] * (x.ndim - 1) + [(0, width - x.shape[-1])])


def kernel(x_prompt, x_sample, cache_diff_k, cache_diff_v, state_delta, c, c_ctx, w_ada, b_ada, norm_ffn1,
           w_ffn1_in, w_ffn1_out, norm_mix, w_in, conv_w, a_log, dt_bias, delta_norm, lambda_q1, lambda_k1,
           lambda_q2, lambda_k2, diff_norm, w_out, norm_ffn2, w_ffn2_in, w_ffn2_out, norm_final):
    bp, lp, _ = x_prompt.shape
    bs, ls, _ = x_sample.shape
    past = cache_diff_k.shape[2]

    w_mix_out = w_out[0].astype(BF16)
    wi = w_in[0]
    a_end = 4 * A_W
    dec_w = wi[:, a_end:a_end + 2 * H_A]
    bet_w = wi[:, a_end + 2 * H_A:a_end + 4 * H_A]
    rest = wi[:, a_end + 4 * H_A:]
    w_pack = jnp.concatenate([wi[:, :a_end], rest, _pad_lanes(dec_w, LANES), _pad_lanes(bet_w, LANES)],
                             axis=1).astype(BF16)
    g1 = norm_ffn1[0].reshape(1, D_MODEL)
    gm = norm_mix[0].reshape(1, D_MODEL)
    g2 = norm_ffn2[0].reshape(1, D_MODEL)
    gf = norm_final.reshape(1, D_MODEL)
    gpar = jnp.zeros((8, LANES), F32)
    gpar = gpar.at[0, :2 * H_A].set(a_log[0].reshape(-1)).at[1, :2 * H_A].set(dt_bias[0].reshape(-1))
    dnorm = jnp.tile(delta_norm[0], 4).reshape(1, GROUP_W)
    lam_pack = jnp.zeros((8, LANES), F32)
    for r, v in enumerate((lambda_q1, lambda_k1, lambda_q2, lambda_k2)):
        lam_pack = lam_pack.at[r, :DQK_B].set(v[0])
    dgain = diff_norm[0].reshape(1, DV_B)
    cos, sin = _rope_tables(ls)

    cond = jnp.concatenate([c_ctx[None, :], c, jnp.zeros((8 - 1 - bs, D_MODEL), F32)], axis=0)
    mod = _adaln(cond, w_ada[0], b_ada[0]).reshape(8, N_MOD, D_MODEL)

    def trunk(x3, rows_per_cond, ctx, w1, w2):
        b, l, _ = x3.shape
        x = x3.reshape(b * l, D_MODEL)
        x1, *w1 = _ffn1(x, mod, g1, *w1, rows_per_cond)
        outs = _inproj(x1, mod, gm, w_pack, conv_w[0], gpar, rows_per_cond, l, ctx is None)
        qkv, z, bq, bk, bv, gcs, beta = outs[:7]
        r3 = lambda a: a.reshape(b, l, a.shape[-1])
        s0 = None if ctx is None else ctx[2]
        oa, st = _delta(r3(qkv), r3(z), r3(gcs), r3(beta), dnorm, s0)
        if ctx is None:
            ob = _attn_ctx(r3(bq), r3(bk), r3(bv), lam_pack, dgain)
        else:
            ob = _attn_lat(r3(bq), r3(bk), r3(bv), ctx[0], ctx[1], cos, sin, lam_pack, dgain)
        y, *w2 = _post(x1, oa.reshape(b * l, A_W), ob.reshape(b * l, B_W), mod, g2, gf, w_mix_out, *w2,
                       rows_per_cond)
        return y.reshape(b, l, D_MODEL), outs[7:], st, w1, w2

    y_prompt, (k_maps, v_heads), s_ctx, w1, w2 = trunk(x_prompt, None, None, (w_ffn1_in[0], w_ffn1_out[0]),
                                                       (w_ffn2_in[0], w_ffn2_out[0]))
    ctx = (cache_diff_k[:, 0].reshape(bs, past, B_W), cache_diff_v[:, 0].reshape(bs, past, B_W), state_delta[:, 0])
    y_sample = trunk(x_sample, ls, ctx, w1, w2)[0]
    new_k = k_maps.reshape(bp, 1, lp, H_B, 2, DQK_B)
    new_v = v_heads.reshape(bp, 1, lp, H_B, DV_B)
    return (y_prompt, y_sample, new_k, new_v, s_ctx[:, None])
```

```python
import functools
import math

import jax
import jax.numpy as jnp
from jax import lax
from jax.experimental import pallas as pl
from jax.experimental.pallas import tpu as pltpu

F32 = jnp.float32
BF16 = jnp.bfloat16

D_MODEL = 1024
D_FF = 2816
N_MOD = 9
H_A = 8
DK_A = 64
A_W = H_A * DK_A
H_B = 4
DQK_B = 64
DV_B = 128
B_W = H_B * DV_B
CONV_K = 5
CHUNK = 64
GRID_W = 64
ROPE_THETA = 10000.0
EPS = 1e-6
LAM_INIT = 0.8 - 0.6 * math.exp(-0.3 * 0)

LANES = 128
GROUP_W = 4 * DK_A
IN_PACK_COLS = 3 * A_W + A_W + 3 * B_W + 2 * LANES
VMEM_LIMIT = 56 * 1024 * 1024

MXU_W = 256
TOKEN_TILE = 512
FF_CHUNKS = ((0, 6 * MXU_W), (6 * MXU_W, D_FF))
Q_TILE = 256


def _cparams(sem):
    return pltpu.CompilerParams(dimension_semantics=sem, vmem_limit_bytes=VMEM_LIMIT)


def _resident(shape):
    nd = len(shape)
    return pl.BlockSpec(shape, lambda *_: (0,) * nd, pipeline_mode=pl.Buffered(1))


def _silu(x):
    half = 0.5 * x
    return half * jnp.tanh(half) + half


def _mod_norm(x, gain, shift, scale):
    y = x * lax.rsqrt(jnp.mean(x * x, axis=-1, keepdims=True) + EPS)
    return (y * gain) * (1.0 + scale) + shift


def _dot(a, b):
    return jnp.dot(a, b, preferred_element_type=F32)


def _dot_split(a, b_bf16):
    hi = a.astype(BF16)
    lo = (a - hi.astype(F32)).astype(BF16)
    return _dot(hi, b_bf16) + _dot(lo, b_bf16)


def _adaln_kernel(c_ref, w_ref, b_ref, o_ref):
    s = _silu(c_ref[...])
    o_ref[...] = _dot(s.astype(BF16), w_ref[...].astype(BF16)) + b_ref[...]


def _adaln(cond, w_ada, b_ada):
    n = N_MOD * D_MODEL
    tn = n // 4
    return pl.pallas_call(
        _adaln_kernel,
        grid=(n // tn,),
        in_specs=[pl.BlockSpec((8, D_MODEL), lambda j: (0, 0)),
                  pl.BlockSpec((D_MODEL, tn), lambda j: (0, j)),
                  pl.BlockSpec((1, tn), lambda j: (0, j))],
        out_specs=pl.BlockSpec((8, tn), lambda j: (0, j)),
        out_shape=jax.ShapeDtypeStruct((8, n), F32),
        compiler_params=_cparams(("arbitrary",)),
        name="adaln",
    )(cond, w_ada, b_ada.reshape(1, n))


def _swiglu_update(x, mod_ref, k0, gain, w_in_ref, w_out_ref):
    h = _mod_norm(x, gain, mod_ref[k0:k0 + 1, :], mod_ref[k0 + 1:k0 + 2, :]).astype(BF16)
    acc = None
    for lo, hi in FF_CHUNKS:
        g = _dot(h, w_in_ref[:, lo:hi])
        u = _dot(h, w_in_ref[:, D_FF + lo:D_FF + hi])
        part = _dot((_silu(g) * u).astype(BF16), w_out_ref[lo:hi, :])
        acc = part if acc is None else acc + part
    return x + (0.5 * mod_ref[k0 + 2:k0 + 3, :]) * acc


W_CHUNKS = 11
W_IN_CHUNK = 2 * D_FF // W_CHUNKS
W_OUT_CHUNK = D_FF // W_CHUNKS


def _stage_ffn_weights(step, w_in_ref, w_out_ref, w_in_bf_ref, w_out_bf_ref, w_in_s, w_out_s):
    for c in range(W_CHUNKS):
        @pl.when(step == c)
        def _():
            a = w_in_ref[...].astype(BF16)
            b = w_out_ref[...].astype(BF16)
            w_in_bf_ref[...] = a
            w_out_bf_ref[...] = b
            w_in_s[:, c * W_IN_CHUNK:(c + 1) * W_IN_CHUNK] = a
            w_out_s[c * W_OUT_CHUNK:(c + 1) * W_OUT_CHUNK, :] = b


def _ffn_weight_plumbing(cast):
    if not cast:
        return [_resident((D_MODEL, 2 * D_FF)), _resident((D_FF, D_MODEL))], [], [], []
    last = W_CHUNKS - 1
    w_in_chunk = pl.BlockSpec((D_MODEL, W_IN_CHUNK), lambda s: (0, jnp.minimum(s, last)))
    w_out_chunk = pl.BlockSpec((W_OUT_CHUNK, D_MODEL), lambda s: (jnp.minimum(s, last), 0))
    out_shape = [jax.ShapeDtypeStruct((D_MODEL, 2 * D_FF), BF16), jax.ShapeDtypeStruct((D_FF, D_MODEL), BF16)]
    scratch = [pltpu.VMEM((D_MODEL, 2 * D_FF), BF16), pltpu.VMEM((D_FF, D_MODEL), BF16)]
    return [w_in_chunk, w_out_chunk], [w_in_chunk, w_out_chunk], out_shape, scratch


def _tile_of(cast):
    off = W_CHUNKS if cast else 0
    return lambda s: jnp.maximum(s - off, 0)


def _mod_spec(rows_per_cond, tm, tile):
    if rows_per_cond is None:
        return pl.BlockSpec((None, N_MOD, D_MODEL), lambda s: (0, 0, 0))
    return pl.BlockSpec((None, N_MOD, D_MODEL), lambda s: (1 + (tile(s) * tm) // rows_per_cond, 0, 0))


def _ffn1_kernel(*refs, cast):
    x_ref, mod_ref, gain_ref, w_in_ref, w_out_ref, o_ref = refs[:6]
    if cast:
        step = pl.program_id(0)
        w_in_s, w_out_s = refs[8:10]
        _stage_ffn_weights(step, w_in_ref, w_out_ref, refs[6], refs[7], w_in_s, w_out_s)

        @pl.when(step >= W_CHUNKS)
        def _():
            o_ref[...] = _swiglu_update(x_ref[...], mod_ref, 0, gain_ref[...], w_in_s, w_out_s)
    else:
        o_ref[...] = _swiglu_update(x_ref[...], mod_ref, 0, gain_ref[...], w_in_ref, w_out_ref)


def _ffn1(x, mod, gain, w_in, w_out, rows_per_cond):
    t = x.shape[0]
    tm = TOKEN_TILE
    cast = w_in.dtype == F32
    tile = _tile_of(cast)
    row = pl.BlockSpec((tm, D_MODEL), lambda s: (tile(s), 0))
    w_in_specs, w_out_specs, w_shapes, w_scratch = _ffn_weight_plumbing(cast)
    outs = pl.pallas_call(
        functools.partial(_ffn1_kernel, cast=cast),
        grid=(t // tm + (W_CHUNKS if cast else 0),),
        in_specs=[row, _mod_spec(rows_per_cond, tm, tile), _resident((1, D_MODEL))] + w_in_specs,
        out_specs=[row] + w_out_specs,
        out_shape=[jax.ShapeDtypeStruct((t, D_MODEL), F32)] + w_shapes,
        scratch_shapes=w_scratch,
        compiler_params=_cparams(("arbitrary",)),
        name="ffn1",
    )(x, mod, gain, w_in, w_out)
    return (outs[0], outs[1], outs[2]) if cast else (outs[0], w_in, w_out)


_IN_SEGS = (("qkv", 0, 3 * A_W), ("z", 3 * A_W, A_W), ("bq", 4 * A_W, B_W), ("bk", 4 * A_W + B_W, B_W),
            ("bv", 4 * A_W + 2 * B_W, B_W), ("dec", 4 * A_W + 3 * B_W, LANES),
            ("bet", 4 * A_W + 3 * B_W + LANES, LANES))


HALO = 8
CONV_SLAB = 256


def _inproj_kernel(*refs, tm, seq_len, emit_maps):
    x_ref, xp_ref, xn_ref, mod_ref, gain_ref, w_ref, cw_ref, gpar_ref = refs[:8]
    qkv_ref, z_ref, bq_ref, bk_ref, bv_ref, gcs_ref, beta_ref = refs[8:15]
    hs, xpad, ybuf = refs[-3:]
    shift, scale = mod_ref[3:4, :], mod_ref[4:5, :]
    hf = _mod_norm(x_ref[...], gain_ref[...], shift, scale)
    h = hf.astype(BF16)
    segs = dict((name, (start, width)) for name, start, width in _IN_SEGS)

    def proj(name):
        start, width = segs[name]
        return _dot(h, w_ref[:, start:start + width])

    seg = min(tm, seq_len)
    n_seg = tm // seg
    ext = seg + 2 * HALO
    m = ext // 8
    pad = (CONV_K // 2) * 8
    qkv_w = 3 * A_W
    units = [(s, c) for s in range(n_seg) for c in range(qkv_w // CONV_SLAB)]
    def put_rows(lo, hi, val):
        for ct in range(D_MODEL // LANES):
            hs[ct, lo:hi, :] = val[:, ct * LANES:(ct + 1) * LANES]

    if seq_len > tm:
        tiles = seq_len // tm
        pos = pl.program_id(0) % tiles
        put_rows(0, HALO, _mod_norm(xp_ref[...], gain_ref[...], shift, scale) * jnp.where(pos > 0, 1.0, 0.0))
        put_rows(HALO + seg, ext,
                 _mod_norm(xn_ref[...], gain_ref[...], shift, scale) * jnp.where(pos < tiles - 1, 1.0, 0.0))
    else:
        for s in range(n_seg):
            put_rows(s * ext, s * ext + HALO, jnp.zeros((HALO, D_MODEL), F32))
            put_rows(s * ext + HALO + seg, (s + 1) * ext, jnp.zeros((HALO, D_MODEL), F32))
    for s in range(n_seg):
        put_rows(s * ext + HALO, s * ext + HALO + seg, hf[s * seg:(s + 1) * seg])
    h_il = [jnp.concatenate(
        [jnp.concatenate([hs[ct, pl.ds(s * ext + v, 8, stride=m), :] for v in range(m)], axis=0)
         for ct in range(D_MODEL // LANES)], axis=1).astype(BF16) for s in range(n_seg)]

    def project(s, c):
        cs = slice(c * CONV_SLAB, (c + 1) * CONV_SLAB)
        base = s * (ext + 2 * pad)
        res = _dot(h_il[s], w_ref[:, cs])
        xpad[base + pad:base + pad + ext, cs] = res
        for j in range(CONV_K // 2):
            xpad[base + 8 * j:base + 8 * (j + 1), cs] = pltpu.roll(res[ext - pad + 8 * j:ext - pad + 8 * (j + 1)], 1, 0)
            xpad[base + pad + ext + 8 * j:base + pad + ext + 8 * (j + 1), cs] = pltpu.roll(res[8 * j:8 * (j + 1)], 7, 0)

    def convolve(s, c):
        cs = slice(c * CONV_SLAB, (c + 1) * CONV_SLAB)
        base = s * (ext + 2 * pad)
        acc = None
        for tap in range(CONV_K):
            term = xpad[base + 8 * tap:base + 8 * tap + ext, cs] * cw_ref[tap:tap + 1, cs]
            acc = term if acc is None else acc + term
        y = _silu(acc)
        for j in range(CONV_SLAB // LANES):
            ct = c * (CONV_SLAB // LANES) + j
            for v in range(m):
                ybuf[ct, pl.ds(v, 8, stride=m), :] = y[8 * v:8 * (v + 1), j * LANES:(j + 1) * LANES]
            qkv_ref[s * seg:(s + 1) * seg, ct * LANES:(ct + 1) * LANES] = ybuf[ct, HALO:HALO + seg, :]

    z_ref[...] = proj("z")
    bq_ref[...] = proj("bq")
    zk = proj("bk")
    zv = proj("bv")
    bk_ref[...] = zk
    bv_ref[...] = zv
    if emit_maps:
        bk8_ref, bv4_ref = refs[15:17]
        for r in range(2 * H_B):
            bk8_ref[:, r, :] = zk[:, r * DQK_B:(r + 1) * DQK_B]
        for r in range(H_B):
            bv4_ref[:, r, :] = zv[:, r * DV_B:(r + 1) * DV_B]
    for unit in units:
        project(*unit)
        convolve(*unit)

    gpar = gpar_ref[...]
    dec_start = segs["dec"][0]
    gate_pre = _dot(h, w_ref[:, dec_start:dec_start + 2 * LANES])
    xg = gate_pre[:, :LANES] + gpar[1:2, :]
    softplus = jnp.maximum(xg, 0.0) + jnp.log1p(jnp.exp(-jnp.abs(xg)))
    lane = lax.broadcasted_iota(jnp.int32, (tm, LANES), 1)
    g = jnp.where(lane < 2 * H_A, -jnp.exp(gpar[0:1, :]) * softplus, 0.0)
    beta_ref[...] = jax.nn.sigmoid(gate_pre[:, LANES:])
    pos_c = lax.broadcasted_iota(jnp.int32, (tm, 1), 0) & (CHUNK - 1)
    gcf = g
    gcb = g
    for sft in (1, 2, 4, 8, 16, 32):
        gcf = gcf + jnp.where(pos_c >= sft, pltpu.roll(gcf, sft, axis=0), 0.0)
        gcb = gcb + jnp.where(pos_c < CHUNK - sft, pltpu.roll(gcb, tm - sft, axis=0), 0.0)
    gcs_ref[...] = jnp.where(lane < H_A, gcf, gcb)


def _inproj(x, mod, gain, w_pack, conv_w, gpar, rows_per_cond, seq_len, emit_maps):
    t = x.shape[0]
    tm = TOKEN_TILE
    hb = tm // HALO
    n_seg = max(1, tm // seq_len)
    ext = tm // n_seg + 2 * HALO
    row = pl.BlockSpec((tm, D_MODEL), lambda i: (i, 0))
    prev = pl.BlockSpec((HALO, D_MODEL), lambda i: (jnp.maximum(i * hb - 1, 0), 0))
    nxt = pl.BlockSpec((HALO, D_MODEL), lambda i: (jnp.minimum((i + 1) * hb, t // HALO - 1), 0))
    widths = [w for _, _, w in _IN_SEGS]
    out_specs = [pl.BlockSpec((tm, w), lambda i: (i, 0)) for w in widths]
    out_shape = [jax.ShapeDtypeStruct((t, w), F32) for w in widths]
    if emit_maps:
        out_specs += [pl.BlockSpec((tm, 2 * H_B, DQK_B), lambda i: (i, 0, 0)),
                      pl.BlockSpec((tm, H_B, DV_B), lambda i: (i, 0, 0))]
        out_shape += [jax.ShapeDtypeStruct((t, 2 * H_B, DQK_B), F32), jax.ShapeDtypeStruct((t, H_B, DV_B), F32)]
    return pl.pallas_call(
        functools.partial(_inproj_kernel, tm=tm, seq_len=seq_len, emit_maps=emit_maps),
        grid=(t // tm,),
        in_specs=[row, prev, nxt, _mod_spec(rows_per_cond, tm, lambda s: s), _resident((1, D_MODEL)),
                  _resident((D_MODEL, IN_PACK_COLS)), _resident((CONV_K, 3 * A_W)), _resident((8, LANES))],
        out_specs=out_specs,
        out_shape=out_shape,
        scratch_shapes=[pltpu.VMEM((D_MODEL // LANES, n_seg * ext, LANES), F32),
                        pltpu.VMEM((n_seg * (ext + 2 * (CONV_K // 2) * 8), 3 * A_W), F32),
                        pltpu.VMEM((3 * A_W // LANES, ext, LANES), F32)],
        compiler_params=_cparams(("arbitrary",)),
        name="inproj",
    )(x, x, x, mod, gain, w_pack, conv_w, gpar)


def _post_kernel(*refs, cast):
    x_ref, oa_ref, ob_ref, mod_ref, gain_ref, gfin_ref, wmix_ref, w_in_ref, w_out_ref, o_ref = refs[:10]

    def tokens(w_in, w_out):
        y = _dot(oa_ref[...].astype(BF16), wmix_ref[0:A_W, :]) + _dot(ob_ref[...].astype(BF16), wmix_ref[A_W:, :])
        x = x_ref[...] + mod_ref[5:6, :] * y
        x = _swiglu_update(x, mod_ref, 6, gain_ref[...], w_in, w_out)
        o_ref[...] = (x * lax.rsqrt(jnp.mean(x * x, axis=-1, keepdims=True) + EPS)) * gfin_ref[...]

    if cast:
        step = pl.program_id(0)
        w_in_s, w_out_s = refs[12:14]
        _stage_ffn_weights(step, w_in_ref, w_out_ref, refs[10], refs[11], w_in_s, w_out_s)
        pl.when(step >= W_CHUNKS)(lambda: tokens(w_in_s, w_out_s))
    else:
        tokens(w_in_ref, w_out_ref)


def _post(x, oa, ob, mod, gain, gfin, w_mix, w_in, w_out, rows_per_cond):
    t = x.shape[0]
    tm = TOKEN_TILE
    cast = w_in.dtype == F32
    tile = _tile_of(cast)
    row = pl.BlockSpec((tm, D_MODEL), lambda s: (tile(s), 0))
    half = pl.BlockSpec((tm, A_W), lambda s: (tile(s), 0))
    w_in_specs, w_out_specs, w_shapes, w_scratch = _ffn_weight_plumbing(cast)
    outs = pl.pallas_call(
        functools.partial(_post_kernel, cast=cast),
        grid=(t // tm + (W_CHUNKS if cast else 0),),
        in_specs=[row, half, half, _mod_spec(rows_per_cond, tm, tile), _resident((1, D_MODEL)),
                  _resident((1, D_MODEL)), _resident((A_W + B_W, D_MODEL))] + w_in_specs,
        out_specs=[row] + w_out_specs,
        out_shape=[jax.ShapeDtypeStruct((t, D_MODEL), F32)] + w_shapes,
        scratch_shapes=w_scratch,
        compiler_params=_cparams(("arbitrary",)),
        name="post",
    )(x, oa, ob, mod, gain, gfin, w_mix, w_in, w_out)
    return (outs[0], outs[1], outs[2]) if cast else (outs[0], w_in, w_out)


def _rope(x, cos, sin_signed):
    w = x.shape[-1]
    lane = lax.broadcasted_iota(jnp.int32, x.shape, 1)
    partner = jnp.where((lane & 31) < 16, pltpu.roll(x, w - 16, axis=1), pltpu.roll(x, 16, axis=1))
    return x * cos + partner * sin_signed


def _store_values(v_s, rows, v):
    for h in range(H_B):
        vh = v[:, h * DV_B:(h + 1) * DV_B] if len(v.shape) == 2 else v[:, h, :]
        v_s[rows, 2 * h * DV_B:(2 * h + 1) * DV_B] = vh.astype(BF16)
        v_s[rows, (2 * h + 1) * DV_B:(2 * h + 2) * DV_B] = jnp.ones(vh.shape, BF16)


def _attn_core(q, k_ref, v_ref, lam, gain_ref, o_ref):
    tq = q.shape[0]
    lane = lax.broadcasted_iota(jnp.int32, (tq, DV_B), 1)
    qs = q * (DQK_B ** -0.5)
    heads = [slice(h * DV_B, (h + 1) * DV_B) for h in range(H_B)]
    scores = []
    for hs in heads:
        qh = qs[:, hs]
        q2 = jnp.concatenate([jnp.where(lane < DQK_B, qh, 0.0), jnp.where(lane >= DQK_B, qh, 0.0)], axis=0)
        scores.append(lax.dot_general(q2.astype(BF16), k_ref[:, hs], (((1,), (1,)), ((), ())),
                                      preferred_element_type=F32))
    for h, (hs, s) in enumerate(zip(heads, scores)):
        e = jnp.exp(s - jnp.max(s, axis=-1, keepdims=True))
        rs = _dot(e.astype(BF16), v_ref[:, 2 * h * DV_B:(2 * h + 2) * DV_B])
        r = rs[:, :DV_B] / rs[:, DV_B:]
        o = r[:tq] - lam * r[tq:]
        o = o * lax.rsqrt(jnp.mean(o * o, axis=-1, keepdims=True) + EPS)
        o_ref[:, hs] = (o * gain_ref[...]) * (1.0 - LAM_INIT)


def _lambda(lam_ref):
    l = lam_ref[...]
    s1 = jnp.sum(l[0:1] * l[1:2], axis=-1, keepdims=True)
    s2 = jnp.sum(l[2:3] * l[3:4], axis=-1, keepdims=True)
    return jnp.exp(s1) - jnp.exp(s2) + LAM_INIT


CTX_SEQS = 4


def _attn_ctx_kernel(q_ref, k_ref, v_ref, lam_ref, gain_ref, o_ref, k_s, v_s):
    lam = _lambda(lam_ref)

    def one_sequence(bi, carry):
        k_s[...] = k_ref[bi].astype(BF16)
        _store_values(v_s, slice(None), v_ref[bi])
        _attn_core(q_ref[bi], k_s, v_s, lam, gain_ref, o_ref.at[bi])
        return carry

    lax.fori_loop(0, CTX_SEQS, one_sequence, 0)


def _attn_ctx(q, k, v, lam_pack, gain):
    b, l, _ = q.shape
    blk = pl.BlockSpec((CTX_SEQS, l, B_W), lambda i: (i, 0, 0))
    return pl.pallas_call(
        _attn_ctx_kernel,
        grid=(b // CTX_SEQS,),
        in_specs=[blk, blk, blk, _resident((8, LANES)), _resident((1, DV_B))],
        out_specs=blk,
        out_shape=jax.ShapeDtypeStruct((b, l, B_W), F32),
        scratch_shapes=[pltpu.VMEM((l, B_W), BF16), pltpu.VMEM((l, 2 * B_W), BF16)],
        compiler_params=_cparams(("arbitrary",)),
        name="attn_ctx",
    )(q, k, v, lam_pack, gain)


def _attn_lat_kernel(q_ref, k_ref, v_ref, ck_ref, cv_ref, cosq_ref, sinq_ref, cos_ref, sin_ref,
                     lam_ref, gain_ref, o_ref, k_s, v_s, *, past):
    @pl.when(pl.program_id(1) == 0)
    def _():
        for r in range(2 * H_B):
            k_s[0:past, r * DQK_B:(r + 1) * DQK_B] = ck_ref[:, r, :].astype(BF16)
        _store_values(v_s, slice(0, past), cv_ref)
        k_s[past:, :] = _rope(k_ref[...], cos_ref[...], sin_ref[...]).astype(BF16)
        _store_values(v_s, slice(past, None), v_ref[...])

    q = _rope(q_ref[...], cosq_ref[...], sinq_ref[...])
    _attn_core(q, k_s, v_s, _lambda(lam_ref), gain_ref, o_ref)


def _attn_lat(q, k, v, ck, cv, cos, sin, lam_pack, gain):
    b, l, _ = q.shape
    past = ck.shape[2]
    tq = Q_TILE
    full = pl.BlockSpec((None, l, B_W), lambda i, j: (i, 0, 0))
    cache_k = pl.BlockSpec((None, None, past, 2 * H_B, DQK_B), lambda i, j: (i, 0, 0, 0, 0))
    cache_v = pl.BlockSpec((None, None, past, H_B, DV_B), lambda i, j: (i, 0, 0, 0, 0))
    qblk = pl.BlockSpec((None, tq, B_W), lambda i, j: (i, j, 0))
    tab_q = pl.BlockSpec((tq, B_W), lambda i, j: (j, 0))
    tab = pl.BlockSpec((l, B_W), lambda i, j: (0, 0))
    return pl.pallas_call(
        functools.partial(_attn_lat_kernel, past=past),
        grid=(b, l // tq),
        in_specs=[qblk, full, full, cache_k, cache_v, tab_q, tab_q, tab, tab,
                  pl.BlockSpec((8, LANES), lambda i, j: (0, 0)), pl.BlockSpec((1, DV_B), lambda i, j: (0, 0))],
        out_specs=qblk,
        out_shape=jax.ShapeDtypeStruct((b, l, B_W), F32),
        scratch_shapes=[pltpu.VMEM((past + l, B_W), BF16), pltpu.VMEM((past + l, 2 * B_W), BF16)],
        compiler_params=_cparams(("arbitrary", "arbitrary")),
        name="attn_lat",
    )(q, k, v, ck, cv, cos, sin, cos, sin, lam_pack, gain)


def _rope_tables(length):
    pairs = DQK_B // 4
    n_rows = length // GRID_W
    pos_row = jnp.repeat(jnp.arange(n_rows), GRID_W).astype(F32)
    pos_col = jnp.tile(jnp.arange(GRID_W), n_rows).astype(F32)
    inv = ROPE_THETA ** (-jnp.arange(pairs, dtype=F32) / pairs)
    ang_r = pos_row[:, None] * inv
    ang_c = pos_col[:, None] * inv
    cos = jnp.concatenate([jnp.cos(ang_r), jnp.cos(ang_r), jnp.cos(ang_c), jnp.cos(ang_c)], axis=-1)
    sin = jnp.concatenate([-jnp.sin(ang_r), jnp.sin(ang_r), -jnp.sin(ang_c), jnp.sin(ang_c)], axis=-1)
    reps = B_W // DQK_B
    return jnp.tile(cos, (1, reps)), jnp.tile(sin, (1, reps))


def _block_diag(x):
    lo, hi = x[:, :LANES], x[:, LANES:]
    first = lax.broadcasted_iota(jnp.int32, lo.shape, 1) < DK_A
    top = jnp.concatenate([jnp.where(first, lo, 0), jnp.where(first, 0, lo)], axis=0)
    bot = jnp.concatenate([jnp.where(first, hi, 0), jnp.where(first, 0, hi)], axis=0)
    zero = jnp.zeros_like(top)
    return jnp.concatenate([jnp.concatenate([top, zero], axis=1), jnp.concatenate([zero, bot], axis=1)], axis=0)


SOLVE_CHUNKS = 8
DELTA_ROWS = 1024


def _delta_kernel(*refs, nb, length, has_state):
    if has_state:
        (q_ref, k_ref, v_ref, z_ref, gcs_ref, beta_ref, dn_ref, s0_ref, o_ref, st_ref, *scr) = refs
    else:
        (q_ref, k_ref, v_ref, z_ref, gcs_ref, beta_ref, dn_ref, o_ref, st_ref, *scr) = refs
        s0_ref = None
    qs, ks, vs, gexp, bexp, u0_s, kg_s, w_s, qg_s, qkm_s, of_s, ob_s, s_s = scr
    L = length
    R = nb * L
    n_chunks = L // CHUNK
    grp = pl.program_id(1)
    W = GROUP_W

    r256 = lax.broadcasted_iota(jnp.int32, (W, W), 0)
    c256 = lax.broadcasted_iota(jnp.int32, (W, W), 1)
    bd_mask = (r256 >> 6) == (c256 >> 6)
    ones_bd = jnp.where(bd_mask, 1.0, 0.0).astype(BF16)
    ri = lax.broadcasted_iota(jnp.int32, (CHUNK, W), 0)
    cj = lax.broadcasted_iota(jnp.int32, (CHUNK, W), 1) & (CHUNK - 1)
    eye_p = ri == cj
    er = lax.broadcasted_iota(jnp.int32, (LANES, 2 * W), 0)
    ec = lax.broadcasted_iota(jnp.int32, (LANES, 2 * W), 1)
    expand = jnp.where(er == (ec >> 8) * H_A + grp * 4 + ((ec & (W - 1)) >> 6), 1.0, 0.0).astype(BF16)

    bd = _block_diag

    def prologue(bi, carry):
        rows_b = pl.ds(pl.multiple_of(bi * L, L), L)

        def l2n(y):
            return y * lax.rsqrt(_dot_split(y * y, ones_bd) + EPS)

        qs[rows_b, :] = l2n(q_ref[bi]) * (DK_A ** -0.5)
        ks[rows_b, :] = l2n(k_ref[bi])
        vs[rows_b, :] = v_ref[bi]
        ge_all = _dot_split(gcs_ref[bi], expand)
        be_all = _dot_split(beta_ref[bi], expand)
        for d in range(2):
            gexp[d, rows_b, :] = ge_all[:, d * W:(d + 1) * W]
            bexp[d, rows_b, :] = be_all[:, d * W:(d + 1) * W]
        return carry

    lax.fori_loop(0, nb, prologue, 0)

    def solve_body(j, carry):
        chunks = []
        for k in range(SOLVE_CHUNKS):
            rows = pl.ds(pl.multiple_of((j * SOLVE_CHUNKS + k) * CHUNK, CHUNK), CHUNK)
            kc = ks[rows, :]
            qc = qs[rows, :]
            kb = kc.astype(BF16)
            res = lax.dot_general(jnp.concatenate([qc.astype(BF16), kb], axis=0), bd(kb),
                                  (((1,), (1,)), ((), ())), preferred_element_type=F32)
            qk, kk = res[:CHUNK], res[CHUNK:]
            ch = dict(rows=rows, bd_k=bd(kb), bd_v=bd(vs[rows, :].astype(BF16)), t=[], scale_u=[], scale_w=[])
            nsum = None
            for d in range(2):
                ge = gexp[d, rows, :]
                be = bexp[d, rows, :]
                if d == 0:
                    incl, strict, gl = ri >= cj, ri > cj, ge[CHUNK - 1:CHUNK, :]
                else:
                    incl, strict, gl = ri <= cj, ri < cj, ge[0:1, :]
                g_row = jnp.sum(jnp.where(eye_p, ge, 0.0), axis=0, keepdims=True)
                b_row = jnp.sum(jnp.where(eye_p, be, 0.0), axis=0, keepdims=True)
                dec_m = jnp.exp(jnp.where(incl, ge - g_row, -jnp.inf))
                nmat = jnp.where(strict, be * kk * dec_m, 0.0)
                nsum = nmat if nsum is None else nsum + nmat
                qkm_s[d, rows, :] = jnp.where(incl, qk * dec_m, 0.0).astype(BF16)
                qg_s[d, rows, :] = (qc * jnp.exp(ge)).astype(BF16)
                kg_s[d, rows, :] = kc * jnp.exp(gl - ge)
                ch["scale_u"].append(b_row)
                ch["scale_w"].append(b_row * jnp.exp(g_row))
                ch["t"].append(jnp.where(eye_p, 1.0, 0.0) - jnp.where((ri >> 1) == (cj >> 1), nmat, 0.0))
            ch["n_both"] = nsum.astype(BF16)
            chunks.append(ch)

        for lb in range(1, 6):
            same = (ri >> (lb + 1)) == (cj >> (lb + 1))
            pair = (same & (((ri >> lb) & 1) == 1) & (((cj >> lb) & 1) == 0),
                    same & (((cj >> lb) & 1) == 1) & (((ri >> lb) & 1) == 0))
            for ch in chunks:
                ch["tb"] = [t.astype(BF16) for t in ch["t"]]
                rhs = bd(jnp.where(pair[0] | pair[1], ch["n_both"], 0))
                ch["y"] = _dot(jnp.concatenate(ch["tb"], axis=0), rhs).astype(BF16)
            for ch in chunks:
                ys = (jnp.where(pair[0], ch["y"][:CHUNK], 0), jnp.where(pair[1], ch["y"][CHUNK:], 0))
                ch["t"] = [ch["t"][d] - _dot(ys[d], bd(ch["tb"][d])) for d in range(2)]
        for ch in chunks:
            tu = jnp.concatenate([(ch["t"][d] * ch["scale_u"][d]).astype(BF16) for d in range(2)], axis=0)
            tw = jnp.concatenate([(ch["t"][d] * ch["scale_w"][d]).astype(BF16) for d in range(2)], axis=0)
            u0 = _dot(tu, ch["bd_v"])
            w = _dot(tw, ch["bd_k"]).astype(BF16)
            for d in range(2):
                u0_s[d, ch["rows"], :] = u0[d * CHUNK:(d + 1) * CHUNK]
                w_s[d, ch["rows"], :] = w[d * CHUNK:(d + 1) * CHUNK]
        return carry

    lax.fori_loop(0, R // (SOLVE_CHUNKS * CHUNK), solve_body, 0)

    s_s[...] = jnp.zeros_like(s_s)
    if has_state:
        for bi in range(nb):
            for d in range(2):
                for h in range(4):
                    s_s[bi, d, h * DK_A:(h + 1) * DK_A, h * DK_A:(h + 1) * DK_A] = s0_ref[bi, d, h]

    def scan_body(i, carry):
        chains = []
        for bi in range(nb):
            for d in range(2):
                c = i if d == 0 else n_chunks - 1 - i
                r0 = pl.multiple_of(bi * L + c * CHUNK, CHUNK)
                rows = pl.ds(r0, CHUNK)
                if d == 0:
                    g_last = gexp[d, pl.ds(r0 + CHUNK - 8, 8), :][7:8]
                else:
                    g_last = gexp[d, pl.ds(r0, 8), :][0:1]
                chains.append(dict(
                    bi=bi, d=d, rows=rows, s=s_s[bi, d], u0=u0_s[d, rows, :], kg=kg_s[d, rows, :],
                    wq=jnp.concatenate([w_s[d, rows, :], qg_s[d, rows, :]], axis=0), qkm=qkm_s[d, rows, :],
                    decay=jnp.exp(g_last)))
        for c in chains:
            c["ws"] = _dot(c["wq"], c["s"].astype(BF16))
        for c in chains:
            c["ub"] = (c["u0"] - c["ws"][:CHUNK]).astype(BF16)
        for c in chains:
            c["o"] = c["ws"][CHUNK:] + _dot(c["qkm"], bd(c["ub"]))
            upd = _dot(c["kg"].T.astype(BF16), c["ub"])
            c["s_new"] = c["s"] * c["decay"] + jnp.where(bd_mask, upd, 0.0)
        for c in chains:
            s_s[c["bi"], c["d"]] = c["s_new"]
            if c["d"] == 0:
                of_s[c["rows"], :] = c["o"]
            else:
                ob_s[c["rows"], :] = c["o"]
        return carry

    lax.fori_loop(0, n_chunks, scan_body, 0)

    def epilogue(bi, carry):
        rows_b = pl.ds(pl.multiple_of(bi * L, L), L)
        o = of_s[rows_b, :] + ob_s[rows_b, :]
        ms = _dot_split(o * o, ones_bd) * (1.0 / DK_A)
        o = o * lax.rsqrt(ms + EPS) * dn_ref[...]
        o_ref[bi] = o * _silu(z_ref[bi])
        return carry

    lax.fori_loop(0, nb, epilogue, 0)
    for bi in range(nb):
        for d in range(2):
            for h in range(4):
                st_ref[bi, d, h] = s_s[bi, d, h * DK_A:(h + 1) * DK_A, h * DK_A:(h + 1) * DK_A]


def _delta(qkv, z, gcs, beta, dnorm, s0):
    b, l, _ = qkv.shape
    has_state = s0 is not None
    w = GROUP_W
    n_grp = A_W // w
    nb = DELTA_ROWS // l
    r = nb * l

    def slab(k):
        return pl.BlockSpec((nb, l, w), lambda i, g: (i, 0, k * n_grp + g))

    narrow = pl.BlockSpec((nb, l, LANES), lambda i, g: (i, 0, 0))
    st_spec = pl.BlockSpec((nb, None, 2, 4, DK_A, DK_A), lambda i, g: (i, 0, 0, g, 0, 0))
    in_specs = [slab(0), slab(1), slab(2),
                pl.BlockSpec((nb, l, w), lambda i, g: (i, 0, g)),
                narrow, narrow,
                pl.BlockSpec((1, w), lambda i, g: (0, 0))]
    args = [qkv, qkv, qkv, z, gcs, beta, dnorm]
    if has_state:
        in_specs.append(st_spec)
        args.append(s0)
    scratch = ([pltpu.VMEM((r, w), F32)] * 3 + [pltpu.VMEM((2, r, w), F32)] * 4
               + [pltpu.VMEM((2, r, w), BF16)] * 3 + [pltpu.VMEM((r, w), F32)] * 2
               + [pltpu.VMEM((nb, 2, w, w), F32)])
    return pl.pallas_call(
        functools.partial(_delta_kernel, nb=nb, length=l, has_state=has_state),
        grid=(b // nb, n_grp),
        in_specs=in_specs,
        out_specs=[pl.BlockSpec((nb, l, w), lambda i, g: (i, 0, g)), st_spec],
        out_shape=[jax.ShapeDtypeStruct((b, l, A_W), F32),
                   jax.ShapeDtypeStruct((b, 1, 2, H_A, DK_A, DK_A), F32)],
        scratch_shapes=scratch,
        compiler_params=_cparams(("arbitrary", "arbitrary")),
        name="delta_state" if has_state else "delta",
    )(*args)


def _pad_lanes(x, width):
    return jnp.pad(x, [(0, 0)] * (x.ndim - 1) + [(0, width - x.shape[-1])])


def kernel(x_prompt, x_sample, cache_diff_k, cache_diff_v, state_delta, c, c_ctx, w_ada, b_ada, norm_ffn1,
           w_ffn1_in, w_ffn1_out, norm_mix, w_in, conv_w, a_log, dt_bias, delta_norm, lambda_q1, lambda_k1,
           lambda_q2, lambda_k2, diff_norm, w_out, norm_ffn2, w_ffn2_in, w_ffn2_out, norm_final):
    bp, lp, _ = x_prompt.shape
    bs, ls, _ = x_sample.shape
    past = cache_diff_k.shape[2]

    w_mix_out = w_out[0].astype(BF16)
    wi = w_in[0]
    a_end = 4 * A_W
    dec_w = wi[:, a_end:a_end + 2 * H_A]
    bet_w = wi[:, a_end + 2 * H_A:a_end + 4 * H_A]
    rest = wi[:, a_end + 4 * H_A:]
    w_pack = jnp.concatenate([wi[:, :a_end], rest, _pad_lanes(dec_w, LANES), _pad_lanes(bet_w, LANES)],
                             axis=1).astype(BF16)
    g1 = norm_ffn1[0].reshape(1, D_MODEL)
    gm = norm_mix[0].reshape(1, D_MODEL)
    g2 = norm_ffn2[0].reshape(1, D_MODEL)
    gf = norm_final.reshape(1, D_MODEL)
    gpar = jnp.zeros((8, LANES), F32)
    gpar = gpar.at[0, :2 * H_A].set(a_log[0].reshape(-1)).at[1, :2 * H_A].set(dt_bias[0].reshape(-1))
    dnorm = jnp.tile(delta_norm[0], 4).reshape(1, GROUP_W)
    lam_pack = jnp.zeros((8, LANES), F32)
    for r, v in enumerate((lambda_q1, lambda_k1, lambda_q2, lambda_k2)):
        lam_pack = lam_pack.at[r, :DQK_B].set(v[0])
    dgain = diff_norm[0].reshape(1, DV_B)
    cos, sin = _rope_tables(ls)

    cond = jnp.concatenate([c_ctx[None, :], c, jnp.zeros((8 - 1 - bs, D_MODEL), F32)], axis=0)
    mod = _adaln(cond, w_ada[0], b_ada[0]).reshape(8, N_MOD, D_MODEL)

    def trunk(x3, rows_per_cond, ctx, w1, w2):
        b, l, _ = x3.shape
        x = x3.reshape(b * l, D_MODEL)
        x1, *w1 = _ffn1(x, mod, g1, *w1, rows_per_cond)
        outs = _inproj(x1, mod, gm, w_pack, conv_w[0], gpar, rows_per_cond, l, ctx is None)
        qkv, z, bq, bk, bv, gcs, beta = outs[:7]
        r3 = lambda a: a.reshape(b, l, a.shape[-1])
        s0 = None if ctx is None else ctx[2]
        oa, st = _delta(r3(qkv), r3(z), r3(gcs), r3(beta), dnorm, s0)
        if ctx is None:
            ob = _attn_ctx(r3(bq), r3(bk), r3(bv), lam_pack, dgain)
        else:
            ob = _attn_lat(r3(bq), r3(bk), r3(bv), ctx[0], ctx[1], cos, sin, lam_pack, dgain)
        y, *w2 = _post(x1, oa.reshape(b * l, A_W), ob.reshape(b * l, B_W), mod, g2, gf, w_mix_out, *w2,
                       rows_per_cond)
        return y.reshape(b, l, D_MODEL), outs[7:], st, w1, w2

    y_prompt, (k_maps, v_heads), s_ctx, w1, w2 = trunk(x_prompt, None, None, (w_ffn1_in[0], w_ffn1_out[0]),
                                                       (w_ffn2_in[0], w_ffn2_out[0]))
    ctx = (cache_diff_k.reshape(bs, 1, past, 2 * H_B, DQK_B), cache_diff_v, state_delta)
    y_sample = trunk(x_sample, ls, ctx, w1, w2)[0]
    new_k = k_maps.reshape(bp, 1, lp, H_B, 2, DQK_B)
    new_v = v_heads.reshape(bp, 1, lp, H_B, DV_B)
    return (y_prompt, y_sample, new_k, new_v, s_ctx)
```

```python
import functools
import math

import jax
import jax.numpy as jnp
from jax import lax
from jax.experimental import pallas as pl
from jax.experimental.pallas import tpu as pltpu

F32 = jnp.float32
BF16 = jnp.bfloat16

D_MODEL = 1024
D_FF = 2816
N_MOD = 9
H_A = 8
DK_A = 64
A_W = H_A * DK_A
H_B = 4
DQK_B = 64
DV_B = 128
B_W = H_B * DV_B
CONV_K = 5
CHUNK = 64
GRID_W = 64
ROPE_THETA = 10000.0
EPS = 1e-6
LAM_INIT = 0.8 - 0.6 * math.exp(-0.3 * 0)

LANES = 128
GROUP_W = 4 * DK_A
IN_PACK_COLS = 3 * A_W + A_W + 3 * B_W + 2 * LANES
VMEM_LIMIT = 56 * 1024 * 1024

MXU_W = 256
TOKEN_TILE = 512
FF_CHUNKS = ((0, 6 * MXU_W), (6 * MXU_W, D_FF))
Q_TILE = 256


def _cparams(sem):
    return pltpu.CompilerParams(dimension_semantics=sem, vmem_limit_bytes=VMEM_LIMIT)


def _resident(shape):
    nd = len(shape)
    return pl.BlockSpec(shape, lambda *_: (0,) * nd, pipeline_mode=pl.Buffered(1))


def _silu(x):
    half = 0.5 * x
    return half * jnp.tanh(half) + half


def _mod_norm(x, gain, shift, scale):
    y = x * lax.rsqrt(jnp.mean(x * x, axis=-1, keepdims=True) + EPS)
    return (y * gain) * (1.0 + scale) + shift


def _dot(a, b):
    return jnp.dot(a, b, preferred_element_type=F32)


def _dot_split(a, b_bf16):
    hi = a.astype(BF16)
    lo = (a - hi.astype(F32)).astype(BF16)
    return _dot(hi, b_bf16) + _dot(lo, b_bf16)


def _adaln_kernel(c_ref, w_ref, b_ref, o_ref):
    s = _silu(c_ref[...])
    o_ref[...] = _dot(s.astype(BF16), w_ref[...].astype(BF16)) + b_ref[...]


def _adaln(cond, w_ada, b_ada):
    n = N_MOD * D_MODEL
    tn = n // 4
    return pl.pallas_call(
        _adaln_kernel,
        grid=(n // tn,),
        in_specs=[pl.BlockSpec((8, D_MODEL), lambda j: (0, 0)),
                  pl.BlockSpec((D_MODEL, tn), lambda j: (0, j)),
                  pl.BlockSpec((1, tn), lambda j: (0, j))],
        out_specs=pl.BlockSpec((8, tn), lambda j: (0, j)),
        out_shape=jax.ShapeDtypeStruct((8, n), F32),
        compiler_params=_cparams(("arbitrary",)),
        name="adaln",
    )(cond, w_ada, b_ada.reshape(1, n))


def _swiglu_update(x, mod_ref, k0, gain, w_in_ref, w_out_ref):
    h = _mod_norm(x, gain, mod_ref[k0:k0 + 1, :], mod_ref[k0 + 1:k0 + 2, :]).astype(BF16)
    acc = None
    for lo, hi in FF_CHUNKS:
        g = _dot(h, w_in_ref[:, lo:hi])
        u = _dot(h, w_in_ref[:, D_FF + lo:D_FF + hi])
        part = _dot((_silu(g) * u).astype(BF16), w_out_ref[lo:hi, :])
        acc = part if acc is None else acc + part
    return x + (0.5 * mod_ref[k0 + 2:k0 + 3, :]) * acc


W_CHUNKS = 11
W_IN_CHUNK = 2 * D_FF // W_CHUNKS
W_OUT_CHUNK = D_FF // W_CHUNKS


def _stage_ffn_weights(step, w_in_ref, w_out_ref, w_in_bf_ref, w_out_bf_ref, w_in_s, w_out_s):
    for c in range(W_CHUNKS):
        @pl.when(step == c)
        def _():
            a = w_in_ref[...].astype(BF16)
            b = w_out_ref[...].astype(BF16)
            w_in_bf_ref[...] = a
            w_out_bf_ref[...] = b
            w_in_s[:, c * W_IN_CHUNK:(c + 1) * W_IN_CHUNK] = a
            w_out_s[c * W_OUT_CHUNK:(c + 1) * W_OUT_CHUNK, :] = b


def _ffn_weight_plumbing(cast):
    if not cast:
        return [_resident((D_MODEL, 2 * D_FF)), _resident((D_FF, D_MODEL))], [], [], []
    last = W_CHUNKS - 1
    w_in_chunk = pl.BlockSpec((D_MODEL, W_IN_CHUNK), lambda s: (0, jnp.minimum(s, last)))
    w_out_chunk = pl.BlockSpec((W_OUT_CHUNK, D_MODEL), lambda s: (jnp.minimum(s, last), 0))
    out_shape = [jax.ShapeDtypeStruct((D_MODEL, 2 * D_FF), BF16), jax.ShapeDtypeStruct((D_FF, D_MODEL), BF16)]
    scratch = [pltpu.VMEM((D_MODEL, 2 * D_FF), BF16), pltpu.VMEM((D_FF, D_MODEL), BF16)]
    return [w_in_chunk, w_out_chunk], [w_in_chunk, w_out_chunk], out_shape, scratch


def _tile_of(cast):
    off = W_CHUNKS if cast else 0
    return lambda s: jnp.maximum(s - off, 0)


def _mod_spec(rows_per_cond, tm, tile):
    if rows_per_cond is None:
        return pl.BlockSpec((None, N_MOD, D_MODEL), lambda s: (0, 0, 0))
    return pl.BlockSpec((None, N_MOD, D_MODEL), lambda s: (1 + (tile(s) * tm) // rows_per_cond, 0, 0))


def _ffn1_kernel(*refs, cast):
    x_ref, mod_ref, gain_ref, w_in_ref, w_out_ref, o_ref = refs[:6]
    if cast:
        step = pl.program_id(0)
        w_in_s, w_out_s = refs[8:10]
        _stage_ffn_weights(step, w_in_ref, w_out_ref, refs[6], refs[7], w_in_s, w_out_s)

        @pl.when(step >= W_CHUNKS)
        def _():
            o_ref[...] = _swiglu_update(x_ref[...], mod_ref, 0, gain_ref[...], w_in_s, w_out_s)
    else:
        o_ref[...] = _swiglu_update(x_ref[...], mod_ref, 0, gain_ref[...], w_in_ref, w_out_ref)


def _ffn1(x, mod, gain, w_in, w_out, rows_per_cond):
    t = x.shape[0]
    tm = TOKEN_TILE
    cast = w_in.dtype == F32
    tile = _tile_of(cast)
    row = pl.BlockSpec((tm, D_MODEL), lambda s: (tile(s), 0))
    w_in_specs, w_out_specs, w_shapes, w_scratch = _ffn_weight_plumbing(cast)
    outs = pl.pallas_call(
        functools.partial(_ffn1_kernel, cast=cast),
        grid=(t // tm + (W_CHUNKS if cast else 0),),
        in_specs=[row, _mod_spec(rows_per_cond, tm, tile), _resident((1, D_MODEL))] + w_in_specs,
        out_specs=[row] + w_out_specs,
        out_shape=[jax.ShapeDtypeStruct((t, D_MODEL), F32)] + w_shapes,
        scratch_shapes=w_scratch,
        compiler_params=_cparams(("arbitrary",)),
        name="ffn1",
    )(x, mod, gain, w_in, w_out)
    return (outs[0], outs[1], outs[2]) if cast else (outs[0], w_in, w_out)


_IN_SEGS = (("qkv", 0, 3 * A_W), ("z", 3 * A_W, A_W), ("bq", 4 * A_W, B_W), ("bk", 4 * A_W + B_W, B_W),
            ("bv", 4 * A_W + 2 * B_W, B_W), ("dec", 4 * A_W + 3 * B_W, LANES),
            ("bet", 4 * A_W + 3 * B_W + LANES, LANES))


HALO = 8
CONV_SLAB = 256


def _inproj_kernel(*refs, tm, seq_len, emit_maps):
    x_ref, xp_ref, xn_ref, mod_ref, gain_ref, w_ref, cw_ref, gpar_ref = refs[:8]
    qkv_ref, z_ref, bq_ref, bk_ref, bv_ref, gcs_ref, beta_ref = refs[8:15]
    hs, xpad, ybuf = refs[-3:]
    shift, scale = mod_ref[3:4, :], mod_ref[4:5, :]
    hf = _mod_norm(x_ref[...], gain_ref[...], shift, scale)
    h = hf.astype(BF16)
    segs = dict((name, (start, width)) for name, start, width in _IN_SEGS)

    def proj(name):
        start, width = segs[name]
        return _dot(h, w_ref[:, start:start + width])

    seg = min(tm, seq_len)
    n_seg = tm // seg
    ext = seg + 2 * HALO
    m = ext // 8
    pad = (CONV_K // 2) * 8
    qkv_w = 3 * A_W
    units = [(s, c) for s in range(n_seg) for c in range(qkv_w // CONV_SLAB)]
    def put_rows(lo, hi, val):
        for ct in range(D_MODEL // LANES):
            hs[ct, lo:hi, :] = val[:, ct * LANES:(ct + 1) * LANES]

    if seq_len > tm:
        tiles = seq_len // tm
        pos = pl.program_id(0) % tiles
        put_rows(0, HALO, _mod_norm(xp_ref[...], gain_ref[...], shift, scale) * jnp.where(pos > 0, 1.0, 0.0))
        put_rows(HALO + seg, ext,
                 _mod_norm(xn_ref[...], gain_ref[...], shift, scale) * jnp.where(pos < tiles - 1, 1.0, 0.0))
    else:
        for s in range(n_seg):
            put_rows(s * ext, s * ext + HALO, jnp.zeros((HALO, D_MODEL), F32))
            put_rows(s * ext + HALO + seg, (s + 1) * ext, jnp.zeros((HALO, D_MODEL), F32))
    for s in range(n_seg):
        put_rows(s * ext + HALO, s * ext + HALO + seg, hf[s * seg:(s + 1) * seg])
    h_il = [jnp.concatenate(
        [jnp.concatenate([hs[ct, pl.ds(s * ext + v, 8, stride=m), :] for v in range(m)], axis=0)
         for ct in range(D_MODEL // LANES)], axis=1).astype(BF16) for s in range(n_seg)]

    def project(s, c):
        cs = slice(c * CONV_SLAB, (c + 1) * CONV_SLAB)
        base = s * (ext + 2 * pad)
        res = _dot(h_il[s], w_ref[:, cs])
        xpad[base + pad:base + pad + ext, cs] = res
        for j in range(CONV_K // 2):
            xpad[base + 8 * j:base + 8 * (j + 1), cs] = pltpu.roll(res[ext - pad + 8 * j:ext - pad + 8 * (j + 1)], 1, 0)
            xpad[base + pad + ext + 8 * j:base + pad + ext + 8 * (j + 1), cs] = pltpu.roll(res[8 * j:8 * (j + 1)], 7, 0)

    def convolve(s, c):
        cs = slice(c * CONV_SLAB, (c + 1) * CONV_SLAB)
        base = s * (ext + 2 * pad)
        acc = None
        for tap in range(CONV_K):
            term = xpad[base + 8 * tap:base + 8 * tap + ext, cs] * cw_ref[tap:tap + 1, cs]
            acc = term if acc is None else acc + term
        y = _silu(acc)
        for j in range(CONV_SLAB // LANES):
            ct = c * (CONV_SLAB // LANES) + j
            for v in range(m):
                ybuf[ct, pl.ds(v, 8, stride=m), :] = y[8 * v:8 * (v + 1), j * LANES:(j + 1) * LANES]
            qkv_ref[s * seg:(s + 1) * seg, ct * LANES:(ct + 1) * LANES] = ybuf[ct, HALO:HALO + seg, :]

    z_ref[...] = proj("z")
    bq_ref[...] = proj("bq")
    zk = proj("bk")
    zv = proj("bv")
    bk_ref[...] = zk
    bv_ref[...] = zv
    if emit_maps:
        bk8_ref, bv4_ref = refs[15:17]
        for r in range(2 * H_B):
            bk8_ref[:, r, :] = zk[:, r * DQK_B:(r + 1) * DQK_B]
        for r in range(H_B):
            bv4_ref[:, r, :] = zv[:, r * DV_B:(r + 1) * DV_B]
    for unit in units:
        project(*unit)
        convolve(*unit)

    gpar = gpar_ref[...]
    dec_start = segs["dec"][0]
    gate_pre = _dot(h, w_ref[:, dec_start:dec_start + 2 * LANES])
    xg = gate_pre[:, :LANES] + gpar[1:2, :]
    softplus = jnp.maximum(xg, 0.0) + jnp.log1p(jnp.exp(-jnp.abs(xg)))
    lane = lax.broadcasted_iota(jnp.int32, (tm, LANES), 1)
    g = jnp.where(lane < 2 * H_A, -jnp.exp(gpar[0:1, :]) * softplus, 0.0)
    beta_ref[...] = jax.nn.sigmoid(gate_pre[:, LANES:])
    pos_c = lax.broadcasted_iota(jnp.int32, (tm, 1), 0) & (CHUNK - 1)
    gcf = g
    gcb = g
    for sft in (1, 2, 4, 8, 16, 32):
        gcf = gcf + jnp.where(pos_c >= sft, pltpu.roll(gcf, sft, axis=0), 0.0)
        gcb = gcb + jnp.where(pos_c < CHUNK - sft, pltpu.roll(gcb, tm - sft, axis=0), 0.0)
    gcs_ref[...] = jnp.where(lane < H_A, gcf, gcb)


def _inproj(x, mod, gain, w_pack, conv_w, gpar, rows_per_cond, seq_len, emit_maps):
    t = x.shape[0]
    tm = TOKEN_TILE
    hb = tm // HALO
    n_seg = max(1, tm // seq_len)
    ext = tm // n_seg + 2 * HALO
    row = pl.BlockSpec((tm, D_MODEL), lambda i: (i, 0))
    prev = pl.BlockSpec((HALO, D_MODEL), lambda i: (jnp.maximum(i * hb - 1, 0), 0))
    nxt = pl.BlockSpec((HALO, D_MODEL), lambda i: (jnp.minimum((i + 1) * hb, t // HALO - 1), 0))
    widths = [w for _, _, w in _IN_SEGS]
    out_specs = [pl.BlockSpec((tm, w), lambda i: (i, 0)) for w in widths]
    out_shape = [jax.ShapeDtypeStruct((t, w), F32) for w in widths]
    if emit_maps:
        out_specs += [pl.BlockSpec((tm, 2 * H_B, DQK_B), lambda i: (i, 0, 0)),
                      pl.BlockSpec((tm, H_B, DV_B), lambda i: (i, 0, 0))]
        out_shape += [jax.ShapeDtypeStruct((t, 2 * H_B, DQK_B), F32), jax.ShapeDtypeStruct((t, H_B, DV_B), F32)]
    return pl.pallas_call(
        functools.partial(_inproj_kernel, tm=tm, seq_len=seq_len, emit_maps=emit_maps),
        grid=(t // tm,),
        in_specs=[row, prev, nxt, _mod_spec(rows_per_cond, tm, lambda s: s), _resident((1, D_MODEL)),
                  _resident((D_MODEL, IN_PACK_COLS)), _resident((CONV_K, 3 * A_W)), _resident((8, LANES))],
        out_specs=out_specs,
        out_shape=out_shape,
        scratch_shapes=[pltpu.VMEM((D_MODEL // LANES, n_seg * ext, LANES), F32),
                        pltpu.VMEM((n_seg * (ext + 2 * (CONV_K // 2) * 8), 3 * A_W), F32),
                        pltpu.VMEM((3 * A_W // LANES, ext, LANES), F32)],
        compiler_params=_cparams(("arbitrary",)),
        name="inproj",
    )(x, x, x, mod, gain, w_pack, conv_w, gpar)


def _post_kernel(*refs, cast):
    x_ref, oa_ref, ob_ref, mod_ref, gain_ref, gfin_ref, wmix_ref, w_in_ref, w_out_ref, o_ref = refs[:10]

    def tokens(w_in, w_out):
        y = _dot(oa_ref[...].astype(BF16), wmix_ref[0:A_W, :]) + _dot(ob_ref[...].astype(BF16), wmix_ref[A_W:, :])
        x = x_ref[...] + mod_ref[5:6, :] * y
        x = _swiglu_update(x, mod_ref, 6, gain_ref[...], w_in, w_out)
        o_ref[...] = (x * lax.rsqrt(jnp.mean(x * x, axis=-1, keepdims=True) + EPS)) * gfin_ref[...]

    if cast:
        step = pl.program_id(0)
        w_in_s, w_out_s = refs[12:14]
        _stage_ffn_weights(step, w_in_ref, w_out_ref, refs[10], refs[11], w_in_s, w_out_s)
        pl.when(step >= W_CHUNKS)(lambda: tokens(w_in_s, w_out_s))
    else:
        tokens(w_in_ref, w_out_ref)


def _post(x, oa, ob, mod, gain, gfin, w_mix, w_in, w_out, rows_per_cond):
    t = x.shape[0]
    tm = TOKEN_TILE
    cast = w_in.dtype == F32
    tile = _tile_of(cast)
    row = pl.BlockSpec((tm, D_MODEL), lambda s: (tile(s), 0))
    half = pl.BlockSpec((tm, A_W), lambda s: (tile(s), 0))
    w_in_specs, w_out_specs, w_shapes, w_scratch = _ffn_weight_plumbing(cast)
    outs = pl.pallas_call(
        functools.partial(_post_kernel, cast=cast),
        grid=(t // tm + (W_CHUNKS if cast else 0),),
        in_specs=[row, half, half, _mod_spec(rows_per_cond, tm, tile), _resident((1, D_MODEL)),
                  _resident((1, D_MODEL)), _resident((A_W + B_W, D_MODEL))] + w_in_specs,
        out_specs=[row] + w_out_specs,
        out_shape=[jax.ShapeDtypeStruct((t, D_MODEL), F32)] + w_shapes,
        scratch_shapes=w_scratch,
        compiler_params=_cparams(("arbitrary",)),
        name="post",
    )(x, oa, ob, mod, gain, gfin, w_mix, w_in, w_out)
    return (outs[0], outs[1], outs[2]) if cast else (outs[0], w_in, w_out)


def _rope(x, cos, sin_signed):
    w = x.shape[-1]
    lane = lax.broadcasted_iota(jnp.int32, x.shape, 1)
    partner = jnp.where((lane & 31) < 16, pltpu.roll(x, w - 16, axis=1), pltpu.roll(x, 16, axis=1))
    return x * cos + partner * sin_signed


def _store_values(v_s, rows, v):
    for h in range(H_B):
        vh = v[:, h * DV_B:(h + 1) * DV_B]
        v_s[rows, 2 * h * DV_B:(2 * h + 1) * DV_B] = vh.astype(BF16)
        v_s[rows, (2 * h + 1) * DV_B:(2 * h + 2) * DV_B] = jnp.ones(vh.shape, BF16)


def _attn_core(q, k_ref, v_ref, lam, gain_ref, o_ref):
    tq = q.shape[0]
    lane = lax.broadcasted_iota(jnp.int32, (tq, DV_B), 1)
    qs = q * (DQK_B ** -0.5)
    heads = [slice(h * DV_B, (h + 1) * DV_B) for h in range(H_B)]
    scores = []
    for hs in heads:
        qh = qs[:, hs]
        q2 = jnp.concatenate([jnp.where(lane < DQK_B, qh, 0.0), jnp.where(lane >= DQK_B, qh, 0.0)], axis=0)
        scores.append(lax.dot_general(q2.astype(BF16), k_ref[:, hs], (((1,), (1,)), ((), ())),
                                      preferred_element_type=F32))
    for h, (hs, s) in enumerate(zip(heads, scores)):
        e = jnp.exp(s - jnp.max(s, axis=-1, keepdims=True))
        rs = _dot(e.astype(BF16), v_ref[:, 2 * h * DV_B:(2 * h + 2) * DV_B])
        r = rs[:, :DV_B] / rs[:, DV_B:]
        o = r[:tq] - lam * r[tq:]
        o = o * lax.rsqrt(jnp.mean(o * o, axis=-1, keepdims=True) + EPS)
        o_ref[:, hs] = (o * gain_ref[...]) * (1.0 - LAM_INIT)


def _lambda(lam_ref):
    l = lam_ref[...]
    s1 = jnp.sum(l[0:1] * l[1:2], axis=-1, keepdims=True)
    s2 = jnp.sum(l[2:3] * l[3:4], axis=-1, keepdims=True)
    return jnp.exp(s1) - jnp.exp(s2) + LAM_INIT


CTX_SEQS = 4


def _attn_ctx_kernel(q_ref, k_ref, v_ref, lam_ref, gain_ref, o_ref, k_s, v_s):
    lam = _lambda(lam_ref)

    def one_sequence(bi, carry):
        k_s[...] = k_ref[bi].astype(BF16)
        _store_values(v_s, slice(None), v_ref[bi])
        _attn_core(q_ref[bi], k_s, v_s, lam, gain_ref, o_ref.at[bi])
        return carry

    lax.fori_loop(0, CTX_SEQS, one_sequence, 0)


def _attn_ctx(q, k, v, lam_pack, gain):
    b, l, _ = q.shape
    blk = pl.BlockSpec((CTX_SEQS, l, B_W), lambda i: (i, 0, 0))
    return pl.pallas_call(
        _attn_ctx_kernel,
        grid=(b // CTX_SEQS,),
        in_specs=[blk, blk, blk, _resident((8, LANES)), _resident((1, DV_B))],
        out_specs=blk,
        out_shape=jax.ShapeDtypeStruct((b, l, B_W), F32),
        scratch_shapes=[pltpu.VMEM((l, B_W), BF16), pltpu.VMEM((l, 2 * B_W), BF16)],
        compiler_params=_cparams(("arbitrary",)),
        name="attn_ctx",
    )(q, k, v, lam_pack, gain)


def _attn_lat_kernel(q_ref, k_ref, v_ref, ck_ref, cv_ref, cosq_ref, sinq_ref, cos_ref, sin_ref,
                     lam_ref, gain_ref, o_ref, k_s, v_s, *, past):
    @pl.when(pl.program_id(1) == 0)
    def _():
        k_s[0:past, :] = ck_ref[...].astype(BF16)
        _store_values(v_s, slice(0, past), cv_ref[...])
        k_s[past:, :] = _rope(k_ref[...], cos_ref[...], sin_ref[...]).astype(BF16)
        _store_values(v_s, slice(past, None), v_ref[...])

    q = _rope(q_ref[...], cosq_ref[...], sinq_ref[...])
    _attn_core(q, k_s, v_s, _lambda(lam_ref), gain_ref, o_ref)


def _attn_lat(q, k, v, ck, cv, cos, sin, lam_pack, gain):
    b, l, _ = q.shape
    past = ck.shape[1]
    tq = Q_TILE
    full = pl.BlockSpec((None, l, B_W), lambda i, j: (i, 0, 0))
    cache_k = cache_v = pl.BlockSpec((None, past, B_W), lambda i, j: (i, 0, 0))
    qblk = pl.BlockSpec((None, tq, B_W), lambda i, j: (i, j, 0))
    tab_q = pl.BlockSpec((tq, B_W), lambda i, j: (j, 0))
    tab = pl.BlockSpec((l, B_W), lambda i, j: (0, 0))
    return pl.pallas_call(
        functools.partial(_attn_lat_kernel, past=past),
        grid=(b, l // tq),
        in_specs=[qblk, full, full, cache_k, cache_v, tab_q, tab_q, tab, tab,
                  pl.BlockSpec((8, LANES), lambda i, j: (0, 0)), pl.BlockSpec((1, DV_B), lambda i, j: (0, 0))],
        out_specs=qblk,
        out_shape=jax.ShapeDtypeStruct((b, l, B_W), F32),
        scratch_shapes=[pltpu.VMEM((past + l, B_W), BF16), pltpu.VMEM((past + l, 2 * B_W), BF16)],
        compiler_params=_cparams(("arbitrary", "arbitrary")),
        name="attn_lat",
    )(q, k, v, ck, cv, cos, sin, cos, sin, lam_pack, gain)


def _rope_tables(length):
    pairs = DQK_B // 4
    n_rows = length // GRID_W
    pos_row = jnp.repeat(jnp.arange(n_rows), GRID_W).astype(F32)
    pos_col = jnp.tile(jnp.arange(GRID_W), n_rows).astype(F32)
    inv = ROPE_THETA ** (-jnp.arange(pairs, dtype=F32) / pairs)
    ang_r = pos_row[:, None] * inv
    ang_c = pos_col[:, None] * inv
    cos = jnp.concatenate([jnp.cos(ang_r), jnp.cos(ang_r), jnp.cos(ang_c), jnp.cos(ang_c)], axis=-1)
    sin = jnp.concatenate([-jnp.sin(ang_r), jnp.sin(ang_r), -jnp.sin(ang_c), jnp.sin(ang_c)], axis=-1)
    reps = B_W // DQK_B
    return jnp.tile(cos, (1, reps)), jnp.tile(sin, (1, reps))


def _block_diag(x):
    lo, hi = x[:, :LANES], x[:, LANES:]
    first = lax.broadcasted_iota(jnp.int32, lo.shape, 1) < DK_A
    top = jnp.concatenate([jnp.where(first, lo, 0), jnp.where(first, 0, lo)], axis=0)
    bot = jnp.concatenate([jnp.where(first, hi, 0), jnp.where(first, 0, hi)], axis=0)
    zero = jnp.zeros_like(top)
    return jnp.concatenate([jnp.concatenate([top, zero], axis=1), jnp.concatenate([zero, bot], axis=1)], axis=0)


SOLVE_CHUNKS = 8
DELTA_ROWS = 1024


def _delta_kernel(*refs, nb, length, has_state):
    if has_state:
        (q_ref, k_ref, v_ref, z_ref, gcs_ref, beta_ref, dn_ref, s0_ref, o_ref, st_ref, *scr) = refs
    else:
        (q_ref, k_ref, v_ref, z_ref, gcs_ref, beta_ref, dn_ref, o_ref, st_ref, *scr) = refs
        s0_ref = None
    qs, ks, vs, gexp, bexp, u0_s, kg_s, dec_s, w_s, qg_s, qkm_s, of_s, ob_s, s_s = scr
    L = length
    R = nb * L
    n_chunks = L // CHUNK
    grp = pl.program_id(1)
    W = GROUP_W
    N_GRP = A_W // W

    r256 = lax.broadcasted_iota(jnp.int32, (W, W), 0)
    c256 = lax.broadcasted_iota(jnp.int32, (W, W), 1)
    bd_mask = (r256 >> 6) == (c256 >> 6)
    ones_bd = jnp.where(bd_mask, 1.0, 0.0).astype(BF16)
    ri = lax.broadcasted_iota(jnp.int32, (CHUNK, W), 0)
    cj = lax.broadcasted_iota(jnp.int32, (CHUNK, W), 1) & (CHUNK - 1)
    eye_p = ri == cj
    er = lax.broadcasted_iota(jnp.int32, (LANES, 2 * W), 0)
    ec = lax.broadcasted_iota(jnp.int32, (LANES, 2 * W), 1)
    expand = jnp.where(er == (ec >> 8) * H_A + grp * 4 + ((ec & (W - 1)) >> 6), 1.0, 0.0).astype(BF16)

    bd = _block_diag

    def prologue(bi, carry):
        rows_b = pl.ds(pl.multiple_of(bi * L, L), L)

        def l2n(y):
            return y * lax.rsqrt(_dot_split(y * y, ones_bd) + EPS)

        qs[rows_b, :] = l2n(q_ref[bi]) * (DK_A ** -0.5)
        ks[rows_b, :] = l2n(k_ref[bi])
        vs[rows_b, :] = v_ref[bi]
        ge_all = _dot_split(gcs_ref[bi], expand)
        be_all = _dot_split(beta_ref[bi], expand)
        for d in range(2):
            gexp[d, rows_b, :] = ge_all[:, d * W:(d + 1) * W]
            bexp[d, rows_b, :] = be_all[:, d * W:(d + 1) * W]
        return carry

    lax.fori_loop(0, nb, prologue, 0)

    def solve_body(j, carry):
        chunks = []
        for k in range(SOLVE_CHUNKS):
            chunk = j * SOLVE_CHUNKS + k
            rows = pl.ds(pl.multiple_of(chunk * CHUNK, CHUNK), CHUNK)
            kc = ks[rows, :]
            qc = qs[rows, :]
            kb = kc.astype(BF16)
            res = lax.dot_general(jnp.concatenate([qc.astype(BF16), kb], axis=0), bd(kb),
                                  (((1,), (1,)), ((), ())), preferred_element_type=F32)
            qk, kk = res[:CHUNK], res[CHUNK:]
            ch = dict(rows=rows, bd_k=bd(kb), bd_v=bd(vs[rows, :].astype(BF16)), t=[], scale_u=[], scale_w=[])
            nsum = None
            for d in range(2):
                ge = gexp[d, rows, :]
                be = bexp[d, rows, :]
                if d == 0:
                    incl, strict, gl = ri >= cj, ri > cj, ge[CHUNK - 1:CHUNK, :]
                else:
                    incl, strict, gl = ri <= cj, ri < cj, ge[0:1, :]
                g_row = jnp.sum(jnp.where(eye_p, ge, 0.0), axis=0, keepdims=True)
                b_row = jnp.sum(jnp.where(eye_p, be, 0.0), axis=0, keepdims=True)
                dec_m = jnp.exp(jnp.where(incl, ge - g_row, -jnp.inf))
                nmat = jnp.where(strict, be * kk * dec_m, 0.0)
                nsum = nmat if nsum is None else nsum + nmat
                qkm_s[grp, d, rows, :] = jnp.where(incl, qk * dec_m, 0.0).astype(BF16)
                qg_s[grp, d, rows, :] = (qc * jnp.exp(ge)).astype(BF16)
                kg_s[grp, d, rows, :] = kc * jnp.exp(gl - ge)
                dec_s[grp, d, pl.ds(pl.multiple_of(chunk * 8, 8), 8), :] = jnp.broadcast_to(jnp.exp(gl), (8, W))
                ch["scale_u"].append(b_row)
                ch["scale_w"].append(b_row * jnp.exp(g_row))
                ch["t"].append(jnp.where(eye_p, 1.0, 0.0) - jnp.where((ri >> 1) == (cj >> 1), nmat, 0.0))
            ch["n_both"] = nsum.astype(BF16)
            chunks.append(ch)

        for lb in range(1, 6):
            same = (ri >> (lb + 1)) == (cj >> (lb + 1))
            pair = (same & (((ri >> lb) & 1) == 1) & (((cj >> lb) & 1) == 0),
                    same & (((cj >> lb) & 1) == 1) & (((ri >> lb) & 1) == 0))
            for ch in chunks:
                ch["tb"] = [t.astype(BF16) for t in ch["t"]]
                rhs = bd(jnp.where(pair[0] | pair[1], ch["n_both"], 0))
                ch["y"] = _dot(jnp.concatenate(ch["tb"], axis=0), rhs).astype(BF16)
            for ch in chunks:
                ys = (jnp.where(pair[0], ch["y"][:CHUNK], 0), jnp.where(pair[1], ch["y"][CHUNK:], 0))
                ch["t"] = [ch["t"][d] - _dot(ys[d], bd(ch["tb"][d])) for d in range(2)]
        for ch in chunks:
            tu = jnp.concatenate([(ch["t"][d] * ch["scale_u"][d]).astype(BF16) for d in range(2)], axis=0)
            tw = jnp.concatenate([(ch["t"][d] * ch["scale_w"][d]).astype(BF16) for d in range(2)], axis=0)
            u0 = _dot(tu, ch["bd_v"])
            w = _dot(tw, ch["bd_k"]).astype(BF16)
            for d in range(2):
                u0_s[grp, d, ch["rows"], :] = u0[d * CHUNK:(d + 1) * CHUNK]
                w_s[grp, d, ch["rows"], :] = w[d * CHUNK:(d + 1) * CHUNK]
        return carry

    lax.fori_loop(0, R // (SOLVE_CHUNKS * CHUNK), solve_body, 0)

    s_s[grp] = jnp.zeros(s_s.shape[1:], F32)
    if has_state:
        for bi in range(nb):
            for d in range(2):
                for h in range(4):
                    s_s[grp, bi, d, h * DK_A:(h + 1) * DK_A, h * DK_A:(h + 1) * DK_A] = s0_ref[bi, d, h]

    def scan_body(i, carry):
        chains = []
        for gi in range(N_GRP):
            for bi in range(nb):
                for d in range(2):
                    c = i if d == 0 else n_chunks - 1 - i
                    rows = pl.ds(pl.multiple_of(bi * L + c * CHUNK, CHUNK), CHUNK)
                    tile = pl.ds(pl.multiple_of((bi * n_chunks + c) * 8, 8), 8)
                    chains.append(dict(
                        gi=gi, bi=bi, d=d, rows=rows, s=s_s[gi, bi, d], u0=u0_s[gi, d, rows, :],
                        kg=kg_s[gi, d, rows, :], qkm=qkm_s[gi, d, rows, :], decay=dec_s[gi, d, tile, :][0:1],
                        wq=jnp.concatenate([w_s[gi, d, rows, :], qg_s[gi, d, rows, :]], axis=0)))
        for c in chains:
            c["ws"] = _dot(c["wq"], c["s"].astype(BF16))
        for c in chains:
            c["ub"] = (c["u0"] - c["ws"][:CHUNK]).astype(BF16)
        for c in chains:
            c["o"] = c["ws"][CHUNK:] + _dot(c["qkm"], bd(c["ub"]))
            upd = _dot(c["kg"].T.astype(BF16), c["ub"])
            c["s_new"] = c["s"] * c["decay"] + jnp.where(bd_mask, upd, 0.0)
        for c in chains:
            s_s[c["gi"], c["bi"], c["d"]] = c["s_new"]
            if c["d"] == 0:
                of_s[c["gi"], c["rows"], :] = c["o"]
            else:
                ob_s[c["gi"], c["rows"], :] = c["o"]
        return carry

    def epilogue(bi, carry):
        rows_b = pl.ds(pl.multiple_of(bi * L, L), L)
        for gi in range(N_GRP):
            lanes = slice(gi * W, (gi + 1) * W)
            o = of_s[gi, rows_b, :] + ob_s[gi, rows_b, :]
            ms = _dot_split(o * o, ones_bd) * (1.0 / DK_A)
            o = o * lax.rsqrt(ms + EPS) * dn_ref[...]
            o_ref[bi, :, lanes] = o * _silu(z_ref[bi, :, lanes])
        return carry

    @pl.when(grp == N_GRP - 1)
    def _():
        lax.fori_loop(0, n_chunks, scan_body, 0)
        lax.fori_loop(0, nb, epilogue, 0)
        for gi in range(N_GRP):
            for bi in range(nb):
                for d in range(2):
                    for h in range(4):
                        st_ref[bi, d, gi * 4 + h] = s_s[gi, bi, d, h * DK_A:(h + 1) * DK_A, h * DK_A:(h + 1) * DK_A]


def _delta(qkv, z, gcs, beta, dnorm, s0):
    b, l, _ = qkv.shape
    has_state = s0 is not None
    w = GROUP_W
    n_grp = A_W // w
    nb = DELTA_ROWS // l
    r = nb * l

    def slab(k):
        return pl.BlockSpec((nb, l, w), lambda i, g: (i, 0, k * n_grp + g))

    narrow = pl.BlockSpec((nb, l, LANES), lambda i, g: (i, 0, 0))
    wide = pl.BlockSpec((nb, l, A_W), lambda i, g: (i, 0, 0))
    in_specs = [slab(0), slab(1), slab(2), wide, narrow, narrow,
                pl.BlockSpec((1, w), lambda i, g: (0, 0))]
    args = [qkv, qkv, qkv, z, gcs, beta, dnorm]
    if has_state:
        in_specs.append(pl.BlockSpec((nb, None, 2, 4, DK_A, DK_A), lambda i, g: (i, 0, 0, g, 0, 0)))
        args.append(s0)
    per_grp = lambda shape, dt: pltpu.VMEM((n_grp,) + shape, dt)
    scratch = ([pltpu.VMEM((r, w), F32)] * 3 + [pltpu.VMEM((2, r, w), F32)] * 2
               + [per_grp((2, r, w), F32)] * 2 + [per_grp((2, r // CHUNK * 8, w), F32)]
               + [per_grp((2, r, w), BF16)] * 3 + [per_grp((r, w), F32)] * 2
               + [per_grp((nb, 2, w, w), F32)])
    return pl.pallas_call(
        functools.partial(_delta_kernel, nb=nb, length=l, has_state=has_state),
        grid=(b // nb, n_grp),
        in_specs=in_specs,
        out_specs=[wide, pl.BlockSpec((nb, None, 2, H_A, DK_A, DK_A), lambda i, g: (i, 0, 0, 0, 0, 0))],
        out_shape=[jax.ShapeDtypeStruct((b, l, A_W), F32),
                   jax.ShapeDtypeStruct((b, 1, 2, H_A, DK_A, DK_A), F32)],
        scratch_shapes=scratch,
        compiler_params=_cparams(("arbitrary", "arbitrary")),
        name="delta_state" if has_state else "delta",
    )(*args)


def _pad_lanes(x, width):
    return jnp.pad(x, [(0, 0)] * (x.ndim - 1) + [(0, width - x.shape[-1])])


def kernel(x_prompt, x_sample, cache_diff_k, cache_diff_v, state_delta, c, c_ctx, w_ada, b_ada, norm_ffn1,
           w_ffn1_in, w_ffn1_out, norm_mix, w_in, conv_w, a_log, dt_bias, delta_norm, lambda_q1, lambda_k1,
           lambda_q2, lambda_k2, diff_norm, w_out, norm_ffn2, w_ffn2_in, w_ffn2_out, norm_final):
    bp, lp, _ = x_prompt.shape
    bs, ls, _ = x_sample.shape
    past = cache_diff_k.shape[2]

    w_mix_out = w_out[0].astype(BF16)
    wi = w_in[0]
    a_end = 4 * A_W
    dec_w = wi[:, a_end:a_end + 2 * H_A]
    bet_w = wi[:, a_end + 2 * H_A:a_end + 4 * H_A]
    rest = wi[:, a_end + 4 * H_A:]
    w_pack = jnp.concatenate([wi[:, :a_end], rest, _pad_lanes(dec_w, LANES), _pad_lanes(bet_w, LANES)],
                             axis=1).astype(BF16)
    g1 = norm_ffn1[0].reshape(1, D_MODEL)
    gm = norm_mix[0].reshape(1, D_MODEL)
    g2 = norm_ffn2[0].reshape(1, D_MODEL)
    gf = norm_final.reshape(1, D_MODEL)
    gpar = jnp.zeros((8, LANES), F32)
    gpar = gpar.at[0, :2 * H_A].set(a_log[0].reshape(-1)).at[1, :2 * H_A].set(dt_bias[0].reshape(-1))
    dnorm = jnp.tile(delta_norm[0], 4).reshape(1, GROUP_W)
    lam_pack = jnp.zeros((8, LANES), F32)
    for r, v in enumerate((lambda_q1, lambda_k1, lambda_q2, lambda_k2)):
        lam_pack = lam_pack.at[r, :DQK_B].set(v[0])
    dgain = diff_norm[0].reshape(1, DV_B)
    cos, sin = _rope_tables(ls)

    cond = jnp.concatenate([c_ctx[None, :], c, jnp.zeros((8 - 1 - bs, D_MODEL), F32)], axis=0)
    mod = _adaln(cond, w_ada[0], b_ada[0]).reshape(8, N_MOD, D_MODEL)

    def trunk(x3, rows_per_cond, ctx, w1, w2):
        b, l, _ = x3.shape
        x = x3.reshape(b * l, D_MODEL)
        x1, *w1 = _ffn1(x, mod, g1, *w1, rows_per_cond)
        outs = _inproj(x1, mod, gm, w_pack, conv_w[0], gpar, rows_per_cond, l, ctx is None)
        qkv, z, bq, bk, bv, gcs, beta = outs[:7]
        r3 = lambda a: a.reshape(b, l, a.shape[-1])
        s0 = None if ctx is None else ctx[2]
        oa, st = _delta(r3(qkv), r3(z), r3(gcs), r3(beta), dnorm, s0)
        if ctx is None:
            ob = _attn_ctx(r3(bq), r3(bk), r3(bv), lam_pack, dgain)
        else:
            ob = _attn_lat(r3(bq), r3(bk), r3(bv), ctx[0], ctx[1], cos, sin, lam_pack, dgain)
        y, *w2 = _post(x1, oa.reshape(b * l, A_W), ob.reshape(b * l, B_W), mod, g2, gf, w_mix_out, *w2,
                       rows_per_cond)
        return y.reshape(b, l, D_MODEL), outs[7:], st, w1, w2

    y_prompt, (k_maps, v_heads), s_ctx, w1, w2 = trunk(x_prompt, None, None, (w_ffn1_in[0], w_ffn1_out[0]),
                                                       (w_ffn2_in[0], w_ffn2_out[0]))
    ctx = (cache_diff_k[:, 0].reshape(bs, past, B_W), cache_diff_v[:, 0].reshape(bs, past, B_W), state_delta)
    y_sample = trunk(x_sample, ls, ctx, w1, w2)[0]
    new_k = k_maps.reshape(bp, 1, lp, H_B, 2, DQK_B)
    new_v = v_heads.reshape(bp, 1, lp, H_B, DV_B)
    return (y_prompt, y_sample, new_k, new_v, s_ctx)
```

```python
import functools
import math

import jax
import jax.numpy as jnp
from jax import lax
from jax.experimental import pallas as pl
from jax.experimental.pallas import tpu as pltpu

F32 = jnp.float32
BF16 = jnp.bfloat16

D_MODEL = 1024
D_FF = 2816
N_MOD = 9
H_A = 8
DK_A = 64
A_W = H_A * DK_A
H_B = 4
DQK_B = 64
DV_B = 128
B_W = H_B * DV_B
CONV_K = 5
CHUNK = 64
GRID_W = 64
ROPE_THETA = 10000.0
EPS = 1e-6
LAM_INIT = 0.8 - 0.6 * math.exp(-0.3 * 0)

LANES = 128
GROUP_W = 4 * DK_A
VMEM_LIMIT = 56 * 1024 * 1024

MXU_W = 256
TOKEN_TILE = 512
FF_CHUNKS = ((0, 6 * MXU_W), (6 * MXU_W, D_FF))
Q_TILE = 256


def _cparams(sem):
    return pltpu.CompilerParams(dimension_semantics=sem, vmem_limit_bytes=VMEM_LIMIT)


def _resident(shape):
    nd = len(shape)
    return pl.BlockSpec(shape, lambda *_: (0,) * nd, pipeline_mode=pl.Buffered(1))


def _silu(x):
    half = 0.5 * x
    return half * jnp.tanh(half) + half


def _mod_norm(x, gain, shift, scale):
    y = x * lax.rsqrt(jnp.mean(x * x, axis=-1, keepdims=True) + EPS)
    return (y * gain) * (1.0 + scale) + shift


def _dot(a, b):
    return jnp.dot(a, b, preferred_element_type=F32)


def _dot_split(a, b_bf16):
    hi = a.astype(BF16)
    lo = (a - hi.astype(F32)).astype(BF16)
    return _dot(hi, b_bf16) + _dot(lo, b_bf16)


def _adaln_kernel(c_ref, w_ref, b_ref, o_ref):
    s = _silu(c_ref[...])
    o_ref[...] = _dot(s.astype(BF16), w_ref[...].astype(BF16)) + b_ref[...]


def _adaln(cond, w_ada, b_ada):
    n = N_MOD * D_MODEL
    tn = n // 4
    return pl.pallas_call(
        _adaln_kernel,
        grid=(n // tn,),
        in_specs=[pl.BlockSpec((8, D_MODEL), lambda j: (0, 0)),
                  pl.BlockSpec((D_MODEL, tn), lambda j: (0, j)),
                  pl.BlockSpec((1, tn), lambda j: (0, j))],
        out_specs=pl.BlockSpec((8, tn), lambda j: (0, j)),
        out_shape=jax.ShapeDtypeStruct((8, n), F32),
        compiler_params=_cparams(("arbitrary",)),
        name="adaln",
    )(cond, w_ada, b_ada.reshape(1, n))


def _swiglu_update(x, mod_ref, k0, gain, w_in_ref, w_out_ref):
    h = _mod_norm(x, gain, mod_ref[k0:k0 + 1, :], mod_ref[k0 + 1:k0 + 2, :]).astype(BF16)
    acc = None
    for lo, hi in FF_CHUNKS:
        g = _dot(h, w_in_ref[:, lo:hi])
        u = _dot(h, w_in_ref[:, D_FF + lo:D_FF + hi])
        part = _dot((_silu(g) * u).astype(BF16), w_out_ref[lo:hi, :])
        acc = part if acc is None else acc + part
    return x + (0.5 * mod_ref[k0 + 2:k0 + 3, :]) * acc


W_CHUNKS = 11
W_IN_CHUNK = 2 * D_FF // W_CHUNKS
W_OUT_CHUNK = D_FF // W_CHUNKS


def _stage_ffn_weights(step, w_in_ref, w_out_ref, w_in_bf_ref, w_out_bf_ref, w_in_s, w_out_s):
    for c in range(W_CHUNKS):
        @pl.when(step == c)
        def _():
            a = w_in_ref[...].astype(BF16)
            b = w_out_ref[...].astype(BF16)
            w_in_bf_ref[...] = a
            w_out_bf_ref[...] = b
            w_in_s[:, c * W_IN_CHUNK:(c + 1) * W_IN_CHUNK] = a
            w_out_s[c * W_OUT_CHUNK:(c + 1) * W_OUT_CHUNK, :] = b


def _ffn_weight_plumbing(cast):
    if not cast:
        return [_resident((D_MODEL, 2 * D_FF)), _resident((D_FF, D_MODEL))], [], [], []
    last = W_CHUNKS - 1
    w_in_chunk = pl.BlockSpec((D_MODEL, W_IN_CHUNK), lambda s: (0, jnp.minimum(s, last)))
    w_out_chunk = pl.BlockSpec((W_OUT_CHUNK, D_MODEL), lambda s: (jnp.minimum(s, last), 0))
    out_shape = [jax.ShapeDtypeStruct((D_MODEL, 2 * D_FF), BF16), jax.ShapeDtypeStruct((D_FF, D_MODEL), BF16)]
    scratch = [pltpu.VMEM((D_MODEL, 2 * D_FF), BF16), pltpu.VMEM((D_FF, D_MODEL), BF16)]
    return [w_in_chunk, w_out_chunk], [w_in_chunk, w_out_chunk], out_shape, scratch


def _tile_of(cast):
    off = W_CHUNKS if cast else 0
    return lambda s: jnp.maximum(s - off, 0)


def _mod_spec(rows_per_cond, tm, tile):
    if rows_per_cond is None:
        return pl.BlockSpec((None, N_MOD, D_MODEL), lambda s: (0, 0, 0))
    return pl.BlockSpec((None, N_MOD, D_MODEL), lambda s: (1 + (tile(s) * tm) // rows_per_cond, 0, 0))


def _ffn1_kernel(*refs, cast):
    x_ref, mod_ref, gain_ref, w_in_ref, w_out_ref, o_ref = refs[:6]
    if cast:
        step = pl.program_id(0)
        w_in_s, w_out_s = refs[8:10]
        _stage_ffn_weights(step, w_in_ref, w_out_ref, refs[6], refs[7], w_in_s, w_out_s)

        @pl.when(step >= W_CHUNKS)
        def _():
            o_ref[...] = _swiglu_update(x_ref[...], mod_ref, 0, gain_ref[...], w_in_s, w_out_s)
    else:
        o_ref[...] = _swiglu_update(x_ref[...], mod_ref, 0, gain_ref[...], w_in_ref, w_out_ref)


def _ffn1(x, mod, gain, w_in, w_out, rows_per_cond):
    t = x.shape[0]
    tm = TOKEN_TILE
    cast = w_in.dtype == F32
    tile = _tile_of(cast)
    row = pl.BlockSpec((tm, D_MODEL), lambda s: (tile(s), 0))
    w_in_specs, w_out_specs, w_shapes, w_scratch = _ffn_weight_plumbing(cast)
    outs = pl.pallas_call(
        functools.partial(_ffn1_kernel, cast=cast),
        grid=(t // tm + (W_CHUNKS if cast else 0),),
        in_specs=[row, _mod_spec(rows_per_cond, tm, tile), _resident((1, D_MODEL))] + w_in_specs,
        out_specs=[row] + w_out_specs,
        out_shape=[jax.ShapeDtypeStruct((t, D_MODEL), F32)] + w_shapes,
        scratch_shapes=w_scratch,
        compiler_params=_cparams(("arbitrary",)),
        name="ffn1",
    )(x, mod, gain, w_in, w_out)
    return (outs[0], outs[1], outs[2]) if cast else (outs[0], w_in, w_out)


_IN_OUT_WIDTHS = (3 * A_W, A_W, B_W, B_W, B_W, LANES, LANES)


HALO = 8
CONV_SLAB = 256


def _inproj_kernel(*refs, tm, seq_len, emit_maps):
    x_ref, xp_ref, xn_ref, mod_ref, gain_ref, wa_ref, wb_ref, wg_ref, cw_ref, gpar_ref = refs[:10]
    qkv_ref, z_ref, bq_ref, bk_ref, bv_ref, gcs_ref, beta_ref = refs[10:17]
    hs, xpad, ybuf = refs[-3:]
    shift, scale = mod_ref[3:4, :], mod_ref[4:5, :]
    hf = _mod_norm(x_ref[...], gain_ref[...], shift, scale)
    h = hf.astype(BF16)

    seg = min(tm, seq_len)
    n_seg = tm // seg
    ext = seg + 2 * HALO
    m = ext // 8
    pad = (CONV_K // 2) * 8
    qkv_w = 3 * A_W
    units = [(s, c) for s in range(n_seg) for c in range(qkv_w // CONV_SLAB)]
    def put_rows(lo, hi, val):
        for ct in range(D_MODEL // LANES):
            hs[ct, lo:hi, :] = val[:, ct * LANES:(ct + 1) * LANES]

    if seq_len > tm:
        tiles = seq_len // tm
        pos = pl.program_id(0) % tiles
        put_rows(0, HALO, _mod_norm(xp_ref[...], gain_ref[...], shift, scale) * jnp.where(pos > 0, 1.0, 0.0))
        put_rows(HALO + seg, ext,
                 _mod_norm(xn_ref[...], gain_ref[...], shift, scale) * jnp.where(pos < tiles - 1, 1.0, 0.0))
    else:
        for s in range(n_seg):
            put_rows(s * ext, s * ext + HALO, jnp.zeros((HALO, D_MODEL), F32))
            put_rows(s * ext + HALO + seg, (s + 1) * ext, jnp.zeros((HALO, D_MODEL), F32))
    for s in range(n_seg):
        put_rows(s * ext + HALO, s * ext + HALO + seg, hf[s * seg:(s + 1) * seg])
    h_il = [jnp.concatenate(
        [jnp.concatenate([hs[ct, pl.ds(s * ext + v, 8, stride=m), :] for v in range(m)], axis=0)
         for ct in range(D_MODEL // LANES)], axis=1).astype(BF16) for s in range(n_seg)]

    def project(s, c):
        cs = slice(c * CONV_SLAB, (c + 1) * CONV_SLAB)
        base = s * (ext + 2 * pad)
        res = _dot(h_il[s], wa_ref[:, cs])
        xpad[base + pad:base + pad + ext, cs] = res
        for j in range(CONV_K // 2):
            xpad[base + 8 * j:base + 8 * (j + 1), cs] = pltpu.roll(res[ext - pad + 8 * j:ext - pad + 8 * (j + 1)], 1, 0)
            xpad[base + pad + ext + 8 * j:base + pad + ext + 8 * (j + 1), cs] = pltpu.roll(res[8 * j:8 * (j + 1)], 7, 0)

    def convolve(s, c):
        cs = slice(c * CONV_SLAB, (c + 1) * CONV_SLAB)
        base = s * (ext + 2 * pad)
        acc = None
        for tap in range(CONV_K):
            term = xpad[base + 8 * tap:base + 8 * tap + ext, cs] * cw_ref[tap:tap + 1, cs]
            acc = term if acc is None else acc + term
        y = _silu(acc)
        for j in range(CONV_SLAB // LANES):
            ct = c * (CONV_SLAB // LANES) + j
            for v in range(m):
                ybuf[ct, pl.ds(v, 8, stride=m), :] = y[8 * v:8 * (v + 1), j * LANES:(j + 1) * LANES]
            qkv_ref[s * seg:(s + 1) * seg, ct * LANES:(ct + 1) * LANES] = ybuf[ct, HALO:HALO + seg, :]

    z_ref[...] = _dot(h, wa_ref[:, qkv_w:qkv_w + A_W])
    bq_ref[...] = _dot(h, wb_ref[:, 0:B_W])
    zk = _dot(h, wb_ref[:, B_W:2 * B_W])
    zv = _dot(h, wb_ref[:, 2 * B_W:3 * B_W])
    bk_ref[...] = zk
    bv_ref[...] = zv
    if emit_maps:
        bk8_ref, bv4_ref = refs[17:19]
        for r in range(2 * H_B):
            bk8_ref[:, r, :] = zk[:, r * DQK_B:(r + 1) * DQK_B]
        for r in range(H_B):
            bv4_ref[:, r, :] = zv[:, r * DV_B:(r + 1) * DV_B]
    for unit in units:
        project(*unit)
        convolve(*unit)

    gpar = gpar_ref[...]
    gate_pre = _dot(h, wg_ref[...])
    xg = gate_pre[:, :LANES] + gpar[1:2, :]
    softplus = jnp.maximum(xg, 0.0) + jnp.log1p(jnp.exp(-jnp.abs(xg)))
    lane = lax.broadcasted_iota(jnp.int32, (tm, LANES), 1)
    g = jnp.where(lane < 2 * H_A, -jnp.exp(gpar[0:1, :]) * softplus, 0.0)
    beta_ref[...] = jax.nn.sigmoid(gate_pre[:, LANES:])
    pos_c = lax.broadcasted_iota(jnp.int32, (tm, 1), 0) & (CHUNK - 1)
    gcf = g
    gcb = g
    for sft in (1, 2, 4, 8, 16, 32):
        gcf = gcf + jnp.where(pos_c >= sft, pltpu.roll(gcf, sft, axis=0), 0.0)
        gcb = gcb + jnp.where(pos_c < CHUNK - sft, pltpu.roll(gcb, tm - sft, axis=0), 0.0)
    gcs_ref[...] = jnp.where(lane < H_A, gcf, gcb)


def _inproj(x, mod, gain, weights, conv_w, gpar, rows_per_cond, seq_len, emit_maps):
    t = x.shape[0]
    tm = TOKEN_TILE
    hb = tm // HALO
    n_seg = max(1, tm // seq_len)
    ext = tm // n_seg + 2 * HALO
    row = pl.BlockSpec((tm, D_MODEL), lambda i: (i, 0))
    prev = pl.BlockSpec((HALO, D_MODEL), lambda i: (jnp.maximum(i * hb - 1, 0), 0))
    nxt = pl.BlockSpec((HALO, D_MODEL), lambda i: (jnp.minimum((i + 1) * hb, t // HALO - 1), 0))
    out_specs = [pl.BlockSpec((tm, w), lambda i: (i, 0)) for w in _IN_OUT_WIDTHS]
    out_shape = [jax.ShapeDtypeStruct((t, w), F32) for w in _IN_OUT_WIDTHS]
    if emit_maps:
        out_specs += [pl.BlockSpec((tm, 2 * H_B, DQK_B), lambda i: (i, 0, 0)),
                      pl.BlockSpec((tm, H_B, DV_B), lambda i: (i, 0, 0))]
        out_shape += [jax.ShapeDtypeStruct((t, 2 * H_B, DQK_B), F32), jax.ShapeDtypeStruct((t, H_B, DV_B), F32)]
    return pl.pallas_call(
        functools.partial(_inproj_kernel, tm=tm, seq_len=seq_len, emit_maps=emit_maps),
        grid=(t // tm,),
        in_specs=[row, prev, nxt, _mod_spec(rows_per_cond, tm, lambda s: s), _resident((1, D_MODEL))]
        + [_resident(w.shape) for w in weights] + [_resident((CONV_K, 3 * A_W)), _resident((8, LANES))],
        out_specs=out_specs,
        out_shape=out_shape,
        scratch_shapes=[pltpu.VMEM((D_MODEL // LANES, n_seg * ext, LANES), F32),
                        pltpu.VMEM((n_seg * (ext + 2 * (CONV_K // 2) * 8), 3 * A_W), F32),
                        pltpu.VMEM((3 * A_W // LANES, ext, LANES), F32)],
        compiler_params=_cparams(("arbitrary",)),
        name="inproj",
    )(x, x, x, mod, gain, *weights, conv_w, gpar)


def _post_kernel(*refs, cast):
    x_ref, oa_ref, ob_ref, mod_ref, gain_ref, gfin_ref, wmix_ref, w_in_ref, w_out_ref, o_ref = refs[:10]

    def tokens(w_in, w_out):
        y = _dot(oa_ref[...].astype(BF16), wmix_ref[0:A_W, :]) + _dot(ob_ref[...].astype(BF16), wmix_ref[A_W:, :])
        x = x_ref[...] + mod_ref[5:6, :] * y
        x = _swiglu_update(x, mod_ref, 6, gain_ref[...], w_in, w_out)
        o_ref[...] = (x * lax.rsqrt(jnp.mean(x * x, axis=-1, keepdims=True) + EPS)) * gfin_ref[...]

    if cast:
        step = pl.program_id(0)
        w_in_s, w_out_s = refs[12:14]
        _stage_ffn_weights(step, w_in_ref, w_out_ref, refs[10], refs[11], w_in_s, w_out_s)
        pl.when(step >= W_CHUNKS)(lambda: tokens(w_in_s, w_out_s))
    else:
        tokens(w_in_ref, w_out_ref)


def _post(x, oa, ob, mod, gain, gfin, w_mix, w_in, w_out, rows_per_cond):
    t = x.shape[0]
    tm = TOKEN_TILE
    cast = w_in.dtype == F32
    tile = _tile_of(cast)
    row = pl.BlockSpec((tm, D_MODEL), lambda s: (tile(s), 0))
    half = pl.BlockSpec((tm, A_W), lambda s: (tile(s), 0))
    w_in_specs, w_out_specs, w_shapes, w_scratch = _ffn_weight_plumbing(cast)
    outs = pl.pallas_call(
        functools.partial(_post_kernel, cast=cast),
        grid=(t // tm + (W_CHUNKS if cast else 0),),
        in_specs=[row, half, half, _mod_spec(rows_per_cond, tm, tile), _resident((1, D_MODEL)),
                  _resident((1, D_MODEL)), _resident((A_W + B_W, D_MODEL))] + w_in_specs,
        out_specs=[row] + w_out_specs,
        out_shape=[jax.ShapeDtypeStruct((t, D_MODEL), F32)] + w_shapes,
        scratch_shapes=w_scratch,
        compiler_params=_cparams(("arbitrary",)),
        name="post",
    )(x, oa, ob, mod, gain, gfin, w_mix, w_in, w_out)
    return (outs[0], outs[1], outs[2]) if cast else (outs[0], w_in, w_out)


def _rope(x, cos, sin_signed):
    w = x.shape[-1]
    lane = lax.broadcasted_iota(jnp.int32, x.shape, 1)
    partner = jnp.where((lane & 31) < 16, pltpu.roll(x, w - 16, axis=1), pltpu.roll(x, 16, axis=1))
    return x * cos + partner * sin_signed


def _store_values(v_s, rows, v):
    for h in range(H_B):
        vh = v[:, h * DV_B:(h + 1) * DV_B]
        v_s[rows, 2 * h * DV_B:(2 * h + 1) * DV_B] = vh.astype(BF16)
        v_s[rows, (2 * h + 1) * DV_B:(2 * h + 2) * DV_B] = jnp.ones(vh.shape, BF16)


def _attn_core(q, k_ref, v_ref, lam, gain_ref, o_ref):
    tq = q.shape[0]
    lane = lax.broadcasted_iota(jnp.int32, (tq, DV_B), 1)
    qs = q * (DQK_B ** -0.5)
    heads = [slice(h * DV_B, (h + 1) * DV_B) for h in range(H_B)]
    scores = []
    for hs in heads:
        qh = qs[:, hs]
        q2 = jnp.concatenate([jnp.where(lane < DQK_B, qh, 0.0), jnp.where(lane >= DQK_B, qh, 0.0)], axis=0)
        scores.append(lax.dot_general(q2.astype(BF16), k_ref[:, hs], (((1,), (1,)), ((), ())),
                                      preferred_element_type=F32))
    for h, (hs, s) in enumerate(zip(heads, scores)):
        e = jnp.exp(s - jnp.max(s, axis=-1, keepdims=True))
        rs = _dot(e.astype(BF16), v_ref[:, 2 * h * DV_B:(2 * h + 2) * DV_B])
        r = rs[:, :DV_B] / rs[:, DV_B:]
        o = r[:tq] - lam * r[tq:]
        o = o * lax.rsqrt(jnp.mean(o * o, axis=-1, keepdims=True) + EPS)
        o_ref[:, hs] = (o * gain_ref[...]) * (1.0 - LAM_INIT)


def _lambda(lam_ref):
    l = lam_ref[...]
    s1 = jnp.sum(l[0:1] * l[1:2], axis=-1, keepdims=True)
    s2 = jnp.sum(l[2:3] * l[3:4], axis=-1, keepdims=True)
    return jnp.exp(s1) - jnp.exp(s2) + LAM_INIT


CTX_SEQS = 4


def _attn_ctx_kernel(q_ref, k_ref, v_ref, lam_ref, gain_ref, o_ref, k_s, v_s):
    lam = _lambda(lam_ref)

    def one_sequence(bi, carry):
        k_s[...] = k_ref[bi].astype(BF16)
        _store_values(v_s, slice(None), v_ref[bi])
        _attn_core(q_ref[bi], k_s, v_s, lam, gain_ref, o_ref.at[bi])
        return carry

    lax.fori_loop(0, CTX_SEQS, one_sequence, 0)


def _attn_ctx(q, k, v, lam_pack, gain):
    b, l, _ = q.shape
    blk = pl.BlockSpec((CTX_SEQS, l, B_W), lambda i: (i, 0, 0))
    return pl.pallas_call(
        _attn_ctx_kernel,
        grid=(b // CTX_SEQS,),
        in_specs=[blk, blk, blk, _resident((8, LANES)), _resident((1, DV_B))],
        out_specs=blk,
        out_shape=jax.ShapeDtypeStruct((b, l, B_W), F32),
        scratch_shapes=[pltpu.VMEM((l, B_W), BF16), pltpu.VMEM((l, 2 * B_W), BF16)],
        compiler_params=_cparams(("arbitrary",)),
        name="attn_ctx",
    )(q, k, v, lam_pack, gain)


def _attn_lat_kernel(q_ref, k_ref, v_ref, ck_ref, cv_ref, cosq_ref, sinq_ref, cos_ref, sin_ref,
                     lam_ref, gain_ref, o_ref, k_s, v_s, *, past):
    @pl.when(pl.program_id(1) == 0)
    def _():
        k_s[0:past, :] = ck_ref[...].astype(BF16)
        _store_values(v_s, slice(0, past), cv_ref[...])
        k_s[past:, :] = _rope(k_ref[...], cos_ref[...], sin_ref[...]).astype(BF16)
        _store_values(v_s, slice(past, None), v_ref[...])

    q = _rope(q_ref[...], cosq_ref[...], sinq_ref[...])
    _attn_core(q, k_s, v_s, _lambda(lam_ref), gain_ref, o_ref)


def _attn_lat(q, k, v, ck, cv, cos, sin, lam_pack, gain):
    b, l, _ = q.shape
    past = ck.shape[1]
    tq = Q_TILE
    full = pl.BlockSpec((None, l, B_W), lambda i, j: (i, 0, 0))
    cache_k = cache_v = pl.BlockSpec((None, past, B_W), lambda i, j: (i, 0, 0))
    qblk = pl.BlockSpec((None, tq, B_W), lambda i, j: (i, j, 0))
    tab_q = pl.BlockSpec((tq, B_W), lambda i, j: (j, 0))
    tab = pl.BlockSpec((l, B_W), lambda i, j: (0, 0))
    return pl.pallas_call(
        functools.partial(_attn_lat_kernel, past=past),
        grid=(b, l // tq),
        in_specs=[qblk, full, full, cache_k, cache_v, tab_q, tab_q, tab, tab,
                  pl.BlockSpec((8, LANES), lambda i, j: (0, 0)), pl.BlockSpec((1, DV_B), lambda i, j: (0, 0))],
        out_specs=qblk,
        out_shape=jax.ShapeDtypeStruct((b, l, B_W), F32),
        scratch_shapes=[pltpu.VMEM((past + l, B_W), BF16), pltpu.VMEM((past + l, 2 * B_W), BF16)],
        compiler_params=_cparams(("arbitrary", "arbitrary")),
        name="attn_lat",
    )(q, k, v, ck, cv, cos, sin, cos, sin, lam_pack, gain)


def _rope_tables(length):
    pairs = DQK_B // 4
    n_rows = length // GRID_W
    pos_row = jnp.repeat(jnp.arange(n_rows), GRID_W).astype(F32)
    pos_col = jnp.tile(jnp.arange(GRID_W), n_rows).astype(F32)
    inv = ROPE_THETA ** (-jnp.arange(pairs, dtype=F32) / pairs)
    ang_r = pos_row[:, None] * inv
    ang_c = pos_col[:, None] * inv
    cos = jnp.concatenate([jnp.cos(ang_r), jnp.cos(ang_r), jnp.cos(ang_c), jnp.cos(ang_c)], axis=-1)
    sin = jnp.concatenate([-jnp.sin(ang_r), jnp.sin(ang_r), -jnp.sin(ang_c), jnp.sin(ang_c)], axis=-1)
    reps = B_W // DQK_B
    return jnp.tile(cos, (1, reps)), jnp.tile(sin, (1, reps))


def _block_diag(x):
    lo, hi = x[:, :LANES], x[:, LANES:]
    first = lax.broadcasted_iota(jnp.int32, lo.shape, 1) < DK_A
    top = jnp.concatenate([jnp.where(first, lo, 0), jnp.where(first, 0, lo)], axis=0)
    bot = jnp.concatenate([jnp.where(first, hi, 0), jnp.where(first, 0, hi)], axis=0)
    zero = jnp.zeros_like(top)
    return jnp.concatenate([jnp.concatenate([top, zero], axis=1), jnp.concatenate([zero, bot], axis=1)], axis=0)


SOLVE_CHUNKS = 8
DELTA_ROWS = 1024


def _delta_kernel(*refs, nb, length, has_state):
    if has_state:
        (q_ref, k_ref, v_ref, z_ref, gcs_ref, beta_ref, dn_ref, s0_ref, o_ref, st_ref, *scr) = refs
    else:
        (q_ref, k_ref, v_ref, z_ref, gcs_ref, beta_ref, dn_ref, o_ref, st_ref, *scr) = refs
        s0_ref = None
    qs, ks, vs, gexp, bexp, u0_s, kg_s, dec_s, w_s, qg_s, qkm_s, of_s, ob_s, s_s = scr
    L = length
    R = nb * L
    n_chunks = L // CHUNK
    grp = pl.program_id(1)
    W = GROUP_W
    N_GRP = A_W // W

    r256 = lax.broadcasted_iota(jnp.int32, (W, W), 0)
    c256 = lax.broadcasted_iota(jnp.int32, (W, W), 1)
    bd_mask = (r256 >> 6) == (c256 >> 6)
    ones_bd = jnp.where(bd_mask, 1.0, 0.0).astype(BF16)
    ri = lax.broadcasted_iota(jnp.int32, (CHUNK, W), 0)
    cj = lax.broadcasted_iota(jnp.int32, (CHUNK, W), 1) & (CHUNK - 1)
    eye_p = ri == cj
    er = lax.broadcasted_iota(jnp.int32, (LANES, 2 * W), 0)
    ec = lax.broadcasted_iota(jnp.int32, (LANES, 2 * W), 1)
    expand = jnp.where(er == (ec >> 8) * H_A + grp * 4 + ((ec & (W - 1)) >> 6), 1.0, 0.0).astype(BF16)

    bd = _block_diag

    def prologue(bi, carry):
        rows_b = pl.ds(pl.multiple_of(bi * L, L), L)

        def l2n(y):
            return y * lax.rsqrt(_dot_split(y * y, ones_bd) + EPS)

        qs[rows_b, :] = l2n(q_ref[bi]) * (DK_A ** -0.5)
        ks[rows_b, :] = l2n(k_ref[bi])
        vs[rows_b, :] = v_ref[bi]
        ge_all = _dot_split(gcs_ref[bi], expand)
        be_all = _dot_split(beta_ref[bi], expand)
        for d in range(2):
            gexp[d, rows_b, :] = ge_all[:, d * W:(d + 1) * W]
            bexp[d, rows_b, :] = be_all[:, d * W:(d + 1) * W]
        return carry

    lax.fori_loop(0, nb, prologue, 0)

    def solve_body(j, carry):
        chunks = []
        for k in range(SOLVE_CHUNKS):
            chunk = j * SOLVE_CHUNKS + k
            rows = pl.ds(pl.multiple_of(chunk * CHUNK, CHUNK), CHUNK)
            kc = ks[rows, :]
            qc = qs[rows, :]
            kb = kc.astype(BF16)
            res = lax.dot_general(jnp.concatenate([qc.astype(BF16), kb], axis=0), bd(kb),
                                  (((1,), (1,)), ((), ())), preferred_element_type=F32)
            qk, kk = res[:CHUNK], res[CHUNK:]
            ch = dict(rows=rows, bd_k=bd(kb), bd_v=bd(vs[rows, :].astype(BF16)), t=[], scale_u=[], scale_w=[])
            nsum = None
            for d in range(2):
                ge = gexp[d, rows, :]
                be = bexp[d, rows, :]
                if d == 0:
                    incl, strict, gl = ri >= cj, ri > cj, ge[CHUNK - 1:CHUNK, :]
                else:
                    incl, strict, gl = ri <= cj, ri < cj, ge[0:1, :]
                g_row = jnp.sum(jnp.where(eye_p, ge, 0.0), axis=0, keepdims=True)
                b_row = jnp.sum(jnp.where(eye_p, be, 0.0), axis=0, keepdims=True)
                dec_m = jnp.exp(jnp.where(incl, ge - g_row, -jnp.inf))
                nmat = jnp.where(strict, be * kk * dec_m, 0.0)
                nsum = nmat if nsum is None else nsum + nmat
                qkm_s[grp, d, rows, :] = jnp.where(incl, qk * dec_m, 0.0).astype(BF16)
                qg_s[grp, d, rows, :] = (qc * jnp.exp(ge)).astype(BF16)
                kg_s[grp, d, rows, :] = kc * jnp.exp(gl - ge)
                dec_s[grp, d, pl.ds(pl.multiple_of(chunk * 8, 8), 8), :] = jnp.broadcast_to(jnp.exp(gl), (8, W))
                ch["scale_u"].append(b_row)
                ch["scale_w"].append(b_row * jnp.exp(g_row))
                ch["t"].append(jnp.where(eye_p, 1.0, 0.0) - jnp.where((ri >> 1) == (cj >> 1), nmat, 0.0))
            ch["n_both"] = nsum.astype(BF16)
            chunks.append(ch)

        for lb in range(1, 6):
            same = (ri >> (lb + 1)) == (cj >> (lb + 1))
            pair = (same & (((ri >> lb) & 1) == 1) & (((cj >> lb) & 1) == 0),
                    same & (((cj >> lb) & 1) == 1) & (((ri >> lb) & 1) == 0))
            for ch in chunks:
                ch["tb"] = [t.astype(BF16) for t in ch["t"]]
                rhs = bd(jnp.where(pair[0] | pair[1], ch["n_both"], 0))
                ch["y"] = _dot(jnp.concatenate(ch["tb"], axis=0), rhs).astype(BF16)
            for ch in chunks:
                ys = (jnp.where(pair[0], ch["y"][:CHUNK], 0), jnp.where(pair[1], ch["y"][CHUNK:], 0))
                ch["t"] = [ch["t"][d] - _dot(ys[d], bd(ch["tb"][d])) for d in range(2)]
        for ch in chunks:
            tu = jnp.concatenate([(ch["t"][d] * ch["scale_u"][d]).astype(BF16) for d in range(2)], axis=0)
            tw = jnp.concatenate([(ch["t"][d] * ch["scale_w"][d]).astype(BF16) for d in range(2)], axis=0)
            u0 = _dot(tu, ch["bd_v"])
            w = _dot(tw, ch["bd_k"]).astype(BF16)
            for d in range(2):
                u0_s[grp, d, ch["rows"], :] = u0[d * CHUNK:(d + 1) * CHUNK]
                w_s[grp, d, ch["rows"], :] = w[d * CHUNK:(d + 1) * CHUNK]
        return carry

    lax.fori_loop(0, R // (SOLVE_CHUNKS * CHUNK), solve_body, 0)

    s_s[grp] = jnp.zeros(s_s.shape[1:], F32)
    if has_state:
        for bi in range(nb):
            for d in range(2):
                for h in range(4):
                    s_s[grp, bi, d, h * DK_A:(h + 1) * DK_A, h * DK_A:(h + 1) * DK_A] = s0_ref[bi, d, h]

    def scan_body(i, carry):
        chains = []
        for gi in range(N_GRP):
            for bi in range(nb):
                for d in range(2):
                    c = i if d == 0 else n_chunks - 1 - i
                    rows = pl.ds(pl.multiple_of(bi * L + c * CHUNK, CHUNK), CHUNK)
                    tile = pl.ds(pl.multiple_of((bi * n_chunks + c) * 8, 8), 8)
                    chains.append(dict(
                        gi=gi, bi=bi, d=d, rows=rows, s=s_s[gi, bi, d], u0=u0_s[gi, d, rows, :],
                        kg=kg_s[gi, d, rows, :], qkm=qkm_s[gi, d, rows, :], decay=dec_s[gi, d, tile, :][0:1],
                        wq=jnp.concatenate([w_s[gi, d, rows, :], qg_s[gi, d, rows, :]], axis=0)))
        for c in chains:
            c["ws"] = _dot(c["wq"], c["s"].astype(BF16))
        for c in chains:
            c["ub"] = (c["u0"] - c["ws"][:CHUNK]).astype(BF16)
        for c in chains:
            c["o"] = c["ws"][CHUNK:] + _dot(c["qkm"], bd(c["ub"]))
            upd = _dot(c["kg"].T.astype(BF16), c["ub"])
            c["s_new"] = c["s"] * c["decay"] + jnp.where(bd_mask, upd, 0.0)
        for c in chains:
            s_s[c["gi"], c["bi"], c["d"]] = c["s_new"]
            if c["d"] == 0:
                of_s[c["gi"], c["rows"], :] = c["o"]
            else:
                ob_s[c["gi"], c["rows"], :] = c["o"]
        return carry

    def epilogue(bi, carry):
        rows_b = pl.ds(pl.multiple_of(bi * L, L), L)
        for gi in range(N_GRP):
            lanes = slice(gi * W, (gi + 1) * W)
            o = of_s[gi, rows_b, :] + ob_s[gi, rows_b, :]
            ms = _dot_split(o * o, ones_bd) * (1.0 / DK_A)
            o = o * lax.rsqrt(ms + EPS) * dn_ref[...]
            o_ref[bi, :, lanes] = o * _silu(z_ref[bi, :, lanes])
        return carry

    @pl.when(grp == N_GRP - 1)
    def _():
        lax.fori_loop(0, n_chunks, scan_body, 0)
        lax.fori_loop(0, nb, epilogue, 0)
        for gi in range(N_GRP):
            for bi in range(nb):
                for d in range(2):
                    for h in range(4):
                        st_ref[bi, d, gi * 4 + h] = s_s[gi, bi, d, h * DK_A:(h + 1) * DK_A, h * DK_A:(h + 1) * DK_A]


def _delta(qkv, z, gcs, beta, dnorm, s0):
    b, l, _ = qkv.shape
    has_state = s0 is not None
    w = GROUP_W
    n_grp = A_W // w
    nb = DELTA_ROWS // l
    r = nb * l

    def slab(k):
        return pl.BlockSpec((nb, l, w), lambda i, g: (i, 0, k * n_grp + g))

    narrow = pl.BlockSpec((nb, l, LANES), lambda i, g: (i, 0, 0))
    wide = pl.BlockSpec((nb, l, A_W), lambda i, g: (i, 0, 0))
    in_specs = [slab(0), slab(1), slab(2), wide, narrow, narrow,
                pl.BlockSpec((1, w), lambda i, g: (0, 0))]
    args = [qkv, qkv, qkv, z, gcs, beta, dnorm]
    if has_state:
        in_specs.append(pl.BlockSpec((nb, None, 2, 4, DK_A, DK_A), lambda i, g: (i, 0, 0, g, 0, 0)))
        args.append(s0)
    per_grp = lambda shape, dt: pltpu.VMEM((n_grp,) + shape, dt)
    scratch = ([pltpu.VMEM((r, w), F32)] * 3 + [pltpu.VMEM((2, r, w), F32)] * 2
               + [per_grp((2, r, w), F32)] * 2 + [per_grp((2, r // CHUNK * 8, w), F32)]
               + [per_grp((2, r, w), BF16)] * 3 + [per_grp((r, w), F32)] * 2
               + [per_grp((nb, 2, w, w), F32)])
    return pl.pallas_call(
        functools.partial(_delta_kernel, nb=nb, length=l, has_state=has_state),
        grid=(b // nb, n_grp),
        in_specs=in_specs,
        out_specs=[wide, pl.BlockSpec((nb, None, 2, H_A, DK_A, DK_A), lambda i, g: (i, 0, 0, 0, 0, 0))],
        out_shape=[jax.ShapeDtypeStruct((b, l, A_W), F32),
                   jax.ShapeDtypeStruct((b, 1, 2, H_A, DK_A, DK_A), F32)],
        scratch_shapes=scratch,
        compiler_params=_cparams(("arbitrary", "arbitrary")),
        name="delta_state" if has_state else "delta",
    )(*args)


def _split_w_in_kernel(w_ref, wa_ref, wb_ref, wg_ref):
    w = w_ref[...]
    a_end = 4 * A_W
    wa_ref[...] = w[:, :a_end].astype(BF16)
    wb_ref[...] = w[:, a_end + 4 * H_A:].astype(BF16)
    lane = lax.broadcasted_iota(jnp.int32, (w.shape[0], 2 * LANES), 1)
    gates = w[:, a_end:a_end + 2 * LANES]
    beta_up = pltpu.roll(gates, LANES - 2 * H_A, axis=1)
    wg_ref[...] = jnp.where(lane < 2 * H_A, gates, jnp.where((lane >= LANES) & (lane < LANES + 2 * H_A),
                                                             beta_up, 0.0)).astype(BF16)


def _split_w_in(w):
    rows = 128
    return pl.pallas_call(
        _split_w_in_kernel,
        grid=(D_MODEL // rows,),
        in_specs=[pl.BlockSpec((rows, w.shape[1]), lambda i: (i, 0))],
        out_specs=[pl.BlockSpec((rows, 4 * A_W), lambda i: (i, 0)), pl.BlockSpec((rows, 3 * B_W), lambda i: (i, 0)),
                   pl.BlockSpec((rows, 2 * LANES), lambda i: (i, 0))],
        out_shape=[jax.ShapeDtypeStruct((D_MODEL, 4 * A_W), BF16), jax.ShapeDtypeStruct((D_MODEL, 3 * B_W), BF16),
                   jax.ShapeDtypeStruct((D_MODEL, 2 * LANES), BF16)],
        compiler_params=_cparams(("arbitrary",)),
        name="split_w_in",
    )(w)


def _pad_lanes(x, width):
    return jnp.pad(x, [(0, 0)] * (x.ndim - 1) + [(0, width - x.shape[-1])])


def kernel(x_prompt, x_sample, cache_diff_k, cache_diff_v, state_delta, c, c_ctx, w_ada, b_ada, norm_ffn1,
           w_ffn1_in, w_ffn1_out, norm_mix, w_in, conv_w, a_log, dt_bias, delta_norm, lambda_q1, lambda_k1,
           lambda_q2, lambda_k2, diff_norm, w_out, norm_ffn2, w_ffn2_in, w_ffn2_out, norm_final):
    bp, lp, _ = x_prompt.shape
    bs, ls, _ = x_sample.shape
    past = cache_diff_k.shape[2]

    w_mix_out = w_out[0].astype(BF16)
    w_mix_in = _split_w_in(w_in[0])
    g1 = norm_ffn1[0].reshape(1, D_MODEL)
    gm = norm_mix[0].reshape(1, D_MODEL)
    g2 = norm_ffn2[0].reshape(1, D_MODEL)
    gf = norm_final.reshape(1, D_MODEL)
    gpar = jnp.zeros((8, LANES), F32)
    gpar = gpar.at[0, :2 * H_A].set(a_log[0].reshape(-1)).at[1, :2 * H_A].set(dt_bias[0].reshape(-1))
    dnorm = jnp.tile(delta_norm[0], 4).reshape(1, GROUP_W)
    lam_pack = jnp.zeros((8, LANES), F32)
    for r, v in enumerate((lambda_q1, lambda_k1, lambda_q2, lambda_k2)):
        lam_pack = lam_pack.at[r, :DQK_B].set(v[0])
    dgain = diff_norm[0].reshape(1, DV_B)
    cos, sin = _rope_tables(ls)

    cond = jnp.concatenate([c_ctx[None, :], c, jnp.zeros((8 - 1 - bs, D_MODEL), F32)], axis=0)
    mod = _adaln(cond, w_ada[0], b_ada[0]).reshape(8, N_MOD, D_MODEL)

    def trunk(x3, rows_per_cond, ctx, w1, w2):
        b, l, _ = x3.shape
        x = x3.reshape(b * l, D_MODEL)
        x1, *w1 = _ffn1(x, mod, g1, *w1, rows_per_cond)
        outs = _inproj(x1, mod, gm, w_mix_in, conv_w[0], gpar, rows_per_cond, l, ctx is None)
        qkv, z, bq, bk, bv, gcs, beta = outs[:7]
        r3 = lambda a: a.reshape(b, l, a.shape[-1])
        s0 = None if ctx is None else ctx[2]
        oa, st = _delta(r3(qkv), r3(z), r3(gcs), r3(beta), dnorm, s0)
        if ctx is None:
            ob = _attn_ctx(r3(bq), r3(bk), r3(bv), lam_pack, dgain)
        else:
            ob = _attn_lat(r3(bq), r3(bk), r3(bv), ctx[0], ctx[1], cos, sin, lam_pack, dgain)
        y, *w2 = _post(x1, oa.reshape(b * l, A_W), ob.reshape(b * l, B_W), mod, g2, gf, w_mix_out, *w2,
                       rows_per_cond)
        return y.reshape(b, l, D_MODEL), outs[7:], st, w1, w2

    y_prompt, (k_maps, v_heads), s_ctx, w1, w2 = trunk(x_prompt, None, None, (w_ffn1_in[0], w_ffn1_out[0]),
                                                       (w_ffn2_in[0], w_ffn2_out[0]))
    ctx = (cache_diff_k[:, 0].reshape(bs, past, B_W), cache_diff_v[:, 0].reshape(bs, past, B_W), state_delta)
    y_sample = trunk(x_sample, ls, ctx, w1, w2)[0]
    new_k = k_maps.reshape(bp, 1, lp, H_B, 2, DQK_B)
    new_v = v_heads.reshape(bp, 1, lp, H_B, DV_B)
    return (y_prompt, y_sample, new_k, new_v, s_ctx)
```

```python
import functools
import math

import jax
import jax.numpy as jnp
import numpy as np
from jax import lax
from jax.experimental import pallas as pl
from jax.experimental.pallas import tpu as pltpu

F32 = jnp.float32
BF16 = jnp.bfloat16

D_MODEL = 1024
D_FF = 2816
N_MOD = 9
H_A = 8
DK_A = 64
A_W = H_A * DK_A
H_B = 4
DQK_B = 64
DV_B = 128
B_W = H_B * DV_B
CONV_K = 5
CHUNK = 64
GRID_W = 64
ROPE_THETA = 10000.0
EPS = 1e-6
LAM_INIT = 0.8 - 0.6 * math.exp(-0.3 * 0)

LANES = 128
GROUP_W = 4 * DK_A
IN_PACK_COLS = 3 * A_W + A_W + 3 * B_W + 2 * LANES
VMEM_LIMIT = 56 * 1024 * 1024

MXU_W = 256
TOKEN_TILE = 512
FF_CHUNKS = ((0, 6 * MXU_W), (6 * MXU_W, D_FF))
Q_TILE = 256


def _cparams(sem):
    return pltpu.CompilerParams(dimension_semantics=sem, vmem_limit_bytes=VMEM_LIMIT)


def _resident(shape):
    nd = len(shape)
    return pl.BlockSpec(shape, lambda *_: (0,) * nd, pipeline_mode=pl.Buffered(1))


def _silu(x):
    half = 0.5 * x
    return half * jnp.tanh(half) + half


def _mod_norm(x, gain, shift, scale):
    y = x * lax.rsqrt(jnp.mean(x * x, axis=-1, keepdims=True) + EPS)
    return (y * gain) * (1.0 + scale) + shift


def _dot(a, b):
    return jnp.dot(a, b, preferred_element_type=F32)


def _dot_split(a, b_bf16):
    hi = a.astype(BF16)
    lo = (a - hi.astype(F32)).astype(BF16)
    return _dot(hi, b_bf16) + _dot(lo, b_bf16)


def _adaln_kernel(c_ref, w_ref, b_ref, o_ref):
    s = _silu(c_ref[...])
    o_ref[...] = _dot(s.astype(BF16), w_ref[...].astype(BF16)) + b_ref[...]


def _adaln(cond, w_ada, b_ada):
    n = N_MOD * D_MODEL
    tn = n // 4
    return pl.pallas_call(
        _adaln_kernel,
        grid=(n // tn,),
        in_specs=[pl.BlockSpec((8, D_MODEL), lambda j: (0, 0)),
                  pl.BlockSpec((D_MODEL, tn), lambda j: (0, j)),
                  pl.BlockSpec((1, tn), lambda j: (0, j))],
        out_specs=pl.BlockSpec((8, tn), lambda j: (0, j)),
        out_shape=jax.ShapeDtypeStruct((8, n), F32),
        compiler_params=_cparams(("arbitrary",)),
        name="adaln",
    )(cond, w_ada, b_ada.reshape(1, n))


def _swiglu_update(x, mod_ref, k0, gain, w_in_ref, w_out_ref):
    h = _mod_norm(x, gain, mod_ref[k0:k0 + 1, :], mod_ref[k0 + 1:k0 + 2, :]).astype(BF16)
    acc = None
    for lo, hi in FF_CHUNKS:
        g = _dot(h, w_in_ref[:, lo:hi])
        u = _dot(h, w_in_ref[:, D_FF + lo:D_FF + hi])
        part = _dot((_silu(g) * u).astype(BF16), w_out_ref[lo:hi, :])
        acc = part if acc is None else acc + part
    return x + (0.5 * mod_ref[k0 + 2:k0 + 3, :]) * acc


W_CHUNKS = 11
W_IN_CHUNK = 2 * D_FF // W_CHUNKS
W_OUT_CHUNK = D_FF // W_CHUNKS


def _stage_ffn_weights(step, w_in_ref, w_out_ref, w_in_bf_ref, w_out_bf_ref, w_in_s, w_out_s):
    for c in range(W_CHUNKS):
        @pl.when(step == c)
        def _():
            a = w_in_ref[...].astype(BF16)
            b = w_out_ref[...].astype(BF16)
            w_in_bf_ref[...] = a
            w_out_bf_ref[...] = b
            w_in_s[:, c * W_IN_CHUNK:(c + 1) * W_IN_CHUNK] = a
            w_out_s[c * W_OUT_CHUNK:(c + 1) * W_OUT_CHUNK, :] = b


def _ffn_weight_plumbing(cast):
    if not cast:
        return [_resident((D_MODEL, 2 * D_FF)), _resident((D_FF, D_MODEL))], [], [], []
    last = W_CHUNKS - 1
    w_in_chunk = pl.BlockSpec((D_MODEL, W_IN_CHUNK), lambda s: (0, jnp.minimum(s, last)))
    w_out_chunk = pl.BlockSpec((W_OUT_CHUNK, D_MODEL), lambda s: (jnp.minimum(s, last), 0))
    out_shape = [jax.ShapeDtypeStruct((D_MODEL, 2 * D_FF), BF16), jax.ShapeDtypeStruct((D_FF, D_MODEL), BF16)]
    scratch = [pltpu.VMEM((D_MODEL, 2 * D_FF), BF16), pltpu.VMEM((D_FF, D_MODEL), BF16)]
    return [w_in_chunk, w_out_chunk], [w_in_chunk, w_out_chunk], out_shape, scratch


def _tile_of(cast):
    off = W_CHUNKS if cast else 0
    return lambda s: jnp.maximum(s - off, 0)


def _mod_spec(rows_per_cond, tm, tile):
    if rows_per_cond is None:
        return pl.BlockSpec((None, N_MOD, D_MODEL), lambda s: (0, 0, 0))
    return pl.BlockSpec((None, N_MOD, D_MODEL), lambda s: (1 + (tile(s) * tm) // rows_per_cond, 0, 0))


def _ffn1_kernel(*refs, cast):
    x_ref, mod_ref, gain_ref, w_in_ref, w_out_ref, o_ref = refs[:6]
    if cast:
        step = pl.program_id(0)
        w_in_s, w_out_s = refs[8:10]
        _stage_ffn_weights(step, w_in_ref, w_out_ref, refs[6], refs[7], w_in_s, w_out_s)

        @pl.when(step >= W_CHUNKS)
        def _():
            o_ref[...] = _swiglu_update(x_ref[...], mod_ref, 0, gain_ref[...], w_in_s, w_out_s)
    else:
        o_ref[...] = _swiglu_update(x_ref[...], mod_ref, 0, gain_ref[...], w_in_ref, w_out_ref)


def _ffn1(x, mod, gain, w_in, w_out, rows_per_cond):
    t = x.shape[0]
    tm = TOKEN_TILE
    cast = w_in.dtype == F32
    tile = _tile_of(cast)
    row = pl.BlockSpec((tm, D_MODEL), lambda s: (tile(s), 0))
    w_in_specs, w_out_specs, w_shapes, w_scratch = _ffn_weight_plumbing(cast)
    outs = pl.pallas_call(
        functools.partial(_ffn1_kernel, cast=cast),
        grid=(t // tm + (W_CHUNKS if cast else 0),),
        in_specs=[row, _mod_spec(rows_per_cond, tm, tile), _resident((1, D_MODEL))] + w_in_specs,
        out_specs=[row] + w_out_specs,
        out_shape=[jax.ShapeDtypeStruct((t, D_MODEL), F32)] + w_shapes,
        scratch_shapes=w_scratch,
        compiler_params=_cparams(("arbitrary",)),
        name="ffn1",
    )(x, mod, gain, w_in, w_out)
    return (outs[0], outs[1], outs[2]) if cast else (outs[0], w_in, w_out)


_IN_SEGS = (("qkv", 0, 3 * A_W), ("z", 3 * A_W, A_W), ("bq", 4 * A_W, B_W), ("bk", 4 * A_W + B_W, B_W),
            ("bv", 4 * A_W + 2 * B_W, B_W), ("dec", 4 * A_W + 3 * B_W, LANES),
            ("bet", 4 * A_W + 3 * B_W + LANES, LANES))


HALO = 8
CONV_SLAB = 256


def _inproj_kernel(*refs, tm, seq_len, emit_maps):
    x_ref, xp_ref, xn_ref, mod_ref, gain_ref, w_ref, cw_ref, gpar_ref = refs[:8]
    qkv_ref, z_ref, bq_ref, bk_ref, bv_ref, gcs_ref, beta_ref = refs[8:15]
    hs, xpad, ybuf = refs[-3:]
    shift, scale = mod_ref[3:4, :], mod_ref[4:5, :]
    hf = _mod_norm(x_ref[...], gain_ref[...], shift, scale)
    h = hf.astype(BF16)
    segs = dict((name, (start, width)) for name, start, width in _IN_SEGS)

    def proj(name):
        start, width = segs[name]
        return _dot(h, w_ref[:, start:start + width])

    seg = min(tm, seq_len)
    n_seg = tm // seg
    ext = seg + 2 * HALO
    m = ext // 8
    pad = (CONV_K // 2) * 8
    qkv_w = 3 * A_W
    units = [(s, c) for s in range(n_seg) for c in range(qkv_w // CONV_SLAB)]
    def put_rows(lo, hi, val):
        for ct in range(D_MODEL // LANES):
            hs[ct, lo:hi, :] = val[:, ct * LANES:(ct + 1) * LANES]

    if seq_len > tm:
        tiles = seq_len // tm
        pos = pl.program_id(0) % tiles
        put_rows(0, HALO, _mod_norm(xp_ref[...], gain_ref[...], shift, scale) * jnp.where(pos > 0, 1.0, 0.0))
        put_rows(HALO + seg, ext,
                 _mod_norm(xn_ref[...], gain_ref[...], shift, scale) * jnp.where(pos < tiles - 1, 1.0, 0.0))
    else:
        for s in range(n_seg):
            put_rows(s * ext, s * ext + HALO, jnp.zeros((HALO, D_MODEL), F32))
            put_rows(s * ext + HALO + seg, (s + 1) * ext, jnp.zeros((HALO, D_MODEL), F32))
    for s in range(n_seg):
        put_rows(s * ext + HALO, s * ext + HALO + seg, hf[s * seg:(s + 1) * seg])
    h_il = [jnp.concatenate(
        [jnp.concatenate([hs[ct, pl.ds(s * ext + v, 8, stride=m), :] for v in range(m)], axis=0)
         for ct in range(D_MODEL // LANES)], axis=1).astype(BF16) for s in range(n_seg)]

    def project(s, c):
        cs = slice(c * CONV_SLAB, (c + 1) * CONV_SLAB)
        base = s * (ext + 2 * pad)
        res = _dot(h_il[s], w_ref[:, cs])
        xpad[base + pad:base + pad + ext, cs] = res
        for j in range(CONV_K // 2):
            xpad[base + 8 * j:base + 8 * (j + 1), cs] = pltpu.roll(res[ext - pad + 8 * j:ext - pad + 8 * (j + 1)], 1, 0)
            xpad[base + pad + ext + 8 * j:base + pad + ext + 8 * (j + 1), cs] = pltpu.roll(res[8 * j:8 * (j + 1)], 7, 0)

    def convolve(s, c):
        cs = slice(c * CONV_SLAB, (c + 1) * CONV_SLAB)
        base = s * (ext + 2 * pad)
        acc = None
        for tap in range(CONV_K):
            term = xpad[base + 8 * tap:base + 8 * tap + ext, cs] * cw_ref[tap:tap + 1, cs]
            acc = term if acc is None else acc + term
        y = _silu(acc)
        for j in range(CONV_SLAB // LANES):
            ct = c * (CONV_SLAB // LANES) + j
            for v in range(m):
                ybuf[ct, pl.ds(v, 8, stride=m), :] = y[8 * v:8 * (v + 1), j * LANES:(j + 1) * LANES]
            qkv_ref[s * seg:(s + 1) * seg, ct * LANES:(ct + 1) * LANES] = ybuf[ct, HALO:HALO + seg, :]

    z_ref[...] = proj("z")
    bq_ref[...] = proj("bq")
    zk = proj("bk")
    zv = proj("bv")
    bk_ref[...] = zk
    bv_ref[...] = zv
    if emit_maps:
        bk8_ref, bv4_ref = refs[15:17]
        for r in range(2 * H_B):
            bk8_ref[:, r, :] = zk[:, r * DQK_B:(r + 1) * DQK_B]
        for r in range(H_B):
            bv4_ref[:, r, :] = zv[:, r * DV_B:(r + 1) * DV_B]
    for unit in units:
        project(*unit)
        convolve(*unit)

    gpar = gpar_ref[...]
    dec_start = segs["dec"][0]
    gate_pre = _dot(h, w_ref[:, dec_start:dec_start + 2 * LANES])
    xg = gate_pre[:, :LANES] + gpar[1:2, :]
    softplus = jnp.maximum(xg, 0.0) + jnp.log1p(jnp.exp(-jnp.abs(xg)))
    lane = lax.broadcasted_iota(jnp.int32, (tm, LANES), 1)
    g = jnp.where(lane < 2 * H_A, -jnp.exp(gpar[0:1, :]) * softplus, 0.0)
    beta_ref[...] = jax.nn.sigmoid(gate_pre[:, LANES:])
    pos_c = lax.broadcasted_iota(jnp.int32, (tm, 1), 0) & (CHUNK - 1)
    gcf = g
    gcb = g
    for sft in (1, 2, 4, 8, 16, 32):
        gcf = gcf + jnp.where(pos_c >= sft, pltpu.roll(gcf, sft, axis=0), 0.0)
        gcb = gcb + jnp.where(pos_c < CHUNK - sft, pltpu.roll(gcb, tm - sft, axis=0), 0.0)
    gcs_ref[...] = jnp.where(lane < H_A, gcf, gcb)


def _inproj(x, mod, gain, w_pack, conv_w, gpar, rows_per_cond, seq_len, emit_maps):
    t = x.shape[0]
    tm = TOKEN_TILE
    hb = tm // HALO
    n_seg = max(1, tm // seq_len)
    ext = tm // n_seg + 2 * HALO
    row = pl.BlockSpec((tm, D_MODEL), lambda i: (i, 0))
    prev = pl.BlockSpec((HALO, D_MODEL), lambda i: (jnp.maximum(i * hb - 1, 0), 0))
    nxt = pl.BlockSpec((HALO, D_MODEL), lambda i: (jnp.minimum((i + 1) * hb, t // HALO - 1), 0))
    widths = [w for _, _, w in _IN_SEGS]
    out_specs = [pl.BlockSpec((tm, w), lambda i: (i, 0)) for w in widths]
    out_shape = [jax.ShapeDtypeStruct((t, w), F32) for w in widths]
    if emit_maps:
        out_specs += [pl.BlockSpec((tm, 2 * H_B, DQK_B), lambda i: (i, 0, 0)),
                      pl.BlockSpec((tm, H_B, DV_B), lambda i: (i, 0, 0))]
        out_shape += [jax.ShapeDtypeStruct((t, 2 * H_B, DQK_B), F32), jax.ShapeDtypeStruct((t, H_B, DV_B), F32)]
    return pl.pallas_call(
        functools.partial(_inproj_kernel, tm=tm, seq_len=seq_len, emit_maps=emit_maps),
        grid=(t // tm,),
        in_specs=[row, prev, nxt, _mod_spec(rows_per_cond, tm, lambda s: s), _resident((1, D_MODEL)),
                  _resident((D_MODEL, IN_PACK_COLS)), _resident((CONV_K, 3 * A_W)), _resident((8, LANES))],
        out_specs=out_specs,
        out_shape=out_shape,
        scratch_shapes=[pltpu.VMEM((D_MODEL // LANES, n_seg * ext, LANES), F32),
                        pltpu.VMEM((n_seg * (ext + 2 * (CONV_K // 2) * 8), 3 * A_W), F32),
                        pltpu.VMEM((3 * A_W // LANES, ext, LANES), F32)],
        compiler_params=_cparams(("arbitrary",)),
        name="inproj",
    )(x, x, x, mod, gain, w_pack, conv_w, gpar)


def _post_kernel(*refs, cast):
    x_ref, oa_ref, ob_ref, mod_ref, gain_ref, gfin_ref, wmix_ref, w_in_ref, w_out_ref, o_ref = refs[:10]

    def tokens(w_in, w_out):
        y = _dot(oa_ref[...].astype(BF16), wmix_ref[0:A_W, :]) + _dot(ob_ref[...].astype(BF16), wmix_ref[A_W:, :])
        x = x_ref[...] + mod_ref[5:6, :] * y
        x = _swiglu_update(x, mod_ref, 6, gain_ref[...], w_in, w_out)
        o_ref[...] = (x * lax.rsqrt(jnp.mean(x * x, axis=-1, keepdims=True) + EPS)) * gfin_ref[...]

    if cast:
        step = pl.program_id(0)
        w_in_s, w_out_s = refs[12:14]
        _stage_ffn_weights(step, w_in_ref, w_out_ref, refs[10], refs[11], w_in_s, w_out_s)
        pl.when(step >= W_CHUNKS)(lambda: tokens(w_in_s, w_out_s))
    else:
        tokens(w_in_ref, w_out_ref)


def _post(x, oa, ob, mod, gain, gfin, w_mix, w_in, w_out, rows_per_cond):
    t = x.shape[0]
    tm = TOKEN_TILE
    cast = w_in.dtype == F32
    tile = _tile_of(cast)
    row = pl.BlockSpec((tm, D_MODEL), lambda s: (tile(s), 0))
    half = pl.BlockSpec((tm, A_W), lambda s: (tile(s), 0))
    w_in_specs, w_out_specs, w_shapes, w_scratch = _ffn_weight_plumbing(cast)
    outs = pl.pallas_call(
        functools.partial(_post_kernel, cast=cast),
        grid=(t // tm + (W_CHUNKS if cast else 0),),
        in_specs=[row, half, half, _mod_spec(rows_per_cond, tm, tile), _resident((1, D_MODEL)),
                  _resident((1, D_MODEL)), _resident((A_W + B_W, D_MODEL))] + w_in_specs,
        out_specs=[row] + w_out_specs,
        out_shape=[jax.ShapeDtypeStruct((t, D_MODEL), F32)] + w_shapes,
        scratch_shapes=w_scratch,
        compiler_params=_cparams(("arbitrary",)),
        name="post",
    )(x, oa, ob, mod, gain, gfin, w_mix, w_in, w_out)
    return (outs[0], outs[1], outs[2]) if cast else (outs[0], w_in, w_out)


def _rope(x, cos, sin_signed):
    w = x.shape[-1]
    lane = lax.broadcasted_iota(jnp.int32, x.shape, 1)
    partner = jnp.where((lane & 31) < 16, pltpu.roll(x, w - 16, axis=1), pltpu.roll(x, 16, axis=1))
    return x * cos + partner * sin_signed


def _store_values(v_s, rows, v):
    for h in range(H_B):
        vh = v[:, h * DV_B:(h + 1) * DV_B]
        v_s[rows, 2 * h * DV_B:(2 * h + 1) * DV_B] = vh.astype(BF16)
        v_s[rows, (2 * h + 1) * DV_B:(2 * h + 2) * DV_B] = jnp.ones(vh.shape, BF16)


def _attn_core(q, k_ref, v_ref, lam, gain_ref, o_ref):
    tq = q.shape[0]
    lane = lax.broadcasted_iota(jnp.int32, (tq, DV_B), 1)
    qs = q * (DQK_B ** -0.5)
    heads = [slice(h * DV_B, (h + 1) * DV_B) for h in range(H_B)]
    scores = []
    for hs in heads:
        qh = qs[:, hs]
        q2 = jnp.concatenate([jnp.where(lane < DQK_B, qh, 0.0), jnp.where(lane >= DQK_B, qh, 0.0)], axis=0)
        scores.append(lax.dot_general(q2.astype(BF16), k_ref[:, hs], (((1,), (1,)), ((), ())),
                                      preferred_element_type=F32))
    for h, (hs, s) in enumerate(zip(heads, scores)):
        e = jnp.exp(s - jnp.max(s, axis=-1, keepdims=True))
        rs = _dot(e.astype(BF16), v_ref[:, 2 * h * DV_B:(2 * h + 2) * DV_B])
        r = rs[:, :DV_B] / rs[:, DV_B:]
        o = r[:tq] - lam * r[tq:]
        o = o * lax.rsqrt(jnp.mean(o * o, axis=-1, keepdims=True) + EPS)
        o_ref[:, hs] = (o * gain_ref[...]) * (1.0 - LAM_INIT)


def _lambda(lam_ref):
    l = lam_ref[...]
    s1 = jnp.sum(l[0:1] * l[1:2], axis=-1, keepdims=True)
    s2 = jnp.sum(l[2:3] * l[3:4], axis=-1, keepdims=True)
    return jnp.exp(s1) - jnp.exp(s2) + LAM_INIT


CTX_SEQS = 4


def _attn_ctx_kernel(q_ref, k_ref, v_ref, lam_ref, gain_ref, o_ref, k_s, v_s):
    lam = _lambda(lam_ref)

    def one_sequence(bi, carry):
        k_s[...] = k_ref[bi].astype(BF16)
        _store_values(v_s, slice(None), v_ref[bi])
        _attn_core(q_ref[bi], k_s, v_s, lam, gain_ref, o_ref.at[bi])
        return carry

    lax.fori_loop(0, CTX_SEQS, one_sequence, 0)


def _attn_ctx(q, k, v, lam_pack, gain):
    b, l, _ = q.shape
    blk = pl.BlockSpec((CTX_SEQS, l, B_W), lambda i: (i, 0, 0))
    return pl.pallas_call(
        _attn_ctx_kernel,
        grid=(b // CTX_SEQS,),
        in_specs=[blk, blk, blk, _resident((8, LANES)), _resident((1, DV_B))],
        out_specs=blk,
        out_shape=jax.ShapeDtypeStruct((b, l, B_W), F32),
        scratch_shapes=[pltpu.VMEM((l, B_W), BF16), pltpu.VMEM((l, 2 * B_W), BF16)],
        compiler_params=_cparams(("arbitrary",)),
        name="attn_ctx",
    )(q, k, v, lam_pack, gain)


def _attn_lat_kernel(q_ref, k_ref, v_ref, ck_ref, cv_ref, cosq_ref, sinq_ref, cos_ref, sin_ref,
                     lam_ref, gain_ref, o_ref, k_s, v_s, *, past):
    @pl.when(pl.program_id(1) == 0)
    def _():
        k_s[0:past, :] = ck_ref[...].astype(BF16)
        _store_values(v_s, slice(0, past), cv_ref[...])
        k_s[past:, :] = _rope(k_ref[...], cos_ref[...], sin_ref[...]).astype(BF16)
        _store_values(v_s, slice(past, None), v_ref[...])

    q = _rope(q_ref[...], cosq_ref[...], sinq_ref[...])
    _attn_core(q, k_s, v_s, _lambda(lam_ref), gain_ref, o_ref)


def _attn_lat(q, k, v, ck, cv, cos, sin, lam_pack, gain):
    b, l, _ = q.shape
    past = ck.shape[1]
    tq = Q_TILE
    full = pl.BlockSpec((None, l, B_W), lambda i, j: (i, 0, 0))
    cache_k = cache_v = pl.BlockSpec((None, past, B_W), lambda i, j: (i, 0, 0))
    qblk = pl.BlockSpec((None, tq, B_W), lambda i, j: (i, j, 0))
    tab_q = pl.BlockSpec((tq, B_W), lambda i, j: (j, 0))
    tab = pl.BlockSpec((l, B_W), lambda i, j: (0, 0))
    return pl.pallas_call(
        functools.partial(_attn_lat_kernel, past=past),
        grid=(b, l // tq),
        in_specs=[qblk, full, full, cache_k, cache_v, tab_q, tab_q, tab, tab,
                  pl.BlockSpec((8, LANES), lambda i, j: (0, 0)), pl.BlockSpec((1, DV_B), lambda i, j: (0, 0))],
        out_specs=qblk,
        out_shape=jax.ShapeDtypeStruct((b, l, B_W), F32),
        scratch_shapes=[pltpu.VMEM((past + l, B_W), BF16), pltpu.VMEM((past + l, 2 * B_W), BF16)],
        compiler_params=_cparams(("arbitrary", "arbitrary")),
        name="attn_lat",
    )(q, k, v, ck, cv, cos, sin, cos, sin, lam_pack, gain)


def _rope_tables(length):
    pairs = DQK_B // 4
    n_rows = length // GRID_W
    pos_row = np.repeat(np.arange(n_rows), GRID_W).astype(np.float32)
    pos_col = np.tile(np.arange(GRID_W), n_rows).astype(np.float32)
    inv = (np.float32(ROPE_THETA) ** (-np.arange(pairs, dtype=np.float32) / np.float32(pairs))).astype(np.float32)
    ang_r = pos_row[:, None] * inv
    ang_c = pos_col[:, None] * inv
    cos = np.concatenate([np.cos(ang_r), np.cos(ang_r), np.cos(ang_c), np.cos(ang_c)], axis=-1)
    sin = np.concatenate([-np.sin(ang_r), np.sin(ang_r), -np.sin(ang_c), np.sin(ang_c)], axis=-1)
    reps = B_W // DQK_B
    return jnp.asarray(np.tile(cos, (1, reps)), F32), jnp.asarray(np.tile(sin, (1, reps)), F32)


def _block_diag(x):
    lo, hi = x[:, :LANES], x[:, LANES:]
    first = lax.broadcasted_iota(jnp.int32, lo.shape, 1) < DK_A
    top = jnp.concatenate([jnp.where(first, lo, 0), jnp.where(first, 0, lo)], axis=0)
    bot = jnp.concatenate([jnp.where(first, hi, 0), jnp.where(first, 0, hi)], axis=0)
    zero = jnp.zeros_like(top)
    return jnp.concatenate([jnp.concatenate([top, zero], axis=1), jnp.concatenate([zero, bot], axis=1)], axis=0)


SOLVE_CHUNKS = 8
DELTA_ROWS = 1024


def _delta_kernel(*refs, nb, length, has_state):
    if has_state:
        (q_ref, k_ref, v_ref, z_ref, gcs_ref, beta_ref, dn_ref, s0_ref, o_ref, st_ref, *scr) = refs
    else:
        (q_ref, k_ref, v_ref, z_ref, gcs_ref, beta_ref, dn_ref, o_ref, st_ref, *scr) = refs
        s0_ref = None
    qs, ks, vs, gexp, bexp, u0_s, kg_s, dec_s, w_s, qg_s, qkm_s, of_s, ob_s, s_s = scr
    L = length
    R = nb * L
    n_chunks = L // CHUNK
    grp = pl.program_id(1)
    W = GROUP_W
    N_GRP = A_W // W

    r256 = lax.broadcasted_iota(jnp.int32, (W, W), 0)
    c256 = lax.broadcasted_iota(jnp.int32, (W, W), 1)
    bd_mask = (r256 >> 6) == (c256 >> 6)
    ones_bd = jnp.where(bd_mask, 1.0, 0.0).astype(BF16)
    ri = lax.broadcasted_iota(jnp.int32, (CHUNK, W), 0)
    cj = lax.broadcasted_iota(jnp.int32, (CHUNK, W), 1) & (CHUNK - 1)
    eye_p = ri == cj
    er = lax.broadcasted_iota(jnp.int32, (LANES, 2 * W), 0)
    ec = lax.broadcasted_iota(jnp.int32, (LANES, 2 * W), 1)
    expand = jnp.where(er == (ec >> 8) * H_A + grp * 4 + ((ec & (W - 1)) >> 6), 1.0, 0.0).astype(BF16)

    bd = _block_diag

    def prologue(bi, carry):
        rows_b = pl.ds(pl.multiple_of(bi * L, L), L)

        def l2n(y):
            return y * lax.rsqrt(_dot_split(y * y, ones_bd) + EPS)

        qs[rows_b, :] = l2n(q_ref[bi]) * (DK_A ** -0.5)
        ks[rows_b, :] = l2n(k_ref[bi])
        vs[rows_b, :] = v_ref[bi]
        ge_all = _dot_split(gcs_ref[bi], expand)
        be_all = _dot_split(beta_ref[bi], expand)
        for d in range(2):
            gexp[d, rows_b, :] = ge_all[:, d * W:(d + 1) * W]
            bexp[d, rows_b, :] = be_all[:, d * W:(d + 1) * W]
        return carry

    lax.fori_loop(0, nb, prologue, 0)

    def solve_body(j, carry):
        chunks = []
        for k in range(SOLVE_CHUNKS):
            chunk = j * SOLVE_CHUNKS + k
            rows = pl.ds(pl.multiple_of(chunk * CHUNK, CHUNK), CHUNK)
            kc = ks[rows, :]
            qc = qs[rows, :]
            kb = kc.astype(BF16)
            res = lax.dot_general(jnp.concatenate([qc.astype(BF16), kb], axis=0), bd(kb),
                                  (((1,), (1,)), ((), ())), preferred_element_type=F32)
            qk, kk = res[:CHUNK], res[CHUNK:]
            ch = dict(rows=rows, bd_k=bd(kb), bd_v=bd(vs[rows, :].astype(BF16)), t=[], scale_u=[], scale_w=[])
            nsum = None
            for d in range(2):
                ge = gexp[d, rows, :]
                be = bexp[d, rows, :]
                if d == 0:
                    incl, strict, gl = ri >= cj, ri > cj, ge[CHUNK - 1:CHUNK, :]
                else:
                    incl, strict, gl = ri <= cj, ri < cj, ge[0:1, :]
                g_row = jnp.sum(jnp.where(eye_p, ge, 0.0), axis=0, keepdims=True)
                b_row = jnp.sum(jnp.where(eye_p, be, 0.0), axis=0, keepdims=True)
                dec_m = jnp.exp(jnp.where(incl, ge - g_row, -jnp.inf))
                nmat = jnp.where(strict, be * kk * dec_m, 0.0)
                nsum = nmat if nsum is None else nsum + nmat
                qkm_s[grp, d, rows, :] = jnp.where(incl, qk * dec_m, 0.0).astype(BF16)
                qg_s[grp, d, rows, :] = (qc * jnp.exp(ge)).astype(BF16)
                kg_s[grp, d, rows, :] = kc * jnp.exp(gl - ge)
                dec_s[grp, d, pl.ds(pl.multiple_of(chunk * 8, 8), 8), :] = jnp.broadcast_to(jnp.exp(gl), (8, W))
                ch["scale_u"].append(b_row)
                ch["scale_w"].append(b_row * jnp.exp(g_row))
                ch["t"].append(jnp.where(eye_p, 1.0, 0.0) - jnp.where((ri >> 1) == (cj >> 1), nmat, 0.0))
            ch["n_both"] = nsum.astype(BF16)
            chunks.append(ch)

        for lb in range(1, 6):
            same = (ri >> (lb + 1)) == (cj >> (lb + 1))
            pair = (same & (((ri >> lb) & 1) == 1) & (((cj >> lb) & 1) == 0),
                    same & (((cj >> lb) & 1) == 1) & (((ri >> lb) & 1) == 0))
            for ch in chunks:
                ch["tb"] = [t.astype(BF16) for t in ch["t"]]
                rhs = bd(jnp.where(pair[0] | pair[1], ch["n_both"], 0))
                ch["y"] = _dot(jnp.concatenate(ch["tb"], axis=0), rhs).astype(BF16)
            for ch in chunks:
                ys = (jnp.where(pair[0], ch["y"][:CHUNK], 0), jnp.where(pair[1], ch["y"][CHUNK:], 0))
                ch["t"] = [ch["t"][d] - _dot(ys[d], bd(ch["tb"][d])) for d in range(2)]
        for ch in chunks:
            tu = jnp.concatenate([(ch["t"][d] * ch["scale_u"][d]).astype(BF16) for d in range(2)], axis=0)
            tw = jnp.concatenate([(ch["t"][d] * ch["scale_w"][d]).astype(BF16) for d in range(2)], axis=0)
            u0 = _dot(tu, ch["bd_v"])
            w = _dot(tw, ch["bd_k"]).astype(BF16)
            for d in range(2):
                u0_s[grp, d, ch["rows"], :] = u0[d * CHUNK:(d + 1) * CHUNK]
                w_s[grp, d, ch["rows"], :] = w[d * CHUNK:(d + 1) * CHUNK]
        return carry

    lax.fori_loop(0, R // (SOLVE_CHUNKS * CHUNK), solve_body, 0)

    s_s[grp] = jnp.zeros(s_s.shape[1:], F32)
    if has_state:
        for bi in range(nb):
            for d in range(2):
                for h in range(4):
                    s_s[grp, bi, d, h * DK_A:(h + 1) * DK_A, h * DK_A:(h + 1) * DK_A] = s0_ref[bi, d, h]

    def scan_body(i, carry):
        chains = []
        for gi in range(N_GRP):
            for bi in range(nb):
                for d in range(2):
                    c = i if d == 0 else n_chunks - 1 - i
                    rows = pl.ds(pl.multiple_of(bi * L + c * CHUNK, CHUNK), CHUNK)
                    tile = pl.ds(pl.multiple_of((bi * n_chunks + c) * 8, 8), 8)
                    chains.append(dict(
                        gi=gi, bi=bi, d=d, rows=rows, s=s_s[gi, bi, d], u0=u0_s[gi, d, rows, :],
                        kg=kg_s[gi, d, rows, :], qkm=qkm_s[gi, d, rows, :], decay=dec_s[gi, d, tile, :][0:1],
                        wq=jnp.concatenate([w_s[gi, d, rows, :], qg_s[gi, d, rows, :]], axis=0)))
        for c in chains:
            c["ws"] = _dot(c["wq"], c["s"].astype(BF16))
        for c in chains:
            c["ub"] = (c["u0"] - c["ws"][:CHUNK]).astype(BF16)
        for c in chains:
            c["o"] = c["ws"][CHUNK:] + _dot(c["qkm"], bd(c["ub"]))
            upd = _dot(c["kg"].T.astype(BF16), c["ub"])
            c["s_new"] = c["s"] * c["decay"] + jnp.where(bd_mask, upd, 0.0)
        for c in chains:
            s_s[c["gi"], c["bi"], c["d"]] = c["s_new"]
            if c["d"] == 0:
                of_s[c["gi"], c["rows"], :] = c["o"]
            else:
                ob_s[c["gi"], c["rows"], :] = c["o"]
        return carry

    def epilogue(bi, carry):
        rows_b = pl.ds(pl.multiple_of(bi * L, L), L)
        for gi in range(N_GRP):
            lanes = slice(gi * W, (gi + 1) * W)
            o = of_s[gi, rows_b, :] + ob_s[gi, rows_b, :]
            ms = _dot_split(o * o, ones_bd) * (1.0 / DK_A)
            o = o * lax.rsqrt(ms + EPS) * dn_ref[...]
            o_ref[bi, :, lanes] = o * _silu(z_ref[bi, :, lanes])
        return carry

    @pl.when(grp == N_GRP - 1)
    def _():
        lax.fori_loop(0, n_chunks, scan_body, 0)
        lax.fori_loop(0, nb, epilogue, 0)
        for gi in range(N_GRP):
            for bi in range(nb):
                for d in range(2):
                    for h in range(4):
                        st_ref[bi, d, gi * 4 + h] = s_s[gi, bi, d, h * DK_A:(h + 1) * DK_A, h * DK_A:(h + 1) * DK_A]


def _delta(qkv, z, gcs, beta, dnorm, s0):
    b, l, _ = qkv.shape
    has_state = s0 is not None
    w = GROUP_W
    n_grp = A_W // w
    nb = DELTA_ROWS // l
    r = nb * l

    def slab(k):
        return pl.BlockSpec((nb, l, w), lambda i, g: (i, 0, k * n_grp + g))

    narrow = pl.BlockSpec((nb, l, LANES), lambda i, g: (i, 0, 0))
    wide = pl.BlockSpec((nb, l, A_W), lambda i, g: (i, 0, 0))
    in_specs = [slab(0), slab(1), slab(2), wide, narrow, narrow,
                pl.BlockSpec((1, w), lambda i, g: (0, 0))]
    args = [qkv, qkv, qkv, z, gcs, beta, dnorm]
    if has_state:
        in_specs.append(pl.BlockSpec((nb, None, 2, 4, DK_A, DK_A), lambda i, g: (i, 0, 0, g, 0, 0)))
        args.append(s0)
    per_grp = lambda shape, dt: pltpu.VMEM((n_grp,) + shape, dt)
    scratch = ([pltpu.VMEM((r, w), F32)] * 3 + [pltpu.VMEM((2, r, w), F32)] * 2
               + [per_grp((2, r, w), F32)] * 2 + [per_grp((2, r // CHUNK * 8, w), F32)]
               + [per_grp((2, r, w), BF16)] * 3 + [per_grp((r, w), F32)] * 2
               + [per_grp((nb, 2, w, w), F32)])
    return pl.pallas_call(
        functools.partial(_delta_kernel, nb=nb, length=l, has_state=has_state),
        grid=(b // nb, n_grp),
        in_specs=in_specs,
        out_specs=[wide, pl.BlockSpec((nb, None, 2, H_A, DK_A, DK_A), lambda i, g: (i, 0, 0, 0, 0, 0))],
        out_shape=[jax.ShapeDtypeStruct((b, l, A_W), F32),
                   jax.ShapeDtypeStruct((b, 1, 2, H_A, DK_A, DK_A), F32)],
        scratch_shapes=scratch,
        compiler_params=_cparams(("arbitrary", "arbitrary")),
        name="delta_state" if has_state else "delta",
    )(*args)


def _pad_lanes(x, width):
    return jnp.pad(x, [(0, 0)] * (x.ndim - 1) + [(0, width - x.shape[-1])])


def kernel(x_prompt, x_sample, cache_diff_k, cache_diff_v, state_delta, c, c_ctx, w_ada, b_ada, norm_ffn1,
           w_ffn1_in, w_ffn1_out, norm_mix, w_in, conv_w, a_log, dt_bias, delta_norm, lambda_q1, lambda_k1,
           lambda_q2, lambda_k2, diff_norm, w_out, norm_ffn2, w_ffn2_in, w_ffn2_out, norm_final):
    bp, lp, _ = x_prompt.shape
    bs, ls, _ = x_sample.shape
    past = cache_diff_k.shape[2]

    w_mix_out = w_out[0].astype(BF16)
    wi = w_in[0]
    a_end = 4 * A_W
    dec_w = wi[:, a_end:a_end + 2 * H_A]
    bet_w = wi[:, a_end + 2 * H_A:a_end + 4 * H_A]
    rest = wi[:, a_end + 4 * H_A:]
    w_pack = jnp.concatenate([wi[:, :a_end], rest, _pad_lanes(dec_w, LANES), _pad_lanes(bet_w, LANES)],
                             axis=1).astype(BF16)
    g1 = norm_ffn1[0].reshape(1, D_MODEL)
    gm = norm_mix[0].reshape(1, D_MODEL)
    g2 = norm_ffn2[0].reshape(1, D_MODEL)
    gf = norm_final.reshape(1, D_MODEL)
    gpar = jnp.pad(jnp.stack([a_log[0].reshape(-1), dt_bias[0].reshape(-1)]), ((0, 6), (0, LANES - 2 * H_A)))
    dnorm = jnp.tile(delta_norm[0], 4).reshape(1, GROUP_W)
    lam_pack = jnp.pad(jnp.concatenate([lambda_q1, lambda_k1, lambda_q2, lambda_k2], axis=0),
                       ((0, 4), (0, LANES - DQK_B)))
    dgain = diff_norm[0].reshape(1, DV_B)
    cos, sin = _rope_tables(ls)

    cond = jnp.pad(jnp.concatenate([c_ctx[None, :], c], axis=0), ((0, 8 - 1 - bs), (0, 0)))
    mod = _adaln(cond, w_ada[0], b_ada[0]).reshape(8, N_MOD, D_MODEL)

    def trunk(x3, rows_per_cond, ctx, w1, w2):
        b, l, _ = x3.shape
        x = x3.reshape(b * l, D_MODEL)
        x1, *w1 = _ffn1(x, mod, g1, *w1, rows_per_cond)
        outs = _inproj(x1, mod, gm, w_pack, conv_w[0], gpar, rows_per_cond, l, ctx is None)
        qkv, z, bq, bk, bv, gcs, beta = outs[:7]
        r3 = lambda a: a.reshape(b, l, a.shape[-1])
        s0 = None if ctx is None else ctx[2]
        oa, st = _delta(r3(qkv), r3(z), r3(gcs), r3(beta), dnorm, s0)
        if ctx is None:
            ob = _attn_ctx(r3(bq), r3(bk), r3(bv), lam_pack, dgain)
        else:
            ob = _attn_lat(r3(bq), r3(bk), r3(bv), ctx[0], ctx[1], cos, sin, lam_pack, dgain)
        y, *w2 = _post(x1, oa.reshape(b * l, A_W), ob.reshape(b * l, B_W), mod, g2, gf, w_mix_out, *w2,
                       rows_per_cond)
        return y.reshape(b, l, D_MODEL), outs[7:], st, w1, w2

    y_prompt, (k_maps, v_heads), s_ctx, w1, w2 = trunk(x_prompt, None, None, (w_ffn1_in[0], w_ffn1_out[0]),
                                                       (w_ffn2_in[0], w_ffn2_out[0]))
    ctx = (cache_diff_k[:, 0].reshape(bs, past, B_W), cache_diff_v[:, 0].reshape(bs, past, B_W), state_delta)
    y_sample = trunk(x_sample, ls, ctx, w1, w2)[0]
    new_k = k_maps.reshape(bp, 1, lp, H_B, 2, DQK_B)
    new_v = v_heads.reshape(bp, 1, lp, H_B, DV_B)
    return (y_prompt, y_sample, new_k, new_v, s_ctx)
```

```python
import functools
import math

import jax
import jax.numpy as jnp
import numpy as np
from jax import lax
from jax.experimental import pallas as pl
from jax.experimental.pallas import tpu as pltpu

F32 = jnp.float32
BF16 = jnp.bfloat16

D_MODEL = 1024
D_FF = 2816
N_MOD = 9
H_A = 8
DK_A = 64
A_W = H_A * DK_A
H_B = 4
DQK_B = 64
DV_B = 128
B_W = H_B * DV_B
CONV_K = 5
CHUNK = 64
GRID_W = 64
ROPE_THETA = 10000.0
EPS = 1e-6
LAM_INIT = 0.8 - 0.6 * math.exp(-0.3 * 0)

LANES = 128
GROUP_W = 4 * DK_A
IN_PACK_COLS = 3 * A_W + A_W + 3 * B_W + 2 * LANES
VMEM_LIMIT = 56 * 1024 * 1024

MXU_W = 256
TOKEN_TILE = 512
FF_CHUNKS = ((0, 6 * MXU_W), (6 * MXU_W, D_FF))
Q_TILE = 256


def _cparams(sem):
    return pltpu.CompilerParams(dimension_semantics=sem, vmem_limit_bytes=VMEM_LIMIT)


def _resident(shape):
    nd = len(shape)
    return pl.BlockSpec(shape, lambda *_: (0,) * nd, pipeline_mode=pl.Buffered(1))


def _silu(x):
    half = 0.5 * x
    return half * jnp.tanh(half) + half


def _mod_norm(x, gain, shift, scale):
    y = x * lax.rsqrt(jnp.mean(x * x, axis=-1, keepdims=True) + EPS)
    return (y * gain) * (1.0 + scale) + shift


def _dot(a, b):
    return jnp.dot(a, b, preferred_element_type=F32)


def _dot_split(a, b_bf16):
    hi = a.astype(BF16)
    lo = (a - hi.astype(F32)).astype(BF16)
    return _dot(hi, b_bf16) + _dot(lo, b_bf16)


def _adaln_kernel(c_ref, w_ref, b_ref, o_ref):
    s = _silu(c_ref[...])
    o_ref[...] = _dot(s.astype(BF16), w_ref[...].astype(BF16)) + b_ref[...]


def _adaln(cond, w_ada, b_ada):
    n = N_MOD * D_MODEL
    tn = n // 4
    return pl.pallas_call(
        _adaln_kernel,
        grid=(n // tn,),
        in_specs=[pl.BlockSpec((8, D_MODEL), lambda j: (0, 0)),
                  pl.BlockSpec((D_MODEL, tn), lambda j: (0, j)),
                  pl.BlockSpec((1, tn), lambda j: (0, j))],
        out_specs=pl.BlockSpec((8, tn), lambda j: (0, j)),
        out_shape=jax.ShapeDtypeStruct((8, n), F32),
        compiler_params=_cparams(("arbitrary",)),
        name="adaln",
    )(cond, w_ada, b_ada.reshape(1, n))


def _swiglu_update(x, mod_ref, k0, gain, w_in_ref, w_out_ref):
    h = _mod_norm(x, gain, mod_ref[k0:k0 + 1, :], mod_ref[k0 + 1:k0 + 2, :]).astype(BF16)
    acc = None
    for lo, hi in FF_CHUNKS:
        g = _dot(h, w_in_ref[:, lo:hi])
        u = _dot(h, w_in_ref[:, D_FF + lo:D_FF + hi])
        part = _dot((_silu(g) * u).astype(BF16), w_out_ref[lo:hi, :])
        acc = part if acc is None else acc + part
    return x + (0.5 * mod_ref[k0 + 2:k0 + 3, :]) * acc


W_CHUNKS = 11
W_IN_CHUNK = 2 * D_FF // W_CHUNKS
W_OUT_CHUNK = D_FF // W_CHUNKS


def _stage_ffn_weights(step, w_in_ref, w_out_ref, w_in_bf_ref, w_out_bf_ref, w_in_s, w_out_s):
    for c in range(W_CHUNKS):
        @pl.when(step == c)
        def _():
            a = w_in_ref[...].astype(BF16)
            b = w_out_ref[...].astype(BF16)
            w_in_bf_ref[...] = a
            w_out_bf_ref[...] = b
            w_in_s[:, c * W_IN_CHUNK:(c + 1) * W_IN_CHUNK] = a
            w_out_s[c * W_OUT_CHUNK:(c + 1) * W_OUT_CHUNK, :] = b


def _ffn_weight_plumbing(cast):
    if not cast:
        return [_resident((D_MODEL, 2 * D_FF)), _resident((D_FF, D_MODEL))], [], [], []
    last = W_CHUNKS - 1
    w_in_chunk = pl.BlockSpec((D_MODEL, W_IN_CHUNK), lambda s: (0, jnp.minimum(s, last)))
    w_out_chunk = pl.BlockSpec((W_OUT_CHUNK, D_MODEL), lambda s: (jnp.minimum(s, last), 0))
    out_shape = [jax.ShapeDtypeStruct((D_MODEL, 2 * D_FF), BF16), jax.ShapeDtypeStruct((D_FF, D_MODEL), BF16)]
    scratch = [pltpu.VMEM((D_MODEL, 2 * D_FF), BF16), pltpu.VMEM((D_FF, D_MODEL), BF16)]
    return [w_in_chunk, w_out_chunk], [w_in_chunk, w_out_chunk], out_shape, scratch


def _tile_of(cast):
    off = W_CHUNKS if cast else 0
    return lambda s: jnp.maximum(s - off, 0)


def _mod_spec(rows_per_cond, tm, tile):
    if rows_per_cond is None:
        return pl.BlockSpec((None, N_MOD, D_MODEL), lambda s: (0, 0, 0))
    return pl.BlockSpec((None, N_MOD, D_MODEL), lambda s: (1 + (tile(s) * tm) // rows_per_cond, 0, 0))


def _ffn1_kernel(*refs, cast):
    x_ref, mod_ref, gain_ref, w_in_ref, w_out_ref, o_ref = refs[:6]
    if cast:
        step = pl.program_id(0)
        w_in_s, w_out_s = refs[8:10]
        _stage_ffn_weights(step, w_in_ref, w_out_ref, refs[6], refs[7], w_in_s, w_out_s)

        @pl.when(step >= W_CHUNKS)
        def _():
            o_ref[...] = _swiglu_update(x_ref[...], mod_ref, 0, gain_ref[...], w_in_s, w_out_s)
    else:
        o_ref[...] = _swiglu_update(x_ref[...], mod_ref, 0, gain_ref[...], w_in_ref, w_out_ref)


def _ffn1(x, mod, gain, w_in, w_out, rows_per_cond):
    t = x.shape[0]
    tm = TOKEN_TILE
    cast = w_in.dtype == F32
    tile = _tile_of(cast)
    row = pl.BlockSpec((tm, D_MODEL), lambda s: (tile(s), 0))
    w_in_specs, w_out_specs, w_shapes, w_scratch = _ffn_weight_plumbing(cast)
    outs = pl.pallas_call(
        functools.partial(_ffn1_kernel, cast=cast),
        grid=(t // tm + (W_CHUNKS if cast else 0),),
        in_specs=[row, _mod_spec(rows_per_cond, tm, tile), _resident((1, D_MODEL))] + w_in_specs,
        out_specs=[row] + w_out_specs,
        out_shape=[jax.ShapeDtypeStruct((t, D_MODEL), F32)] + w_shapes,
        scratch_shapes=w_scratch,
        compiler_params=_cparams(("arbitrary",)),
        name="ffn1",
    )(x, mod, gain, w_in, w_out)
    return (outs[0], outs[1], outs[2]) if cast else (outs[0], w_in, w_out)


_IN_SEGS = (("qkv", 0, 3 * A_W), ("z", 3 * A_W, A_W), ("bq", 4 * A_W, B_W), ("bk", 4 * A_W + B_W, B_W),
            ("bv", 4 * A_W + 2 * B_W, B_W), ("dec", 4 * A_W + 3 * B_W, LANES),
            ("bet", 4 * A_W + 3 * B_W + LANES, LANES))


HALO = 8
CONV_SLAB = 256


def _inproj_kernel(*refs, tm, seq_len, emit_maps):
    x_ref, xp_ref, xn_ref, mod_ref, gain_ref, w_ref, cw_ref, gpar_ref = refs[:8]
    qkv_ref, z_ref, bq_ref, bk_ref, bv_ref, gcs_ref, beta_ref = refs[8:15]
    hs, xpad, ybuf = refs[-3:]
    shift, scale = mod_ref[3:4, :], mod_ref[4:5, :]
    hf = _mod_norm(x_ref[...], gain_ref[...], shift, scale)
    h = hf.astype(BF16)
    segs = dict((name, (start, width)) for name, start, width in _IN_SEGS)

    def proj(name):
        start, width = segs[name]
        return _dot(h, w_ref[:, start:start + width])

    seg = min(tm, seq_len)
    n_seg = tm // seg
    ext = seg + 2 * HALO
    m = ext // 8
    pad = (CONV_K // 2) * 8
    qkv_w = 3 * A_W
    units = [(s, c) for s in range(n_seg) for c in range(qkv_w // CONV_SLAB)]
    def put_rows(lo, hi, val):
        for ct in range(D_MODEL // LANES):
            hs[ct, lo:hi, :] = val[:, ct * LANES:(ct + 1) * LANES]

    if seq_len > tm:
        tiles = seq_len // tm
        pos = pl.program_id(0) % tiles
        put_rows(0, HALO, _mod_norm(xp_ref[...], gain_ref[...], shift, scale) * jnp.where(pos > 0, 1.0, 0.0))
        put_rows(HALO + seg, ext,
                 _mod_norm(xn_ref[...], gain_ref[...], shift, scale) * jnp.where(pos < tiles - 1, 1.0, 0.0))
    else:
        for s in range(n_seg):
            put_rows(s * ext, s * ext + HALO, jnp.zeros((HALO, D_MODEL), F32))
            put_rows(s * ext + HALO + seg, (s + 1) * ext, jnp.zeros((HALO, D_MODEL), F32))
    for s in range(n_seg):
        put_rows(s * ext + HALO, s * ext + HALO + seg, hf[s * seg:(s + 1) * seg])
    h_il = [jnp.concatenate(
        [jnp.concatenate([hs[ct, pl.ds(s * ext + v, 8, stride=m), :] for v in range(m)], axis=0)
         for ct in range(D_MODEL // LANES)], axis=1).astype(BF16) for s in range(n_seg)]

    def project(s, c):
        cs = slice(c * CONV_SLAB, (c + 1) * CONV_SLAB)
        base = s * (ext + 2 * pad)
        res = _dot(h_il[s], w_ref[:, cs])
        xpad[base + pad:base + pad + ext, cs] = res
        for j in range(CONV_K // 2):
            xpad[base + 8 * j:base + 8 * (j + 1), cs] = pltpu.roll(res[ext - pad + 8 * j:ext - pad + 8 * (j + 1)], 1, 0)
            xpad[base + pad + ext + 8 * j:base + pad + ext + 8 * (j + 1), cs] = pltpu.roll(res[8 * j:8 * (j + 1)], 7, 0)

    def convolve(s, c):
        cs = slice(c * CONV_SLAB, (c + 1) * CONV_SLAB)
        base = s * (ext + 2 * pad)
        acc = None
        for tap in range(CONV_K):
            term = xpad[base + 8 * tap:base + 8 * tap + ext, cs] * cw_ref[tap:tap + 1, cs]
            acc = term if acc is None else acc + term
        y = _silu(acc)
        for j in range(CONV_SLAB // LANES):
            ct = c * (CONV_SLAB // LANES) + j
            for v in range(m):
                ybuf[ct, pl.ds(v, 8, stride=m), :] = y[8 * v:8 * (v + 1), j * LANES:(j + 1) * LANES]
            qkv_ref[s * seg:(s + 1) * seg, ct * LANES:(ct + 1) * LANES] = ybuf[ct, HALO:HALO + seg, :]

    z_ref[...] = proj("z")
    bq_ref[...] = proj("bq")
    zk = proj("bk")
    zv = proj("bv")
    bk_ref[...] = zk
    bv_ref[...] = zv
    if emit_maps:
        bk8_ref, bv4_ref = refs[15:17]
        for r in range(2 * H_B):
            bk8_ref[:, r, :] = zk[:, r * DQK_B:(r + 1) * DQK_B]
        for r in range(H_B):
            bv4_ref[:, r, :] = zv[:, r * DV_B:(r + 1) * DV_B]
    for unit in units:
        project(*unit)
        convolve(*unit)

    gpar = gpar_ref[...]
    dec_start = segs["dec"][0]
    gate_pre = _dot(h, w_ref[:, dec_start:dec_start + 2 * LANES])
    xg = gate_pre[:, :LANES] + gpar[1:2, :]
    softplus = jnp.maximum(xg, 0.0) + jnp.log1p(jnp.exp(-jnp.abs(xg)))
    lane = lax.broadcasted_iota(jnp.int32, (tm, LANES), 1)
    g = jnp.where(lane < 2 * H_A, -jnp.exp(gpar[0:1, :]) * softplus, 0.0)
    beta_ref[...] = jax.nn.sigmoid(gate_pre[:, LANES:])
    pos_c = lax.broadcasted_iota(jnp.int32, (tm, 1), 0) & (CHUNK - 1)
    gcf = g
    gcb = g
    for sft in (1, 2, 4, 8, 16, 32):
        gcf = gcf + jnp.where(pos_c >= sft, pltpu.roll(gcf, sft, axis=0), 0.0)
        gcb = gcb + jnp.where(pos_c < CHUNK - sft, pltpu.roll(gcb, tm - sft, axis=0), 0.0)
    gcs_ref[...] = jnp.where(lane < H_A, gcf, gcb)


def _inproj(x, mod, gain, w_pack, conv_w, gpar, rows_per_cond, seq_len, emit_maps):
    t = x.shape[0]
    tm = TOKEN_TILE
    hb = tm // HALO
    n_seg = max(1, tm // seq_len)
    ext = tm // n_seg + 2 * HALO
    row = pl.BlockSpec((tm, D_MODEL), lambda i: (i, 0))
    prev = pl.BlockSpec((HALO, D_MODEL), lambda i: (jnp.maximum(i * hb - 1, 0), 0))
    nxt = pl.BlockSpec((HALO, D_MODEL), lambda i: (jnp.minimum((i + 1) * hb, t // HALO - 1), 0))
    widths = [w for _, _, w in _IN_SEGS]
    out_specs = [pl.BlockSpec((tm, w), lambda i: (i, 0)) for w in widths]
    out_shape = [jax.ShapeDtypeStruct((t, w), F32) for w in widths]
    if emit_maps:
        out_specs += [pl.BlockSpec((tm, 2 * H_B, DQK_B), lambda i: (i, 0, 0)),
                      pl.BlockSpec((tm, H_B, DV_B), lambda i: (i, 0, 0))]
        out_shape += [jax.ShapeDtypeStruct((t, 2 * H_B, DQK_B), F32), jax.ShapeDtypeStruct((t, H_B, DV_B), F32)]
    return pl.pallas_call(
        functools.partial(_inproj_kernel, tm=tm, seq_len=seq_len, emit_maps=emit_maps),
        grid=(t // tm,),
        in_specs=[row, prev, nxt, _mod_spec(rows_per_cond, tm, lambda s: s), _resident((1, D_MODEL)),
                  _resident((D_MODEL, IN_PACK_COLS)), _resident((CONV_K, 3 * A_W)), _resident((8, LANES))],
        out_specs=out_specs,
        out_shape=out_shape,
        scratch_shapes=[pltpu.VMEM((D_MODEL // LANES, n_seg * ext, LANES), F32),
                        pltpu.VMEM((n_seg * (ext + 2 * (CONV_K // 2) * 8), 3 * A_W), F32),
                        pltpu.VMEM((3 * A_W // LANES, ext, LANES), F32)],
        compiler_params=_cparams(("arbitrary",)),
        name="inproj",
    )(x, x, x, mod, gain, w_pack, conv_w, gpar)


def _post_kernel(*refs, cast):
    x_ref, oa_ref, ob_ref, mod_ref, gain_ref, gfin_ref, wmix_ref, w_in_ref, w_out_ref, o_ref = refs[:10]

    def tokens(w_in, w_out):
        y = _dot(oa_ref[...].astype(BF16), wmix_ref[0:A_W, :]) + _dot(ob_ref[...].astype(BF16), wmix_ref[A_W:, :])
        x = x_ref[...] + mod_ref[5:6, :] * y
        x = _swiglu_update(x, mod_ref, 6, gain_ref[...], w_in, w_out)
        o_ref[...] = (x * lax.rsqrt(jnp.mean(x * x, axis=-1, keepdims=True) + EPS)) * gfin_ref[...]

    if cast:
        step = pl.program_id(0)
        w_in_s, w_out_s = refs[12:14]
        _stage_ffn_weights(step, w_in_ref, w_out_ref, refs[10], refs[11], w_in_s, w_out_s)
        pl.when(step >= W_CHUNKS)(lambda: tokens(w_in_s, w_out_s))
    else:
        tokens(w_in_ref, w_out_ref)


def _post(x, oa, ob, mod, gain, gfin, w_mix, w_in, w_out, rows_per_cond):
    t = x.shape[0]
    tm = TOKEN_TILE
    cast = w_in.dtype == F32
    tile = _tile_of(cast)
    row = pl.BlockSpec((tm, D_MODEL), lambda s: (tile(s), 0))
    half = pl.BlockSpec((tm, A_W), lambda s: (tile(s), 0))
    w_in_specs, w_out_specs, w_shapes, w_scratch = _ffn_weight_plumbing(cast)
    outs = pl.pallas_call(
        functools.partial(_post_kernel, cast=cast),
        grid=(t // tm + (W_CHUNKS if cast else 0),),
        in_specs=[row, half, half, _mod_spec(rows_per_cond, tm, tile), _resident((1, D_MODEL)),
                  _resident((1, D_MODEL)), _resident((A_W + B_W, D_MODEL))] + w_in_specs,
        out_specs=[row] + w_out_specs,
        out_shape=[jax.ShapeDtypeStruct((t, D_MODEL), F32)] + w_shapes,
        scratch_shapes=w_scratch,
        compiler_params=_cparams(("arbitrary",)),
        name="post",
    )(x, oa, ob, mod, gain, gfin, w_mix, w_in, w_out)
    return (outs[0], outs[1], outs[2]) if cast else (outs[0], w_in, w_out)


def _rope(x, cos, sin_signed):
    w = x.shape[-1]
    lane = lax.broadcasted_iota(jnp.int32, x.shape, 1)
    partner = jnp.where((lane & 31) < 16, pltpu.roll(x, w - 16, axis=1), pltpu.roll(x, 16, axis=1))
    return x * cos + partner * sin_signed


def _store_values(v_s, rows, v):
    for h in range(H_B):
        vh = v[:, h * DV_B:(h + 1) * DV_B]
        v_s[rows, 2 * h * DV_B:(2 * h + 1) * DV_B] = vh.astype(BF16)
        v_s[rows, (2 * h + 1) * DV_B:(2 * h + 2) * DV_B] = jnp.ones(vh.shape, BF16)


def _attn_core(q, k_ref, v_ref, lam, gain_ref, o_ref):
    tq = q.shape[0]
    lane = lax.broadcasted_iota(jnp.int32, (tq, DV_B), 1)
    qs = q * (DQK_B ** -0.5)
    heads = [slice(h * DV_B, (h + 1) * DV_B) for h in range(H_B)]
    scores = []
    for hs in heads:
        qh = qs[:, hs]
        q2 = jnp.concatenate([jnp.where(lane < DQK_B, qh, 0.0), jnp.where(lane >= DQK_B, qh, 0.0)], axis=0)
        scores.append(lax.dot_general(q2.astype(BF16), k_ref[:, hs], (((1,), (1,)), ((), ())),
                                      preferred_element_type=F32))
    for h, (hs, s) in enumerate(zip(heads, scores)):
        e = jnp.exp(s - jnp.max(s, axis=-1, keepdims=True))
        rs = _dot(e.astype(BF16), v_ref[:, 2 * h * DV_B:(2 * h + 2) * DV_B])
        r = rs[:, :DV_B] / rs[:, DV_B:]
        o = r[:tq] - lam * r[tq:]
        o = o * lax.rsqrt(jnp.mean(o * o, axis=-1, keepdims=True) + EPS)
        o_ref[:, hs] = (o * gain_ref[...]) * (1.0 - LAM_INIT)


def _lambda(lam_ref):
    l = lam_ref[...]
    s1 = jnp.sum(l[0:1] * l[1:2], axis=-1, keepdims=True)
    s2 = jnp.sum(l[2:3] * l[3:4], axis=-1, keepdims=True)
    return jnp.exp(s1) - jnp.exp(s2) + LAM_INIT


CTX_SEQS = 4


def _attn_ctx_kernel(q_ref, k_ref, v_ref, lam_ref, gain_ref, o_ref, k_s, v_s):
    lam = _lambda(lam_ref)

    def one_sequence(bi, carry):
        k_s[...] = k_ref[bi].astype(BF16)
        _store_values(v_s, slice(None), v_ref[bi])
        _attn_core(q_ref[bi], k_s, v_s, lam, gain_ref, o_ref.at[bi])
        return carry

    lax.fori_loop(0, CTX_SEQS, one_sequence, 0)


def _attn_ctx(q, k, v, lam_pack, gain):
    b, l, _ = q.shape
    blk = pl.BlockSpec((CTX_SEQS, l, B_W), lambda i: (i, 0, 0))
    return pl.pallas_call(
        _attn_ctx_kernel,
        grid=(b // CTX_SEQS,),
        in_specs=[blk, blk, blk, _resident((8, LANES)), _resident((1, DV_B))],
        out_specs=blk,
        out_shape=jax.ShapeDtypeStruct((b, l, B_W), F32),
        scratch_shapes=[pltpu.VMEM((l, B_W), BF16), pltpu.VMEM((l, 2 * B_W), BF16)],
        compiler_params=_cparams(("arbitrary",)),
        name="attn_ctx",
    )(q, k, v, lam_pack, gain)


def _attn_lat_kernel(q_ref, k_ref, v_ref, ck_ref, cv_ref, cosq_ref, sinq_ref, cos_ref, sin_ref,
                     lam_ref, gain_ref, o_ref, k_s, v_s, *, past):
    @pl.when(pl.program_id(1) == 0)
    def _():
        k_s[0:past, :] = ck_ref[...].astype(BF16)
        _store_values(v_s, slice(0, past), cv_ref[...])
        k_s[past:, :] = _rope(k_ref[...], cos_ref[...], sin_ref[...]).astype(BF16)
        _store_values(v_s, slice(past, None), v_ref[...])

    q = _rope(q_ref[...], cosq_ref[...], sinq_ref[...])
    _attn_core(q, k_s, v_s, _lambda(lam_ref), gain_ref, o_ref)


def _attn_lat(q, k, v, ck, cv, cos, sin, lam_pack, gain):
    b, l, _ = q.shape
    past = ck.shape[1]
    tq = Q_TILE
    full = pl.BlockSpec((None, l, B_W), lambda i, j: (i, 0, 0))
    cache_k = cache_v = pl.BlockSpec((None, past, B_W), lambda i, j: (i, 0, 0))
    qblk = pl.BlockSpec((None, tq, B_W), lambda i, j: (i, j, 0))
    tab_q = pl.BlockSpec((tq, B_W), lambda i, j: (j, 0))
    tab = pl.BlockSpec((l, B_W), lambda i, j: (0, 0))
    return pl.pallas_call(
        functools.partial(_attn_lat_kernel, past=past),
        grid=(b, l // tq),
        in_specs=[qblk, full, full, cache_k, cache_v, tab_q, tab_q, tab, tab,
                  pl.BlockSpec((8, LANES), lambda i, j: (0, 0)), pl.BlockSpec((1, DV_B), lambda i, j: (0, 0))],
        out_specs=qblk,
        out_shape=jax.ShapeDtypeStruct((b, l, B_W), F32),
        scratch_shapes=[pltpu.VMEM((past + l, B_W), BF16), pltpu.VMEM((past + l, 2 * B_W), BF16)],
        compiler_params=_cparams(("arbitrary", "arbitrary")),
        name="attn_lat",
    )(q, k, v, ck, cv, cos, sin, cos, sin, lam_pack, gain)


def _rope_tables(length):
    pairs = DQK_B // 4
    n_rows = length // GRID_W
    pos_row = np.repeat(np.arange(n_rows), GRID_W).astype(np.float32)
    pos_col = np.tile(np.arange(GRID_W), n_rows).astype(np.float32)
    inv = (np.float32(ROPE_THETA) ** (-np.arange(pairs, dtype=np.float32) / np.float32(pairs))).astype(np.float32)
    ang_r = pos_row[:, None] * inv
    ang_c = pos_col[:, None] * inv
    cos = np.concatenate([np.cos(ang_r), np.cos(ang_r), np.cos(ang_c), np.cos(ang_c)], axis=-1)
    sin = np.concatenate([-np.sin(ang_r), np.sin(ang_r), -np.sin(ang_c), np.sin(ang_c)], axis=-1)
    reps = B_W // DQK_B
    return jnp.asarray(np.tile(cos, (1, reps)), F32), jnp.asarray(np.tile(sin, (1, reps)), F32)


def _block_diag(x):
    lo, hi = x[:, :LANES], x[:, LANES:]
    first = lax.broadcasted_iota(jnp.int32, lo.shape, 1) < DK_A
    top = jnp.concatenate([jnp.where(first, lo, 0), jnp.where(first, 0, lo)], axis=0)
    bot = jnp.concatenate([jnp.where(first, hi, 0), jnp.where(first, 0, hi)], axis=0)
    zero = jnp.zeros_like(top)
    return jnp.concatenate([jnp.concatenate([top, zero], axis=1), jnp.concatenate([zero, bot], axis=1)], axis=0)


SOLVE_CHUNKS = 8
DELTA_ROWS = 1024


def _delta_kernel(*refs, nb, length, has_state):
    if has_state:
        (q_ref, k_ref, v_ref, z_ref, gcs_ref, beta_ref, dn_ref, s0_ref, o_ref, st_ref, *scr) = refs
    else:
        (q_ref, k_ref, v_ref, z_ref, gcs_ref, beta_ref, dn_ref, o_ref, st_ref, *scr) = refs
        s0_ref = None
    qs, ks, vs, gexp, bexp, u0_s, kg_s, dec_s, w_s, qg_s, qkm_s, of_s, ob_s, s_s = scr
    L = length
    R = nb * L
    n_chunks = L // CHUNK
    grp = pl.program_id(1)
    W = GROUP_W
    N_GRP = A_W // W

    r256 = lax.broadcasted_iota(jnp.int32, (W, W), 0)
    c256 = lax.broadcasted_iota(jnp.int32, (W, W), 1)
    bd_mask = (r256 >> 6) == (c256 >> 6)
    ones_bd = jnp.where(bd_mask, 1.0, 0.0).astype(BF16)
    ri = lax.broadcasted_iota(jnp.int32, (CHUNK, W), 0)
    cj = lax.broadcasted_iota(jnp.int32, (CHUNK, W), 1) & (CHUNK - 1)
    eye_p = ri == cj
    NS = 2 * H_A
    er = lax.broadcasted_iota(jnp.int32, (LANES, 4 * W), 0)
    ec = lax.broadcasted_iota(jnp.int32, (LANES, 4 * W), 1)
    src = ((ec >> 8) & 1) * H_A + grp * 4 + ((ec & (W - 1)) >> 6)
    expand = jnp.where((er < 4 * NS) & ((er & (NS - 1)) == src) & (((er >> 4) & 1) == (ec >> 9)),
                       1.0, 0.0).astype(BF16)
    lane_s = lax.broadcasted_iota(jnp.int32, (L, LANES), 1)

    bd = _block_diag

    def head_sums(y):
        return _dot(y.astype(BF16), ones_bd)

    def prologue(bi, carry):
        rows_b = pl.ds(pl.multiple_of(bi * L, L), L)

        def l2n(y):
            return y * lax.rsqrt(head_sums(y * y) + EPS)

        qs[rows_b, :] = l2n(q_ref[bi]) * (DK_A ** -0.5)
        ks[rows_b, :] = l2n(k_ref[bi])
        vs[rows_b, :] = v_ref[bi]
        gc = gcs_ref[bi]
        bt = jnp.where(lane_s < NS, beta_ref[bi], 0.0)
        gc_hi = gc.astype(BF16).astype(F32)
        bt_hi = bt.astype(BF16).astype(F32)
        packed = (gc_hi + pltpu.roll(bt_hi, NS, axis=1) + pltpu.roll(gc - gc_hi, 2 * NS, axis=1)
                  + pltpu.roll(bt - bt_hi, 3 * NS, axis=1))
        both = _dot(packed.astype(BF16), expand)
        for d in range(2):
            gexp[d, rows_b, :] = both[:, d * W:(d + 1) * W]
            bexp[d, rows_b, :] = both[:, 2 * W + d * W:2 * W + (d + 1) * W]
        return carry

    lax.fori_loop(0, nb, prologue, 0)

    def solve_body(j, carry):
        chunks = []
        for k in range(SOLVE_CHUNKS):
            chunk = j * SOLVE_CHUNKS + k
            rows = pl.ds(pl.multiple_of(chunk * CHUNK, CHUNK), CHUNK)
            kc = ks[rows, :]
            qc = qs[rows, :]
            kb = kc.astype(BF16)
            res = lax.dot_general(jnp.concatenate([qc.astype(BF16), kb], axis=0), bd(kb),
                                  (((1,), (1,)), ((), ())), preferred_element_type=F32)
            qk, kk = res[:CHUNK], res[CHUNK:]
            ch = dict(rows=rows, bd_k=bd(kb), bd_v=bd(vs[rows, :].astype(BF16)), t=[], scale_u=[], scale_w=[])
            nsum = None
            for d in range(2):
                ge = gexp[d, rows, :]
                be = bexp[d, rows, :]
                if d == 0:
                    incl, strict, gl = ri >= cj, ri > cj, ge[CHUNK - 1:CHUNK, :]
                else:
                    incl, strict, gl = ri <= cj, ri < cj, ge[0:1, :]
                g_row = jnp.sum(jnp.where(eye_p, ge, 0.0), axis=0, keepdims=True)
                b_row = jnp.sum(jnp.where(eye_p, be, 0.0), axis=0, keepdims=True)
                dec_m = jnp.exp(jnp.where(incl, ge - g_row, -jnp.inf))
                nmat = jnp.where(strict, be * kk * dec_m, 0.0)
                nsum = nmat if nsum is None else nsum + nmat
                qkm_s[grp, d, rows, :] = jnp.where(incl, qk * dec_m, 0.0).astype(BF16)
                qg_s[grp, d, rows, :] = (qc * jnp.exp(ge)).astype(BF16)
                kg_s[grp, d, rows, :] = kc * jnp.exp(gl - ge)
                dec_s[grp, d, pl.ds(pl.multiple_of(chunk * 8, 8), 8), :] = jnp.broadcast_to(jnp.exp(gl), (8, W))
                ch["scale_u"].append(b_row)
                ch["scale_w"].append(b_row * jnp.exp(g_row))
                ch["t"].append(jnp.where(eye_p, 1.0, 0.0) - jnp.where((ri >> 1) == (cj >> 1), nmat, 0.0))
            ch["n_both"] = nsum.astype(BF16)
            chunks.append(ch)

        for lb in range(1, 6):
            same = (ri >> (lb + 1)) == (cj >> (lb + 1))
            pair = (same & (((ri >> lb) & 1) == 1) & (((cj >> lb) & 1) == 0),
                    same & (((cj >> lb) & 1) == 1) & (((ri >> lb) & 1) == 0))
            for ch in chunks:
                ch["tb"] = [t.astype(BF16) for t in ch["t"]]
                rhs = bd(jnp.where(pair[0] | pair[1], ch["n_both"], 0))
                ch["y"] = _dot(jnp.concatenate(ch["tb"], axis=0), rhs).astype(BF16)
            for ch in chunks:
                ys = (jnp.where(pair[0], ch["y"][:CHUNK], 0), jnp.where(pair[1], ch["y"][CHUNK:], 0))
                ch["t"] = [ch["t"][d] - _dot(ys[d], bd(ch["tb"][d])) for d in range(2)]
        for ch in chunks:
            tu = jnp.concatenate([(ch["t"][d] * ch["scale_u"][d]).astype(BF16) for d in range(2)], axis=0)
            tw = jnp.concatenate([(ch["t"][d] * ch["scale_w"][d]).astype(BF16) for d in range(2)], axis=0)
            u0 = _dot(tu, ch["bd_v"])
            w = _dot(tw, ch["bd_k"]).astype(BF16)
            for d in range(2):
                u0_s[grp, d, ch["rows"], :] = u0[d * CHUNK:(d + 1) * CHUNK]
                w_s[grp, d, ch["rows"], :] = w[d * CHUNK:(d + 1) * CHUNK]
        return carry

    lax.fori_loop(0, R // (SOLVE_CHUNKS * CHUNK), solve_body, 0)

    s_s[grp] = jnp.zeros(s_s.shape[1:], F32)
    if has_state:
        for bi in range(nb):
            for d in range(2):
                for h in range(4):
                    s_s[grp, bi, d, h * DK_A:(h + 1) * DK_A, h * DK_A:(h + 1) * DK_A] = s0_ref[bi, d, h]

    def scan_body(i, carry):
        chains = []
        for gi in range(N_GRP):
            for bi in range(nb):
                for d in range(2):
                    c = i if d == 0 else n_chunks - 1 - i
                    rows = pl.ds(pl.multiple_of(bi * L + c * CHUNK, CHUNK), CHUNK)
                    tile = pl.ds(pl.multiple_of((bi * n_chunks + c) * 8, 8), 8)
                    chains.append(dict(
                        gi=gi, bi=bi, d=d, rows=rows, s=s_s[gi, bi, d], u0=u0_s[gi, d, rows, :],
                        kg=kg_s[gi, d, rows, :], qkm=qkm_s[gi, d, rows, :], decay=dec_s[gi, d, tile, :][0:1],
                        wq=jnp.concatenate([w_s[gi, d, rows, :], qg_s[gi, d, rows, :]], axis=0)))
        for c in chains:
            c["ws"] = _dot(c["wq"], c["s"].astype(BF16))
        for c in chains:
            c["ub"] = (c["u0"] - c["ws"][:CHUNK]).astype(BF16)
        for c in chains:
            c["o"] = c["ws"][CHUNK:] + _dot(c["qkm"], bd(c["ub"]))
            upd = _dot(c["kg"].T.astype(BF16), c["ub"])
            c["s_new"] = c["s"] * c["decay"] + jnp.where(bd_mask, upd, 0.0)
        for c in chains:
            s_s[c["gi"], c["bi"], c["d"]] = c["s_new"]
            if c["d"] == 0:
                of_s[c["gi"], c["rows"], :] = c["o"]
            else:
                ob_s[c["gi"], c["rows"], :] = c["o"]
        return carry

    def epilogue(bi, carry):
        rows_b = pl.ds(pl.multiple_of(bi * L, L), L)
        for gi in range(N_GRP):
            lanes = slice(gi * W, (gi + 1) * W)
            o = of_s[gi, rows_b, :] + ob_s[gi, rows_b, :]
            ms = head_sums(o * o) * (1.0 / DK_A)
            o = o * lax.rsqrt(ms + EPS) * dn_ref[...]
            o_ref[bi, :, lanes] = o * _silu(z_ref[bi, :, lanes])
        return carry

    @pl.when(grp == N_GRP - 1)
    def _():
        lax.fori_loop(0, n_chunks, scan_body, 0)
        lax.fori_loop(0, nb, epilogue, 0)
        for gi in range(N_GRP):
            for bi in range(nb):
                for d in range(2):
                    for h in range(4):
                        st_ref[bi, d, gi * 4 + h] = s_s[gi, bi, d, h * DK_A:(h + 1) * DK_A, h * DK_A:(h + 1) * DK_A]


def _delta(qkv, z, gcs, beta, dnorm, s0):
    b, l, _ = qkv.shape
    has_state = s0 is not None
    w = GROUP_W
    n_grp = A_W // w
    nb = DELTA_ROWS // l
    r = nb * l

    def slab(k):
        return pl.BlockSpec((nb, l, w), lambda i, g: (i, 0, k * n_grp + g))

    narrow = pl.BlockSpec((nb, l, LANES), lambda i, g: (i, 0, 0))
    wide = pl.BlockSpec((nb, l, A_W), lambda i, g: (i, 0, 0))
    in_specs = [slab(0), slab(1), slab(2), wide, narrow, narrow,
                pl.BlockSpec((1, w), lambda i, g: (0, 0))]
    args = [qkv, qkv, qkv, z, gcs, beta, dnorm]
    if has_state:
        in_specs.append(pl.BlockSpec((nb, None, 2, 4, DK_A, DK_A), lambda i, g: (i, 0, 0, g, 0, 0)))
        args.append(s0)
    per_grp = lambda shape, dt: pltpu.VMEM((n_grp,) + shape, dt)
    scratch = ([pltpu.VMEM((r, w), F32)] * 3 + [pltpu.VMEM((2, r, w), F32)] * 2
               + [per_grp((2, r, w), F32)] * 2 + [per_grp((2, r // CHUNK * 8, w), F32)]
               + [per_grp((2, r, w), BF16)] * 3 + [per_grp((r, w), F32)] * 2
               + [per_grp((nb, 2, w, w), F32)])
    return pl.pallas_call(
        functools.partial(_delta_kernel, nb=nb, length=l, has_state=has_state),
        grid=(b // nb, n_grp),
        in_specs=in_specs,
        out_specs=[wide, pl.BlockSpec((nb, None, 2, H_A, DK_A, DK_A), lambda i, g: (i, 0, 0, 0, 0, 0))],
        out_shape=[jax.ShapeDtypeStruct((b, l, A_W), F32),
                   jax.ShapeDtypeStruct((b, 1, 2, H_A, DK_A, DK_A), F32)],
        scratch_shapes=scratch,
        compiler_params=_cparams(("arbitrary", "arbitrary")),
        name="delta_state" if has_state else "delta",
    )(*args)


def _pad_lanes(x, width):
    return jnp.pad(x, [(0, 0)] * (x.ndim - 1) + [(0, width - x.shape[-1])])


def kernel(x_prompt, x_sample, cache_diff_k, cache_diff_v, state_delta, c, c_ctx, w_ada, b_ada, norm_ffn1,
           w_ffn1_in, w_ffn1_out, norm_mix, w_in, conv_w, a_log, dt_bias, delta_norm, lambda_q1, lambda_k1,
           lambda_q2, lambda_k2, diff_norm, w_out, norm_ffn2, w_ffn2_in, w_ffn2_out, norm_final):
    bp, lp, _ = x_prompt.shape
    bs, ls, _ = x_sample.shape
    past = cache_diff_k.shape[2]

    w_mix_out = w_out[0].astype(BF16)
    wi = w_in[0]
    a_end = 4 * A_W
    dec_w = wi[:, a_end:a_end + 2 * H_A]
    bet_w = wi[:, a_end + 2 * H_A:a_end + 4 * H_A]
    rest = wi[:, a_end + 4 * H_A:]
    w_pack = jnp.concatenate([wi[:, :a_end], rest, _pad_lanes(dec_w, LANES), _pad_lanes(bet_w, LANES)],
                             axis=1).astype(BF16)
    g1 = norm_ffn1[0].reshape(1, D_MODEL)
    gm = norm_mix[0].reshape(1, D_MODEL)
    g2 = norm_ffn2[0].reshape(1, D_MODEL)
    gf = norm_final.reshape(1, D_MODEL)
    gpar = jnp.pad(jnp.stack([a_log[0].reshape(-1), dt_bias[0].reshape(-1)]), ((0, 6), (0, LANES - 2 * H_A)))
    dnorm = jnp.tile(delta_norm[0], 4).reshape(1, GROUP_W)
    lam_pack = jnp.pad(jnp.concatenate([lambda_q1, lambda_k1, lambda_q2, lambda_k2], axis=0),
                       ((0, 4), (0, LANES - DQK_B)))
    dgain = diff_norm[0].reshape(1, DV_B)
    cos, sin = _rope_tables(ls)

    cond = jnp.pad(jnp.concatenate([c_ctx[None, :], c], axis=0), ((0, 8 - 1 - bs), (0, 0)))
    mod = _adaln(cond, w_ada[0], b_ada[0]).reshape(8, N_MOD, D_MODEL)

    def trunk(x3, rows_per_cond, ctx, w1, w2):
        b, l, _ = x3.shape
        x = x3.reshape(b * l, D_MODEL)
        x1, *w1 = _ffn1(x, mod, g1, *w1, rows_per_cond)
        outs = _inproj(x1, mod, gm, w_pack, conv_w[0], gpar, rows_per_cond, l, ctx is None)
        qkv, z, bq, bk, bv, gcs, beta = outs[:7]
        r3 = lambda a: a.reshape(b, l, a.shape[-1])
        s0 = None if ctx is None else ctx[2]
        oa, st = _delta(r3(qkv), r3(z), r3(gcs), r3(beta), dnorm, s0)
        if ctx is None:
            ob = _attn_ctx(r3(bq), r3(bk), r3(bv), lam_pack, dgain)
        else:
            ob = _attn_lat(r3(bq), r3(bk), r3(bv), ctx[0], ctx[1], cos, sin, lam_pack, dgain)
        y, *w2 = _post(x1, oa.reshape(b * l, A_W), ob.reshape(b * l, B_W), mod, g2, gf, w_mix_out, *w2,
                       rows_per_cond)
        return y.reshape(b, l, D_MODEL), outs[7:], st, w1, w2

    y_prompt, (k_maps, v_heads), s_ctx, w1, w2 = trunk(x_prompt, None, None, (w_ffn1_in[0], w_ffn1_out[0]),
                                                       (w_ffn2_in[0], w_ffn2_out[0]))
    ctx = (cache_diff_k[:, 0].reshape(bs, past, B_W), cache_diff_v[:, 0].reshape(bs, past, B_W), state_delta)
    y_sample = trunk(x_sample, ls, ctx, w1, w2)[0]
    new_k = k_maps.reshape(bp, 1, lp, H_B, 2, DQK_B)
    new_v = v_heads.reshape(bp, 1, lp, H_B, DV_B)
    return (y_prompt, y_sample, new_k, new_v, s_ctx)
```

```python
import functools
import math

import jax
import jax.numpy as jnp
import numpy as np
from jax import lax
from jax.experimental import pallas as pl
from jax.experimental.pallas import tpu as pltpu

F32 = jnp.float32
BF16 = jnp.bfloat16

D_MODEL = 1024
D_FF = 2816
N_MOD = 9
H_A = 8
DK_A = 64
A_W = H_A * DK_A
H_B = 4
DQK_B = 64
DV_B = 128
B_W = H_B * DV_B
CONV_K = 5
CHUNK = 64
GRID_W = 64
ROPE_THETA = 10000.0
EPS = 1e-6
LAM_INIT = 0.8 - 0.6 * math.exp(-0.3 * 0)

LANES = 128
GROUP_W = 4 * DK_A
IN_PACK_COLS = 3 * A_W + A_W + 3 * B_W + 2 * LANES
VMEM_LIMIT = 56 * 1024 * 1024

MXU_W = 256
TOKEN_TILE = 512
FF_CHUNKS = ((0, 6 * MXU_W), (6 * MXU_W, D_FF))
Q_TILE = 256


def _cparams(sem):
    return pltpu.CompilerParams(dimension_semantics=sem, vmem_limit_bytes=VMEM_LIMIT)


def _resident(shape):
    nd = len(shape)
    return pl.BlockSpec(shape, lambda *_: (0,) * nd, pipeline_mode=pl.Buffered(1))


def _silu(x):
    half = 0.5 * x
    return half * jnp.tanh(half) + half


def _mod_norm(x, gain, shift, scale):
    y = x * lax.rsqrt(jnp.mean(x * x, axis=-1, keepdims=True) + EPS)
    return (y * gain) * (1.0 + scale) + shift


def _dot(a, b):
    return jnp.dot(a, b, preferred_element_type=F32)


def _dot_split(a, b_bf16):
    hi = a.astype(BF16)
    lo = (a - hi.astype(F32)).astype(BF16)
    return _dot(hi, b_bf16) + _dot(lo, b_bf16)


def _adaln_kernel(c_ref, w_ref, b_ref, o_ref):
    s = _silu(c_ref[...])
    o_ref[...] = _dot(s.astype(BF16), w_ref[...].astype(BF16)) + b_ref[...]


def _adaln(cond, w_ada, b_ada):
    n = N_MOD * D_MODEL
    tn = n // 4
    return pl.pallas_call(
        _adaln_kernel,
        grid=(n // tn,),
        in_specs=[pl.BlockSpec((8, D_MODEL), lambda j: (0, 0)),
                  pl.BlockSpec((D_MODEL, tn), lambda j: (0, j)),
                  pl.BlockSpec((1, tn), lambda j: (0, j))],
        out_specs=pl.BlockSpec((8, tn), lambda j: (0, j)),
        out_shape=jax.ShapeDtypeStruct((8, n), F32),
        compiler_params=_cparams(("arbitrary",)),
        name="adaln",
    )(cond, w_ada, b_ada.reshape(1, n))


def _swiglu_update(x, mod_ref, k0, gain, w_in_ref, w_out_ref):
    h = _mod_norm(x, gain, mod_ref[k0:k0 + 1, :], mod_ref[k0 + 1:k0 + 2, :]).astype(BF16)
    acc = None
    for lo, hi in FF_CHUNKS:
        g = _dot(h, w_in_ref[:, lo:hi])
        u = _dot(h, w_in_ref[:, D_FF + lo:D_FF + hi])
        part = _dot((_silu(g) * u).astype(BF16), w_out_ref[lo:hi, :])
        acc = part if acc is None else acc + part
    return x + (0.5 * mod_ref[k0 + 2:k0 + 3, :]) * acc


W_CHUNKS = 11
W_IN_CHUNK = 2 * D_FF // W_CHUNKS
W_OUT_CHUNK = D_FF // W_CHUNKS


def _stage_ffn_weights(step, w_in_ref, w_out_ref, w_in_bf_ref, w_out_bf_ref, w_in_s, w_out_s):
    for c in range(W_CHUNKS):
        @pl.when(step == c)
        def _():
            a = w_in_ref[...].astype(BF16)
            b = w_out_ref[...].astype(BF16)
            w_in_bf_ref[...] = a
            w_out_bf_ref[...] = b
            w_in_s[:, c * W_IN_CHUNK:(c + 1) * W_IN_CHUNK] = a
            w_out_s[c * W_OUT_CHUNK:(c + 1) * W_OUT_CHUNK, :] = b


def _ffn_weight_plumbing(cast):
    if not cast:
        return [_resident((D_MODEL, 2 * D_FF)), _resident((D_FF, D_MODEL))], [], [], []
    last = W_CHUNKS - 1
    w_in_chunk = pl.BlockSpec((D_MODEL, W_IN_CHUNK), lambda s: (0, jnp.minimum(s, last)))
    w_out_chunk = pl.BlockSpec((W_OUT_CHUNK, D_MODEL), lambda s: (jnp.minimum(s, last), 0))
    out_shape = [jax.ShapeDtypeStruct((D_MODEL, 2 * D_FF), BF16), jax.ShapeDtypeStruct((D_FF, D_MODEL), BF16)]
    scratch = [pltpu.VMEM((D_MODEL, 2 * D_FF), BF16), pltpu.VMEM((D_FF, D_MODEL), BF16)]
    return [w_in_chunk, w_out_chunk], [w_in_chunk, w_out_chunk], out_shape, scratch


def _tile_of(cast):
    off = W_CHUNKS if cast else 0
    return lambda s: jnp.maximum(s - off, 0)


def _mod_spec(rows_per_cond, tm, tile):
    if rows_per_cond is None:
        return pl.BlockSpec((None, N_MOD, D_MODEL), lambda s: (0, 0, 0))
    return pl.BlockSpec((None, N_MOD, D_MODEL), lambda s: (1 + (tile(s) * tm) // rows_per_cond, 0, 0))


def _ffn1_kernel(*refs, cast):
    x_ref, mod_ref, gain_ref, w_in_ref, w_out_ref, o_ref = refs[:6]
    if cast:
        step = pl.program_id(0)
        w_in_s, w_out_s = refs[8:10]
        _stage_ffn_weights(step, w_in_ref, w_out_ref, refs[6], refs[7], w_in_s, w_out_s)

        @pl.when(step >= W_CHUNKS)
        def _():
            o_ref[...] = _swiglu_update(x_ref[...], mod_ref, 0, gain_ref[...], w_in_s, w_out_s)
    else:
        o_ref[...] = _swiglu_update(x_ref[...], mod_ref, 0, gain_ref[...], w_in_ref, w_out_ref)


def _ffn1(x, mod, gain, w_in, w_out, rows_per_cond):
    t = x.shape[0]
    tm = TOKEN_TILE
    cast = w_in.dtype == F32
    tile = _tile_of(cast)
    row = pl.BlockSpec((tm, D_MODEL), lambda s: (tile(s), 0))
    w_in_specs, w_out_specs, w_shapes, w_scratch = _ffn_weight_plumbing(cast)
    outs = pl.pallas_call(
        functools.partial(_ffn1_kernel, cast=cast),
        grid=(t // tm + (W_CHUNKS if cast else 0),),
        in_specs=[row, _mod_spec(rows_per_cond, tm, tile), _resident((1, D_MODEL))] + w_in_specs,
        out_specs=[row] + w_out_specs,
        out_shape=[jax.ShapeDtypeStruct((t, D_MODEL), F32)] + w_shapes,
        scratch_shapes=w_scratch,
        compiler_params=_cparams(("arbitrary",)),
        name="ffn1",
    )(x, mod, gain, w_in, w_out)
    return (outs[0], outs[1], outs[2]) if cast else (outs[0], w_in, w_out)


_IN_SEGS = (("qkv", 0, 3 * A_W), ("z", 3 * A_W, A_W), ("bq", 4 * A_W, B_W), ("bk", 4 * A_W + B_W, B_W),
            ("bv", 4 * A_W + 2 * B_W, B_W), ("dec", 4 * A_W + 3 * B_W, LANES),
            ("bet", 4 * A_W + 3 * B_W + LANES, LANES))


HALO = 8
CONV_SLAB = 256


def _inproj_kernel(*refs, tm, seq_len, emit_maps):
    x_ref, xp_ref, xn_ref, mod_ref, gain_ref, w_ref, cw_ref, gpar_ref = refs[:8]
    qkv_ref, z_ref, bq_ref, bk_ref, bv_ref, gcs_ref, beta_ref = refs[8:15]
    hs, xpad, ybuf = refs[-3:]
    shift, scale = mod_ref[3:4, :], mod_ref[4:5, :]
    hf = _mod_norm(x_ref[...], gain_ref[...], shift, scale)
    h = hf.astype(BF16)
    segs = dict((name, (start, width)) for name, start, width in _IN_SEGS)

    def proj(name):
        start, width = segs[name]
        return _dot(h, w_ref[:, start:start + width])

    seg = min(tm, seq_len)
    n_seg = tm // seg
    ext = seg + 2 * HALO
    m = ext // 8
    pad = (CONV_K // 2) * 8
    qkv_w = 3 * A_W
    units = [(s, c) for s in range(n_seg) for c in range(qkv_w // CONV_SLAB)]
    def put_rows(lo, hi, val):
        for ct in range(D_MODEL // LANES):
            hs[ct, lo:hi, :] = val[:, ct * LANES:(ct + 1) * LANES]

    if seq_len > tm:
        tiles = seq_len // tm
        pos = pl.program_id(0) % tiles
        put_rows(0, HALO, _mod_norm(xp_ref[...], gain_ref[...], shift, scale) * jnp.where(pos > 0, 1.0, 0.0))
        put_rows(HALO + seg, ext,
                 _mod_norm(xn_ref[...], gain_ref[...], shift, scale) * jnp.where(pos < tiles - 1, 1.0, 0.0))
    else:
        for s in range(n_seg):
            put_rows(s * ext, s * ext + HALO, jnp.zeros((HALO, D_MODEL), F32))
            put_rows(s * ext + HALO + seg, (s + 1) * ext, jnp.zeros((HALO, D_MODEL), F32))
    for s in range(n_seg):
        put_rows(s * ext + HALO, s * ext + HALO + seg, hf[s * seg:(s + 1) * seg])
    h_il = [jnp.concatenate(
        [jnp.concatenate([hs[ct, pl.ds(s * ext + v, 8, stride=m), :] for v in range(m)], axis=0)
         for ct in range(D_MODEL // LANES)], axis=1).astype(BF16) for s in range(n_seg)]

    def project(s, c):
        cs = slice(c * CONV_SLAB, (c + 1) * CONV_SLAB)
        base = s * (ext + 2 * pad)
        res = _dot(h_il[s], w_ref[:, cs])
        xpad[base + pad:base + pad + ext, cs] = res
        for j in range(CONV_K // 2):
            xpad[base + 8 * j:base + 8 * (j + 1), cs] = pltpu.roll(res[ext - pad + 8 * j:ext - pad + 8 * (j + 1)], 1, 0)
            xpad[base + pad + ext + 8 * j:base + pad + ext + 8 * (j + 1), cs] = pltpu.roll(res[8 * j:8 * (j + 1)], 7, 0)

    def convolve(s, c):
        cs = slice(c * CONV_SLAB, (c + 1) * CONV_SLAB)
        base = s * (ext + 2 * pad)
        acc = None
        for tap in range(CONV_K):
            term = xpad[base + 8 * tap:base + 8 * tap + ext, cs] * cw_ref[tap:tap + 1, cs]
            acc = term if acc is None else acc + term
        y = _silu(acc)
        for j in range(CONV_SLAB // LANES):
            ct = c * (CONV_SLAB // LANES) + j
            for v in range(m):
                ybuf[ct, pl.ds(v, 8, stride=m), :] = y[8 * v:8 * (v + 1), j * LANES:(j + 1) * LANES]
            qkv_ref[s * seg:(s + 1) * seg, ct * LANES:(ct + 1) * LANES] = ybuf[ct, HALO:HALO + seg, :]

    z_ref[...] = proj("z")
    bq_ref[...] = proj("bq")
    zk = proj("bk")
    zv = proj("bv")
    bk_ref[...] = zk
    bv_ref[...] = zv
    if emit_maps:
        bk8_ref, bv4_ref = refs[15:17]
        maps_s = refs[-4]
        pitch = tm + 8
        for c in range(H_B):
            pair = zk[:, c * LANES:(c + 1) * LANES]
            maps_s[2 * c * pitch:2 * c * pitch + tm, :] = pair
            maps_s[(2 * c + 1) * pitch:(2 * c + 1) * pitch + tm, :] = pltpu.roll(pair, DQK_B, axis=1)
            maps_s[(2 * H_B + c) * pitch:(2 * H_B + c) * pitch + tm, :] = zv[:, c * DV_B:(c + 1) * DV_B]

        for t in range(tm):
            bk8_ref[t] = maps_s[pl.ds(t, 2 * H_B, stride=pitch), :][:, 0:DQK_B]
            bv4_ref[t] = maps_s[pl.ds(2 * H_B * pitch + t, H_B, stride=pitch), :]
    for unit in units:
        project(*unit)
        convolve(*unit)

    gpar = gpar_ref[...]
    dec_start = segs["dec"][0]
    gate_pre = _dot(h, w_ref[:, dec_start:dec_start + 2 * LANES])
    xg = gate_pre[:, :LANES] + gpar[1:2, :]
    softplus = jnp.maximum(xg, 0.0) + jnp.log1p(jnp.exp(-jnp.abs(xg)))
    lane = lax.broadcasted_iota(jnp.int32, (tm, LANES), 1)
    g = jnp.where(lane < 2 * H_A, -jnp.exp(gpar[0:1, :]) * softplus, 0.0)
    beta_ref[...] = jax.nn.sigmoid(gate_pre[:, LANES:])
    pos_c = lax.broadcasted_iota(jnp.int32, (tm, 1), 0) & (CHUNK - 1)
    gcf = g
    gcb = g
    for sft in (1, 2, 4, 8, 16, 32):
        gcf = gcf + jnp.where(pos_c >= sft, pltpu.roll(gcf, sft, axis=0), 0.0)
        gcb = gcb + jnp.where(pos_c < CHUNK - sft, pltpu.roll(gcb, tm - sft, axis=0), 0.0)
    gcs_ref[...] = jnp.where(lane < H_A, gcf, gcb)


def _inproj(x, mod, gain, w_pack, conv_w, gpar, rows_per_cond, seq_len, emit_maps):
    t = x.shape[0]
    tm = TOKEN_TILE
    hb = tm // HALO
    n_seg = max(1, tm // seq_len)
    ext = tm // n_seg + 2 * HALO
    row = pl.BlockSpec((tm, D_MODEL), lambda i: (i, 0))
    prev = pl.BlockSpec((HALO, D_MODEL), lambda i: (jnp.maximum(i * hb - 1, 0), 0))
    nxt = pl.BlockSpec((HALO, D_MODEL), lambda i: (jnp.minimum((i + 1) * hb, t // HALO - 1), 0))
    widths = [w for _, _, w in _IN_SEGS]
    out_specs = [pl.BlockSpec((tm, w), lambda i: (i, 0)) for w in widths]
    out_shape = [jax.ShapeDtypeStruct((t, w), F32) for w in widths]
    if emit_maps:
        out_specs += [pl.BlockSpec((tm, 2 * H_B, DQK_B), lambda i: (i, 0, 0)),
                      pl.BlockSpec((tm, H_B, DV_B), lambda i: (i, 0, 0))]
        out_shape += [jax.ShapeDtypeStruct((t, 2 * H_B, DQK_B), F32), jax.ShapeDtypeStruct((t, H_B, DV_B), F32)]
    return pl.pallas_call(
        functools.partial(_inproj_kernel, tm=tm, seq_len=seq_len, emit_maps=emit_maps),
        grid=(t // tm,),
        in_specs=[row, prev, nxt, _mod_spec(rows_per_cond, tm, lambda s: s), _resident((1, D_MODEL)),
                  _resident((D_MODEL, IN_PACK_COLS)), _resident((CONV_K, 3 * A_W)), _resident((8, LANES))],
        out_specs=out_specs,
        out_shape=out_shape,
        scratch_shapes=([pltpu.VMEM((3 * H_B * (tm + 8), LANES), F32)] if emit_maps else [])
        + [pltpu.VMEM((D_MODEL // LANES, n_seg * ext, LANES), F32),
                        pltpu.VMEM((n_seg * (ext + 2 * (CONV_K // 2) * 8), 3 * A_W), F32),
                        pltpu.VMEM((3 * A_W // LANES, ext, LANES), F32)],
        compiler_params=_cparams(("arbitrary",)),
        name="inproj",
    )(x, x, x, mod, gain, w_pack, conv_w, gpar)


def _post_kernel(*refs, cast):
    x_ref, oa_ref, ob_ref, mod_ref, gain_ref, gfin_ref, wmix_ref, w_in_ref, w_out_ref, o_ref = refs[:10]

    def tokens(w_in, w_out):
        y = _dot(oa_ref[...].astype(BF16), wmix_ref[0:A_W, :]) + _dot(ob_ref[...].astype(BF16), wmix_ref[A_W:, :])
        x = x_ref[...] + mod_ref[5:6, :] * y
        x = _swiglu_update(x, mod_ref, 6, gain_ref[...], w_in, w_out)
        o_ref[...] = (x * lax.rsqrt(jnp.mean(x * x, axis=-1, keepdims=True) + EPS)) * gfin_ref[...]

    if cast:
        step = pl.program_id(0)
        w_in_s, w_out_s = refs[12:14]
        _stage_ffn_weights(step, w_in_ref, w_out_ref, refs[10], refs[11], w_in_s, w_out_s)
        pl.when(step >= W_CHUNKS)(lambda: tokens(w_in_s, w_out_s))
    else:
        tokens(w_in_ref, w_out_ref)


def _post(x, oa, ob, mod, gain, gfin, w_mix, w_in, w_out, rows_per_cond):
    t = x.shape[0]
    tm = TOKEN_TILE
    cast = w_in.dtype == F32
    tile = _tile_of(cast)
    row = pl.BlockSpec((tm, D_MODEL), lambda s: (tile(s), 0))
    half = pl.BlockSpec((tm, A_W), lambda s: (tile(s), 0))
    w_in_specs, w_out_specs, w_shapes, w_scratch = _ffn_weight_plumbing(cast)
    outs = pl.pallas_call(
        functools.partial(_post_kernel, cast=cast),
        grid=(t // tm + (W_CHUNKS if cast else 0),),
        in_specs=[row, half, half, _mod_spec(rows_per_cond, tm, tile), _resident((1, D_MODEL)),
                  _resident((1, D_MODEL)), _resident((A_W + B_W, D_MODEL))] + w_in_specs,
        out_specs=[row] + w_out_specs,
        out_shape=[jax.ShapeDtypeStruct((t, D_MODEL), F32)] + w_shapes,
        scratch_shapes=w_scratch,
        compiler_params=_cparams(("arbitrary",)),
        name="post",
    )(x, oa, ob, mod, gain, gfin, w_mix, w_in, w_out)
    return (outs[0], outs[1], outs[2]) if cast else (outs[0], w_in, w_out)


def _rope(x, cos, sin_signed):
    w = x.shape[-1]
    lane = lax.broadcasted_iota(jnp.int32, x.shape, 1)
    partner = jnp.where((lane & 31) < 16, pltpu.roll(x, w - 16, axis=1), pltpu.roll(x, 16, axis=1))
    return x * cos + partner * sin_signed


def _store_values(v_s, rows, v):
    for h in range(H_B):
        vh = v[:, h * DV_B:(h + 1) * DV_B]
        v_s[rows, 2 * h * DV_B:(2 * h + 1) * DV_B] = vh.astype(BF16)
        v_s[rows, (2 * h + 1) * DV_B:(2 * h + 2) * DV_B] = jnp.ones(vh.shape, BF16)


def _attn_core(q, k_ref, v_ref, lam, gain_ref, o_ref):
    tq = q.shape[0]
    lane = lax.broadcasted_iota(jnp.int32, (tq, DV_B), 1)
    qs = q * (DQK_B ** -0.5)
    heads = [slice(h * DV_B, (h + 1) * DV_B) for h in range(H_B)]
    scores = []
    for hs in heads:
        qh = qs[:, hs]
        q2 = jnp.concatenate([jnp.where(lane < DQK_B, qh, 0.0), jnp.where(lane >= DQK_B, qh, 0.0)], axis=0)
        scores.append(lax.dot_general(q2.astype(BF16), k_ref[:, hs], (((1,), (1,)), ((), ())),
                                      preferred_element_type=F32))
    for h, (hs, s) in enumerate(zip(heads, scores)):
        e = jnp.exp(s - jnp.max(s, axis=-1, keepdims=True))
        rs = _dot(e.astype(BF16), v_ref[:, 2 * h * DV_B:(2 * h + 2) * DV_B])
        r = rs[:, :DV_B] / rs[:, DV_B:]
        o = r[:tq] - lam * r[tq:]
        o = o * lax.rsqrt(jnp.mean(o * o, axis=-1, keepdims=True) + EPS)
        o_ref[:, hs] = (o * gain_ref[...]) * (1.0 - LAM_INIT)


def _lambda(lam_ref):
    l = lam_ref[...]
    s1 = jnp.sum(l[0:1] * l[1:2], axis=-1, keepdims=True)
    s2 = jnp.sum(l[2:3] * l[3:4], axis=-1, keepdims=True)
    return jnp.exp(s1) - jnp.exp(s2) + LAM_INIT


CTX_SEQS = 4


def _attn_ctx_kernel(q_ref, k_ref, v_ref, lam_ref, gain_ref, o_ref, k_s, v_s):
    lam = _lambda(lam_ref)

    def one_sequence(bi, carry):
        k_s[...] = k_ref[bi].astype(BF16)
        _store_values(v_s, slice(None), v_ref[bi])
        _attn_core(q_ref[bi], k_s, v_s, lam, gain_ref, o_ref.at[bi])
        return carry

    lax.fori_loop(0, CTX_SEQS, one_sequence, 0)


def _attn_ctx(q, k, v, lam_pack, gain):
    b, l, _ = q.shape
    blk = pl.BlockSpec((CTX_SEQS, l, B_W), lambda i: (i, 0, 0))
    return pl.pallas_call(
        _attn_ctx_kernel,
        grid=(b // CTX_SEQS,),
        in_specs=[blk, blk, blk, _resident((8, LANES)), _resident((1, DV_B))],
        out_specs=blk,
        out_shape=jax.ShapeDtypeStruct((b, l, B_W), F32),
        scratch_shapes=[pltpu.VMEM((l, B_W), BF16), pltpu.VMEM((l, 2 * B_W), BF16)],
        compiler_params=_cparams(("arbitrary",)),
        name="attn_ctx",
    )(q, k, v, lam_pack, gain)


def _attn_lat_kernel(q_ref, k_ref, v_ref, ck_ref, cv_ref, cosq_ref, sinq_ref, cos_ref, sin_ref,
                     lam_ref, gain_ref, o_ref, k_s, v_s, *, past):
    @pl.when(pl.program_id(1) == 0)
    def _():
        k_s[0:past, :] = ck_ref[...].astype(BF16)
        _store_values(v_s, slice(0, past), cv_ref[...])
        k_s[past:, :] = _rope(k_ref[...], cos_ref[...], sin_ref[...]).astype(BF16)
        _store_values(v_s, slice(past, None), v_ref[...])

    q = _rope(q_ref[...], cosq_ref[...], sinq_ref[...])
    _attn_core(q, k_s, v_s, _lambda(lam_ref), gain_ref, o_ref)


def _attn_lat(q, k, v, ck, cv, cos, sin, lam_pack, gain):
    b, l, _ = q.shape
    past = ck.shape[1]
    tq = Q_TILE
    full = pl.BlockSpec((None, l, B_W), lambda i, j: (i, 0, 0))
    cache_k = cache_v = pl.BlockSpec((None, past, B_W), lambda i, j: (i, 0, 0))
    qblk = pl.BlockSpec((None, tq, B_W), lambda i, j: (i, j, 0))
    tab_q = pl.BlockSpec((tq, B_W), lambda i, j: (j, 0))
    tab = pl.BlockSpec((l, B_W), lambda i, j: (0, 0))
    return pl.pallas_call(
        functools.partial(_attn_lat_kernel, past=past),
        grid=(b, l // tq),
        in_specs=[qblk, full, full, cache_k, cache_v, tab_q, tab_q, tab, tab,
                  pl.BlockSpec((8, LANES), lambda i, j: (0, 0)), pl.BlockSpec((1, DV_B), lambda i, j: (0, 0))],
        out_specs=qblk,
        out_shape=jax.ShapeDtypeStruct((b, l, B_W), F32),
        scratch_shapes=[pltpu.VMEM((past + l, B_W), BF16), pltpu.VMEM((past + l, 2 * B_W), BF16)],
        compiler_params=_cparams(("arbitrary", "arbitrary")),
        name="attn_lat",
    )(q, k, v, ck, cv, cos, sin, cos, sin, lam_pack, gain)


def _rope_tables(length):
    pairs = DQK_B // 4
    n_rows = length // GRID_W
    pos_row = np.repeat(np.arange(n_rows), GRID_W).astype(np.float32)
    pos_col = np.tile(np.arange(GRID_W), n_rows).astype(np.float32)
    inv = (np.float32(ROPE_THETA) ** (-np.arange(pairs, dtype=np.float32) / np.float32(pairs))).astype(np.float32)
    ang_r = pos_row[:, None] * inv
    ang_c = pos_col[:, None] * inv
    cos = np.concatenate([np.cos(ang_r), np.cos(ang_r), np.cos(ang_c), np.cos(ang_c)], axis=-1)
    sin = np.concatenate([-np.sin(ang_r), np.sin(ang_r), -np.sin(ang_c), np.sin(ang_c)], axis=-1)
    reps = B_W // DQK_B
    return jnp.asarray(np.tile(cos, (1, reps)), F32), jnp.asarray(np.tile(sin, (1, reps)), F32)


def _block_diag(x):
    lo, hi = x[:, :LANES], x[:, LANES:]
    first = lax.broadcasted_iota(jnp.int32, lo.shape, 1) < DK_A
    top = jnp.concatenate([jnp.where(first, lo, 0), jnp.where(first, 0, lo)], axis=0)
    bot = jnp.concatenate([jnp.where(first, hi, 0), jnp.where(first, 0, hi)], axis=0)
    zero = jnp.zeros_like(top)
    return jnp.concatenate([jnp.concatenate([top, zero], axis=1), jnp.concatenate([zero, bot], axis=1)], axis=0)


SOLVE_CHUNKS = 8
DELTA_ROWS = 1024


def _delta_kernel(*refs, nb, length, has_state):
    if has_state:
        (q_ref, k_ref, v_ref, z_ref, gcs_ref, beta_ref, dn_ref, s0_ref, o_ref, st_ref, *scr) = refs
    else:
        (q_ref, k_ref, v_ref, z_ref, gcs_ref, beta_ref, dn_ref, o_ref, st_ref, *scr) = refs
        s0_ref = None
    qs, ks, vs, gexp, bexp, u0_s, kg_s, dec_s, w_s, qg_s, qkm_s, of_s, ob_s, s_s = scr
    L = length
    R = nb * L
    n_chunks = L // CHUNK
    grp = pl.program_id(1)
    W = GROUP_W
    N_GRP = A_W // W

    r256 = lax.broadcasted_iota(jnp.int32, (W, W), 0)
    c256 = lax.broadcasted_iota(jnp.int32, (W, W), 1)
    bd_mask = (r256 >> 6) == (c256 >> 6)
    ones_bd = jnp.where(bd_mask, 1.0, 0.0).astype(BF16)
    ri = lax.broadcasted_iota(jnp.int32, (CHUNK, W), 0)
    cj = lax.broadcasted_iota(jnp.int32, (CHUNK, W), 1) & (CHUNK - 1)
    eye_p = ri == cj
    NS = 2 * H_A
    er = lax.broadcasted_iota(jnp.int32, (LANES, 4 * W), 0)
    ec = lax.broadcasted_iota(jnp.int32, (LANES, 4 * W), 1)
    src = ((ec >> 8) & 1) * H_A + grp * 4 + ((ec & (W - 1)) >> 6)
    expand = jnp.where((er < 4 * NS) & ((er & (NS - 1)) == src) & (((er >> 4) & 1) == (ec >> 9)),
                       1.0, 0.0).astype(BF16)
    lane_s = lax.broadcasted_iota(jnp.int32, (L, LANES), 1)

    bd = _block_diag

    def head_sums(y):
        return _dot(y.astype(BF16), ones_bd)

    def prologue(bi, carry):
        rows_b = pl.ds(pl.multiple_of(bi * L, L), L)

        def l2n(y):
            return y * lax.rsqrt(head_sums(y * y) + EPS)

        qs[rows_b, :] = l2n(q_ref[bi]) * (DK_A ** -0.5)
        ks[rows_b, :] = l2n(k_ref[bi])
        vs[rows_b, :] = v_ref[bi]
        gc = gcs_ref[bi]
        bt = jnp.where(lane_s < NS, beta_ref[bi], 0.0)
        gc_hi = gc.astype(BF16).astype(F32)
        bt_hi = bt.astype(BF16).astype(F32)
        packed = (gc_hi + pltpu.roll(bt_hi, NS, axis=1) + pltpu.roll(gc - gc_hi, 2 * NS, axis=1)
                  + pltpu.roll(bt - bt_hi, 3 * NS, axis=1))
        both = _dot(packed.astype(BF16), expand)
        for d in range(2):
            gexp[d, rows_b, :] = both[:, d * W:(d + 1) * W]
            bexp[d, rows_b, :] = both[:, 2 * W + d * W:2 * W + (d + 1) * W]
        return carry

    lax.fori_loop(0, nb, prologue, 0)

    def solve_body(j, carry):
        chunks = []
        for k in range(SOLVE_CHUNKS):
            chunk = j * SOLVE_CHUNKS + k
            rows = pl.ds(pl.multiple_of(chunk * CHUNK, CHUNK), CHUNK)
            kc = ks[rows, :]
            qc = qs[rows, :]
            kb = kc.astype(BF16)
            res = lax.dot_general(jnp.concatenate([qc.astype(BF16), kb], axis=0), bd(kb),
                                  (((1,), (1,)), ((), ())), preferred_element_type=F32)
            qk, kk = res[:CHUNK], res[CHUNK:]
            ch = dict(rows=rows, bd_k=bd(kb), bd_v=bd(vs[rows, :].astype(BF16)), t=[], scale_u=[], scale_w=[])
            nsum = None
            for d in range(2):
                ge = gexp[d, rows, :]
                be = bexp[d, rows, :]
                if d == 0:
                    incl, strict, gl = ri >= cj, ri > cj, ge[CHUNK - 1:CHUNK, :]
                else:
                    incl, strict, gl = ri <= cj, ri < cj, ge[0:1, :]
                g_row = jnp.sum(jnp.where(eye_p, ge, 0.0), axis=0, keepdims=True)
                b_row = jnp.sum(jnp.where(eye_p, be, 0.0), axis=0, keepdims=True)
                dec_m = jnp.exp(jnp.where(incl, ge - g_row, -jnp.inf))
                nmat = jnp.where(strict, be * kk * dec_m, 0.0)
                nsum = nmat if nsum is None else nsum + nmat
                qkm_s[grp, d, rows, :] = jnp.where(incl, qk * dec_m, 0.0).astype(BF16)
                qg_s[grp, d, rows, :] = (qc * jnp.exp(ge)).astype(BF16)
                kg_s[grp, d, rows, :] = kc * jnp.exp(gl - ge)
                dec_s[grp, d, pl.ds(pl.multiple_of(chunk * 8, 8), 8), :] = jnp.broadcast_to(jnp.exp(gl), (8, W))
                ch["scale_u"].append(b_row)
                ch["scale_w"].append(b_row * jnp.exp(g_row))
                ch["t"].append(jnp.where(eye_p, 1.0, 0.0) - jnp.where((ri >> 1) == (cj >> 1), nmat, 0.0))
            ch["n_both"] = nsum.astype(BF16)
            chunks.append(ch)

        for lb in range(1, 6):
            same = (ri >> (lb + 1)) == (cj >> (lb + 1))
            pair = (same & (((ri >> lb) & 1) == 1) & (((cj >> lb) & 1) == 0),
                    same & (((cj >> lb) & 1) == 1) & (((ri >> lb) & 1) == 0))
            for ch in chunks:
                ch["tb"] = [t.astype(BF16) for t in ch["t"]]
                rhs = bd(jnp.where(pair[0] | pair[1], ch["n_both"], 0))
                ch["y"] = _dot(jnp.concatenate(ch["tb"], axis=0), rhs).astype(BF16)
            for ch in chunks:
                ys = (jnp.where(pair[0], ch["y"][:CHUNK], 0), jnp.where(pair[1], ch["y"][CHUNK:], 0))
                ch["t"] = [ch["t"][d] - _dot(ys[d], bd(ch["tb"][d])) for d in range(2)]
        for ch in chunks:
            tu = jnp.concatenate([(ch["t"][d] * ch["scale_u"][d]).astype(BF16) for d in range(2)], axis=0)
            tw = jnp.concatenate([(ch["t"][d] * ch["scale_w"][d]).astype(BF16) for d in range(2)], axis=0)
            u0 = _dot(tu, ch["bd_v"])
            w = _dot(tw, ch["bd_k"]).astype(BF16)
            for d in range(2):
                u0_s[grp, d, ch["rows"], :] = u0[d * CHUNK:(d + 1) * CHUNK]
                w_s[grp, d, ch["rows"], :] = w[d * CHUNK:(d + 1) * CHUNK]
        return carry

    lax.fori_loop(0, R // (SOLVE_CHUNKS * CHUNK), solve_body, 0)

    s_s[grp] = jnp.zeros(s_s.shape[1:], F32)
    if has_state:
        for bi in range(nb):
            for d in range(2):
                for h in range(4):
                    s_s[grp, bi, d, h * DK_A:(h + 1) * DK_A, h * DK_A:(h + 1) * DK_A] = s0_ref[bi, d, h]

    def scan_body(i, carry):
        chains = []
        for gi in range(N_GRP):
            for bi in range(nb):
                for d in range(2):
                    c = i if d == 0 else n_chunks - 1 - i
                    rows = pl.ds(pl.multiple_of(bi * L + c * CHUNK, CHUNK), CHUNK)
                    tile = pl.ds(pl.multiple_of((bi * n_chunks + c) * 8, 8), 8)
                    chains.append(dict(
                        gi=gi, bi=bi, d=d, rows=rows, s=s_s[gi, bi, d], u0=u0_s[gi, d, rows, :],
                        kg=kg_s[gi, d, rows, :], qkm=qkm_s[gi, d, rows, :], decay=dec_s[gi, d, tile, :][0:1],
                        wq=jnp.concatenate([w_s[gi, d, rows, :], qg_s[gi, d, rows, :]], axis=0)))
        for c in chains:
            c["ws"] = _dot(c["wq"], c["s"].astype(BF16))
        for c in chains:
            c["ub"] = (c["u0"] - c["ws"][:CHUNK]).astype(BF16)
        for c in chains:
            c["o"] = c["ws"][CHUNK:] + _dot(c["qkm"], bd(c["ub"]))
            upd = _dot(c["kg"].T.astype(BF16), c["ub"])
            c["s_new"] = c["s"] * c["decay"] + jnp.where(bd_mask, upd, 0.0)
        for c in chains:
            s_s[c["gi"], c["bi"], c["d"]] = c["s_new"]
            if c["d"] == 0:
                of_s[c["gi"], c["rows"], :] = c["o"]
            else:
                ob_s[c["gi"], c["rows"], :] = c["o"]
        return carry

    def epilogue(bi, carry):
        rows_b = pl.ds(pl.multiple_of(bi * L, L), L)
        for gi in range(N_GRP):
            lanes = slice(gi * W, (gi + 1) * W)
            o = of_s[gi, rows_b, :] + ob_s[gi, rows_b, :]
            ms = head_sums(o * o) * (1.0 / DK_A)
            o = o * lax.rsqrt(ms + EPS) * dn_ref[...]
            o_ref[bi, :, lanes] = o * _silu(z_ref[bi, :, lanes])
        return carry

    @pl.when(grp == N_GRP - 1)
    def _():
        lax.fori_loop(0, n_chunks, scan_body, 0)
        lax.fori_loop(0, nb, epilogue, 0)
        for gi in range(N_GRP):
            for bi in range(nb):
                for d in range(2):
                    for h in range(4):
                        st_ref[bi, d, gi * 4 + h] = s_s[gi, bi, d, h * DK_A:(h + 1) * DK_A, h * DK_A:(h + 1) * DK_A]


def _delta(qkv, z, gcs, beta, dnorm, s0):
    b, l, _ = qkv.shape
    has_state = s0 is not None
    w = GROUP_W
    n_grp = A_W // w
    nb = DELTA_ROWS // l
    r = nb * l

    def slab(k):
        return pl.BlockSpec((nb, l, w), lambda i, g: (i, 0, k * n_grp + g))

    narrow = pl.BlockSpec((nb, l, LANES), lambda i, g: (i, 0, 0))
    wide = pl.BlockSpec((nb, l, A_W), lambda i, g: (i, 0, 0))
    in_specs = [slab(0), slab(1), slab(2), wide, narrow, narrow,
                pl.BlockSpec((1, w), lambda i, g: (0, 0))]
    args = [qkv, qkv, qkv, z, gcs, beta, dnorm]
    if has_state:
        in_specs.append(pl.BlockSpec((nb, None, 2, 4, DK_A, DK_A), lambda i, g: (i, 0, 0, g, 0, 0)))
        args.append(s0)
    per_grp = lambda shape, dt: pltpu.VMEM((n_grp,) + shape, dt)
    scratch = ([pltpu.VMEM((r, w), F32)] * 3 + [pltpu.VMEM((2, r, w), F32)] * 2
               + [per_grp((2, r, w), F32)] * 2 + [per_grp((2, r // CHUNK * 8, w), F32)]
               + [per_grp((2, r, w), BF16)] * 3 + [per_grp((r, w), F32)] * 2
               + [per_grp((nb, 2, w, w), F32)])
    return pl.pallas_call(
        functools.partial(_delta_kernel, nb=nb, length=l, has_state=has_state),
        grid=(b // nb, n_grp),
        in_specs=in_specs,
        out_specs=[wide, pl.BlockSpec((nb, None, 2, H_A, DK_A, DK_A), lambda i, g: (i, 0, 0, 0, 0, 0))],
        out_shape=[jax.ShapeDtypeStruct((b, l, A_W), F32),
                   jax.ShapeDtypeStruct((b, 1, 2, H_A, DK_A, DK_A), F32)],
        scratch_shapes=scratch,
        compiler_params=_cparams(("arbitrary", "arbitrary")),
        name="delta_state" if has_state else "delta",
    )(*args)


def _pad_lanes(x, width):
    return jnp.pad(x, [(0, 0)] * (x.ndim - 1) + [(0, width - x.shape[-1])])


def kernel(x_prompt, x_sample, cache_diff_k, cache_diff_v, state_delta, c, c_ctx, w_ada, b_ada, norm_ffn1,
           w_ffn1_in, w_ffn1_out, norm_mix, w_in, conv_w, a_log, dt_bias, delta_norm, lambda_q1, lambda_k1,
           lambda_q2, lambda_k2, diff_norm, w_out, norm_ffn2, w_ffn2_in, w_ffn2_out, norm_final):
    bp, lp, _ = x_prompt.shape
    bs, ls, _ = x_sample.shape
    past = cache_diff_k.shape[2]

    w_mix_out = w_out[0].astype(BF16)
    wi = w_in[0]
    a_end = 4 * A_W
    dec_w = wi[:, a_end:a_end + 2 * H_A]
    bet_w = wi[:, a_end + 2 * H_A:a_end + 4 * H_A]
    rest = wi[:, a_end + 4 * H_A:]
    w_pack = jnp.concatenate([wi[:, :a_end], rest, _pad_lanes(dec_w, LANES), _pad_lanes(bet_w, LANES)],
                             axis=1).astype(BF16)
    g1 = norm_ffn1[0].reshape(1, D_MODEL)
    gm = norm_mix[0].reshape(1, D_MODEL)
    g2 = norm_ffn2[0].reshape(1, D_MODEL)
    gf = norm_final.reshape(1, D_MODEL)
    gpar = jnp.pad(jnp.stack([a_log[0].reshape(-1), dt_bias[0].reshape(-1)]), ((0, 6), (0, LANES - 2 * H_A)))
    dnorm = jnp.tile(delta_norm[0], 4).reshape(1, GROUP_W)
    lam_pack = jnp.pad(jnp.concatenate([lambda_q1, lambda_k1, lambda_q2, lambda_k2], axis=0),
                       ((0, 4), (0, LANES - DQK_B)))
    dgain = diff_norm[0].reshape(1, DV_B)
    cos, sin = _rope_tables(ls)

    cond = jnp.pad(jnp.concatenate([c_ctx[None, :], c], axis=0), ((0, 8 - 1 - bs), (0, 0)))
    mod = _adaln(cond, w_ada[0], b_ada[0]).reshape(8, N_MOD, D_MODEL)

    def trunk(x3, rows_per_cond, ctx, w1, w2):
        b, l, _ = x3.shape
        x = x3.reshape(b * l, D_MODEL)
        x1, *w1 = _ffn1(x, mod, g1, *w1, rows_per_cond)
        outs = _inproj(x1, mod, gm, w_pack, conv_w[0], gpar, rows_per_cond, l, ctx is None)
        qkv, z, bq, bk, bv, gcs, beta = outs[:7]
        r3 = lambda a: a.reshape(b, l, a.shape[-1])
        s0 = None if ctx is None else ctx[2]
        oa, st = _delta(r3(qkv), r3(z), r3(gcs), r3(beta), dnorm, s0)
        if ctx is None:
            ob = _attn_ctx(r3(bq), r3(bk), r3(bv), lam_pack, dgain)
        else:
            ob = _attn_lat(r3(bq), r3(bk), r3(bv), ctx[0], ctx[1], cos, sin, lam_pack, dgain)
        y, *w2 = _post(x1, oa.reshape(b * l, A_W), ob.reshape(b * l, B_W), mod, g2, gf, w_mix_out, *w2,
                       rows_per_cond)
        return y.reshape(b, l, D_MODEL), outs[7:], st, w1, w2

    y_prompt, (k_maps, v_heads), s_ctx, w1, w2 = trunk(x_prompt, None, None, (w_ffn1_in[0], w_ffn1_out[0]),
                                                       (w_ffn2_in[0], w_ffn2_out[0]))
    ctx = (cache_diff_k[:, 0].reshape(bs, past, B_W), cache_diff_v[:, 0].reshape(bs, past, B_W), state_delta)
    y_sample = trunk(x_sample, ls, ctx, w1, w2)[0]
    new_k = k_maps.reshape(bp, 1, lp, H_B, 2, DQK_B)
    new_v = v_heads.reshape(bp, 1, lp, H_B, DV_B)
    return (y_prompt, y_sample, new_k, new_v, s_ctx)
```

```python
import functools
import math

import jax
import jax.numpy as jnp
import numpy as np
from jax import lax
from jax.experimental import pallas as pl
from jax.experimental.pallas import tpu as pltpu

F32 = jnp.float32
BF16 = jnp.bfloat16

D_MODEL = 1024
D_FF = 2816
N_MOD = 9
H_A = 8
DK_A = 64
A_W = H_A * DK_A
H_B = 4
DQK_B = 64
DV_B = 128
B_W = H_B * DV_B
CONV_K = 5
CHUNK = 64
GRID_W = 64
ROPE_THETA = 10000.0
ROPE_PAIRS = DQK_B // 4
EPS = 1e-6
LAM_INIT = 0.8 - 0.6 * math.exp(-0.3 * 0)

LANES = 128
GROUP_W = 4 * DK_A
IN_PACK_COLS = 3 * A_W + A_W + 3 * B_W + 2 * LANES
VMEM_LIMIT = 56 * 1024 * 1024

MXU_W = 256
TOKEN_TILE = 512
FF_CHUNKS = ((0, 6 * MXU_W), (6 * MXU_W, D_FF))
Q_TILE = 256


def _cparams(sem):
    return pltpu.CompilerParams(dimension_semantics=sem, vmem_limit_bytes=VMEM_LIMIT)


def _resident(shape):
    nd = len(shape)
    return pl.BlockSpec(shape, lambda *_: (0,) * nd, pipeline_mode=pl.Buffered(1))


def _silu(x):
    half = 0.5 * x
    return half * jnp.tanh(half) + half


def _mod_norm(x, gain, shift, scale):
    y = x * lax.rsqrt(jnp.mean(x * x, axis=-1, keepdims=True) + EPS)
    return (y * gain) * (1.0 + scale) + shift


def _dot(a, b):
    return jnp.dot(a, b, preferred_element_type=F32)


def _adaln_kernel(c_ref, w_ref, b_ref, o_ref):
    s = _silu(c_ref[...])
    o_ref[...] = _dot(s.astype(BF16), w_ref[...].astype(BF16)) + b_ref[...]


def _adaln(cond, w_ada, b_ada):
    n = N_MOD * D_MODEL
    tn = n // 4
    return pl.pallas_call(
        _adaln_kernel,
        grid=(n // tn,),
        in_specs=[pl.BlockSpec((8, D_MODEL), lambda j: (0, 0)),
                  pl.BlockSpec((D_MODEL, tn), lambda j: (0, j)),
                  pl.BlockSpec((1, tn), lambda j: (0, j))],
        out_specs=pl.BlockSpec((8, tn), lambda j: (0, j)),
        out_shape=jax.ShapeDtypeStruct((8, n), F32),
        compiler_params=_cparams(("arbitrary",)),
        name="adaln",
    )(cond, w_ada, b_ada.reshape(1, n))


def _swiglu_update(x, mod_ref, k0, gain, w_in_ref, w_out_ref):
    h = _mod_norm(x, gain, mod_ref[k0:k0 + 1, :], mod_ref[k0 + 1:k0 + 2, :]).astype(BF16)
    acc = None
    for lo, hi in FF_CHUNKS:
        g = _dot(h, w_in_ref[:, lo:hi])
        u = _dot(h, w_in_ref[:, D_FF + lo:D_FF + hi])
        part = _dot((_silu(g) * u).astype(BF16), w_out_ref[lo:hi, :])
        acc = part if acc is None else acc + part
    return x + (0.5 * mod_ref[k0 + 2:k0 + 3, :]) * acc


W_CHUNKS = 11
W_IN_CHUNK = 2 * D_FF // W_CHUNKS
W_OUT_CHUNK = D_FF // W_CHUNKS


def _stage_ffn_weights(step, w_in_ref, w_out_ref, w_in_bf_ref, w_out_bf_ref, w_in_s, w_out_s):
    for c in range(W_CHUNKS):
        @pl.when(step == c)
        def _():
            a = w_in_ref[...].astype(BF16)
            b = w_out_ref[...].astype(BF16)
            w_in_bf_ref[...] = a
            w_out_bf_ref[...] = b
            w_in_s[:, c * W_IN_CHUNK:(c + 1) * W_IN_CHUNK] = a
            w_out_s[c * W_OUT_CHUNK:(c + 1) * W_OUT_CHUNK, :] = b


def _ffn_weight_plumbing(cast):
    if not cast:
        return [_resident((D_MODEL, 2 * D_FF)), _resident((D_FF, D_MODEL))], [], [], []
    last = W_CHUNKS - 1
    w_in_chunk = pl.BlockSpec((D_MODEL, W_IN_CHUNK), lambda s: (0, jnp.minimum(s, last)))
    w_out_chunk = pl.BlockSpec((W_OUT_CHUNK, D_MODEL), lambda s: (jnp.minimum(s, last), 0))
    out_shape = [jax.ShapeDtypeStruct((D_MODEL, 2 * D_FF), BF16), jax.ShapeDtypeStruct((D_FF, D_MODEL), BF16)]
    scratch = [pltpu.VMEM((D_MODEL, 2 * D_FF), BF16), pltpu.VMEM((D_FF, D_MODEL), BF16)]
    return [w_in_chunk, w_out_chunk], [w_in_chunk, w_out_chunk], out_shape, scratch


def _tile_of(cast):
    off = W_CHUNKS if cast else 0
    return lambda s: jnp.maximum(s - off, 0)


def _mod_spec(rows_per_cond, tm, tile):
    if rows_per_cond is None:
        return pl.BlockSpec((None, N_MOD, D_MODEL), lambda s: (0, 0, 0))
    return pl.BlockSpec((None, N_MOD, D_MODEL), lambda s: (1 + (tile(s) * tm) // rows_per_cond, 0, 0))


def _ffn1_kernel(*refs, cast):
    x_ref, mod_ref, gain_ref, w_in_ref, w_out_ref, o_ref = refs[:6]
    if cast:
        step = pl.program_id(0)
        w_in_s, w_out_s = refs[8:10]
        _stage_ffn_weights(step, w_in_ref, w_out_ref, refs[6], refs[7], w_in_s, w_out_s)

        @pl.when(step >= W_CHUNKS)
        def _():
            o_ref[...] = _swiglu_update(x_ref[...], mod_ref, 0, gain_ref[...], w_in_s, w_out_s)
    else:
        o_ref[...] = _swiglu_update(x_ref[...], mod_ref, 0, gain_ref[...], w_in_ref, w_out_ref)


def _ffn1(x, mod, gain, w_in, w_out, rows_per_cond):
    t = x.shape[0]
    tm = TOKEN_TILE
    cast = w_in.dtype == F32
    tile = _tile_of(cast)
    row = pl.BlockSpec((tm, D_MODEL), lambda s: (tile(s), 0))
    w_in_specs, w_out_specs, w_shapes, w_scratch = _ffn_weight_plumbing(cast)
    outs = pl.pallas_call(
        functools.partial(_ffn1_kernel, cast=cast),
        grid=(t // tm + (W_CHUNKS if cast else 0),),
        in_specs=[row, _mod_spec(rows_per_cond, tm, tile), _resident((1, D_MODEL))] + w_in_specs,
        out_specs=[row] + w_out_specs,
        out_shape=[jax.ShapeDtypeStruct((t, D_MODEL), F32)] + w_shapes,
        scratch_shapes=w_scratch,
        compiler_params=_cparams(("arbitrary",)),
        name="ffn1",
    )(x, mod, gain, w_in, w_out)
    return (outs[0], outs[1], outs[2]) if cast else (outs[0], w_in, w_out)


_IN_SEGS = (("qkv", 0, 3 * A_W), ("z", 3 * A_W, A_W), ("bq", 4 * A_W, B_W), ("bk", 4 * A_W + B_W, B_W),
            ("bv", 4 * A_W + 2 * B_W, B_W), ("dec", 4 * A_W + 3 * B_W, LANES),
            ("bet", 4 * A_W + 3 * B_W + LANES, LANES))


HALO = 8
CONV_SLAB = 256


def _inproj_kernel(*refs, tm, seq_len, emit_maps):
    x_ref, xp_ref, xn_ref, mod_ref, gain_ref, w_ref, cw_ref, gpar_ref = refs[:8]
    qkv_ref, z_ref, bq_ref, bk_ref, bv_ref, gcs_ref, beta_ref = refs[8:15]
    hs, xpad, ybuf = refs[-3:]
    shift, scale = mod_ref[3:4, :], mod_ref[4:5, :]
    hf = _mod_norm(x_ref[...], gain_ref[...], shift, scale)
    h = hf.astype(BF16)
    segs = dict((name, (start, width)) for name, start, width in _IN_SEGS)

    def proj(name):
        start, width = segs[name]
        return _dot(h, w_ref[:, start:start + width])

    seg = min(tm, seq_len)
    n_seg = tm // seg
    ext = seg + 2 * HALO
    m = ext // 8
    pad = (CONV_K // 2) * 8
    qkv_w = 3 * A_W
    units = [(s, c) for s in range(n_seg) for c in range(qkv_w // CONV_SLAB)]
    def put_rows(lo, hi, val):
        for ct in range(D_MODEL // LANES):
            hs[ct, lo:hi, :] = val[:, ct * LANES:(ct + 1) * LANES]

    if seq_len > tm:
        tiles = seq_len // tm
        pos = pl.program_id(0) % tiles
        put_rows(0, HALO, _mod_norm(xp_ref[...], gain_ref[...], shift, scale) * jnp.where(pos > 0, 1.0, 0.0))
        put_rows(HALO + seg, ext,
                 _mod_norm(xn_ref[...], gain_ref[...], shift, scale) * jnp.where(pos < tiles - 1, 1.0, 0.0))
    else:
        for s in range(n_seg):
            put_rows(s * ext, s * ext + HALO, jnp.zeros((HALO, D_MODEL), F32))
            put_rows(s * ext + HALO + seg, (s + 1) * ext, jnp.zeros((HALO, D_MODEL), F32))
    for s in range(n_seg):
        put_rows(s * ext + HALO, s * ext + HALO + seg, hf[s * seg:(s + 1) * seg])
    h_il = [jnp.concatenate(
        [jnp.concatenate([hs[ct, pl.ds(s * ext + v, 8, stride=m), :] for v in range(m)], axis=0)
         for ct in range(D_MODEL // LANES)], axis=1).astype(BF16) for s in range(n_seg)]

    def project(s, c):
        cs = slice(c * CONV_SLAB, (c + 1) * CONV_SLAB)
        base = s * (ext + 2 * pad)
        res = _dot(h_il[s], w_ref[:, cs])
        xpad[base + pad:base + pad + ext, cs] = res
        for j in range(CONV_K // 2):
            xpad[base + 8 * j:base + 8 * (j + 1), cs] = pltpu.roll(res[ext - pad + 8 * j:ext - pad + 8 * (j + 1)], 1, 0)
            xpad[base + pad + ext + 8 * j:base + pad + ext + 8 * (j + 1), cs] = pltpu.roll(res[8 * j:8 * (j + 1)], 7, 0)

    def convolve(s, c):
        cs = slice(c * CONV_SLAB, (c + 1) * CONV_SLAB)
        base = s * (ext + 2 * pad)
        acc = None
        for tap in range(CONV_K):
            term = xpad[base + 8 * tap:base + 8 * tap + ext, cs] * cw_ref[tap:tap + 1, cs]
            acc = term if acc is None else acc + term
        y = _silu(acc)
        for j in range(CONV_SLAB // LANES):
            ct = c * (CONV_SLAB // LANES) + j
            for v in range(m):
                ybuf[ct, pl.ds(v, 8, stride=m), :] = y[8 * v:8 * (v + 1), j * LANES:(j + 1) * LANES]
            qkv_ref[s * seg:(s + 1) * seg, ct * LANES:(ct + 1) * LANES] = ybuf[ct, HALO:HALO + seg, :]

    z_ref[...] = proj("z")
    bq_ref[...] = proj("bq").astype(bq_ref.dtype)
    zk = proj("bk")
    zv = proj("bv")
    bk_ref[...] = zk.astype(bk_ref.dtype)
    bv_ref[...] = zv.astype(bv_ref.dtype)
    if emit_maps:
        bk8_ref, bv4_ref = refs[15:17]
        maps_s = refs[-4]
        pitch = tm + 8
        for c in range(H_B):
            pair = zk[:, c * LANES:(c + 1) * LANES]
            maps_s[2 * c * pitch:2 * c * pitch + tm, :] = pair
            maps_s[(2 * c + 1) * pitch:(2 * c + 1) * pitch + tm, :] = pltpu.roll(pair, DQK_B, axis=1)
            maps_s[(2 * H_B + c) * pitch:(2 * H_B + c) * pitch + tm, :] = zv[:, c * DV_B:(c + 1) * DV_B]

        for t in range(tm):
            bk8_ref[t] = maps_s[pl.ds(t, 2 * H_B, stride=pitch), :][:, 0:DQK_B]
            bv4_ref[t] = maps_s[pl.ds(2 * H_B * pitch + t, H_B, stride=pitch), :]
    for unit in units:
        project(*unit)
        convolve(*unit)

    gpar = gpar_ref[...]
    dec_start = segs["dec"][0]
    gate_pre = _dot(h, w_ref[:, dec_start:dec_start + 2 * LANES])
    xg = gate_pre[:, :LANES] + gpar[1:2, :]
    softplus = jnp.maximum(xg, 0.0) + jnp.log1p(jnp.exp(-jnp.abs(xg)))
    lane = lax.broadcasted_iota(jnp.int32, (tm, LANES), 1)
    g = jnp.where(lane < 2 * H_A, -jnp.exp(gpar[0:1, :]) * softplus, 0.0)
    beta_ref[...] = jax.nn.sigmoid(gate_pre[:, LANES:])
    pos_c = lax.broadcasted_iota(jnp.int32, (tm, 1), 0) & (CHUNK - 1)
    gcf = g
    gcb = g
    for sft in (1, 2, 4, 8, 16, 32):
        gcf = gcf + jnp.where(pos_c >= sft, pltpu.roll(gcf, sft, axis=0), 0.0)
        gcb = gcb + jnp.where(pos_c < CHUNK - sft, pltpu.roll(gcb, tm - sft, axis=0), 0.0)
    gcs_ref[...] = jnp.where(lane < H_A, gcf, gcb)


def _inproj(x, mod, gain, w_pack, conv_w, gpar, rows_per_cond, seq_len, emit_maps):
    t = x.shape[0]
    tm = TOKEN_TILE
    hb = tm // HALO
    n_seg = max(1, tm // seq_len)
    ext = tm // n_seg + 2 * HALO
    row = pl.BlockSpec((tm, D_MODEL), lambda i: (i, 0))
    prev = pl.BlockSpec((HALO, D_MODEL), lambda i: (jnp.maximum(i * hb - 1, 0), 0))
    nxt = pl.BlockSpec((HALO, D_MODEL), lambda i: (jnp.minimum((i + 1) * hb, t // HALO - 1), 0))
    widths = [w for _, _, w in _IN_SEGS]
    out_specs = [pl.BlockSpec((tm, w), lambda i: (i, 0)) for w in widths]
    attn_dtype = BF16 if emit_maps else F32
    out_shape = [jax.ShapeDtypeStruct((t, w), attn_dtype if name in ("bq", "bk", "bv") else F32)
                 for name, _, w in _IN_SEGS]
    if emit_maps:
        out_specs += [pl.BlockSpec((tm, 2 * H_B, DQK_B), lambda i: (i, 0, 0)),
                      pl.BlockSpec((tm, H_B, DV_B), lambda i: (i, 0, 0))]
        out_shape += [jax.ShapeDtypeStruct((t, 2 * H_B, DQK_B), F32), jax.ShapeDtypeStruct((t, H_B, DV_B), F32)]
    return pl.pallas_call(
        functools.partial(_inproj_kernel, tm=tm, seq_len=seq_len, emit_maps=emit_maps),
        grid=(t // tm,),
        in_specs=[row, prev, nxt, _mod_spec(rows_per_cond, tm, lambda s: s), _resident((1, D_MODEL)),
                  _resident((D_MODEL, IN_PACK_COLS)), _resident((CONV_K, 3 * A_W)), _resident((8, LANES))],
        out_specs=out_specs,
        out_shape=out_shape,
        scratch_shapes=([pltpu.VMEM((3 * H_B * (tm + 8), LANES), F32)] if emit_maps else [])
        + [pltpu.VMEM((D_MODEL // LANES, n_seg * ext, LANES), F32),
                        pltpu.VMEM((n_seg * (ext + 2 * (CONV_K // 2) * 8), 3 * A_W), F32),
                        pltpu.VMEM((3 * A_W // LANES, ext, LANES), F32)],
        compiler_params=_cparams(("arbitrary",)),
        name="inproj",
    )(x, x, x, mod, gain, w_pack, conv_w, gpar)


def _post_kernel(*refs, cast):
    x_ref, oa_ref, ob_ref, mod_ref, gain_ref, gfin_ref, wmix_ref, w_in_ref, w_out_ref, o_ref = refs[:10]

    def tokens(w_in, w_out):
        y = _dot(oa_ref[...].astype(BF16), wmix_ref[0:A_W, :]) + _dot(ob_ref[...].astype(BF16), wmix_ref[A_W:, :])
        x = x_ref[...] + mod_ref[5:6, :] * y
        x = _swiglu_update(x, mod_ref, 6, gain_ref[...], w_in, w_out)
        o_ref[...] = (x * lax.rsqrt(jnp.mean(x * x, axis=-1, keepdims=True) + EPS)) * gfin_ref[...]

    if cast:
        step = pl.program_id(0)
        w_in_s, w_out_s = refs[12:14]
        _stage_ffn_weights(step, w_in_ref, w_out_ref, refs[10], refs[11], w_in_s, w_out_s)
        pl.when(step >= W_CHUNKS)(lambda: tokens(w_in_s, w_out_s))
    else:
        tokens(w_in_ref, w_out_ref)


def _post(x, oa, ob, mod, gain, gfin, w_mix, w_in, w_out, rows_per_cond):
    t = x.shape[0]
    tm = TOKEN_TILE
    cast = w_in.dtype == F32
    tile = _tile_of(cast)
    row = pl.BlockSpec((tm, D_MODEL), lambda s: (tile(s), 0))
    half = pl.BlockSpec((tm, A_W), lambda s: (tile(s), 0))
    w_in_specs, w_out_specs, w_shapes, w_scratch = _ffn_weight_plumbing(cast)
    outs = pl.pallas_call(
        functools.partial(_post_kernel, cast=cast),
        grid=(t // tm + (W_CHUNKS if cast else 0),),
        in_specs=[row, half, half, _mod_spec(rows_per_cond, tm, tile), _resident((1, D_MODEL)),
                  _resident((1, D_MODEL)), _resident((A_W + B_W, D_MODEL))] + w_in_specs,
        out_specs=[row] + w_out_specs,
        out_shape=[jax.ShapeDtypeStruct((t, D_MODEL), F32)] + w_shapes,
        scratch_shapes=w_scratch,
        compiler_params=_cparams(("arbitrary",)),
        name="post",
    )(x, oa, ob, mod, gain, gfin, w_mix, w_in, w_out)
    return (outs[0], outs[1], outs[2]) if cast else (outs[0], w_in, w_out)


def _rope(x, cos, sin_signed):
    w = x.shape[-1]
    lane = lax.broadcasted_iota(jnp.int32, x.shape, 1)
    p = ROPE_PAIRS
    partner = jnp.where((lane & (2 * p - 1)) < p, pltpu.roll(x, w - p, axis=1), pltpu.roll(x, p, axis=1))
    return x * cos + partner * sin_signed


def _store_values(v_s, rows, v):
    for h in range(H_B):
        vh = v[:, h * DV_B:(h + 1) * DV_B]
        v_s[rows, 2 * h * DV_B:(2 * h + 1) * DV_B] = vh.astype(BF16)
        v_s[rows, (2 * h + 1) * DV_B:(2 * h + 2) * DV_B] = jnp.ones(vh.shape, BF16)


def _attn_core(q, k_ref, v_ref, lam, gain_ref, o_ref):
    tq = q.shape[0]
    lane = lax.broadcasted_iota(jnp.int32, (tq, DV_B), 1)
    qs = q * (DQK_B ** -0.5)
    heads = [slice(h * DV_B, (h + 1) * DV_B) for h in range(H_B)]
    scores = []
    for hs in heads:
        qh = qs[:, hs]
        q2 = jnp.concatenate([jnp.where(lane < DQK_B, qh, 0.0), jnp.where(lane >= DQK_B, qh, 0.0)], axis=0)
        scores.append(lax.dot_general(q2.astype(BF16), k_ref[:, hs], (((1,), (1,)), ((), ())),
                                      preferred_element_type=F32))
    for h, (hs, s) in enumerate(zip(heads, scores)):
        e = jnp.exp(s - jnp.max(s, axis=-1, keepdims=True))
        rs = _dot(e.astype(BF16), v_ref[:, 2 * h * DV_B:(2 * h + 2) * DV_B])
        r = rs[:, :DV_B] / rs[:, DV_B:]
        o = r[:tq] - lam * r[tq:]
        o = o * lax.rsqrt(jnp.mean(o * o, axis=-1, keepdims=True) + EPS)
        o_ref[:, hs] = (o * gain_ref[...]) * (1.0 - LAM_INIT)


def _lambda(lam_ref):
    l = lam_ref[...]
    s1 = jnp.sum(l[0:1] * l[1:2], axis=-1, keepdims=True)
    s2 = jnp.sum(l[2:3] * l[3:4], axis=-1, keepdims=True)
    return jnp.exp(s1) - jnp.exp(s2) + LAM_INIT


CTX_SEQS = 4


def _attn_ctx_kernel(q_ref, k_ref, v_ref, lam_ref, gain_ref, o_ref, v_s):
    lam = _lambda(lam_ref)

    def one_sequence(bi, carry):
        _store_values(v_s, slice(None), v_ref[bi])
        _attn_core(q_ref[bi], k_ref.at[bi], v_s, lam, gain_ref, o_ref.at[bi])
        return carry

    lax.fori_loop(0, CTX_SEQS, one_sequence, 0)


def _attn_ctx(q, k, v, lam_pack, gain):
    b, l, _ = q.shape
    blk = pl.BlockSpec((CTX_SEQS, l, B_W), lambda i: (i, 0, 0))
    return pl.pallas_call(
        _attn_ctx_kernel,
        grid=(b // CTX_SEQS,),
        in_specs=[blk, blk, blk, _resident((8, LANES)), _resident((1, DV_B))],
        out_specs=blk,
        out_shape=jax.ShapeDtypeStruct((b, l, B_W), F32),
        scratch_shapes=[pltpu.VMEM((l, 2 * B_W), BF16)],
        compiler_params=_cparams(("arbitrary",)),
        name="attn_ctx",
    )(q, k, v, lam_pack, gain)


def _attn_lat_kernel(q_ref, k_ref, v_ref, ck_ref, cv_ref, cosq_ref, sinq_ref, cos_ref, sin_ref,
                     lam_ref, gain_ref, o_ref, k_s, v_s, *, past):
    @pl.when(pl.program_id(1) == 0)
    def _():
        k_s[0:past, :] = ck_ref[...].astype(BF16)
        _store_values(v_s, slice(0, past), cv_ref[...])
        k_s[past:, :] = _rope(k_ref[...], cos_ref[...], sin_ref[...]).astype(BF16)
        _store_values(v_s, slice(past, None), v_ref[...])

    q = _rope(q_ref[...], cosq_ref[...], sinq_ref[...])
    _attn_core(q, k_s, v_s, _lambda(lam_ref), gain_ref, o_ref)


def _attn_lat(q, k, v, ck, cv, cos, sin, lam_pack, gain):
    b, l, _ = q.shape
    past = ck.shape[1]
    tq = Q_TILE
    full = pl.BlockSpec((None, l, B_W), lambda i, j: (i, 0, 0))
    cache_k = cache_v = pl.BlockSpec((None, past, B_W), lambda i, j: (i, 0, 0))
    qblk = pl.BlockSpec((None, tq, B_W), lambda i, j: (i, j, 0))
    tab_q = pl.BlockSpec((tq, B_W), lambda i, j: (j, 0))
    tab = pl.BlockSpec((l, B_W), lambda i, j: (0, 0))
    return pl.pallas_call(
        functools.partial(_attn_lat_kernel, past=past),
        grid=(b, l // tq),
        in_specs=[qblk, full, full, cache_k, cache_v, tab_q, tab_q, tab, tab,
                  pl.BlockSpec((8, LANES), lambda i, j: (0, 0)), pl.BlockSpec((1, DV_B), lambda i, j: (0, 0))],
        out_specs=qblk,
        out_shape=jax.ShapeDtypeStruct((b, l, B_W), F32),
        scratch_shapes=[pltpu.VMEM((past + l, B_W), BF16), pltpu.VMEM((past + l, 2 * B_W), BF16)],
        compiler_params=_cparams(("arbitrary", "arbitrary")),
        name="attn_lat",
    )(q, k, v, ck, cv, cos, sin, cos, sin, lam_pack, gain)


def _rope_tables(length):
    pairs = ROPE_PAIRS
    n_rows = length // GRID_W
    pos_row = np.repeat(np.arange(n_rows), GRID_W).astype(np.float32)
    pos_col = np.tile(np.arange(GRID_W), n_rows).astype(np.float32)
    inv = (np.float32(ROPE_THETA) ** (-np.arange(pairs, dtype=np.float32) / np.float32(pairs))).astype(np.float32)
    ang_r = pos_row[:, None] * inv
    ang_c = pos_col[:, None] * inv
    cos = np.concatenate([np.cos(ang_r), np.cos(ang_r), np.cos(ang_c), np.cos(ang_c)], axis=-1)
    sin = np.concatenate([-np.sin(ang_r), np.sin(ang_r), -np.sin(ang_c), np.sin(ang_c)], axis=-1)
    reps = B_W // DQK_B
    return jnp.asarray(np.tile(cos, (1, reps)), F32), jnp.asarray(np.tile(sin, (1, reps)), F32)


def _block_diag(x):
    lo, hi = x[:, :LANES], x[:, LANES:]
    first = lax.broadcasted_iota(jnp.int32, lo.shape, 1) < DK_A
    top = jnp.concatenate([jnp.where(first, lo, 0), jnp.where(first, 0, lo)], axis=0)
    bot = jnp.concatenate([jnp.where(first, hi, 0), jnp.where(first, 0, hi)], axis=0)
    zero = jnp.zeros_like(top)
    return jnp.concatenate([jnp.concatenate([top, zero], axis=1), jnp.concatenate([zero, bot], axis=1)], axis=0)


SOLVE_CHUNKS = 8
DELTA_ROWS = 1024


def _delta_kernel(*refs, nb, length, has_state):
    if has_state:
        (q_ref, k_ref, v_ref, z_ref, gcs_ref, beta_ref, dn_ref, s0_ref, o_ref, st_ref, *scr) = refs
    else:
        (q_ref, k_ref, v_ref, z_ref, gcs_ref, beta_ref, dn_ref, o_ref, st_ref, *scr) = refs
        s0_ref = None
    qs, ks, vs, gexp, bexp, u0_s, kg_s, dec_s, w_s, qg_s, qkm_s, of_s, ob_s, s_s = scr
    L = length
    R = nb * L
    n_chunks = L // CHUNK
    grp = pl.program_id(1)
    W = GROUP_W
    N_GRP = A_W // W

    r256 = lax.broadcasted_iota(jnp.int32, (W, W), 0)
    c256 = lax.broadcasted_iota(jnp.int32, (W, W), 1)
    dk_bits, w_bits = DK_A.bit_length() - 1, W.bit_length() - 1
    bd_mask = (r256 >> dk_bits) == (c256 >> dk_bits)
    ones_bd = jnp.where(bd_mask, 1.0, 0.0).astype(BF16)
    ri = lax.broadcasted_iota(jnp.int32, (CHUNK, W), 0)
    cj = lax.broadcasted_iota(jnp.int32, (CHUNK, W), 1) & (CHUNK - 1)
    eye_p = ri == cj
    NS = 2 * H_A
    er = lax.broadcasted_iota(jnp.int32, (LANES, 4 * W), 0)
    ec = lax.broadcasted_iota(jnp.int32, (LANES, 4 * W), 1)
    src = ((ec >> w_bits) & 1) * H_A + grp * 4 + ((ec & (W - 1)) >> dk_bits)
    ns_bits = NS.bit_length() - 1
    expand = jnp.where((er < 4 * NS) & ((er & (NS - 1)) == src) & (((er >> ns_bits) & 1) == (ec >> (w_bits + 1))),
                       1.0, 0.0).astype(BF16)
    lane_s = lax.broadcasted_iota(jnp.int32, (L, LANES), 1)

    bd = _block_diag

    def head_sums(y):
        return _dot(y.astype(BF16), ones_bd)

    def prologue(bi, carry):
        rows_b = pl.ds(pl.multiple_of(bi * L, L), L)

        def l2n(y):
            return y * lax.rsqrt(head_sums(y * y) + EPS)

        qs[rows_b, :] = l2n(q_ref[bi]) * (DK_A ** -0.5)
        ks[rows_b, :] = l2n(k_ref[bi])
        vs[rows_b, :] = v_ref[bi]
        gc = gcs_ref[bi]
        bt = jnp.where(lane_s < NS, beta_ref[bi], 0.0)
        gc_hi = gc.astype(BF16).astype(F32)
        bt_hi = bt.astype(BF16).astype(F32)
        packed = (gc_hi + pltpu.roll(bt_hi, NS, axis=1) + pltpu.roll(gc - gc_hi, 2 * NS, axis=1)
                  + pltpu.roll(bt - bt_hi, 3 * NS, axis=1))
        both = _dot(packed.astype(BF16), expand)
        for d in range(2):
            gexp[d, rows_b, :] = both[:, d * W:(d + 1) * W]
            bexp[d, rows_b, :] = both[:, 2 * W + d * W:2 * W + (d + 1) * W]
        return carry

    lax.fori_loop(0, nb, prologue, 0)

    def solve_body(j, carry):
        chunks = []
        for k in range(SOLVE_CHUNKS):
            chunk = j * SOLVE_CHUNKS + k
            rows = pl.ds(pl.multiple_of(chunk * CHUNK, CHUNK), CHUNK)
            kc = ks[rows, :]
            qc = qs[rows, :]
            kb = kc.astype(BF16)
            res = lax.dot_general(jnp.concatenate([qc.astype(BF16), kb], axis=0), bd(kb),
                                  (((1,), (1,)), ((), ())), preferred_element_type=F32)
            qk, kk = res[:CHUNK], res[CHUNK:]
            ch = dict(rows=rows, bd_k=bd(kb), bd_v=bd(vs[rows, :].astype(BF16)), t=[], scale_u=[], scale_w=[])
            nsum = None
            for d in range(2):
                ge = gexp[d, rows, :]
                be = bexp[d, rows, :]
                if d == 0:
                    incl, strict, gl = ri >= cj, ri > cj, ge[CHUNK - 1:CHUNK, :]
                else:
                    incl, strict, gl = ri <= cj, ri < cj, ge[0:1, :]
                g_row = jnp.sum(jnp.where(eye_p, ge, 0.0), axis=0, keepdims=True)
                b_row = jnp.sum(jnp.where(eye_p, be, 0.0), axis=0, keepdims=True)
                dec_m = jnp.exp(jnp.where(incl, ge - g_row, -jnp.inf))
                nmat = jnp.where(strict, be * kk * dec_m, 0.0)
                nsum = nmat if nsum is None else nsum + nmat
                qkm_s[grp, d, rows, :] = jnp.where(incl, qk * dec_m, 0.0).astype(BF16)
                qg_s[grp, d, rows, :] = (qc * jnp.exp(ge)).astype(BF16)
                kgt = (kc * jnp.exp(gl - ge)).T
                kg_s[grp, d, rows, :] = jnp.concatenate(
                    [kgt[h * DK_A:(h + 1) * DK_A, :] for h in range(4)], axis=1).astype(BF16)
                dec_s[grp, d, pl.ds(pl.multiple_of(chunk * 8, 8), 8), :] = jnp.broadcast_to(jnp.exp(gl), (8, W))
                ch["scale_u"].append(b_row)
                ch["scale_w"].append(b_row * jnp.exp(g_row))
                ch["t"].append(jnp.where(eye_p, 1.0, 0.0) - jnp.where((ri >> 1) == (cj >> 1), nmat, 0.0))
            ch["n_both"] = bd(nsum.astype(BF16))
            chunks.append(ch)

        for lb in range(1, 6):
            same = (ri >> (lb + 1)) == (cj >> (lb + 1))
            pair = (same & (((ri >> lb) & 1) == 1) & (((cj >> lb) & 1) == 0),
                    same & (((cj >> lb) & 1) == 1) & (((ri >> lb) & 1) == 0))
            for ch in chunks:
                ch["tb"] = [t.astype(BF16) for t in ch["t"]]
                ch["y"] = _dot(jnp.concatenate(ch["tb"], axis=0), ch["n_both"]).astype(BF16)
            for ch in chunks:
                ys = (jnp.where(pair[0], ch["y"][:CHUNK], 0), jnp.where(pair[1], ch["y"][CHUNK:], 0))
                ch["t"] = [ch["t"][d] - _dot(ys[d], bd(ch["tb"][d])) for d in range(2)]
        for ch in chunks:
            tu = jnp.concatenate([(ch["t"][d] * ch["scale_u"][d]).astype(BF16) for d in range(2)], axis=0)
            tw = jnp.concatenate([(ch["t"][d] * ch["scale_w"][d]).astype(BF16) for d in range(2)], axis=0)
            u0 = _dot(tu, ch["bd_v"])
            w = _dot(tw, ch["bd_k"]).astype(BF16)
            for d in range(2):
                u0_s[grp, d, ch["rows"], :] = u0[d * CHUNK:(d + 1) * CHUNK]
                w_s[grp, d, ch["rows"], :] = w[d * CHUNK:(d + 1) * CHUNK]
        return carry

    lax.fori_loop(0, R // (SOLVE_CHUNKS * CHUNK), solve_body, 0)

    if has_state:
        for bi in range(nb):
            for d in range(2):
                for h in range(4):
                    s_s[grp, bi, d, :, h * DK_A:(h + 1) * DK_A] = s0_ref[bi, d, h]
    else:
        s_s[grp] = jnp.zeros(s_s.shape[1:], F32)

    def scan_body(i, carry):
        chains = []
        for gi in range(N_GRP):
            for bi in range(nb):
                for d in range(2):
                    c = i if d == 0 else n_chunks - 1 - i
                    rows = pl.ds(pl.multiple_of(bi * L + c * CHUNK, CHUNK), CHUNK)
                    tile = pl.ds(pl.multiple_of((bi * n_chunks + c) * 8, 8), 8)
                    chains.append(dict(
                        gi=gi, bi=bi, d=d, rows=rows, s=s_s[gi, bi, d], u0=u0_s[gi, d, rows, :],
                        kg=kg_s[gi, d, rows, :], qkm=qkm_s[gi, d, rows, :], decay=dec_s[gi, d, tile, :][0:1],
                        wq=jnp.concatenate([w_s[gi, d, rows, :], qg_s[gi, d, rows, :]], axis=0)))
        for c in chains:
            c["ws"] = _dot(c["wq"], bd(c["s"].astype(BF16)))
        for c in chains:
            c["ub"] = (c["u0"] - c["ws"][:CHUNK]).astype(BF16)
        for c in chains:
            both = _dot(jnp.concatenate([c["qkm"], c["kg"]], axis=0), bd(c["ub"]))
            c["o"] = c["ws"][CHUNK:] + both[:CHUNK]
            c["s_new"] = c["s"] * c["decay"] + both[CHUNK:]
        for c in chains:
            s_s[c["gi"], c["bi"], c["d"]] = c["s_new"]
            if c["d"] == 0:
                of_s[c["gi"], c["rows"], :] = c["o"]
            else:
                ob_s[c["gi"], c["rows"], :] = c["o"]
        return carry

    def epilogue(bi, carry):
        rows_b = pl.ds(pl.multiple_of(bi * L, L), L)
        for gi in range(N_GRP):
            lanes = slice(gi * W, (gi + 1) * W)
            o = of_s[gi, rows_b, :] + ob_s[gi, rows_b, :]
            ms = head_sums(o * o) * (1.0 / DK_A)
            o = o * lax.rsqrt(ms + EPS) * dn_ref[...]
            o_ref[bi, :, lanes] = o * _silu(z_ref[bi, :, lanes])
        return carry

    @pl.when(grp == N_GRP - 1)
    def _():
        lax.fori_loop(0, n_chunks, scan_body, 0)
        lax.fori_loop(0, nb, epilogue, 0)
        for gi in range(N_GRP):
            for bi in range(nb):
                for d in range(2):
                    for h in range(4):
                        st_ref[bi, d, gi * 4 + h] = s_s[gi, bi, d, :, h * DK_A:(h + 1) * DK_A]


def _delta(qkv, z, gcs, beta, dnorm, s0):
    b, l, _ = qkv.shape
    has_state = s0 is not None
    w = GROUP_W
    n_grp = A_W // w
    nb = DELTA_ROWS // l
    r = nb * l

    def slab(k):
        return pl.BlockSpec((nb, l, w), lambda i, g: (i, 0, k * n_grp + g))

    narrow = pl.BlockSpec((nb, l, LANES), lambda i, g: (i, 0, 0))
    wide = pl.BlockSpec((nb, l, A_W), lambda i, g: (i, 0, 0))
    in_specs = [slab(0), slab(1), slab(2), wide, narrow, narrow,
                pl.BlockSpec((1, w), lambda i, g: (0, 0))]
    args = [qkv, qkv, qkv, z, gcs, beta, dnorm]
    if has_state:
        in_specs.append(pl.BlockSpec((nb, None, 2, 4, DK_A, DK_A), lambda i, g: (i, 0, 0, g, 0, 0)))
        args.append(s0)
    per_grp = lambda shape, dt: pltpu.VMEM((n_grp,) + shape, dt)
    scratch = ([pltpu.VMEM((r, w), F32)] * 3 + [pltpu.VMEM((2, r, w), F32)] * 2
               + [per_grp((2, r, w), F32), per_grp((2, r, w), BF16), per_grp((2, r // CHUNK * 8, w), F32)]
               + [per_grp((2, r, w), BF16)] * 3 + [per_grp((r, w), F32)] * 2
               + [per_grp((nb, 2, DK_A, w), F32)])
    return pl.pallas_call(
        functools.partial(_delta_kernel, nb=nb, length=l, has_state=has_state),
        grid=(b // nb, n_grp),
        in_specs=in_specs,
        out_specs=[wide, pl.BlockSpec((nb, None, 2, H_A, DK_A, DK_A), lambda i, g: (i, 0, 0, 0, 0, 0))],
        out_shape=[jax.ShapeDtypeStruct((b, l, A_W), F32),
                   jax.ShapeDtypeStruct((b, 1, 2, H_A, DK_A, DK_A), F32)],
        scratch_shapes=scratch,
        compiler_params=_cparams(("arbitrary", "arbitrary")),
        name="delta_state" if has_state else "delta",
    )(*args)


def _pad_lanes(x, width):
    return jnp.pad(x, [(0, 0)] * (x.ndim - 1) + [(0, width - x.shape[-1])])


def kernel(x_prompt, x_sample, cache_diff_k, cache_diff_v, state_delta, c, c_ctx, w_ada, b_ada, norm_ffn1,
           w_ffn1_in, w_ffn1_out, norm_mix, w_in, conv_w, a_log, dt_bias, delta_norm, lambda_q1, lambda_k1,
           lambda_q2, lambda_k2, diff_norm, w_out, norm_ffn2, w_ffn2_in, w_ffn2_out, norm_final):
    bp, lp, _ = x_prompt.shape
    bs, ls, _ = x_sample.shape
    past = cache_diff_k.shape[2]

    w_mix_out = w_out[0].astype(BF16)
    wi = w_in[0]
    a_end = 4 * A_W
    dec_w = wi[:, a_end:a_end + 2 * H_A]
    bet_w = wi[:, a_end + 2 * H_A:a_end + 4 * H_A]
    rest = wi[:, a_end + 4 * H_A:]
    w_pack = jnp.concatenate([wi[:, :a_end], rest, _pad_lanes(dec_w, LANES), _pad_lanes(bet_w, LANES)],
                             axis=1).astype(BF16)
    g1 = norm_ffn1[0].reshape(1, D_MODEL)
    gm = norm_mix[0].reshape(1, D_MODEL)
    g2 = norm_ffn2[0].reshape(1, D_MODEL)
    gf = norm_final.reshape(1, D_MODEL)
    gpar = jnp.pad(jnp.stack([a_log[0].reshape(-1), dt_bias[0].reshape(-1)]), ((0, 6), (0, LANES - 2 * H_A)))
    dnorm = jnp.tile(delta_norm[0], 4).reshape(1, GROUP_W)
    lam_pack = jnp.pad(jnp.concatenate([lambda_q1, lambda_k1, lambda_q2, lambda_k2], axis=0),
                       ((0, 4), (0, LANES - DQK_B)))
    dgain = diff_norm[0].reshape(1, DV_B)
    cos, sin = _rope_tables(ls)

    cond = jnp.pad(jnp.concatenate([c_ctx[None, :], c], axis=0), ((0, 8 - 1 - bs), (0, 0)))
    mod = _adaln(cond, w_ada[0], b_ada[0]).reshape(8, N_MOD, D_MODEL)

    def trunk(x3, rows_per_cond, ctx, w1, w2):
        b, l, _ = x3.shape
        x = x3.reshape(b * l, D_MODEL)
        x1, *w1 = _ffn1(x, mod, g1, *w1, rows_per_cond)
        outs = _inproj(x1, mod, gm, w_pack, conv_w[0], gpar, rows_per_cond, l, ctx is None)
        qkv, z, bq, bk, bv, gcs, beta = outs[:7]
        r3 = lambda a: a.reshape(b, l, a.shape[-1])
        s0 = None if ctx is None else ctx[2]
        oa, st = _delta(r3(qkv), r3(z), r3(gcs), r3(beta), dnorm, s0)
        if ctx is None:
            ob = _attn_ctx(r3(bq), r3(bk), r3(bv), lam_pack, dgain)
        else:
            ob = _attn_lat(r3(bq), r3(bk), r3(bv), ctx[0], ctx[1], cos, sin, lam_pack, dgain)
        y, *w2 = _post(x1, oa.reshape(b * l, A_W), ob.reshape(b * l, B_W), mod, g2, gf, w_mix_out, *w2,
                       rows_per_cond)
        return y.reshape(b, l, D_MODEL), outs[7:], st, w1, w2

    y_prompt, (k_maps, v_heads), s_ctx, w1, w2 = trunk(x_prompt, None, None, (w_ffn1_in[0], w_ffn1_out[0]),
                                                       (w_ffn2_in[0], w_ffn2_out[0]))
    ctx = (cache_diff_k[:, 0].reshape(bs, past, B_W), cache_diff_v[:, 0].reshape(bs, past, B_W), state_delta)
    y_sample = trunk(x_sample, ls, ctx, w1, w2)[0]
    new_k = k_maps.reshape(bp, 1, lp, H_B, 2, DQK_B)
    new_v = v_heads.reshape(bp, 1, lp, H_B, DV_B)
    return (y_prompt, y_sample, new_k, new_v, s_ctx)
```

```python
import functools
import math

import jax
import jax.numpy as jnp
import numpy as np
from jax import lax
from jax.experimental import pallas as pl
from jax.experimental.pallas import tpu as pltpu

F32 = jnp.float32
BF16 = jnp.bfloat16

D_MODEL = 1024
D_FF = 2816
N_MOD = 9
H_A = 8
DK_A = 64
A_W = H_A * DK_A
H_B = 4
DQK_B = 64
DV_B = 128
B_W = H_B * DV_B
CONV_K = 5
CHUNK = 64
GRID_W = 64
ROPE_THETA = 10000.0
ROPE_PAIRS = DQK_B // 4
EPS = 1e-6
LAM_INIT = 0.8 - 0.6 * math.exp(-0.3 * 0)

LANES = 128
GROUP_W = 4 * DK_A
IN_PACK_COLS = 3 * A_W + A_W + 3 * B_W + 2 * LANES
VMEM_LIMIT = 56 * 1024 * 1024

MXU_W = 256
TOKEN_TILE = 512
FF_CHUNKS = ((0, 6 * MXU_W), (6 * MXU_W, D_FF))
Q_TILE = 256


def _cparams(sem):
    return pltpu.CompilerParams(dimension_semantics=sem, vmem_limit_bytes=VMEM_LIMIT)


def _resident(shape):
    nd = len(shape)
    return pl.BlockSpec(shape, lambda *_: (0,) * nd, pipeline_mode=pl.Buffered(1))


def _silu(x):
    half = 0.5 * x
    return half * jnp.tanh(half) + half


def _mod_norm(x, gain, shift, scale):
    y = x * lax.rsqrt(jnp.mean(x * x, axis=-1, keepdims=True) + EPS)
    return (y * gain) * (1.0 + scale) + shift


def _dot(a, b):
    return jnp.dot(a, b, preferred_element_type=F32)


def _adaln_kernel(c_ref, w_ref, b_ref, o_ref):
    s = _silu(c_ref[...])
    o_ref[...] = _dot(s.astype(BF16), w_ref[...].astype(BF16)) + b_ref[...]


def _adaln(cond, w_ada, b_ada):
    n = N_MOD * D_MODEL
    tn = n // 4
    return pl.pallas_call(
        _adaln_kernel,
        grid=(n // tn,),
        in_specs=[pl.BlockSpec((8, D_MODEL), lambda j: (0, 0)),
                  pl.BlockSpec((D_MODEL, tn), lambda j: (0, j)),
                  pl.BlockSpec((1, tn), lambda j: (0, j))],
        out_specs=pl.BlockSpec((8, tn), lambda j: (0, j)),
        out_shape=jax.ShapeDtypeStruct((8, n), F32),
        compiler_params=_cparams(("arbitrary",)),
        name="adaln",
    )(cond, w_ada, b_ada.reshape(1, n))


def _swiglu_update(x, mod_ref, k0, gain, w_in_ref, w_out_ref):
    h = _mod_norm(x, gain, mod_ref[k0:k0 + 1, :], mod_ref[k0 + 1:k0 + 2, :]).astype(BF16)
    acc = None
    for lo, hi in FF_CHUNKS:
        g = _dot(h, w_in_ref[:, lo:hi])
        u = _dot(h, w_in_ref[:, D_FF + lo:D_FF + hi])
        part = _dot((_silu(g) * u).astype(BF16), w_out_ref[lo:hi, :])
        acc = part if acc is None else acc + part
    return x + (0.5 * mod_ref[k0 + 2:k0 + 3, :]) * acc


W_CHUNKS = 11
W_IN_CHUNK = 2 * D_FF // W_CHUNKS
W_OUT_CHUNK = D_FF // W_CHUNKS


def _stage_ffn_weights(step, w_in_ref, w_out_ref, w_in_bf_ref, w_out_bf_ref, w_in_s, w_out_s):
    for c in range(W_CHUNKS):
        @pl.when(step == c)
        def _():
            a = w_in_ref[...].astype(BF16)
            b = w_out_ref[...].astype(BF16)
            w_in_bf_ref[...] = a
            w_out_bf_ref[...] = b
            w_in_s[:, c * W_IN_CHUNK:(c + 1) * W_IN_CHUNK] = a
            w_out_s[c * W_OUT_CHUNK:(c + 1) * W_OUT_CHUNK, :] = b


def _ffn_weight_plumbing(cast):
    if not cast:
        return [_resident((D_MODEL, 2 * D_FF)), _resident((D_FF, D_MODEL))], [], [], []
    last = W_CHUNKS - 1
    w_in_chunk = pl.BlockSpec((D_MODEL, W_IN_CHUNK), lambda s: (0, jnp.minimum(s, last)))
    w_out_chunk = pl.BlockSpec((W_OUT_CHUNK, D_MODEL), lambda s: (jnp.minimum(s, last), 0))
    out_shape = [jax.ShapeDtypeStruct((D_MODEL, 2 * D_FF), BF16), jax.ShapeDtypeStruct((D_FF, D_MODEL), BF16)]
    scratch = [pltpu.VMEM((D_MODEL, 2 * D_FF), BF16), pltpu.VMEM((D_FF, D_MODEL), BF16)]
    return [w_in_chunk, w_out_chunk], [w_in_chunk, w_out_chunk], out_shape, scratch


def _tile_of(cast):
    off = W_CHUNKS if cast else 0
    return lambda s: jnp.maximum(s - off, 0)


def _mod_spec(rows_per_cond, tm, tile):
    if rows_per_cond is None:
        return pl.BlockSpec((None, N_MOD, D_MODEL), lambda s: (0, 0, 0))
    return pl.BlockSpec((None, N_MOD, D_MODEL), lambda s: (1 + (tile(s) * tm) // rows_per_cond, 0, 0))


def _ffn1_kernel(*refs, cast):
    x_ref, mod_ref, gain_ref, w_in_ref, w_out_ref, o_ref = refs[:6]
    if cast:
        step = pl.program_id(0)
        w_in_s, w_out_s = refs[8:10]
        _stage_ffn_weights(step, w_in_ref, w_out_ref, refs[6], refs[7], w_in_s, w_out_s)

        @pl.when(step >= W_CHUNKS)
        def _():
            o_ref[...] = _swiglu_update(x_ref[...], mod_ref, 0, gain_ref[...], w_in_s, w_out_s)
    else:
        o_ref[...] = _swiglu_update(x_ref[...], mod_ref, 0, gain_ref[...], w_in_ref, w_out_ref)


def _ffn1(x, mod, gain, w_in, w_out, rows_per_cond):
    t = x.shape[0]
    tm = TOKEN_TILE
    cast = w_in.dtype == F32
    tile = _tile_of(cast)
    row = pl.BlockSpec((tm, D_MODEL), lambda s: (tile(s), 0))
    w_in_specs, w_out_specs, w_shapes, w_scratch = _ffn_weight_plumbing(cast)
    outs = pl.pallas_call(
        functools.partial(_ffn1_kernel, cast=cast),
        grid=(t // tm + (W_CHUNKS if cast else 0),),
        in_specs=[row, _mod_spec(rows_per_cond, tm, tile), _resident((1, D_MODEL))] + w_in_specs,
        out_specs=[row] + w_out_specs,
        out_shape=[jax.ShapeDtypeStruct((t, D_MODEL), F32)] + w_shapes,
        scratch_shapes=w_scratch,
        compiler_params=_cparams(("arbitrary",)),
        name="ffn1",
    )(x, mod, gain, w_in, w_out)
    return (outs[0], outs[1], outs[2]) if cast else (outs[0], w_in, w_out)


_IN_SEGS = (("qkv", 0, 3 * A_W), ("z", 3 * A_W, A_W), ("bq", 4 * A_W, B_W), ("bk", 4 * A_W + B_W, B_W),
            ("bv", 4 * A_W + 2 * B_W, B_W), ("dec", 4 * A_W + 3 * B_W, LANES),
            ("bet", 4 * A_W + 3 * B_W + LANES, LANES))


HALO = 8
CONV_SLAB = 256


def _inproj_kernel(*refs, tm, seq_len, emit_maps):
    x_ref, xp_ref, xn_ref, mod_ref, gain_ref, w_ref, cw_ref, gpar_ref = refs[:8]
    qkv_ref, z_ref, bq_ref, bk_ref, bv_ref, gcs_ref, beta_ref = refs[8:15]
    hs, xpad, ybuf = refs[-3:]
    shift, scale = mod_ref[3:4, :], mod_ref[4:5, :]
    hf = _mod_norm(x_ref[...], gain_ref[...], shift, scale)
    h = hf.astype(BF16)
    segs = dict((name, (start, width)) for name, start, width in _IN_SEGS)

    def proj(name):
        start, width = segs[name]
        return _dot(h, w_ref[:, start:start + width])

    seg = min(tm, seq_len)
    n_seg = tm // seg
    ext = seg + 2 * HALO
    m = ext // 8
    pad = (CONV_K // 2) * 8
    qkv_w = 3 * A_W
    units = [(s, c) for s in range(n_seg) for c in range(qkv_w // CONV_SLAB)]
    def put_rows(lo, hi, val):
        for ct in range(D_MODEL // LANES):
            hs[ct, lo:hi, :] = val[:, ct * LANES:(ct + 1) * LANES]

    if seq_len > tm:
        tiles = seq_len // tm
        pos = pl.program_id(0) % tiles
        put_rows(0, HALO, _mod_norm(xp_ref[...], gain_ref[...], shift, scale) * jnp.where(pos > 0, 1.0, 0.0))
        put_rows(HALO + seg, ext,
                 _mod_norm(xn_ref[...], gain_ref[...], shift, scale) * jnp.where(pos < tiles - 1, 1.0, 0.0))
    else:
        for s in range(n_seg):
            put_rows(s * ext, s * ext + HALO, jnp.zeros((HALO, D_MODEL), F32))
            put_rows(s * ext + HALO + seg, (s + 1) * ext, jnp.zeros((HALO, D_MODEL), F32))
    for s in range(n_seg):
        put_rows(s * ext + HALO, s * ext + HALO + seg, hf[s * seg:(s + 1) * seg])
    h_il = [jnp.concatenate(
        [jnp.concatenate([hs[ct, pl.ds(s * ext + v, 8, stride=m), :] for v in range(m)], axis=0)
         for ct in range(D_MODEL // LANES)], axis=1).astype(BF16) for s in range(n_seg)]

    def project(s, c):
        cs = slice(c * CONV_SLAB, (c + 1) * CONV_SLAB)
        base = s * (ext + 2 * pad)
        res = _dot(h_il[s], w_ref[:, cs])
        xpad[base + pad:base + pad + ext, cs] = res
        for j in range(CONV_K // 2):
            xpad[base + 8 * j:base + 8 * (j + 1), cs] = pltpu.roll(res[ext - pad + 8 * j:ext - pad + 8 * (j + 1)], 1, 0)
            xpad[base + pad + ext + 8 * j:base + pad + ext + 8 * (j + 1), cs] = pltpu.roll(res[8 * j:8 * (j + 1)], 7, 0)

    def convolve(s, c):
        cs = slice(c * CONV_SLAB, (c + 1) * CONV_SLAB)
        base = s * (ext + 2 * pad)
        acc = None
        for tap in range(CONV_K):
            term = xpad[base + 8 * tap:base + 8 * tap + ext, cs] * cw_ref[tap:tap + 1, cs]
            acc = term if acc is None else acc + term
        y = _silu(acc)
        for j in range(CONV_SLAB // LANES):
            ct = c * (CONV_SLAB // LANES) + j
            for v in range(m):
                ybuf[ct, pl.ds(v, 8, stride=m), :] = y[8 * v:8 * (v + 1), j * LANES:(j + 1) * LANES]
            qkv_ref[s * seg:(s + 1) * seg, ct * LANES:(ct + 1) * LANES] = ybuf[ct, HALO:HALO + seg, :]

    z_ref[...] = proj("z")
    bq_ref[...] = proj("bq").astype(bq_ref.dtype)
    zk = proj("bk")
    zv = proj("bv")
    bk_ref[...] = zk.astype(bk_ref.dtype)
    bv_ref[...] = zv.astype(bv_ref.dtype)
    if emit_maps:
        bk8_ref, bv4_ref = refs[15:17]
        maps_s = refs[-4]
        pitch = tm + 8
        for c in range(H_B):
            pair = zk[:, c * LANES:(c + 1) * LANES]
            maps_s[2 * c * pitch:2 * c * pitch + tm, :] = pair
            maps_s[(2 * c + 1) * pitch:(2 * c + 1) * pitch + tm, :] = pltpu.roll(pair, DQK_B, axis=1)
            maps_s[(2 * H_B + c) * pitch:(2 * H_B + c) * pitch + tm, :] = zv[:, c * DV_B:(c + 1) * DV_B]

        for t in range(tm):
            bk8_ref[t] = maps_s[pl.ds(t, 2 * H_B, stride=pitch), :][:, 0:DQK_B]
            bv4_ref[t] = maps_s[pl.ds(2 * H_B * pitch + t, H_B, stride=pitch), :]
    for unit in units:
        project(*unit)
        convolve(*unit)

    gpar = gpar_ref[...]
    dec_start = segs["dec"][0]
    gate_pre = _dot(h, w_ref[:, dec_start:dec_start + 2 * LANES])
    xg = gate_pre[:, :LANES] + gpar[1:2, :]
    softplus = jnp.maximum(xg, 0.0) + jnp.log1p(jnp.exp(-jnp.abs(xg)))
    lane = lax.broadcasted_iota(jnp.int32, (tm, LANES), 1)
    g = jnp.where(lane < 2 * H_A, -jnp.exp(gpar[0:1, :]) * softplus, 0.0)
    beta_ref[...] = jax.nn.sigmoid(gate_pre[:, LANES:])
    pos_c = lax.broadcasted_iota(jnp.int32, (tm, 1), 0) & (CHUNK - 1)
    gcf = g
    gcb = g
    for sft in (1, 2, 4, 8, 16, 32):
        gcf = gcf + jnp.where(pos_c >= sft, pltpu.roll(gcf, sft, axis=0), 0.0)
        gcb = gcb + jnp.where(pos_c < CHUNK - sft, pltpu.roll(gcb, tm - sft, axis=0), 0.0)
    gcs_ref[...] = jnp.where(lane < H_A, gcf, gcb)


def _inproj(x, mod, gain, w_pack, conv_w, gpar, rows_per_cond, seq_len, emit_maps):
    t = x.shape[0]
    tm = TOKEN_TILE
    hb = tm // HALO
    n_seg = max(1, tm // seq_len)
    ext = tm // n_seg + 2 * HALO
    row = pl.BlockSpec((tm, D_MODEL), lambda i: (i, 0))
    prev = pl.BlockSpec((HALO, D_MODEL), lambda i: (jnp.maximum(i * hb - 1, 0), 0))
    nxt = pl.BlockSpec((HALO, D_MODEL), lambda i: (jnp.minimum((i + 1) * hb, t // HALO - 1), 0))
    widths = [w for _, _, w in _IN_SEGS]
    out_specs = [pl.BlockSpec((tm, w), lambda i: (i, 0)) for w in widths]
    attn_dtype = BF16 if emit_maps else F32
    out_shape = [jax.ShapeDtypeStruct((t, w), attn_dtype if name in ("bq", "bk", "bv") else F32)
                 for name, _, w in _IN_SEGS]
    if emit_maps:
        out_specs += [pl.BlockSpec((tm, 2 * H_B, DQK_B), lambda i: (i, 0, 0)),
                      pl.BlockSpec((tm, H_B, DV_B), lambda i: (i, 0, 0))]
        out_shape += [jax.ShapeDtypeStruct((t, 2 * H_B, DQK_B), F32), jax.ShapeDtypeStruct((t, H_B, DV_B), F32)]
    return pl.pallas_call(
        functools.partial(_inproj_kernel, tm=tm, seq_len=seq_len, emit_maps=emit_maps),
        grid=(t // tm,),
        in_specs=[row, prev, nxt, _mod_spec(rows_per_cond, tm, lambda s: s), _resident((1, D_MODEL)),
                  _resident((D_MODEL, IN_PACK_COLS)), _resident((CONV_K, 3 * A_W)), _resident((8, LANES))],
        out_specs=out_specs,
        out_shape=out_shape,
        scratch_shapes=([pltpu.VMEM((3 * H_B * (tm + 8), LANES), F32)] if emit_maps else [])
        + [pltpu.VMEM((D_MODEL // LANES, n_seg * ext, LANES), F32),
                        pltpu.VMEM((n_seg * (ext + 2 * (CONV_K // 2) * 8), 3 * A_W), F32),
                        pltpu.VMEM((3 * A_W // LANES, ext, LANES), F32)],
        compiler_params=_cparams(("arbitrary",)),
        name="inproj",
    )(x, x, x, mod, gain, w_pack, conv_w, gpar)


def _post_kernel(*refs, cast):
    x_ref, oa_ref, ob_ref, mod_ref, gain_ref, gfin_ref, wmix_ref, w_in_ref, w_out_ref, o_ref = refs[:10]

    def tokens(w_in, w_out):
        y = _dot(oa_ref[...].astype(BF16), wmix_ref[0:A_W, :]) + _dot(ob_ref[...].astype(BF16), wmix_ref[A_W:, :])
        x = x_ref[...] + mod_ref[5:6, :] * y
        x = _swiglu_update(x, mod_ref, 6, gain_ref[...], w_in, w_out)
        o_ref[...] = (x * lax.rsqrt(jnp.mean(x * x, axis=-1, keepdims=True) + EPS)) * gfin_ref[...]

    if cast:
        step = pl.program_id(0)
        w_in_s, w_out_s = refs[12:14]
        _stage_ffn_weights(step, w_in_ref, w_out_ref, refs[10], refs[11], w_in_s, w_out_s)
        pl.when(step >= W_CHUNKS)(lambda: tokens(w_in_s, w_out_s))
    else:
        tokens(w_in_ref, w_out_ref)


def _post(x, oa, ob, mod, gain, gfin, w_mix, w_in, w_out, rows_per_cond):
    t = x.shape[0]
    tm = TOKEN_TILE
    cast = w_in.dtype == F32
    tile = _tile_of(cast)
    row = pl.BlockSpec((tm, D_MODEL), lambda s: (tile(s), 0))
    half = pl.BlockSpec((tm, A_W), lambda s: (tile(s), 0))
    w_in_specs, w_out_specs, w_shapes, w_scratch = _ffn_weight_plumbing(cast)
    outs = pl.pallas_call(
        functools.partial(_post_kernel, cast=cast),
        grid=(t // tm + (W_CHUNKS if cast else 0),),
        in_specs=[row, half, half, _mod_spec(rows_per_cond, tm, tile), _resident((1, D_MODEL)),
                  _resident((1, D_MODEL)), _resident((A_W + B_W, D_MODEL))] + w_in_specs,
        out_specs=[row] + w_out_specs,
        out_shape=[jax.ShapeDtypeStruct((t, D_MODEL), F32)] + w_shapes,
        scratch_shapes=w_scratch,
        compiler_params=_cparams(("arbitrary",)),
        name="post",
    )(x, oa, ob, mod, gain, gfin, w_mix, w_in, w_out)
    return (outs[0], outs[1], outs[2]) if cast else (outs[0], w_in, w_out)


def _rope(x, cos, sin_signed):
    w = x.shape[-1]
    lane = lax.broadcasted_iota(jnp.int32, x.shape, 1)
    p = ROPE_PAIRS
    partner = jnp.where((lane & (2 * p - 1)) < p, pltpu.roll(x, w - p, axis=1), pltpu.roll(x, p, axis=1))
    return x * cos + partner * sin_signed


def _store_values(v_s, rows, v):
    for h in range(H_B):
        vh = v[:, h * DV_B:(h + 1) * DV_B]
        v_s[rows, 2 * h * DV_B:(2 * h + 1) * DV_B] = vh.astype(BF16)
        v_s[rows, (2 * h + 1) * DV_B:(2 * h + 2) * DV_B] = jnp.ones(vh.shape, BF16)


def _attn_core(q, k_ref, v_ref, lam, gain_ref, o_ref):
    tq = q.shape[0]
    lane = lax.broadcasted_iota(jnp.int32, (tq, DV_B), 1)
    qs = q * (DQK_B ** -0.5)
    heads = [slice(h * DV_B, (h + 1) * DV_B) for h in range(H_B)]
    scores = []
    for hs in heads:
        qh = qs[:, hs]
        q2 = jnp.concatenate([jnp.where(lane < DQK_B, qh, 0.0), jnp.where(lane >= DQK_B, qh, 0.0)], axis=0)
        scores.append(lax.dot_general(q2.astype(BF16), k_ref[:, hs], (((1,), (1,)), ((), ())),
                                      preferred_element_type=F32))
    for h, (hs, s) in enumerate(zip(heads, scores)):
        e = jnp.exp(s - jnp.max(s, axis=-1, keepdims=True))
        rs = _dot(e.astype(BF16), v_ref[:, 2 * h * DV_B:(2 * h + 2) * DV_B])
        r = rs[:, :DV_B] / rs[:, DV_B:]
        o = r[:tq] - lam * r[tq:]
        o = o * lax.rsqrt(jnp.mean(o * o, axis=-1, keepdims=True) + EPS)
        o_ref[:, hs] = (o * gain_ref[...]) * (1.0 - LAM_INIT)


def _lambda(lam_ref):
    l = lam_ref[...]
    s1 = jnp.sum(l[0:1] * l[1:2], axis=-1, keepdims=True)
    s2 = jnp.sum(l[2:3] * l[3:4], axis=-1, keepdims=True)
    return jnp.exp(s1) - jnp.exp(s2) + LAM_INIT


CTX_SEQS = 4


def _attn_ctx_kernel(q_ref, k_ref, v_ref, lam_ref, gain_ref, o_ref, v_s):
    lam = _lambda(lam_ref)

    def one_sequence(bi, carry):
        _store_values(v_s, slice(None), v_ref[bi])
        _attn_core(q_ref[bi], k_ref.at[bi], v_s, lam, gain_ref, o_ref.at[bi])
        return carry

    lax.fori_loop(0, CTX_SEQS, one_sequence, 0)


def _attn_ctx(q, k, v, lam_pack, gain):
    b, l, _ = q.shape
    blk = pl.BlockSpec((CTX_SEQS, l, B_W), lambda i: (i, 0, 0))
    return pl.pallas_call(
        _attn_ctx_kernel,
        grid=(b // CTX_SEQS,),
        in_specs=[blk, blk, blk, _resident((8, LANES)), _resident((1, DV_B))],
        out_specs=blk,
        out_shape=jax.ShapeDtypeStruct((b, l, B_W), F32),
        scratch_shapes=[pltpu.VMEM((l, 2 * B_W), BF16)],
        compiler_params=_cparams(("arbitrary",)),
        name="attn_ctx",
    )(q, k, v, lam_pack, gain)


def _attn_lat_kernel(q_ref, k_ref, v_ref, ck_ref, cv_ref, cosq_ref, sinq_ref, cos_ref, sin_ref,
                     lam_ref, gain_ref, o_ref, k_s, v_s, *, past):
    @pl.when(pl.program_id(1) == 0)
    def _():
        k_s[0:past, :] = ck_ref[...].astype(BF16)
        _store_values(v_s, slice(0, past), cv_ref[...])
        k_s[past:, :] = _rope(k_ref[...], cos_ref[...], sin_ref[...]).astype(BF16)
        _store_values(v_s, slice(past, None), v_ref[...])

    q = _rope(q_ref[...], cosq_ref[...], sinq_ref[...])
    _attn_core(q, k_s, v_s, _lambda(lam_ref), gain_ref, o_ref)


def _attn_lat(q, k, v, ck, cv, cos, sin, lam_pack, gain):
    b, l, _ = q.shape
    past = ck.shape[1]
    tq = Q_TILE
    full = pl.BlockSpec((None, l, B_W), lambda i, j: (i, 0, 0))
    cache_k = cache_v = pl.BlockSpec((None, past, B_W), lambda i, j: (i, 0, 0))
    qblk = pl.BlockSpec((None, tq, B_W), lambda i, j: (i, j, 0))
    tab_q = pl.BlockSpec((tq, B_W), lambda i, j: (j, 0))
    tab = pl.BlockSpec((l, B_W), lambda i, j: (0, 0))
    return pl.pallas_call(
        functools.partial(_attn_lat_kernel, past=past),
        grid=(b, l // tq),
        in_specs=[qblk, full, full, cache_k, cache_v, tab_q, tab_q, tab, tab,
                  pl.BlockSpec((8, LANES), lambda i, j: (0, 0)), pl.BlockSpec((1, DV_B), lambda i, j: (0, 0))],
        out_specs=qblk,
        out_shape=jax.ShapeDtypeStruct((b, l, B_W), F32),
        scratch_shapes=[pltpu.VMEM((past + l, B_W), BF16), pltpu.VMEM((past + l, 2 * B_W), BF16)],
        compiler_params=_cparams(("arbitrary", "arbitrary")),
        name="attn_lat",
    )(q, k, v, ck, cv, cos, sin, cos, sin, lam_pack, gain)


def _rope_tables(length):
    pairs = ROPE_PAIRS
    n_rows = length // GRID_W
    pos_row = np.repeat(np.arange(n_rows), GRID_W).astype(np.float32)
    pos_col = np.tile(np.arange(GRID_W), n_rows).astype(np.float32)
    inv = (np.float32(ROPE_THETA) ** (-np.arange(pairs, dtype=np.float32) / np.float32(pairs))).astype(np.float32)
    ang_r = pos_row[:, None] * inv
    ang_c = pos_col[:, None] * inv
    cos = np.concatenate([np.cos(ang_r), np.cos(ang_r), np.cos(ang_c), np.cos(ang_c)], axis=-1)
    sin = np.concatenate([-np.sin(ang_r), np.sin(ang_r), -np.sin(ang_c), np.sin(ang_c)], axis=-1)
    reps = B_W // DQK_B
    return jnp.asarray(np.tile(cos, (1, reps)), F32), jnp.asarray(np.tile(sin, (1, reps)), F32)


def _block_diag(x):
    lo, hi = x[:, :LANES], x[:, LANES:]
    first = lax.broadcasted_iota(jnp.int32, lo.shape, 1) < DK_A
    top = jnp.concatenate([jnp.where(first, lo, 0), jnp.where(first, 0, lo)], axis=0)
    bot = jnp.concatenate([jnp.where(first, hi, 0), jnp.where(first, 0, hi)], axis=0)
    zero = jnp.zeros_like(top)
    return jnp.concatenate([jnp.concatenate([top, zero], axis=1), jnp.concatenate([zero, bot], axis=1)], axis=0)


SOLVE_CHUNKS = 8
DELTA_ROWS = 1024


def _delta_kernel(*refs, nb, length, has_state):
    if has_state:
        (q_ref, k_ref, v_ref, z_ref, gcs_ref, beta_ref, dn_ref, s0_ref, o_ref, st_ref, *scr) = refs
    else:
        (q_ref, k_ref, v_ref, z_ref, gcs_ref, beta_ref, dn_ref, o_ref, st_ref, *scr) = refs
        s0_ref = None
    qs, ks, vs, gexp, bexp, u0_s, kg_s, dec_s, w_s, qg_s, qkm_s, of_s, ob_s, s_s = scr
    L = length
    R = nb * L
    n_chunks = L // CHUNK
    grp = pl.program_id(1)
    W = GROUP_W
    N_GRP = A_W // W

    r256 = lax.broadcasted_iota(jnp.int32, (W, W), 0)
    c256 = lax.broadcasted_iota(jnp.int32, (W, W), 1)
    dk_bits, w_bits = DK_A.bit_length() - 1, W.bit_length() - 1
    bd_mask = (r256 >> dk_bits) == (c256 >> dk_bits)
    ones_bd = jnp.where(bd_mask, 1.0, 0.0).astype(BF16)
    ri = lax.broadcasted_iota(jnp.int32, (CHUNK, W), 0)
    cj = lax.broadcasted_iota(jnp.int32, (CHUNK, W), 1) & (CHUNK - 1)
    eye_p = ri == cj
    NS = 2 * H_A
    er = lax.broadcasted_iota(jnp.int32, (LANES, 4 * W), 0)
    ec = lax.broadcasted_iota(jnp.int32, (LANES, 4 * W), 1)
    src = ((ec >> w_bits) & 1) * H_A + grp * 4 + ((ec & (W - 1)) >> dk_bits)
    ns_bits = NS.bit_length() - 1
    expand = jnp.where((er < 4 * NS) & ((er & (NS - 1)) == src) & (((er >> ns_bits) & 1) == (ec >> (w_bits + 1))),
                       1.0, 0.0).astype(BF16)
    lane_s = lax.broadcasted_iota(jnp.int32, (L, LANES), 1)

    bd = _block_diag

    def head_sums(y):
        return _dot(y.astype(BF16), ones_bd)

    def prologue(bi, carry):
        rows_b = pl.ds(pl.multiple_of(bi * L, L), L)

        def l2n(y):
            return y * lax.rsqrt(head_sums(y * y) + EPS)

        qs[rows_b, :] = l2n(q_ref[bi]) * (DK_A ** -0.5)
        ks[rows_b, :] = l2n(k_ref[bi])
        vs[rows_b, :] = v_ref[bi]
        gc = gcs_ref[bi]
        bt = jnp.where(lane_s < NS, beta_ref[bi], 0.0)
        gc_hi = gc.astype(BF16).astype(F32)
        bt_hi = bt.astype(BF16).astype(F32)
        packed = (gc_hi + pltpu.roll(bt_hi, NS, axis=1) + pltpu.roll(gc - gc_hi, 2 * NS, axis=1)
                  + pltpu.roll(bt - bt_hi, 3 * NS, axis=1))
        both = _dot(packed.astype(BF16), expand)
        for d in range(2):
            gexp[d, rows_b, :] = both[:, d * W:(d + 1) * W]
            bexp[d, rows_b, :] = both[:, 2 * W + d * W:2 * W + (d + 1) * W]
        return carry

    lax.fori_loop(0, nb, prologue, 0)

    def solve_body(j, carry):
        chunks = []
        for k in range(SOLVE_CHUNKS):
            chunk = j * SOLVE_CHUNKS + k
            rows = pl.ds(pl.multiple_of(chunk * CHUNK, CHUNK), CHUNK)
            kc = ks[rows, :]
            qc = qs[rows, :]
            kb = kc.astype(BF16)
            res = lax.dot_general(jnp.concatenate([qc.astype(BF16), kb], axis=0), bd(kb),
                                  (((1,), (1,)), ((), ())), preferred_element_type=F32)
            qk, kk = res[:CHUNK], res[CHUNK:]
            ch = dict(rows=rows, bd_k=bd(kb), bd_v=bd(vs[rows, :].astype(BF16)), t=[], scale_u=[], scale_w=[])
            nsum = None
            for d in range(2):
                ge = gexp[d, rows, :]
                be = bexp[d, rows, :]
                if d == 0:
                    incl, strict, gl = ri >= cj, ri > cj, ge[CHUNK - 1:CHUNK, :]
                else:
                    incl, strict, gl = ri <= cj, ri < cj, ge[0:1, :]
                g_row = jnp.sum(jnp.where(eye_p, ge, 0.0), axis=0, keepdims=True)
                b_row = jnp.sum(jnp.where(eye_p, be, 0.0), axis=0, keepdims=True)
                dec_m = jnp.exp(jnp.where(incl, ge - g_row, -jnp.inf))
                nmat = jnp.where(strict, be * kk * dec_m, 0.0)
                nsum = nmat if nsum is None else nsum + nmat
                qkm_s[grp, d, rows, :] = jnp.where(incl, qk * dec_m, 0.0).astype(BF16)
                qg_s[grp, d, rows, :] = (qc * jnp.exp(ge)).astype(BF16)
                kgt = (kc * jnp.exp(gl - ge)).T
                kg_s[grp, d, rows, :] = jnp.concatenate(
                    [kgt[h * DK_A:(h + 1) * DK_A, :] for h in range(4)], axis=1).astype(BF16)
                dec_s[grp, d, pl.ds(pl.multiple_of(chunk * 8, 8), 8), :] = jnp.broadcast_to(jnp.exp(gl), (8, W))
                ch["scale_u"].append(b_row)
                ch["scale_w"].append(b_row * jnp.exp(g_row))
                ch["t"].append(jnp.where(eye_p, 1.0, 0.0) - jnp.where((ri >> 1) == (cj >> 1), nmat, 0.0))
            ch["n_both"] = bd(nsum.astype(BF16))
            chunks.append(ch)

        for lb in range(1, 6):
            same = (ri >> (lb + 1)) == (cj >> (lb + 1))
            pair = (same & (((ri >> lb) & 1) == 1) & (((cj >> lb) & 1) == 0),
                    same & (((cj >> lb) & 1) == 1) & (((ri >> lb) & 1) == 0))
            for ch in chunks:
                ch["tb"] = [t.astype(BF16) for t in ch["t"]]
                ch["y"] = _dot(jnp.concatenate(ch["tb"], axis=0), ch["n_both"]).astype(BF16)
            even_rows = ((ri >> lb) & 1) == 0
            for ch in chunks:
                ys = jnp.concatenate([jnp.where(pair[0], ch["y"][:CHUNK], 0),
                                      jnp.where(pair[1], ch["y"][CHUNK:], 0)], axis=0)
                upd = _dot(ys, bd(jnp.where(even_rows, ch["tb"][0], ch["tb"][1])))
                ch["t"] = [ch["t"][d] - upd[d * CHUNK:(d + 1) * CHUNK] for d in range(2)]
        for ch in chunks:
            tu = jnp.concatenate([(ch["t"][d] * ch["scale_u"][d]).astype(BF16) for d in range(2)], axis=0)
            tw = jnp.concatenate([(ch["t"][d] * ch["scale_w"][d]).astype(BF16) for d in range(2)], axis=0)
            u0 = _dot(tu, ch["bd_v"])
            w = _dot(tw, ch["bd_k"]).astype(BF16)
            for d in range(2):
                u0_s[grp, d, ch["rows"], :] = u0[d * CHUNK:(d + 1) * CHUNK]
                w_s[grp, d, ch["rows"], :] = w[d * CHUNK:(d + 1) * CHUNK]
        return carry

    lax.fori_loop(0, R // (SOLVE_CHUNKS * CHUNK), solve_body, 0)

    if has_state:
        for bi in range(nb):
            for d in range(2):
                for h in range(4):
                    s_s[grp, bi, d, :, h * DK_A:(h + 1) * DK_A] = s0_ref[bi, d, h]
    else:
        s_s[grp] = jnp.zeros(s_s.shape[1:], F32)

    def scan_body(i, carry):
        chains = []
        for gi in range(N_GRP):
            for bi in range(nb):
                for d in range(2):
                    c = i if d == 0 else n_chunks - 1 - i
                    rows = pl.ds(pl.multiple_of(bi * L + c * CHUNK, CHUNK), CHUNK)
                    tile = pl.ds(pl.multiple_of((bi * n_chunks + c) * 8, 8), 8)
                    chains.append(dict(
                        gi=gi, bi=bi, d=d, rows=rows, s=s_s[gi, bi, d], u0=u0_s[gi, d, rows, :],
                        kg=kg_s[gi, d, rows, :], qkm=qkm_s[gi, d, rows, :], decay=dec_s[gi, d, tile, :][0:1],
                        wq=jnp.concatenate([w_s[gi, d, rows, :], qg_s[gi, d, rows, :]], axis=0)))
        for c in chains:
            c["ws"] = _dot(c["wq"], bd(c["s"].astype(BF16)))
        for c in chains:
            c["ub"] = (c["u0"] - c["ws"][:CHUNK]).astype(BF16)
        for c in chains:
            both = _dot(jnp.concatenate([c["qkm"], c["kg"]], axis=0), bd(c["ub"]))
            c["o"] = c["ws"][CHUNK:] + both[:CHUNK]
            c["s_new"] = c["s"] * c["decay"] + both[CHUNK:]
        for c in chains:
            s_s[c["gi"], c["bi"], c["d"]] = c["s_new"]
            if c["d"] == 0:
                of_s[c["gi"], c["rows"], :] = c["o"]
            else:
                ob_s[c["gi"], c["rows"], :] = c["o"]
        return carry

    def epilogue(bi, carry):
        rows_b = pl.ds(pl.multiple_of(bi * L, L), L)
        for gi in range(N_GRP):
            lanes = slice(gi * W, (gi + 1) * W)
            o = of_s[gi, rows_b, :] + ob_s[gi, rows_b, :]
            ms = head_sums(o * o) * (1.0 / DK_A)
            o = o * lax.rsqrt(ms + EPS) * dn_ref[...]
            o_ref[bi, :, lanes] = o * _silu(z_ref[bi, :, lanes])
        return carry

    @pl.when(grp == N_GRP - 1)
    def _():
        lax.fori_loop(0, n_chunks, scan_body, 0)
        lax.fori_loop(0, nb, epilogue, 0)
        for gi in range(N_GRP):
            for bi in range(nb):
                for d in range(2):
                    for h in range(4):
                        st_ref[bi, d, gi * 4 + h] = s_s[gi, bi, d, :, h * DK_A:(h + 1) * DK_A]


def _delta(qkv, z, gcs, beta, dnorm, s0):
    b, l, _ = qkv.shape
    has_state = s0 is not None
    w = GROUP_W
    n_grp = A_W // w
    nb = DELTA_ROWS // l
    r = nb * l

    def slab(k):
        return pl.BlockSpec((nb, l, w), lambda i, g: (i, 0, k * n_grp + g))

    narrow = pl.BlockSpec((nb, l, LANES), lambda i, g: (i, 0, 0))
    wide = pl.BlockSpec((nb, l, A_W), lambda i, g: (i, 0, 0))
    in_specs = [slab(0), slab(1), slab(2), wide, narrow, narrow,
                pl.BlockSpec((1, w), lambda i, g: (0, 0))]
    args = [qkv, qkv, qkv, z, gcs, beta, dnorm]
    if has_state:
        in_specs.append(pl.BlockSpec((nb, None, 2, 4, DK_A, DK_A), lambda i, g: (i, 0, 0, g, 0, 0)))
        args.append(s0)
    per_grp = lambda shape, dt: pltpu.VMEM((n_grp,) + shape, dt)
    scratch = ([pltpu.VMEM((r, w), F32)] * 3 + [pltpu.VMEM((2, r, w), F32)] * 2
               + [per_grp((2, r, w), F32), per_grp((2, r, w), BF16), per_grp((2, r // CHUNK * 8, w), F32)]
               + [per_grp((2, r, w), BF16)] * 3 + [per_grp((r, w), F32)] * 2
               + [per_grp((nb, 2, DK_A, w), F32)])
    return pl.pallas_call(
        functools.partial(_delta_kernel, nb=nb, length=l, has_state=has_state),
        grid=(b // nb, n_grp),
        in_specs=in_specs,
        out_specs=[wide, pl.BlockSpec((nb, None, 2, H_A, DK_A, DK_A), lambda i, g: (i, 0, 0, 0, 0, 0))],
        out_shape=[jax.ShapeDtypeStruct((b, l, A_W), F32),
                   jax.ShapeDtypeStruct((b, 1, 2, H_A, DK_A, DK_A), F32)],
        scratch_shapes=scratch,
        compiler_params=_cparams(("arbitrary", "arbitrary")),
        name="delta_state" if has_state else "delta",
    )(*args)


def _pad_lanes(x, width):
    return jnp.pad(x, [(0, 0)] * (x.ndim - 1) + [(0, width - x.shape[-1])])


def kernel(x_prompt, x_sample, cache_diff_k, cache_diff_v, state_delta, c, c_ctx, w_ada, b_ada, norm_ffn1,
           w_ffn1_in, w_ffn1_out, norm_mix, w_in, conv_w, a_log, dt_bias, delta_norm, lambda_q1, lambda_k1,
           lambda_q2, lambda_k2, diff_norm, w_out, norm_ffn2, w_ffn2_in, w_ffn2_out, norm_final):
    bp, lp, _ = x_prompt.shape
    bs, ls, _ = x_sample.shape
    past = cache_diff_k.shape[2]

    w_mix_out = w_out[0].astype(BF16)
    wi = w_in[0]
    a_end = 4 * A_W
    dec_w = wi[:, a_end:a_end + 2 * H_A]
    bet_w = wi[:, a_end + 2 * H_A:a_end + 4 * H_A]
    rest = wi[:, a_end + 4 * H_A:]
    w_pack = jnp.concatenate([wi[:, :a_end], rest, _pad_lanes(dec_w, LANES), _pad_lanes(bet_w, LANES)],
                             axis=1).astype(BF16)
    g1 = norm_ffn1[0].reshape(1, D_MODEL)
    gm = norm_mix[0].reshape(1, D_MODEL)
    g2 = norm_ffn2[0].reshape(1, D_MODEL)
    gf = norm_final.reshape(1, D_MODEL)
    gpar = jnp.pad(jnp.stack([a_log[0].reshape(-1), dt_bias[0].reshape(-1)]), ((0, 6), (0, LANES - 2 * H_A)))
    dnorm = jnp.tile(delta_norm[0], 4).reshape(1, GROUP_W)
    lam_pack = jnp.pad(jnp.concatenate([lambda_q1, lambda_k1, lambda_q2, lambda_k2], axis=0),
                       ((0, 4), (0, LANES - DQK_B)))
    dgain = diff_norm[0].reshape(1, DV_B)
    cos, sin = _rope_tables(ls)

    cond = jnp.pad(jnp.concatenate([c_ctx[None, :], c], axis=0), ((0, 8 - 1 - bs), (0, 0)))
    mod = _adaln(cond, w_ada[0], b_ada[0]).reshape(8, N_MOD, D_MODEL)

    def trunk(x3, rows_per_cond, ctx, w1, w2):
        b, l, _ = x3.shape
        x = x3.reshape(b * l, D_MODEL)
        x1, *w1 = _ffn1(x, mod, g1, *w1, rows_per_cond)
        outs = _inproj(x1, mod, gm, w_pack, conv_w[0], gpar, rows_per_cond, l, ctx is None)
        qkv, z, bq, bk, bv, gcs, beta = outs[:7]
        r3 = lambda a: a.reshape(b, l, a.shape[-1])
        s0 = None if ctx is None else ctx[2]
        oa, st = _delta(r3(qkv), r3(z), r3(gcs), r3(beta), dnorm, s0)
        if ctx is None:
            ob = _attn_ctx(r3(bq), r3(bk), r3(bv), lam_pack, dgain)
        else:
            ob = _attn_lat(r3(bq), r3(bk), r3(bv), ctx[0], ctx[1], cos, sin, lam_pack, dgain)
        y, *w2 = _post(x1, oa.reshape(b * l, A_W), ob.reshape(b * l, B_W), mod, g2, gf, w_mix_out, *w2,
                       rows_per_cond)
        return y.reshape(b, l, D_MODEL), outs[7:], st, w1, w2

    y_prompt, (k_maps, v_heads), s_ctx, w1, w2 = trunk(x_prompt, None, None, (w_ffn1_in[0], w_ffn1_out[0]),
                                                       (w_ffn2_in[0], w_ffn2_out[0]))
    ctx = (cache_diff_k[:, 0].reshape(bs, past, B_W), cache_diff_v[:, 0].reshape(bs, past, B_W), state_delta)
    y_sample = trunk(x_sample, ls, ctx, w1, w2)[0]
    new_k = k_maps.reshape(bp, 1, lp, H_B, 2, DQK_B)
    new_v = v_heads.reshape(bp, 1, lp, H_B, DV_B)
    return (y_prompt, y_sample, new_k, new_v, s_ctx)
```

```python
import functools
import math

import jax
import jax.numpy as jnp
import numpy as np
from jax import lax
from jax.experimental import pallas as pl
from jax.experimental.pallas import tpu as pltpu

F32 = jnp.float32
BF16 = jnp.bfloat16

D_MODEL = 1024
D_FF = 2816
N_MOD = 9
H_A = 8
DK_A = 64
A_W = H_A * DK_A
H_B = 4
DQK_B = 64
DV_B = 128
B_W = H_B * DV_B
CONV_K = 5
CHUNK = 64
GRID_W = 64
ROPE_THETA = 10000.0
ROPE_PAIRS = DQK_B // 4
EPS = 1e-6
LAM_INIT = 0.8 - 0.6 * math.exp(-0.3 * 0)

LANES = 128
GROUP_W = 4 * DK_A
GATE_COL = 4 * A_W
DIFF_COL = GATE_COL + 4 * H_A
IN_COLS = DIFF_COL + 3 * B_W
VMEM_LIMIT = 56 * 1024 * 1024

MXU_W = 256
TOKEN_TILE = 512
FF_CHUNKS = ((0, 6 * MXU_W), (6 * MXU_W, D_FF))
Q_TILE = 256


def _cparams(sem):
    return pltpu.CompilerParams(dimension_semantics=sem, vmem_limit_bytes=VMEM_LIMIT)


def _resident(shape):
    nd = len(shape)
    return pl.BlockSpec(shape, lambda *_: (0,) * nd, pipeline_mode=pl.Buffered(1))


def _silu(x):
    half = 0.5 * x
    return half * jnp.tanh(half) + half


def _mod_norm(x, gain, shift, scale):
    y = x * lax.rsqrt(jnp.mean(x * x, axis=-1, keepdims=True) + EPS)
    return (y * gain) * (1.0 + scale) + shift


def _dot(a, b):
    return jnp.dot(a, b, preferred_element_type=F32)


def _adaln_kernel(c_ref, w_ref, b_ref, o_ref):
    s = _silu(c_ref[...])
    o_ref[...] = _dot(s.astype(BF16), w_ref[...].astype(BF16)) + b_ref[...]


def _adaln(cond, w_ada, b_ada):
    n = N_MOD * D_MODEL
    tn = n // 4
    return pl.pallas_call(
        _adaln_kernel,
        grid=(n // tn,),
        in_specs=[pl.BlockSpec((8, D_MODEL), lambda j: (0, 0)),
                  pl.BlockSpec((D_MODEL, tn), lambda j: (0, j)),
                  pl.BlockSpec((1, tn), lambda j: (0, j))],
        out_specs=pl.BlockSpec((8, tn), lambda j: (0, j)),
        out_shape=jax.ShapeDtypeStruct((8, n), F32),
        compiler_params=_cparams(("arbitrary",)),
        name="adaln",
    )(cond, w_ada, b_ada.reshape(1, n))


def _swiglu_update(x, mod_ref, k0, gain, w_in_ref, w_out_ref):
    h = _mod_norm(x, gain, mod_ref[k0:k0 + 1, :], mod_ref[k0 + 1:k0 + 2, :]).astype(BF16)
    acc = None
    for lo, hi in FF_CHUNKS:
        g = _dot(h, w_in_ref[:, lo:hi])
        u = _dot(h, w_in_ref[:, D_FF + lo:D_FF + hi])
        part = _dot((_silu(g) * u).astype(BF16), w_out_ref[lo:hi, :])
        acc = part if acc is None else acc + part
    return x + (0.5 * mod_ref[k0 + 2:k0 + 3, :]) * acc


W_CHUNKS = 11
W_IN_CHUNK = 2 * D_FF // W_CHUNKS
W_OUT_CHUNK = D_FF // W_CHUNKS


def _stage_ffn_weights(step, w_in_ref, w_out_ref, w_in_bf_ref, w_out_bf_ref, w_in_s, w_out_s):
    for c in range(W_CHUNKS):
        @pl.when(step == c)
        def _():
            a = w_in_ref[...].astype(BF16)
            b = w_out_ref[...].astype(BF16)
            w_in_bf_ref[...] = a
            w_out_bf_ref[...] = b
            w_in_s[:, c * W_IN_CHUNK:(c + 1) * W_IN_CHUNK] = a
            w_out_s[c * W_OUT_CHUNK:(c + 1) * W_OUT_CHUNK, :] = b


def _ffn_weight_plumbing(cast):
    if not cast:
        return [_resident((D_MODEL, 2 * D_FF)), _resident((D_FF, D_MODEL))], [], [], []
    last = W_CHUNKS - 1
    w_in_chunk = pl.BlockSpec((D_MODEL, W_IN_CHUNK), lambda s: (0, jnp.minimum(s, last)))
    w_out_chunk = pl.BlockSpec((W_OUT_CHUNK, D_MODEL), lambda s: (jnp.minimum(s, last), 0))
    out_shape = [jax.ShapeDtypeStruct((D_MODEL, 2 * D_FF), BF16), jax.ShapeDtypeStruct((D_FF, D_MODEL), BF16)]
    scratch = [pltpu.VMEM((D_MODEL, 2 * D_FF), BF16), pltpu.VMEM((D_FF, D_MODEL), BF16)]
    return [w_in_chunk, w_out_chunk], [w_in_chunk, w_out_chunk], out_shape, scratch


def _tile_of(cast):
    off = W_CHUNKS if cast else 0
    return lambda s: jnp.maximum(s - off, 0)


def _mod_spec(rows_per_cond, tm, tile):
    if rows_per_cond is None:
        return pl.BlockSpec((None, N_MOD, D_MODEL), lambda s: (0, 0, 0))
    return pl.BlockSpec((None, N_MOD, D_MODEL), lambda s: (1 + (tile(s) * tm) // rows_per_cond, 0, 0))


def _ffn1_kernel(*refs, cast):
    x_ref, mod_ref, gain_ref, w_in_ref, w_out_ref, o_ref = refs[:6]
    if cast:
        step = pl.program_id(0)
        w_in_s, w_out_s = refs[8:10]
        _stage_ffn_weights(step, w_in_ref, w_out_ref, refs[6], refs[7], w_in_s, w_out_s)

        @pl.when(step >= W_CHUNKS)
        def _():
            o_ref[...] = _swiglu_update(x_ref[...], mod_ref, 0, gain_ref[...], w_in_s, w_out_s)
    else:
        o_ref[...] = _swiglu_update(x_ref[...], mod_ref, 0, gain_ref[...], w_in_ref, w_out_ref)


def _ffn1(x, mod, gain, w_in, w_out, rows_per_cond):
    t = x.shape[0]
    tm = TOKEN_TILE
    cast = w_in.dtype == F32
    tile = _tile_of(cast)
    row = pl.BlockSpec((tm, D_MODEL), lambda s: (tile(s), 0))
    w_in_specs, w_out_specs, w_shapes, w_scratch = _ffn_weight_plumbing(cast)
    outs = pl.pallas_call(
        functools.partial(_ffn1_kernel, cast=cast),
        grid=(t // tm + (W_CHUNKS if cast else 0),),
        in_specs=[row, _mod_spec(rows_per_cond, tm, tile), _resident((1, D_MODEL))] + w_in_specs,
        out_specs=[row] + w_out_specs,
        out_shape=[jax.ShapeDtypeStruct((t, D_MODEL), F32)] + w_shapes,
        scratch_shapes=w_scratch,
        compiler_params=_cparams(("arbitrary",)),
        name="ffn1",
    )(x, mod, gain, w_in, w_out)
    return (outs[0], outs[1], outs[2]) if cast else (outs[0], w_in, w_out)


_IN_SEGS = (("qkv", 0, 3 * A_W), ("z", 3 * A_W, A_W), ("bq", 0, B_W), ("bk", B_W, B_W), ("bv", 2 * B_W, B_W),
            ("dec", GATE_COL, LANES), ("bet", GATE_COL, LANES))


HALO = 8
CONV_SLAB = 256


def _inproj_kernel(*refs, tm, seq_len, emit_maps):
    x_ref, xp_ref, xn_ref, mod_ref, gain_ref, w_ref, cw_ref, gpar_ref = refs[:8]
    qkv_ref, z_ref, bq_ref, bk_ref, bv_ref, gcs_ref, beta_ref = refs[8:15]
    hs, xpad, ybuf = refs[-3:]
    wdiff_s = refs[17 if emit_maps else 15]

    @pl.when(pl.program_id(0) == 0)
    def _():
        for r in range(0, D_MODEL, LANES):
            wdiff_s[r:r + LANES, :] = w_ref[r:r + LANES, DIFF_COL:DIFF_COL + 3 * B_W]

    shift, scale = mod_ref[3:4, :], mod_ref[4:5, :]
    hf = _mod_norm(x_ref[...], gain_ref[...], shift, scale)
    h = hf.astype(BF16)
    segs = dict((name, (start, width)) for name, start, width in _IN_SEGS)

    def proj(name):
        start, width = segs[name]
        src = wdiff_s if name in ("bq", "bk", "bv") else w_ref
        return _dot(h, src[:, start:start + width])

    seg = min(tm, seq_len)
    n_seg = tm // seg
    ext = seg + 2 * HALO
    m = ext // 8
    pad = (CONV_K // 2) * 8
    qkv_w = 3 * A_W
    units = [(s, c) for s in range(n_seg) for c in range(qkv_w // CONV_SLAB)]
    def put_rows(lo, hi, val):
        for ct in range(D_MODEL // LANES):
            hs[ct, lo:hi, :] = val[:, ct * LANES:(ct + 1) * LANES]

    if seq_len > tm:
        tiles = seq_len // tm
        pos = pl.program_id(0) % tiles
        put_rows(0, HALO, _mod_norm(xp_ref[...], gain_ref[...], shift, scale) * jnp.where(pos > 0, 1.0, 0.0))
        put_rows(HALO + seg, ext,
                 _mod_norm(xn_ref[...], gain_ref[...], shift, scale) * jnp.where(pos < tiles - 1, 1.0, 0.0))
    else:
        for s in range(n_seg):
            put_rows(s * ext, s * ext + HALO, jnp.zeros((HALO, D_MODEL), F32))
            put_rows(s * ext + HALO + seg, (s + 1) * ext, jnp.zeros((HALO, D_MODEL), F32))
    for s in range(n_seg):
        put_rows(s * ext + HALO, s * ext + HALO + seg, hf[s * seg:(s + 1) * seg])
    h_il = [jnp.concatenate(
        [jnp.concatenate([hs[ct, pl.ds(s * ext + v, 8, stride=m), :] for v in range(m)], axis=0)
         for ct in range(D_MODEL // LANES)], axis=1).astype(BF16) for s in range(n_seg)]

    def project(s, c):
        cs = slice(c * CONV_SLAB, (c + 1) * CONV_SLAB)
        base = s * (ext + 2 * pad)
        res = _dot(h_il[s], w_ref[:, cs])
        xpad[base + pad:base + pad + ext, cs] = res
        for j in range(CONV_K // 2):
            xpad[base + 8 * j:base + 8 * (j + 1), cs] = pltpu.roll(res[ext - pad + 8 * j:ext - pad + 8 * (j + 1)], 1, 0)
            xpad[base + pad + ext + 8 * j:base + pad + ext + 8 * (j + 1), cs] = pltpu.roll(res[8 * j:8 * (j + 1)], 7, 0)

    def convolve(s, c):
        cs = slice(c * CONV_SLAB, (c + 1) * CONV_SLAB)
        base = s * (ext + 2 * pad)
        acc = None
        for tap in range(CONV_K):
            term = xpad[base + 8 * tap:base + 8 * tap + ext, cs] * cw_ref[tap:tap + 1, cs]
            acc = term if acc is None else acc + term
        y = _silu(acc)
        for j in range(CONV_SLAB // LANES):
            ct = c * (CONV_SLAB // LANES) + j
            for v in range(m):
                ybuf[ct, pl.ds(v, 8, stride=m), :] = y[8 * v:8 * (v + 1), j * LANES:(j + 1) * LANES]
            qkv_ref[s * seg:(s + 1) * seg, ct * LANES:(ct + 1) * LANES] = ybuf[ct, HALO:HALO + seg, :]

    z_ref[...] = proj("z")
    bq_ref[...] = proj("bq").astype(bq_ref.dtype)
    zk = proj("bk")
    zv = proj("bv")
    bk_ref[...] = zk.astype(bk_ref.dtype)
    bv_ref[...] = zv.astype(bv_ref.dtype)
    if emit_maps:
        bk8_ref, bv4_ref = refs[15:17]
        maps_s = refs[-4]
        pitch = tm + 8
        for c in range(H_B):
            pair = zk[:, c * LANES:(c + 1) * LANES]
            maps_s[2 * c * pitch:2 * c * pitch + tm, :] = pair
            maps_s[(2 * c + 1) * pitch:(2 * c + 1) * pitch + tm, :] = pltpu.roll(pair, DQK_B, axis=1)
            maps_s[(2 * H_B + c) * pitch:(2 * H_B + c) * pitch + tm, :] = zv[:, c * DV_B:(c + 1) * DV_B]

        for t in range(tm):
            bk8_ref[t] = maps_s[pl.ds(t, 2 * H_B, stride=pitch), :][:, 0:DQK_B]
            bv4_ref[t] = maps_s[pl.ds(2 * H_B * pitch + t, H_B, stride=pitch), :]
    for unit in units:
        project(*unit)
        convolve(*unit)

    gpar = gpar_ref[...]
    gate_pre = proj("dec")
    xg = gate_pre + gpar[1:2, :]
    softplus = jnp.maximum(xg, 0.0) + jnp.log1p(jnp.exp(-jnp.abs(xg)))
    lane = lax.broadcasted_iota(jnp.int32, (tm, LANES), 1)
    g = jnp.where(lane < 2 * H_A, -jnp.exp(gpar[0:1, :]) * softplus, 0.0)
    beta_ref[...] = jax.nn.sigmoid(pltpu.roll(gate_pre, LANES - 2 * H_A, axis=1))
    pos_c = lax.broadcasted_iota(jnp.int32, (tm, 1), 0) & (CHUNK - 1)
    gcf = g
    gcb = g
    for sft in (1, 2, 4, 8, 16, 32):
        gcf = gcf + jnp.where(pos_c >= sft, pltpu.roll(gcf, sft, axis=0), 0.0)
        gcb = gcb + jnp.where(pos_c < CHUNK - sft, pltpu.roll(gcb, tm - sft, axis=0), 0.0)
    gcs_ref[...] = jnp.where(lane < H_A, gcf, gcb)


def _inproj(x, mod, gain, w_mix_in, conv_w, gpar, rows_per_cond, seq_len, emit_maps):
    t = x.shape[0]
    tm = TOKEN_TILE
    hb = tm // HALO
    n_seg = max(1, tm // seq_len)
    ext = tm // n_seg + 2 * HALO
    row = pl.BlockSpec((tm, D_MODEL), lambda i: (i, 0))
    prev = pl.BlockSpec((HALO, D_MODEL), lambda i: (jnp.maximum(i * hb - 1, 0), 0))
    nxt = pl.BlockSpec((HALO, D_MODEL), lambda i: (jnp.minimum((i + 1) * hb, t // HALO - 1), 0))
    widths = [w for _, _, w in _IN_SEGS]
    out_specs = [pl.BlockSpec((tm, w), lambda i: (i, 0)) for w in widths]
    attn_dtype = BF16 if emit_maps else F32
    out_shape = [jax.ShapeDtypeStruct((t, w), attn_dtype if name in ("bq", "bk", "bv") else F32)
                 for name, _, w in _IN_SEGS]
    if emit_maps:
        out_specs += [pl.BlockSpec((tm, 2 * H_B, DQK_B), lambda i: (i, 0, 0)),
                      pl.BlockSpec((tm, H_B, DV_B), lambda i: (i, 0, 0))]
        out_shape += [jax.ShapeDtypeStruct((t, 2 * H_B, DQK_B), F32), jax.ShapeDtypeStruct((t, H_B, DV_B), F32)]
    return pl.pallas_call(
        functools.partial(_inproj_kernel, tm=tm, seq_len=seq_len, emit_maps=emit_maps),
        grid=(t // tm,),
        in_specs=[row, prev, nxt, _mod_spec(rows_per_cond, tm, lambda s: s), _resident((1, D_MODEL)),
                  _resident((D_MODEL, IN_COLS)), _resident((CONV_K, 3 * A_W)), _resident((8, LANES))],
        out_specs=out_specs,
        out_shape=out_shape,
        scratch_shapes=[pltpu.VMEM((D_MODEL, 3 * B_W), BF16)]
        + ([pltpu.VMEM((3 * H_B * (tm + 8), LANES), F32)] if emit_maps else [])
        + [pltpu.VMEM((D_MODEL // LANES, n_seg * ext, LANES), F32),
                        pltpu.VMEM((n_seg * (ext + 2 * (CONV_K // 2) * 8), 3 * A_W), F32),
                        pltpu.VMEM((3 * A_W // LANES, ext, LANES), F32)],
        compiler_params=_cparams(("arbitrary",)),
        name="inproj",
    )(x, x, x, mod, gain, w_mix_in, conv_w, gpar)


def _post_kernel(*refs, cast):
    x_ref, oa_ref, ob_ref, mod_ref, gain_ref, gfin_ref, wmix_ref, w_in_ref, w_out_ref, o_ref = refs[:10]

    def tokens(w_in, w_out):
        y = _dot(oa_ref[...].astype(BF16), wmix_ref[0:A_W, :]) + _dot(ob_ref[...].astype(BF16), wmix_ref[A_W:, :])
        x = x_ref[...] + mod_ref[5:6, :] * y
        x = _swiglu_update(x, mod_ref, 6, gain_ref[...], w_in, w_out)
        o_ref[...] = (x * lax.rsqrt(jnp.mean(x * x, axis=-1, keepdims=True) + EPS)) * gfin_ref[...]

    if cast:
        step = pl.program_id(0)
        w_in_s, w_out_s = refs[12:14]
        _stage_ffn_weights(step, w_in_ref, w_out_ref, refs[10], refs[11], w_in_s, w_out_s)
        pl.when(step >= W_CHUNKS)(lambda: tokens(w_in_s, w_out_s))
    else:
        tokens(w_in_ref, w_out_ref)


def _post(x, oa, ob, mod, gain, gfin, w_mix, w_in, w_out, rows_per_cond):
    t = x.shape[0]
    tm = TOKEN_TILE
    cast = w_in.dtype == F32
    tile = _tile_of(cast)
    row = pl.BlockSpec((tm, D_MODEL), lambda s: (tile(s), 0))
    half = pl.BlockSpec((tm, A_W), lambda s: (tile(s), 0))
    w_in_specs, w_out_specs, w_shapes, w_scratch = _ffn_weight_plumbing(cast)
    outs = pl.pallas_call(
        functools.partial(_post_kernel, cast=cast),
        grid=(t // tm + (W_CHUNKS if cast else 0),),
        in_specs=[row, half, half, _mod_spec(rows_per_cond, tm, tile), _resident((1, D_MODEL)),
                  _resident((1, D_MODEL)), _resident((A_W + B_W, D_MODEL))] + w_in_specs,
        out_specs=[row] + w_out_specs,
        out_shape=[jax.ShapeDtypeStruct((t, D_MODEL), F32)] + w_shapes,
        scratch_shapes=w_scratch,
        compiler_params=_cparams(("arbitrary",)),
        name="post",
    )(x, oa, ob, mod, gain, gfin, w_mix, w_in, w_out)
    return (outs[0], outs[1], outs[2]) if cast else (outs[0], w_in, w_out)


def _rope(x, cos, sin_signed):
    w = x.shape[-1]
    lane = lax.broadcasted_iota(jnp.int32, x.shape, 1)
    p = ROPE_PAIRS
    partner = jnp.where((lane & (2 * p - 1)) < p, pltpu.roll(x, w - p, axis=1), pltpu.roll(x, p, axis=1))
    return x * cos + partner * sin_signed


def _store_values(v_s, rows, v):
    for h in range(H_B):
        vh = v[:, h * DV_B:(h + 1) * DV_B]
        v_s[rows, 2 * h * DV_B:(2 * h + 1) * DV_B] = vh.astype(BF16)
        v_s[rows, (2 * h + 1) * DV_B:(2 * h + 2) * DV_B] = jnp.ones(vh.shape, BF16)


def _attn_core(q, k_ref, v_ref, lam, gain_ref, o_ref):
    tq = q.shape[0]
    lane = lax.broadcasted_iota(jnp.int32, (tq, DV_B), 1)
    qs = q * (DQK_B ** -0.5)
    heads = [slice(h * DV_B, (h + 1) * DV_B) for h in range(H_B)]
    scores = []
    for hs in heads:
        qh = qs[:, hs]
        q2 = jnp.concatenate([jnp.where(lane < DQK_B, qh, 0.0), jnp.where(lane >= DQK_B, qh, 0.0)], axis=0)
        scores.append(lax.dot_general(q2.astype(BF16), k_ref[:, hs], (((1,), (1,)), ((), ())),
                                      preferred_element_type=F32))
    for h, (hs, s) in enumerate(zip(heads, scores)):
        e = jnp.exp(s - jnp.max(s, axis=-1, keepdims=True))
        rs = _dot(e.astype(BF16), v_ref[:, 2 * h * DV_B:(2 * h + 2) * DV_B])
        r = rs[:, :DV_B] / rs[:, DV_B:]
        o = r[:tq] - lam * r[tq:]
        o = o * lax.rsqrt(jnp.mean(o * o, axis=-1, keepdims=True) + EPS)
        o_ref[:, hs] = (o * gain_ref[...]) * (1.0 - LAM_INIT)


def _lambda(lam_ref):
    l = lam_ref[...]
    s1 = jnp.sum(l[0:1] * l[1:2], axis=-1, keepdims=True)
    s2 = jnp.sum(l[2:3] * l[3:4], axis=-1, keepdims=True)
    return jnp.exp(s1) - jnp.exp(s2) + LAM_INIT


CTX_SEQS = 4


def _attn_ctx_kernel(q_ref, k_ref, v_ref, lam_ref, gain_ref, o_ref, v_s):
    lam = _lambda(lam_ref)

    def one_sequence(bi, carry):
        _store_values(v_s, slice(None), v_ref[bi])
        _attn_core(q_ref[bi], k_ref.at[bi], v_s, lam, gain_ref, o_ref.at[bi])
        return carry

    lax.fori_loop(0, CTX_SEQS, one_sequence, 0)


def _attn_ctx(q, k, v, lam_pack, gain):
    b, l, _ = q.shape
    blk = pl.BlockSpec((CTX_SEQS, l, B_W), lambda i: (i, 0, 0))
    return pl.pallas_call(
        _attn_ctx_kernel,
        grid=(b // CTX_SEQS,),
        in_specs=[blk, blk, blk, _resident((8, LANES)), _resident((1, DV_B))],
        out_specs=blk,
        out_shape=jax.ShapeDtypeStruct((b, l, B_W), F32),
        scratch_shapes=[pltpu.VMEM((l, 2 * B_W), BF16)],
        compiler_params=_cparams(("arbitrary",)),
        name="attn_ctx",
    )(q, k, v, lam_pack, gain)


def _attn_lat_kernel(q_ref, k_ref, v_ref, ck_ref, cv_ref, cosq_ref, sinq_ref, cos_ref, sin_ref,
                     lam_ref, gain_ref, o_ref, k_s, v_s, *, past):
    @pl.when(pl.program_id(1) == 0)
    def _():
        k_s[0:past, :] = ck_ref[...].astype(BF16)
        _store_values(v_s, slice(0, past), cv_ref[...])
        k_s[past:, :] = _rope(k_ref[...], cos_ref[...], sin_ref[...]).astype(BF16)
        _store_values(v_s, slice(past, None), v_ref[...])

    q = _rope(q_ref[...], cosq_ref[...], sinq_ref[...])
    _attn_core(q, k_s, v_s, _lambda(lam_ref), gain_ref, o_ref)


def _attn_lat(q, k, v, ck, cv, cos, sin, lam_pack, gain):
    b, l, _ = q.shape
    past = ck.shape[1]
    tq = Q_TILE
    full = pl.BlockSpec((None, l, B_W), lambda i, j: (i, 0, 0))
    cache_k = cache_v = pl.BlockSpec((None, past, B_W), lambda i, j: (i, 0, 0))
    qblk = pl.BlockSpec((None, tq, B_W), lambda i, j: (i, j, 0))
    tab_q = pl.BlockSpec((tq, B_W), lambda i, j: (j, 0))
    tab = pl.BlockSpec((l, B_W), lambda i, j: (0, 0))
    return pl.pallas_call(
        functools.partial(_attn_lat_kernel, past=past),
        grid=(b, l // tq),
        in_specs=[qblk, full, full, cache_k, cache_v, tab_q, tab_q, tab, tab,
                  pl.BlockSpec((8, LANES), lambda i, j: (0, 0)), pl.BlockSpec((1, DV_B), lambda i, j: (0, 0))],
        out_specs=qblk,
        out_shape=jax.ShapeDtypeStruct((b, l, B_W), F32),
        scratch_shapes=[pltpu.VMEM((past + l, B_W), BF16), pltpu.VMEM((past + l, 2 * B_W), BF16)],
        compiler_params=_cparams(("arbitrary", "arbitrary")),
        name="attn_lat",
    )(q, k, v, ck, cv, cos, sin, cos, sin, lam_pack, gain)


def _rope_tables(length):
    pairs = ROPE_PAIRS
    n_rows = length // GRID_W
    pos_row = np.repeat(np.arange(n_rows), GRID_W).astype(np.float32)
    pos_col = np.tile(np.arange(GRID_W), n_rows).astype(np.float32)
    inv = (np.float32(ROPE_THETA) ** (-np.arange(pairs, dtype=np.float32) / np.float32(pairs))).astype(np.float32)
    ang_r = pos_row[:, None] * inv
    ang_c = pos_col[:, None] * inv
    cos = np.concatenate([np.cos(ang_r), np.cos(ang_r), np.cos(ang_c), np.cos(ang_c)], axis=-1)
    sin = np.concatenate([-np.sin(ang_r), np.sin(ang_r), -np.sin(ang_c), np.sin(ang_c)], axis=-1)
    reps = B_W // DQK_B
    return jnp.asarray(np.tile(cos, (1, reps)), F32), jnp.asarray(np.tile(sin, (1, reps)), F32)


def _block_diag(x):
    lo, hi = x[:, :LANES], x[:, LANES:]
    first = lax.broadcasted_iota(jnp.int32, lo.shape, 1) < DK_A
    top = jnp.concatenate([jnp.where(first, lo, 0), jnp.where(first, 0, lo)], axis=0)
    bot = jnp.concatenate([jnp.where(first, hi, 0), jnp.where(first, 0, hi)], axis=0)
    zero = jnp.zeros_like(top)
    return jnp.concatenate([jnp.concatenate([top, zero], axis=1), jnp.concatenate([zero, bot], axis=1)], axis=0)


SOLVE_CHUNKS = 8
DELTA_ROWS = 1024


def _delta_kernel(*refs, nb, length, has_state):
    if has_state:
        (q_ref, k_ref, v_ref, z_ref, gcs_ref, beta_ref, dn_ref, s0_ref, o_ref, st_ref, *scr) = refs
    else:
        (q_ref, k_ref, v_ref, z_ref, gcs_ref, beta_ref, dn_ref, o_ref, st_ref, *scr) = refs
        s0_ref = None
    qs, ks, vs, gexp, bexp, u0_s, kg_s, dec_s, w_s, qg_s, qkm_s, of_s, ob_s, s_s = scr
    L = length
    R = nb * L
    n_chunks = L // CHUNK
    grp = pl.program_id(1)
    W = GROUP_W
    N_GRP = A_W // W

    r256 = lax.broadcasted_iota(jnp.int32, (W, W), 0)
    c256 = lax.broadcasted_iota(jnp.int32, (W, W), 1)
    dk_bits, w_bits = DK_A.bit_length() - 1, W.bit_length() - 1
    bd_mask = (r256 >> dk_bits) == (c256 >> dk_bits)
    ones_bd = jnp.where(bd_mask, 1.0, 0.0).astype(BF16)
    ri = lax.broadcasted_iota(jnp.int32, (CHUNK, W), 0)
    cj = lax.broadcasted_iota(jnp.int32, (CHUNK, W), 1) & (CHUNK - 1)
    eye_p = ri == cj
    NS = 2 * H_A
    er = lax.broadcasted_iota(jnp.int32, (LANES, 4 * W), 0)
    ec = lax.broadcasted_iota(jnp.int32, (LANES, 4 * W), 1)
    src = ((ec >> w_bits) & 1) * H_A + grp * 4 + ((ec & (W - 1)) >> dk_bits)
    ns_bits = NS.bit_length() - 1
    expand = jnp.where((er < 4 * NS) & ((er & (NS - 1)) == src) & (((er >> ns_bits) & 1) == (ec >> (w_bits + 1))),
                       1.0, 0.0).astype(BF16)
    lane_s = lax.broadcasted_iota(jnp.int32, (L, LANES), 1)

    bd = _block_diag

    def head_sums(y):
        return _dot(y.astype(BF16), ones_bd)

    def prologue(bi, carry):
        rows_b = pl.ds(pl.multiple_of(bi * L, L), L)

        def l2n(y):
            return y * lax.rsqrt(head_sums(y * y) + EPS)

        qs[rows_b, :] = l2n(q_ref[bi]) * (DK_A ** -0.5)
        ks[rows_b, :] = l2n(k_ref[bi])
        vs[rows_b, :] = v_ref[bi]
        gc = gcs_ref[bi]
        bt = jnp.where(lane_s < NS, beta_ref[bi], 0.0)
        gc_hi = gc.astype(BF16).astype(F32)
        bt_hi = bt.astype(BF16).astype(F32)
        packed = (gc_hi + pltpu.roll(bt_hi, NS, axis=1) + pltpu.roll(gc - gc_hi, 2 * NS, axis=1)
                  + pltpu.roll(bt - bt_hi, 3 * NS, axis=1))
        both = _dot(packed.astype(BF16), expand)
        for d in range(2):
            gexp[d, rows_b, :] = both[:, d * W:(d + 1) * W]
            bexp[d, rows_b, :] = both[:, 2 * W + d * W:2 * W + (d + 1) * W]
        return carry

    lax.fori_loop(0, nb, prologue, 0)

    def solve_body(j, carry):
        chunks = []
        for k in range(SOLVE_CHUNKS):
            chunk = j * SOLVE_CHUNKS + k
            rows = pl.ds(pl.multiple_of(chunk * CHUNK, CHUNK), CHUNK)
            kc = ks[rows, :]
            qc = qs[rows, :]
            kb = kc.astype(BF16)
            res = lax.dot_general(jnp.concatenate([qc.astype(BF16), kb], axis=0), bd(kb),
                                  (((1,), (1,)), ((), ())), preferred_element_type=F32)
            qk, kk = res[:CHUNK], res[CHUNK:]
            ch = dict(rows=rows, bd_k=bd(kb), bd_v=bd(vs[rows, :].astype(BF16)), t=[], scale_u=[], scale_w=[])
            nsum = None
            for d in range(2):
                ge = gexp[d, rows, :]
                be = bexp[d, rows, :]
                if d == 0:
                    incl, strict, gl = ri >= cj, ri > cj, ge[CHUNK - 1:CHUNK, :]
                else:
                    incl, strict, gl = ri <= cj, ri < cj, ge[0:1, :]
                g_row = jnp.sum(jnp.where(eye_p, ge, 0.0), axis=0, keepdims=True)
                b_row = jnp.sum(jnp.where(eye_p, be, 0.0), axis=0, keepdims=True)
                dec_m = jnp.exp(jnp.where(incl, ge - g_row, -jnp.inf))
                nmat = jnp.where(strict, be * kk * dec_m, 0.0)
                nsum = nmat if nsum is None else nsum + nmat
                qkm_s[grp, d, rows, :] = jnp.where(incl, qk * dec_m, 0.0).astype(BF16)
                qg_s[grp, d, rows, :] = (qc * jnp.exp(ge)).astype(BF16)
                kgt = (kc * jnp.exp(gl - ge)).T
                kg_s[grp, d, rows, :] = jnp.concatenate(
                    [kgt[h * DK_A:(h + 1) * DK_A, :] for h in range(4)], axis=1).astype(BF16)
                dec_s[grp, d, pl.ds(pl.multiple_of(chunk * 8, 8), 8), :] = jnp.broadcast_to(jnp.exp(gl), (8, W))
                ch["scale_u"].append(b_row)
                ch["scale_w"].append(b_row * jnp.exp(g_row))
                ch["t"].append(jnp.where(eye_p, 1.0, 0.0) - jnp.where((ri >> 1) == (cj >> 1), nmat, 0.0))
            ch["n_both"] = bd(nsum.astype(BF16))
            chunks.append(ch)

        for lb in range(1, 6):
            same = (ri >> (lb + 1)) == (cj >> (lb + 1))
            pair = (same & (((ri >> lb) & 1) == 1) & (((cj >> lb) & 1) == 0),
                    same & (((cj >> lb) & 1) == 1) & (((ri >> lb) & 1) == 0))
            for ch in chunks:
                ch["tb"] = [t.astype(BF16) for t in ch["t"]]
                ch["y"] = _dot(jnp.concatenate(ch["tb"], axis=0), ch["n_both"]).astype(BF16)
            even_rows = ((ri >> lb) & 1) == 0
            for ch in chunks:
                ys = jnp.concatenate([jnp.where(pair[0], ch["y"][:CHUNK], 0),
                                      jnp.where(pair[1], ch["y"][CHUNK:], 0)], axis=0)
                upd = _dot(ys, bd(jnp.where(even_rows, ch["tb"][0], ch["tb"][1])))
                ch["t"] = [ch["t"][d] - upd[d * CHUNK:(d + 1) * CHUNK] for d in range(2)]
        for ch in chunks:
            tu = jnp.concatenate([(ch["t"][d] * ch["scale_u"][d]).astype(BF16) for d in range(2)], axis=0)
            tw = jnp.concatenate([(ch["t"][d] * ch["scale_w"][d]).astype(BF16) for d in range(2)], axis=0)
            u0 = _dot(tu, ch["bd_v"])
            w = _dot(tw, ch["bd_k"]).astype(BF16)
            for d in range(2):
                u0_s[grp, d, ch["rows"], :] = u0[d * CHUNK:(d + 1) * CHUNK]
                w_s[grp, d, ch["rows"], :] = w[d * CHUNK:(d + 1) * CHUNK]
        return carry

    lax.fori_loop(0, R // (SOLVE_CHUNKS * CHUNK), solve_body, 0)

    if has_state:
        for bi in range(nb):
            for d in range(2):
                for h in range(4):
                    s_s[grp, bi, d, :, h * DK_A:(h + 1) * DK_A] = s0_ref[bi, d, h]
    else:
        s_s[grp] = jnp.zeros(s_s.shape[1:], F32)

    def scan_body(i, carry):
        chains = []
        for gi in range(N_GRP):
            for bi in range(nb):
                for d in range(2):
                    c = i if d == 0 else n_chunks - 1 - i
                    rows = pl.ds(pl.multiple_of(bi * L + c * CHUNK, CHUNK), CHUNK)
                    tile = pl.ds(pl.multiple_of((bi * n_chunks + c) * 8, 8), 8)
                    chains.append(dict(
                        gi=gi, bi=bi, d=d, rows=rows, s=s_s[gi, bi, d], u0=u0_s[gi, d, rows, :],
                        kg=kg_s[gi, d, rows, :], qkm=qkm_s[gi, d, rows, :], decay=dec_s[gi, d, tile, :][0:1],
                        wq=jnp.concatenate([w_s[gi, d, rows, :], qg_s[gi, d, rows, :]], axis=0)))
        for c in chains:
            c["ws"] = _dot(c["wq"], bd(c["s"].astype(BF16)))
        for c in chains:
            c["ub"] = (c["u0"] - c["ws"][:CHUNK]).astype(BF16)
        for c in chains:
            both = _dot(jnp.concatenate([c["qkm"], c["kg"]], axis=0), bd(c["ub"]))
            c["o"] = c["ws"][CHUNK:] + both[:CHUNK]
            c["s_new"] = c["s"] * c["decay"] + both[CHUNK:]
        for c in chains:
            s_s[c["gi"], c["bi"], c["d"]] = c["s_new"]
            if c["d"] == 0:
                of_s[c["gi"], c["rows"], :] = c["o"]
            else:
                ob_s[c["gi"], c["rows"], :] = c["o"]
        return carry

    def epilogue(bi, carry):
        rows_b = pl.ds(pl.multiple_of(bi * L, L), L)
        for gi in range(N_GRP):
            lanes = slice(gi * W, (gi + 1) * W)
            o = of_s[gi, rows_b, :] + ob_s[gi, rows_b, :]
            ms = head_sums(o * o) * (1.0 / DK_A)
            o = o * lax.rsqrt(ms + EPS) * dn_ref[...]
            o_ref[bi, :, lanes] = o * _silu(z_ref[bi, :, lanes])
        return carry

    @pl.when(grp == N_GRP - 1)
    def _():
        lax.fori_loop(0, n_chunks, scan_body, 0)
        lax.fori_loop(0, nb, epilogue, 0)
        for gi in range(N_GRP):
            for bi in range(nb):
                for d in range(2):
                    for h in range(4):
                        st_ref[bi, d, gi * 4 + h] = s_s[gi, bi, d, :, h * DK_A:(h + 1) * DK_A]


def _delta(qkv, z, gcs, beta, dnorm, s0):
    b, l, _ = qkv.shape
    has_state = s0 is not None
    w = GROUP_W
    n_grp = A_W // w
    nb = DELTA_ROWS // l
    r = nb * l

    def slab(k):
        return pl.BlockSpec((nb, l, w), lambda i, g: (i, 0, k * n_grp + g))

    narrow = pl.BlockSpec((nb, l, LANES), lambda i, g: (i, 0, 0))
    wide = pl.BlockSpec((nb, l, A_W), lambda i, g: (i, 0, 0))
    in_specs = [slab(0), slab(1), slab(2), wide, narrow, narrow,
                pl.BlockSpec((1, w), lambda i, g: (0, 0))]
    args = [qkv, qkv, qkv, z, gcs, beta, dnorm]
    if has_state:
        in_specs.append(pl.BlockSpec((nb, None, 2, 4, DK_A, DK_A), lambda i, g: (i, 0, 0, g, 0, 0)))
        args.append(s0)
    per_grp = lambda shape, dt: pltpu.VMEM((n_grp,) + shape, dt)
    scratch = ([pltpu.VMEM((r, w), F32)] * 3 + [pltpu.VMEM((2, r, w), F32)] * 2
               + [per_grp((2, r, w), F32), per_grp((2, r, w), BF16), per_grp((2, r // CHUNK * 8, w), F32)]
               + [per_grp((2, r, w), BF16)] * 3 + [per_grp((r, w), F32)] * 2
               + [per_grp((nb, 2, DK_A, w), F32)])
    return pl.pallas_call(
        functools.partial(_delta_kernel, nb=nb, length=l, has_state=has_state),
        grid=(b // nb, n_grp),
        in_specs=in_specs,
        out_specs=[wide, pl.BlockSpec((nb, None, 2, H_A, DK_A, DK_A), lambda i, g: (i, 0, 0, 0, 0, 0))],
        out_shape=[jax.ShapeDtypeStruct((b, l, A_W), F32),
                   jax.ShapeDtypeStruct((b, 1, 2, H_A, DK_A, DK_A), F32)],
        scratch_shapes=scratch,
        compiler_params=_cparams(("arbitrary", "arbitrary")),
        name="delta_state" if has_state else "delta",
    )(*args)


def kernel(x_prompt, x_sample, cache_diff_k, cache_diff_v, state_delta, c, c_ctx, w_ada, b_ada, norm_ffn1,
           w_ffn1_in, w_ffn1_out, norm_mix, w_in, conv_w, a_log, dt_bias, delta_norm, lambda_q1, lambda_k1,
           lambda_q2, lambda_k2, diff_norm, w_out, norm_ffn2, w_ffn2_in, w_ffn2_out, norm_final):
    bp, lp, _ = x_prompt.shape
    bs, ls, _ = x_sample.shape
    past = cache_diff_k.shape[2]

    w_mix_out = w_out[0].astype(BF16)
    w_mix_in = w_in[0].astype(BF16)
    g1 = norm_ffn1[0].reshape(1, D_MODEL)
    gm = norm_mix[0].reshape(1, D_MODEL)
    g2 = norm_ffn2[0].reshape(1, D_MODEL)
    gf = norm_final.reshape(1, D_MODEL)
    gpar = jnp.pad(jnp.stack([a_log[0].reshape(-1), dt_bias[0].reshape(-1)]), ((0, 6), (0, LANES - 2 * H_A)))
    dnorm = jnp.tile(delta_norm[0], 4).reshape(1, GROUP_W)
    lam_pack = jnp.pad(jnp.concatenate([lambda_q1, lambda_k1, lambda_q2, lambda_k2], axis=0),
                       ((0, 4), (0, LANES - DQK_B)))
    dgain = diff_norm[0].reshape(1, DV_B)
    cos, sin = _rope_tables(ls)

    cond = jnp.pad(jnp.concatenate([c_ctx[None, :], c], axis=0), ((0, 8 - 1 - bs), (0, 0)))
    mod = _adaln(cond, w_ada[0], b_ada[0]).reshape(8, N_MOD, D_MODEL)

    def trunk(x3, rows_per_cond, ctx, w1, w2):
        b, l, _ = x3.shape
        x = x3.reshape(b * l, D_MODEL)
        x1, *w1 = _ffn1(x, mod, g1, *w1, rows_per_cond)
        outs = _inproj(x1, mod, gm, w_mix_in, conv_w[0], gpar, rows_per_cond, l, ctx is None)
        qkv, z, bq, bk, bv, gcs, beta = outs[:7]
        r3 = lambda a: a.reshape(b, l, a.shape[-1])
        s0 = None if ctx is None else ctx[2]
        oa, st = _delta(r3(qkv), r3(z), r3(gcs), r3(beta), dnorm, s0)
        if ctx is None:
            ob = _attn_ctx(r3(bq), r3(bk), r3(bv), lam_pack, dgain)
        else:
            ob = _attn_lat(r3(bq), r3(bk), r3(bv), ctx[0], ctx[1], cos, sin, lam_pack, dgain)
        y, *w2 = _post(x1, oa.reshape(b * l, A_W), ob.reshape(b * l, B_W), mod, g2, gf, w_mix_out, *w2,
                       rows_per_cond)
        return y.reshape(b, l, D_MODEL), outs[7:], st, w1, w2

    y_prompt, (k_maps, v_heads), s_ctx, w1, w2 = trunk(x_prompt, None, None, (w_ffn1_in[0], w_ffn1_out[0]),
                                                       (w_ffn2_in[0], w_ffn2_out[0]))
    ctx = (cache_diff_k[:, 0].reshape(bs, past, B_W), cache_diff_v[:, 0].reshape(bs, past, B_W), state_delta)
    y_sample = trunk(x_sample, ls, ctx, w1, w2)[0]
    new_k = k_maps.reshape(bp, 1, lp, H_B, 2, DQK_B)
    new_v = v_heads.reshape(bp, 1, lp, H_B, DV_B)
    return (y_prompt, y_sample, new_k, new_v, s_ctx)
```

```python
import functools
import math

import jax
import jax.numpy as jnp
import numpy as np
from jax import lax
from jax.experimental import pallas as pl
from jax.experimental.pallas import tpu as pltpu

F32 = jnp.float32
BF16 = jnp.bfloat16

D_MODEL = 1024
D_FF = 2816
N_MOD = 9
H_A = 8
DK_A = 64
A_W = H_A * DK_A
H_B = 4
DQK_B = 64
DV_B = 128
B_W = H_B * DV_B
CONV_K = 5
CHUNK = 64
GRID_W = 64
ROPE_THETA = 10000.0
ROPE_PAIRS = DQK_B // 4
EPS = 1e-6
LAM_INIT = 0.8 - 0.6 * math.exp(-0.3 * 0)

LANES = 128
GROUP_W = 4 * DK_A
GATE_COL = 4 * A_W
DIFF_COL = GATE_COL + 4 * H_A
IN_COLS = DIFF_COL + 3 * B_W
VMEM_LIMIT = 56 * 1024 * 1024

MXU_W = 256
TOKEN_TILE = 512
FF_CHUNKS = ((0, 6 * MXU_W), (6 * MXU_W, D_FF))
Q_TILE = 256


def _cparams(sem):
    return pltpu.CompilerParams(dimension_semantics=sem, vmem_limit_bytes=VMEM_LIMIT)


def _resident(shape):
    nd = len(shape)
    return pl.BlockSpec(shape, lambda *_: (0,) * nd, pipeline_mode=pl.Buffered(1))


def _silu(x):
    half = 0.5 * x
    return half * jnp.tanh(half) + half


def _mod_norm(x, gain, shift, scale):
    y = x * lax.rsqrt(jnp.mean(x * x, axis=-1, keepdims=True) + EPS)
    return (y * gain) * (1.0 + scale) + shift


def _dot(a, b):
    return jnp.dot(a, b, preferred_element_type=F32)


def _adaln_kernel(c_ref, w_ref, b_ref, o_ref):
    s = _silu(c_ref[...])
    o_ref[...] = _dot(s.astype(BF16), w_ref[...].astype(BF16)) + b_ref[...]


def _adaln(cond, w_ada, b_ada):
    n = N_MOD * D_MODEL
    tn = n // 4
    return pl.pallas_call(
        _adaln_kernel,
        grid=(n // tn,),
        in_specs=[pl.BlockSpec((8, D_MODEL), lambda j: (0, 0)),
                  pl.BlockSpec((D_MODEL, tn), lambda j: (0, j)),
                  pl.BlockSpec((1, tn), lambda j: (0, j))],
        out_specs=pl.BlockSpec((8, tn), lambda j: (0, j)),
        out_shape=jax.ShapeDtypeStruct((8, n), F32),
        compiler_params=_cparams(("arbitrary",)),
        name="adaln",
    )(cond, w_ada, b_ada.reshape(1, n))


def _swiglu_update(x, mod_ref, k0, gain, w_in_ref, w_out_ref):
    h = _mod_norm(x, gain, mod_ref[k0:k0 + 1, :], mod_ref[k0 + 1:k0 + 2, :]).astype(BF16)
    acc = None
    for lo, hi in FF_CHUNKS:
        g = _dot(h, w_in_ref[:, lo:hi])
        u = _dot(h, w_in_ref[:, D_FF + lo:D_FF + hi])
        part = _dot((_silu(g) * u).astype(BF16), w_out_ref[lo:hi, :])
        acc = part if acc is None else acc + part
    return x + (0.5 * mod_ref[k0 + 2:k0 + 3, :]) * acc


W_CHUNKS = 11
W_IN_CHUNK = 2 * D_FF // W_CHUNKS
W_OUT_CHUNK = D_FF // W_CHUNKS


def _stage_ffn_weights(step, w_in_ref, w_out_ref, w_in_bf_ref, w_out_bf_ref, w_in_s, w_out_s):
    for c in range(W_CHUNKS):
        @pl.when(step == c)
        def _():
            a = w_in_ref[...].astype(BF16)
            b = w_out_ref[...].astype(BF16)
            w_in_bf_ref[...] = a
            w_out_bf_ref[...] = b
            w_in_s[:, c * W_IN_CHUNK:(c + 1) * W_IN_CHUNK] = a
            w_out_s[c * W_OUT_CHUNK:(c + 1) * W_OUT_CHUNK, :] = b


def _ffn_weight_plumbing(cast):
    if not cast:
        return [_resident((D_MODEL, 2 * D_FF)), _resident((D_FF, D_MODEL))], [], [], []
    last = W_CHUNKS - 1
    w_in_chunk = pl.BlockSpec((D_MODEL, W_IN_CHUNK), lambda s: (0, jnp.minimum(s, last)))
    w_out_chunk = pl.BlockSpec((W_OUT_CHUNK, D_MODEL), lambda s: (jnp.minimum(s, last), 0))
    out_shape = [jax.ShapeDtypeStruct((D_MODEL, 2 * D_FF), BF16), jax.ShapeDtypeStruct((D_FF, D_MODEL), BF16)]
    scratch = [pltpu.VMEM((D_MODEL, 2 * D_FF), BF16), pltpu.VMEM((D_FF, D_MODEL), BF16)]
    return [w_in_chunk, w_out_chunk], [w_in_chunk, w_out_chunk], out_shape, scratch


def _tile_of(cast):
    off = W_CHUNKS if cast else 0
    return lambda s: jnp.maximum(s - off, 0)


def _mod_spec(rows_per_cond, tm, tile):
    if rows_per_cond is None:
        return pl.BlockSpec((None, N_MOD, D_MODEL), lambda s: (0, 0, 0))
    return pl.BlockSpec((None, N_MOD, D_MODEL), lambda s: (1 + (tile(s) * tm) // rows_per_cond, 0, 0))


def _ffn1_kernel(*refs, cast):
    x_ref, mod_ref, gain_ref, w_in_ref, w_out_ref, o_ref = refs[:6]
    if cast:
        step = pl.program_id(0)
        w_in_s, w_out_s = refs[8:10]
        _stage_ffn_weights(step, w_in_ref, w_out_ref, refs[6], refs[7], w_in_s, w_out_s)

        @pl.when(step >= W_CHUNKS)
        def _():
            o_ref[...] = _swiglu_update(x_ref[...], mod_ref, 0, gain_ref[...], w_in_s, w_out_s)
    else:
        o_ref[...] = _swiglu_update(x_ref[...], mod_ref, 0, gain_ref[...], w_in_ref, w_out_ref)


def _ffn1(x, mod, gain, w_in, w_out, rows_per_cond):
    t = x.shape[0]
    tm = TOKEN_TILE
    cast = w_in.dtype == F32
    tile = _tile_of(cast)
    row = pl.BlockSpec((tm, D_MODEL), lambda s: (tile(s), 0))
    w_in_specs, w_out_specs, w_shapes, w_scratch = _ffn_weight_plumbing(cast)
    outs = pl.pallas_call(
        functools.partial(_ffn1_kernel, cast=cast),
        grid=(t // tm + (W_CHUNKS if cast else 0),),
        in_specs=[row, _mod_spec(rows_per_cond, tm, tile), _resident((1, D_MODEL))] + w_in_specs,
        out_specs=[row] + w_out_specs,
        out_shape=[jax.ShapeDtypeStruct((t, D_MODEL), F32)] + w_shapes,
        scratch_shapes=w_scratch,
        compiler_params=_cparams(("arbitrary",)),
        name="ffn1",
    )(x, mod, gain, w_in, w_out)
    return (outs[0], outs[1], outs[2]) if cast else (outs[0], w_in, w_out)


_IN_SEGS = (("qkv", 0, 3 * A_W), ("z", 3 * A_W, A_W), ("bq", DIFF_COL, B_W), ("bk", DIFF_COL + B_W, B_W),
            ("bv", DIFF_COL + 2 * B_W, B_W), ("dec", GATE_COL, LANES), ("bet", GATE_COL, LANES))


HALO = 8
CONV_SLAB = 256


def _inproj_kernel(*refs, tm, seq_len, emit_maps):
    x_ref, xp_ref, xn_ref, mod_ref, gain_ref, w_ref, cw_ref, gpar_ref = refs[:8]
    qkv_ref, z_ref, bq_ref, bk_ref, bv_ref, gcs_ref, beta_ref = refs[8:15]
    hs, xpad, ybuf = refs[-3:]
    shift, scale = mod_ref[3:4, :], mod_ref[4:5, :]
    hf = _mod_norm(x_ref[...], gain_ref[...], shift, scale)
    h = hf.astype(BF16)
    segs = dict((name, (start, width)) for name, start, width in _IN_SEGS)

    def dot_wt(lhs, start, width):
        return lax.dot_general(lhs, w_ref[start:start + width, :], (((1,), (1,)), ((), ())),
                               preferred_element_type=F32)

    def proj(name):
        return dot_wt(h, *segs[name])

    seg = min(tm, seq_len)
    n_seg = tm // seg
    ext = seg + 2 * HALO
    m = ext // 8
    pad = (CONV_K // 2) * 8
    qkv_w = 3 * A_W
    units = [(s, c) for s in range(n_seg) for c in range(qkv_w // CONV_SLAB)]
    def put_rows(lo, hi, val):
        for ct in range(D_MODEL // LANES):
            hs[ct, lo:hi, :] = val[:, ct * LANES:(ct + 1) * LANES]

    if seq_len > tm:
        tiles = seq_len // tm
        pos = pl.program_id(0) % tiles
        put_rows(0, HALO, _mod_norm(xp_ref[...], gain_ref[...], shift, scale) * jnp.where(pos > 0, 1.0, 0.0))
        put_rows(HALO + seg, ext,
                 _mod_norm(xn_ref[...], gain_ref[...], shift, scale) * jnp.where(pos < tiles - 1, 1.0, 0.0))
    else:
        for s in range(n_seg):
            put_rows(s * ext, s * ext + HALO, jnp.zeros((HALO, D_MODEL), F32))
            put_rows(s * ext + HALO + seg, (s + 1) * ext, jnp.zeros((HALO, D_MODEL), F32))
    for s in range(n_seg):
        put_rows(s * ext + HALO, s * ext + HALO + seg, hf[s * seg:(s + 1) * seg])
    h_il = [jnp.concatenate(
        [jnp.concatenate([hs[ct, pl.ds(s * ext + v, 8, stride=m), :] for v in range(m)], axis=0)
         for ct in range(D_MODEL // LANES)], axis=1).astype(BF16) for s in range(n_seg)]

    def project(s, c):
        cs = slice(c * CONV_SLAB, (c + 1) * CONV_SLAB)
        base = s * (ext + 2 * pad)
        res = dot_wt(h_il[s], c * CONV_SLAB, CONV_SLAB)
        xpad[base + pad:base + pad + ext, cs] = res
        for j in range(CONV_K // 2):
            xpad[base + 8 * j:base + 8 * (j + 1), cs] = pltpu.roll(res[ext - pad + 8 * j:ext - pad + 8 * (j + 1)], 1, 0)
            xpad[base + pad + ext + 8 * j:base + pad + ext + 8 * (j + 1), cs] = pltpu.roll(res[8 * j:8 * (j + 1)], 7, 0)

    def convolve(s, c):
        cs = slice(c * CONV_SLAB, (c + 1) * CONV_SLAB)
        base = s * (ext + 2 * pad)
        acc = None
        for tap in range(CONV_K):
            term = xpad[base + 8 * tap:base + 8 * tap + ext, cs] * cw_ref[tap:tap + 1, cs]
            acc = term if acc is None else acc + term
        y = _silu(acc)
        for j in range(CONV_SLAB // LANES):
            ct = c * (CONV_SLAB // LANES) + j
            for v in range(m):
                ybuf[ct, pl.ds(v, 8, stride=m), :] = y[8 * v:8 * (v + 1), j * LANES:(j + 1) * LANES]
            qkv_ref[s * seg:(s + 1) * seg, ct * LANES:(ct + 1) * LANES] = ybuf[ct, HALO:HALO + seg, :]

    z_ref[...] = proj("z")
    bq_ref[...] = proj("bq").astype(bq_ref.dtype)
    zk = proj("bk")
    zv = proj("bv")
    bk_ref[...] = zk.astype(bk_ref.dtype)
    bv_ref[...] = zv.astype(bv_ref.dtype)
    if emit_maps:
        bk8_ref, bv4_ref = refs[15:17]
        maps_s = refs[-4]
        pitch = tm + 8
        for c in range(H_B):
            pair = zk[:, c * LANES:(c + 1) * LANES]
            maps_s[2 * c * pitch:2 * c * pitch + tm, :] = pair
            maps_s[(2 * c + 1) * pitch:(2 * c + 1) * pitch + tm, :] = pltpu.roll(pair, DQK_B, axis=1)
            maps_s[(2 * H_B + c) * pitch:(2 * H_B + c) * pitch + tm, :] = zv[:, c * DV_B:(c + 1) * DV_B]

        for t in range(tm):
            bk8_ref[t] = maps_s[pl.ds(t, 2 * H_B, stride=pitch), :][:, 0:DQK_B]
            bv4_ref[t] = maps_s[pl.ds(2 * H_B * pitch + t, H_B, stride=pitch), :]
    for unit in units:
        project(*unit)
        convolve(*unit)

    gpar = gpar_ref[...]
    gate_pre = proj("dec")
    xg = gate_pre + gpar[1:2, :]
    softplus = jnp.maximum(xg, 0.0) + jnp.log1p(jnp.exp(-jnp.abs(xg)))
    lane = lax.broadcasted_iota(jnp.int32, (tm, LANES), 1)
    g = jnp.where(lane < 2 * H_A, -jnp.exp(gpar[0:1, :]) * softplus, 0.0)
    beta_ref[...] = jax.nn.sigmoid(pltpu.roll(gate_pre, LANES - 2 * H_A, axis=1))
    pos_c = lax.broadcasted_iota(jnp.int32, (tm, 1), 0) & (CHUNK - 1)
    gcf = g
    gcb = g
    for sft in (1, 2, 4, 8, 16, 32):
        gcf = gcf + jnp.where(pos_c >= sft, pltpu.roll(gcf, sft, axis=0), 0.0)
        gcb = gcb + jnp.where(pos_c < CHUNK - sft, pltpu.roll(gcb, tm - sft, axis=0), 0.0)
    gcs_ref[...] = jnp.where(lane < H_A, gcf, gcb)


def _inproj(x, mod, gain, w_mix_in, conv_w, gpar, rows_per_cond, seq_len, emit_maps):
    t = x.shape[0]
    tm = TOKEN_TILE
    hb = tm // HALO
    n_seg = max(1, tm // seq_len)
    ext = tm // n_seg + 2 * HALO
    row = pl.BlockSpec((tm, D_MODEL), lambda i: (i, 0))
    prev = pl.BlockSpec((HALO, D_MODEL), lambda i: (jnp.maximum(i * hb - 1, 0), 0))
    nxt = pl.BlockSpec((HALO, D_MODEL), lambda i: (jnp.minimum((i + 1) * hb, t // HALO - 1), 0))
    widths = [w for _, _, w in _IN_SEGS]
    out_specs = [pl.BlockSpec((tm, w), lambda i: (i, 0)) for w in widths]
    attn_dtype = BF16 if emit_maps else F32
    out_shape = [jax.ShapeDtypeStruct((t, w), attn_dtype if name in ("bq", "bk", "bv") else F32)
                 for name, _, w in _IN_SEGS]
    if emit_maps:
        out_specs += [pl.BlockSpec((tm, 2 * H_B, DQK_B), lambda i: (i, 0, 0)),
                      pl.BlockSpec((tm, H_B, DV_B), lambda i: (i, 0, 0))]
        out_shape += [jax.ShapeDtypeStruct((t, 2 * H_B, DQK_B), F32), jax.ShapeDtypeStruct((t, H_B, DV_B), F32)]
    return pl.pallas_call(
        functools.partial(_inproj_kernel, tm=tm, seq_len=seq_len, emit_maps=emit_maps),
        grid=(t // tm,),
        in_specs=[row, prev, nxt, _mod_spec(rows_per_cond, tm, lambda s: s), _resident((1, D_MODEL)),
                  _resident((IN_COLS, D_MODEL)), _resident((CONV_K, 3 * A_W)), _resident((8, LANES))],
        out_specs=out_specs,
        out_shape=out_shape,
        scratch_shapes=([pltpu.VMEM((3 * H_B * (tm + 8), LANES), F32)] if emit_maps else [])
        + [pltpu.VMEM((D_MODEL // LANES, n_seg * ext, LANES), F32),
                        pltpu.VMEM((n_seg * (ext + 2 * (CONV_K // 2) * 8), 3 * A_W), F32),
                        pltpu.VMEM((3 * A_W // LANES, ext, LANES), F32)],
        compiler_params=_cparams(("arbitrary",)),
        name="inproj",
    )(x, x, x, mod, gain, w_mix_in, conv_w, gpar)


def _post_kernel(*refs, cast):
    x_ref, oa_ref, ob_ref, mod_ref, gain_ref, gfin_ref, wmix_ref, w_in_ref, w_out_ref, o_ref = refs[:10]

    def tokens(w_in, w_out):
        y = _dot(oa_ref[...].astype(BF16), wmix_ref[0:A_W, :]) + _dot(ob_ref[...].astype(BF16), wmix_ref[A_W:, :])
        x = x_ref[...] + mod_ref[5:6, :] * y
        x = _swiglu_update(x, mod_ref, 6, gain_ref[...], w_in, w_out)
        o_ref[...] = (x * lax.rsqrt(jnp.mean(x * x, axis=-1, keepdims=True) + EPS)) * gfin_ref[...]

    if cast:
        step = pl.program_id(0)
        w_in_s, w_out_s = refs[12:14]
        _stage_ffn_weights(step, w_in_ref, w_out_ref, refs[10], refs[11], w_in_s, w_out_s)
        pl.when(step >= W_CHUNKS)(lambda: tokens(w_in_s, w_out_s))
    else:
        tokens(w_in_ref, w_out_ref)


def _post(x, oa, ob, mod, gain, gfin, w_mix, w_in, w_out, rows_per_cond):
    t = x.shape[0]
    tm = TOKEN_TILE
    cast = w_in.dtype == F32
    tile = _tile_of(cast)
    row = pl.BlockSpec((tm, D_MODEL), lambda s: (tile(s), 0))
    half = pl.BlockSpec((tm, A_W), lambda s: (tile(s), 0))
    w_in_specs, w_out_specs, w_shapes, w_scratch = _ffn_weight_plumbing(cast)
    outs = pl.pallas_call(
        functools.partial(_post_kernel, cast=cast),
        grid=(t // tm + (W_CHUNKS if cast else 0),),
        in_specs=[row, half, half, _mod_spec(rows_per_cond, tm, tile), _resident((1, D_MODEL)),
                  _resident((1, D_MODEL)), _resident((A_W + B_W, D_MODEL))] + w_in_specs,
        out_specs=[row] + w_out_specs,
        out_shape=[jax.ShapeDtypeStruct((t, D_MODEL), F32)] + w_shapes,
        scratch_shapes=w_scratch,
        compiler_params=_cparams(("arbitrary",)),
        name="post",
    )(x, oa, ob, mod, gain, gfin, w_mix, w_in, w_out)
    return (outs[0], outs[1], outs[2]) if cast else (outs[0], w_in, w_out)


def _rope(x, cos, sin_signed):
    w = x.shape[-1]
    lane = lax.broadcasted_iota(jnp.int32, x.shape, 1)
    p = ROPE_PAIRS
    partner = jnp.where((lane & (2 * p - 1)) < p, pltpu.roll(x, w - p, axis=1), pltpu.roll(x, p, axis=1))
    return x * cos + partner * sin_signed


def _store_values(v_s, rows, v):
    for h in range(H_B):
        vh = v[:, h * DV_B:(h + 1) * DV_B]
        v_s[rows, 2 * h * DV_B:(2 * h + 1) * DV_B] = vh.astype(BF16)
        v_s[rows, (2 * h + 1) * DV_B:(2 * h + 2) * DV_B] = jnp.ones(vh.shape, BF16)


def _attn_core(q, k_ref, v_ref, lam, gain_ref, o_ref):
    tq = q.shape[0]
    lane = lax.broadcasted_iota(jnp.int32, (tq, DV_B), 1)
    qs = q * (DQK_B ** -0.5)
    heads = [slice(h * DV_B, (h + 1) * DV_B) for h in range(H_B)]
    scores = []
    for hs in heads:
        qh = qs[:, hs]
        q2 = jnp.concatenate([jnp.where(lane < DQK_B, qh, 0.0), jnp.where(lane >= DQK_B, qh, 0.0)], axis=0)
        scores.append(lax.dot_general(q2.astype(BF16), k_ref[:, hs], (((1,), (1,)), ((), ())),
                                      preferred_element_type=F32))
    for h, (hs, s) in enumerate(zip(heads, scores)):
        e = jnp.exp(s - jnp.max(s, axis=-1, keepdims=True))
        rs = _dot(e.astype(BF16), v_ref[:, 2 * h * DV_B:(2 * h + 2) * DV_B])
        r = rs[:, :DV_B] / rs[:, DV_B:]
        o = r[:tq] - lam * r[tq:]
        o = o * lax.rsqrt(jnp.mean(o * o, axis=-1, keepdims=True) + EPS)
        o_ref[:, hs] = (o * gain_ref[...]) * (1.0 - LAM_INIT)


def _lambda(lam_ref):
    l = lam_ref[...]
    s1 = jnp.sum(l[0:1] * l[1:2], axis=-1, keepdims=True)
    s2 = jnp.sum(l[2:3] * l[3:4], axis=-1, keepdims=True)
    return jnp.exp(s1) - jnp.exp(s2) + LAM_INIT


CTX_SEQS = 4


def _attn_ctx_kernel(q_ref, k_ref, v_ref, lam_ref, gain_ref, o_ref, v_s):
    lam = _lambda(lam_ref)

    def one_sequence(bi, carry):
        _store_values(v_s, slice(None), v_ref[bi])
        _attn_core(q_ref[bi], k_ref.at[bi], v_s, lam, gain_ref, o_ref.at[bi])
        return carry

    lax.fori_loop(0, CTX_SEQS, one_sequence, 0)


def _attn_ctx(q, k, v, lam_pack, gain):
    b, l, _ = q.shape
    blk = pl.BlockSpec((CTX_SEQS, l, B_W), lambda i: (i, 0, 0))
    return pl.pallas_call(
        _attn_ctx_kernel,
        grid=(b // CTX_SEQS,),
        in_specs=[blk, blk, blk, _resident((8, LANES)), _resident((1, DV_B))],
        out_specs=blk,
        out_shape=jax.ShapeDtypeStruct((b, l, B_W), F32),
        scratch_shapes=[pltpu.VMEM((l, 2 * B_W), BF16)],
        compiler_params=_cparams(("arbitrary",)),
        name="attn_ctx",
    )(q, k, v, lam_pack, gain)


def _attn_lat_kernel(q_ref, k_ref, v_ref, ck_ref, cv_ref, cosq_ref, sinq_ref, cos_ref, sin_ref,
                     lam_ref, gain_ref, o_ref, k_s, v_s, *, past):
    @pl.when(pl.program_id(1) == 0)
    def _():
        k_s[0:past, :] = ck_ref[...].astype(BF16)
        _store_values(v_s, slice(0, past), cv_ref[...])
        k_s[past:, :] = _rope(k_ref[...], cos_ref[...], sin_ref[...]).astype(BF16)
        _store_values(v_s, slice(past, None), v_ref[...])

    q = _rope(q_ref[...], cosq_ref[...], sinq_ref[...])
    _attn_core(q, k_s, v_s, _lambda(lam_ref), gain_ref, o_ref)


def _attn_lat(q, k, v, ck, cv, cos, sin, lam_pack, gain):
    b, l, _ = q.shape
    past = ck.shape[1]
    tq = Q_TILE
    full = pl.BlockSpec((None, l, B_W), lambda i, j: (i, 0, 0))
    cache_k = cache_v = pl.BlockSpec((None, past, B_W), lambda i, j: (i, 0, 0))
    qblk = pl.BlockSpec((None, tq, B_W), lambda i, j: (i, j, 0))
    tab_q = pl.BlockSpec((tq, B_W), lambda i, j: (j, 0))
    tab = pl.BlockSpec((l, B_W), lambda i, j: (0, 0))
    return pl.pallas_call(
        functools.partial(_attn_lat_kernel, past=past),
        grid=(b, l // tq),
        in_specs=[qblk, full, full, cache_k, cache_v, tab_q, tab_q, tab, tab,
                  pl.BlockSpec((8, LANES), lambda i, j: (0, 0)), pl.BlockSpec((1, DV_B), lambda i, j: (0, 0))],
        out_specs=qblk,
        out_shape=jax.ShapeDtypeStruct((b, l, B_W), F32),
        scratch_shapes=[pltpu.VMEM((past + l, B_W), BF16), pltpu.VMEM((past + l, 2 * B_W), BF16)],
        compiler_params=_cparams(("arbitrary", "arbitrary")),
        name="attn_lat",
    )(q, k, v, ck, cv, cos, sin, cos, sin, lam_pack, gain)


def _rope_tables(length):
    pairs = ROPE_PAIRS
    n_rows = length // GRID_W
    pos_row = np.repeat(np.arange(n_rows), GRID_W).astype(np.float32)
    pos_col = np.tile(np.arange(GRID_W), n_rows).astype(np.float32)
    inv = (np.float32(ROPE_THETA) ** (-np.arange(pairs, dtype=np.float32) / np.float32(pairs))).astype(np.float32)
    ang_r = pos_row[:, None] * inv
    ang_c = pos_col[:, None] * inv
    cos = np.concatenate([np.cos(ang_r), np.cos(ang_r), np.cos(ang_c), np.cos(ang_c)], axis=-1)
    sin = np.concatenate([-np.sin(ang_r), np.sin(ang_r), -np.sin(ang_c), np.sin(ang_c)], axis=-1)
    reps = B_W // DQK_B
    return jnp.asarray(np.tile(cos, (1, reps)), F32), jnp.asarray(np.tile(sin, (1, reps)), F32)


def _block_diag(x):
    lo, hi = x[:, :LANES], x[:, LANES:]
    first = lax.broadcasted_iota(jnp.int32, lo.shape, 1) < DK_A
    top = jnp.concatenate([jnp.where(first, lo, 0), jnp.where(first, 0, lo)], axis=0)
    bot = jnp.concatenate([jnp.where(first, hi, 0), jnp.where(first, 0, hi)], axis=0)
    zero = jnp.zeros_like(top)
    return jnp.concatenate([jnp.concatenate([top, zero], axis=1), jnp.concatenate([zero, bot], axis=1)], axis=0)


SOLVE_CHUNKS = 8
DELTA_ROWS = 1024


def _delta_kernel(*refs, nb, length, has_state):
    if has_state:
        (q_ref, k_ref, v_ref, z_ref, gcs_ref, beta_ref, dn_ref, s0_ref, o_ref, st_ref, *scr) = refs
    else:
        (q_ref, k_ref, v_ref, z_ref, gcs_ref, beta_ref, dn_ref, o_ref, st_ref, *scr) = refs
        s0_ref = None
    qs, ks, vs, gexp, bexp, u0_s, kg_s, dec_s, w_s, qg_s, qkm_s, of_s, ob_s, s_s = scr
    L = length
    R = nb * L
    n_chunks = L // CHUNK
    grp = pl.program_id(1)
    W = GROUP_W
    N_GRP = A_W // W

    r256 = lax.broadcasted_iota(jnp.int32, (W, W), 0)
    c256 = lax.broadcasted_iota(jnp.int32, (W, W), 1)
    dk_bits, w_bits = DK_A.bit_length() - 1, W.bit_length() - 1
    bd_mask = (r256 >> dk_bits) == (c256 >> dk_bits)
    ones_bd = jnp.where(bd_mask, 1.0, 0.0).astype(BF16)
    ri = lax.broadcasted_iota(jnp.int32, (CHUNK, W), 0)
    cj = lax.broadcasted_iota(jnp.int32, (CHUNK, W), 1) & (CHUNK - 1)
    eye_p = ri == cj
    NS = 2 * H_A
    er = lax.broadcasted_iota(jnp.int32, (LANES, 4 * W), 0)
    ec = lax.broadcasted_iota(jnp.int32, (LANES, 4 * W), 1)
    src = ((ec >> w_bits) & 1) * H_A + grp * 4 + ((ec & (W - 1)) >> dk_bits)
    ns_bits = NS.bit_length() - 1
    expand = jnp.where((er < 4 * NS) & ((er & (NS - 1)) == src) & (((er >> ns_bits) & 1) == (ec >> (w_bits + 1))),
                       1.0, 0.0).astype(BF16)
    lane_s = lax.broadcasted_iota(jnp.int32, (L, LANES), 1)

    bd = _block_diag

    def head_sums(y):
        return _dot(y.astype(BF16), ones_bd)

    def prologue(bi, carry):
        rows_b = pl.ds(pl.multiple_of(bi * L, L), L)

        def l2n(y):
            return y * lax.rsqrt(head_sums(y * y) + EPS)

        qs[rows_b, :] = l2n(q_ref[bi]) * (DK_A ** -0.5)
        ks[rows_b, :] = l2n(k_ref[bi])
        vs[rows_b, :] = v_ref[bi]
        gc = gcs_ref[bi]
        bt = jnp.where(lane_s < NS, beta_ref[bi], 0.0)
        gc_hi = gc.astype(BF16).astype(F32)
        bt_hi = bt.astype(BF16).astype(F32)
        packed = (gc_hi + pltpu.roll(bt_hi, NS, axis=1) + pltpu.roll(gc - gc_hi, 2 * NS, axis=1)
                  + pltpu.roll(bt - bt_hi, 3 * NS, axis=1))
        both = _dot(packed.astype(BF16), expand)
        for d in range(2):
            gexp[d, rows_b, :] = both[:, d * W:(d + 1) * W]
            bexp[d, rows_b, :] = both[:, 2 * W + d * W:2 * W + (d + 1) * W]
        return carry

    lax.fori_loop(0, nb, prologue, 0)

    def solve_body(j, carry):
        chunks = []
        for k in range(SOLVE_CHUNKS):
            chunk = j * SOLVE_CHUNKS + k
            rows = pl.ds(pl.multiple_of(chunk * CHUNK, CHUNK), CHUNK)
            kc = ks[rows, :]
            qc = qs[rows, :]
            kb = kc.astype(BF16)
            res = lax.dot_general(jnp.concatenate([qc.astype(BF16), kb], axis=0), bd(kb),
                                  (((1,), (1,)), ((), ())), preferred_element_type=F32)
            qk, kk = res[:CHUNK], res[CHUNK:]
            ch = dict(rows=rows, bd_k=bd(kb), bd_v=bd(vs[rows, :].astype(BF16)), t=[], scale_u=[], scale_w=[])
            nsum = None
            for d in range(2):
                ge = gexp[d, rows, :]
                be = bexp[d, rows, :]
                if d == 0:
                    incl, strict, gl = ri >= cj, ri > cj, ge[CHUNK - 1:CHUNK, :]
                else:
                    incl, strict, gl = ri <= cj, ri < cj, ge[0:1, :]
                g_row = jnp.sum(jnp.where(eye_p, ge, 0.0), axis=0, keepdims=True)
                b_row = jnp.sum(jnp.where(eye_p, be, 0.0), axis=0, keepdims=True)
                dec_m = jnp.exp(jnp.where(incl, ge - g_row, -jnp.inf))
                nmat = jnp.where(strict, be * kk * dec_m, 0.0)
                nsum = nmat if nsum is None else nsum + nmat
                qkm_s[grp, d, rows, :] = jnp.where(incl, qk * dec_m, 0.0).astype(BF16)
                qg_s[grp, d, rows, :] = (qc * jnp.exp(ge)).astype(BF16)
                kgt = (kc * jnp.exp(gl - ge)).T
                kg_s[grp, d, rows, :] = jnp.concatenate(
                    [kgt[h * DK_A:(h + 1) * DK_A, :] for h in range(4)], axis=1).astype(BF16)
                dec_s[grp, d, pl.ds(pl.multiple_of(chunk * 8, 8), 8), :] = jnp.broadcast_to(jnp.exp(gl), (8, W))
                ch["scale_u"].append(b_row)
                ch["scale_w"].append(b_row * jnp.exp(g_row))
                ch["t"].append(jnp.where(eye_p, 1.0, 0.0) - jnp.where((ri >> 1) == (cj >> 1), nmat, 0.0))
            ch["n_both"] = bd(nsum.astype(BF16))
            chunks.append(ch)

        for lb in range(1, 6):
            same = (ri >> (lb + 1)) == (cj >> (lb + 1))
            pair = (same & (((ri >> lb) & 1) == 1) & (((cj >> lb) & 1) == 0),
                    same & (((cj >> lb) & 1) == 1) & (((ri >> lb) & 1) == 0))
            for ch in chunks:
                ch["tb"] = [t.astype(BF16) for t in ch["t"]]
                ch["y"] = _dot(jnp.concatenate(ch["tb"], axis=0), ch["n_both"]).astype(BF16)
            even_rows = ((ri >> lb) & 1) == 0
            for ch in chunks:
                ys = jnp.concatenate([jnp.where(pair[0], ch["y"][:CHUNK], 0),
                                      jnp.where(pair[1], ch["y"][CHUNK:], 0)], axis=0)
                upd = _dot(ys, bd(jnp.where(even_rows, ch["tb"][0], ch["tb"][1])))
                ch["t"] = [ch["t"][d] - upd[d * CHUNK:(d + 1) * CHUNK] for d in range(2)]
        for ch in chunks:
            tu = jnp.concatenate([(ch["t"][d] * ch["scale_u"][d]).astype(BF16) for d in range(2)], axis=0)
            tw = jnp.concatenate([(ch["t"][d] * ch["scale_w"][d]).astype(BF16) for d in range(2)], axis=0)
            u0 = _dot(tu, ch["bd_v"])
            w = _dot(tw, ch["bd_k"]).astype(BF16)
            for d in range(2):
                u0_s[grp, d, ch["rows"], :] = u0[d * CHUNK:(d + 1) * CHUNK]
                w_s[grp, d, ch["rows"], :] = w[d * CHUNK:(d + 1) * CHUNK]
        return carry

    lax.fori_loop(0, R // (SOLVE_CHUNKS * CHUNK), solve_body, 0)

    if has_state:
        for bi in range(nb):
            for d in range(2):
                for h in range(4):
                    s_s[grp, bi, d, :, h * DK_A:(h + 1) * DK_A] = s0_ref[bi, d, h]
    else:
        s_s[grp] = jnp.zeros(s_s.shape[1:], F32)

    def scan_body(i, carry):
        chains = []
        for gi in range(N_GRP):
            for bi in range(nb):
                for d in range(2):
                    c = i if d == 0 else n_chunks - 1 - i
                    rows = pl.ds(pl.multiple_of(bi * L + c * CHUNK, CHUNK), CHUNK)
                    tile = pl.ds(pl.multiple_of((bi * n_chunks + c) * 8, 8), 8)
                    chains.append(dict(
                        gi=gi, bi=bi, d=d, rows=rows, s=s_s[gi, bi, d], u0=u0_s[gi, d, rows, :],
                        kg=kg_s[gi, d, rows, :], qkm=qkm_s[gi, d, rows, :], decay=dec_s[gi, d, tile, :][0:1],
                        wq=jnp.concatenate([w_s[gi, d, rows, :], qg_s[gi, d, rows, :]], axis=0)))
        for c in chains:
            c["ws"] = _dot(c["wq"], bd(c["s"].astype(BF16)))
        for c in chains:
            c["ub"] = (c["u0"] - c["ws"][:CHUNK]).astype(BF16)
        for c in chains:
            both = _dot(jnp.concatenate([c["qkm"], c["kg"]], axis=0), bd(c["ub"]))
            c["o"] = c["ws"][CHUNK:] + both[:CHUNK]
            c["s_new"] = c["s"] * c["decay"] + both[CHUNK:]
        for c in chains:
            s_s[c["gi"], c["bi"], c["d"]] = c["s_new"]
            if c["d"] == 0:
                of_s[c["gi"], c["rows"], :] = c["o"]
            else:
                ob_s[c["gi"], c["rows"], :] = c["o"]
        return carry

    def epilogue(bi, carry):
        rows_b = pl.ds(pl.multiple_of(bi * L, L), L)
        for gi in range(N_GRP):
            lanes = slice(gi * W, (gi + 1) * W)
            o = of_s[gi, rows_b, :] + ob_s[gi, rows_b, :]
            ms = head_sums(o * o) * (1.0 / DK_A)
            o = o * lax.rsqrt(ms + EPS) * dn_ref[...]
            o_ref[bi, :, lanes] = o * _silu(z_ref[bi, :, lanes])
        return carry

    @pl.when(grp == N_GRP - 1)
    def _():
        lax.fori_loop(0, n_chunks, scan_body, 0)
        lax.fori_loop(0, nb, epilogue, 0)
        for gi in range(N_GRP):
            for bi in range(nb):
                for d in range(2):
                    for h in range(4):
                        st_ref[bi, d, gi * 4 + h] = s_s[gi, bi, d, :, h * DK_A:(h + 1) * DK_A]


def _delta(qkv, z, gcs, beta, dnorm, s0):
    b, l, _ = qkv.shape
    has_state = s0 is not None
    w = GROUP_W
    n_grp = A_W // w
    nb = DELTA_ROWS // l
    r = nb * l

    def slab(k):
        return pl.BlockSpec((nb, l, w), lambda i, g: (i, 0, k * n_grp + g))

    narrow = pl.BlockSpec((nb, l, LANES), lambda i, g: (i, 0, 0))
    wide = pl.BlockSpec((nb, l, A_W), lambda i, g: (i, 0, 0))
    in_specs = [slab(0), slab(1), slab(2), wide, narrow, narrow,
                pl.BlockSpec((1, w), lambda i, g: (0, 0))]
    args = [qkv, qkv, qkv, z, gcs, beta, dnorm]
    if has_state:
        in_specs.append(pl.BlockSpec((nb, None, 2, 4, DK_A, DK_A), lambda i, g: (i, 0, 0, g, 0, 0)))
        args.append(s0)
    per_grp = lambda shape, dt: pltpu.VMEM((n_grp,) + shape, dt)
    scratch = ([pltpu.VMEM((r, w), F32)] * 3 + [pltpu.VMEM((2, r, w), F32)] * 2
               + [per_grp((2, r, w), F32), per_grp((2, r, w), BF16), per_grp((2, r // CHUNK * 8, w), F32)]
               + [per_grp((2, r, w), BF16)] * 3 + [per_grp((r, w), F32)] * 2
               + [per_grp((nb, 2, DK_A, w), F32)])
    return pl.pallas_call(
        functools.partial(_delta_kernel, nb=nb, length=l, has_state=has_state),
        grid=(b // nb, n_grp),
        in_specs=in_specs,
        out_specs=[wide, pl.BlockSpec((nb, None, 2, H_A, DK_A, DK_A), lambda i, g: (i, 0, 0, 0, 0, 0))],
        out_shape=[jax.ShapeDtypeStruct((b, l, A_W), F32),
                   jax.ShapeDtypeStruct((b, 1, 2, H_A, DK_A, DK_A), F32)],
        scratch_shapes=scratch,
        compiler_params=_cparams(("arbitrary", "arbitrary")),
        name="delta_state" if has_state else "delta",
    )(*args)


def kernel(x_prompt, x_sample, cache_diff_k, cache_diff_v, state_delta, c, c_ctx, w_ada, b_ada, norm_ffn1,
           w_ffn1_in, w_ffn1_out, norm_mix, w_in, conv_w, a_log, dt_bias, delta_norm, lambda_q1, lambda_k1,
           lambda_q2, lambda_k2, diff_norm, w_out, norm_ffn2, w_ffn2_in, w_ffn2_out, norm_final):
    bp, lp, _ = x_prompt.shape
    bs, ls, _ = x_sample.shape
    past = cache_diff_k.shape[2]

    w_mix_out = w_out[0].astype(BF16)
    w_mix_in = jnp.swapaxes(w_in, 1, 2)[0].astype(BF16)
    g1 = norm_ffn1[0].reshape(1, D_MODEL)
    gm = norm_mix[0].reshape(1, D_MODEL)
    g2 = norm_ffn2[0].reshape(1, D_MODEL)
    gf = norm_final.reshape(1, D_MODEL)
    gpar = jnp.pad(jnp.stack([a_log[0].reshape(-1), dt_bias[0].reshape(-1)]), ((0, 6), (0, LANES - 2 * H_A)))
    dnorm = jnp.tile(delta_norm[0], 4).reshape(1, GROUP_W)
    lam_pack = jnp.pad(jnp.concatenate([lambda_q1, lambda_k1, lambda_q2, lambda_k2], axis=0),
                       ((0, 4), (0, LANES - DQK_B)))
    dgain = diff_norm[0].reshape(1, DV_B)
    cos, sin = _rope_tables(ls)

    cond = jnp.pad(jnp.concatenate([c_ctx[None, :], c], axis=0), ((0, 8 - 1 - bs), (0, 0)))
    mod = _adaln(cond, w_ada[0], b_ada[0]).reshape(8, N_MOD, D_MODEL)

    def trunk(x3, rows_per_cond, ctx, w1, w2):
        b, l, _ = x3.shape
        x = x3.reshape(b * l, D_MODEL)
        x1, *w1 = _ffn1(x, mod, g1, *w1, rows_per_cond)
        outs = _inproj(x1, mod, gm, w_mix_in, conv_w[0], gpar, rows_per_cond, l, ctx is None)
        qkv, z, bq, bk, bv, gcs, beta = outs[:7]
        r3 = lambda a: a.reshape(b, l, a.shape[-1])
        s0 = None if ctx is None else ctx[2]
        oa, st = _delta(r3(qkv), r3(z), r3(gcs), r3(beta), dnorm, s0)
        if ctx is None:
            ob = _attn_ctx(r3(bq), r3(bk), r3(bv), lam_pack, dgain)
        else:
            ob = _attn_lat(r3(bq), r3(bk), r3(bv), ctx[0], ctx[1], cos, sin, lam_pack, dgain)
        y, *w2 = _post(x1, oa.reshape(b * l, A_W), ob.reshape(b * l, B_W), mod, g2, gf, w_mix_out, *w2,
                       rows_per_cond)
        return y.reshape(b, l, D_MODEL), outs[7:], st, w1, w2

    y_prompt, (k_maps, v_heads), s_ctx, w1, w2 = trunk(x_prompt, None, None, (w_ffn1_in[0], w_ffn1_out[0]),
                                                       (w_ffn2_in[0], w_ffn2_out[0]))
    ctx = (cache_diff_k[:, 0].reshape(bs, past, B_W), cache_diff_v[:, 0].reshape(bs, past, B_W), state_delta)
    y_sample = trunk(x_sample, ls, ctx, w1, w2)[0]
    new_k = k_maps.reshape(bp, 1, lp, H_B, 2, DQK_B)
    new_v = v_heads.reshape(bp, 1, lp, H_B, DV_B)
    return (y_prompt, y_sample, new_k, new_v, s_ctx)
```

```python
import functools
import math

import jax
import jax.numpy as jnp
import numpy as np
from jax import lax
from jax.experimental import pallas as pl
from jax.experimental.pallas import tpu as pltpu

F32 = jnp.float32
BF16 = jnp.bfloat16

D_MODEL = 1024
D_FF = 2816
N_MOD = 9
H_A = 8
DK_A = 64
A_W = H_A * DK_A
H_B = 4
DQK_B = 64
DV_B = 128
B_W = H_B * DV_B
CONV_K = 5
CHUNK = 64
GRID_W = 64
ROPE_THETA = 10000.0
ROPE_PAIRS = DQK_B // 4
EPS = 1e-6
LAM_INIT = 0.8 - 0.6 * math.exp(-0.3 * 0)

LANES = 128
GROUP_W = 4 * DK_A
GATE_COL = 4 * A_W
DIFF_COL = GATE_COL + 4 * H_A
IN_COLS = DIFF_COL + 3 * B_W
VMEM_LIMIT = 56 * 1024 * 1024

MXU_W = 256
TOKEN_TILE = 512
FF_CHUNKS = ((0, 6 * MXU_W), (6 * MXU_W, D_FF))
Q_TILE = 256


def _cparams(sem):
    return pltpu.CompilerParams(dimension_semantics=sem, vmem_limit_bytes=VMEM_LIMIT)


def _resident(shape):
    nd = len(shape)
    return pl.BlockSpec(shape, lambda *_: (0,) * nd, pipeline_mode=pl.Buffered(1))


def _silu(x):
    half = 0.5 * x
    return half * jnp.tanh(half) + half


def _mod_norm(x, gain, shift, scale):
    y = x * lax.rsqrt(jnp.mean(x * x, axis=-1, keepdims=True) + EPS)
    return (y * gain) * (1.0 + scale) + shift


def _dot(a, b):
    return jnp.dot(a, b, preferred_element_type=F32)


def _adaln_kernel(c_ref, w_ref, b_ref, o_ref):
    s = _silu(c_ref[...])
    o_ref[...] = _dot(s.astype(BF16), w_ref[...].astype(BF16)) + b_ref[...]


def _adaln(cond, w_ada, b_ada):
    n = N_MOD * D_MODEL
    tn = n // 4
    return pl.pallas_call(
        _adaln_kernel,
        grid=(n // tn,),
        in_specs=[pl.BlockSpec((8, D_MODEL), lambda j: (0, 0)),
                  pl.BlockSpec((D_MODEL, tn), lambda j: (0, j)),
                  pl.BlockSpec((1, tn), lambda j: (0, j))],
        out_specs=pl.BlockSpec((8, tn), lambda j: (0, j)),
        out_shape=jax.ShapeDtypeStruct((8, n), F32),
        compiler_params=_cparams(("arbitrary",)),
        name="adaln",
    )(cond, w_ada, b_ada.reshape(1, n))


def _swiglu_update(x, mod_ref, k0, gain, w_in_ref, w_out_ref):
    h = _mod_norm(x, gain, mod_ref[k0:k0 + 1, :], mod_ref[k0 + 1:k0 + 2, :]).astype(BF16)
    acc = None
    for lo, hi in FF_CHUNKS:
        g = _dot(h, w_in_ref[:, lo:hi])
        u = _dot(h, w_in_ref[:, D_FF + lo:D_FF + hi])
        part = _dot((_silu(g) * u).astype(BF16), w_out_ref[lo:hi, :])
        acc = part if acc is None else acc + part
    return x + (0.5 * mod_ref[k0 + 2:k0 + 3, :]) * acc


W_CHUNKS = 11
W_IN_CHUNK = 2 * D_FF // W_CHUNKS
W_OUT_CHUNK = D_FF // W_CHUNKS


def _stage_ffn_weights(step, w_in_ref, w_out_ref, w_in_bf_ref, w_out_bf_ref, w_in_s, w_out_s):
    for c in range(W_CHUNKS):
        @pl.when(step == c)
        def _():
            a = w_in_ref[...].astype(BF16)
            b = w_out_ref[...].astype(BF16)
            w_in_bf_ref[...] = a
            w_out_bf_ref[...] = b
            w_in_s[:, c * W_IN_CHUNK:(c + 1) * W_IN_CHUNK] = a
            w_out_s[c * W_OUT_CHUNK:(c + 1) * W_OUT_CHUNK, :] = b


def _ffn_weight_plumbing(cast):
    if not cast:
        return [_resident((D_MODEL, 2 * D_FF)), _resident((D_FF, D_MODEL))], [], [], []
    last = W_CHUNKS - 1
    w_in_chunk = pl.BlockSpec((D_MODEL, W_IN_CHUNK), lambda s: (0, jnp.minimum(s, last)))
    w_out_chunk = pl.BlockSpec((W_OUT_CHUNK, D_MODEL), lambda s: (jnp.minimum(s, last), 0))
    out_shape = [jax.ShapeDtypeStruct((D_MODEL, 2 * D_FF), BF16), jax.ShapeDtypeStruct((D_FF, D_MODEL), BF16)]
    scratch = [pltpu.VMEM((D_MODEL, 2 * D_FF), BF16), pltpu.VMEM((D_FF, D_MODEL), BF16)]
    return [w_in_chunk, w_out_chunk], [w_in_chunk, w_out_chunk], out_shape, scratch


def _tile_of(cast):
    off = W_CHUNKS if cast else 0
    return lambda s: jnp.maximum(s - off, 0)


def _mod_spec(rows_per_cond, tm, tile):
    if rows_per_cond is None:
        return pl.BlockSpec((None, N_MOD, D_MODEL), lambda s: (0, 0, 0))
    return pl.BlockSpec((None, N_MOD, D_MODEL), lambda s: (1 + (tile(s) * tm) // rows_per_cond, 0, 0))


def _ffn1_kernel(xp_ref, xs_ref, mod_ref, gain_ref, w_in_ref, w_out_ref, op_ref, os_ref, w_in_s, w_out_s, *,
                 prompt_tiles):
    step = pl.program_id(0)
    for c in range(W_CHUNKS):
        @pl.when(step == c)
        def _():
            w_in_s[:, c * W_IN_CHUNK:(c + 1) * W_IN_CHUNK] = w_in_ref[...].astype(BF16)
            w_out_s[c * W_OUT_CHUNK:(c + 1) * W_OUT_CHUNK, :] = w_out_ref[...].astype(BF16)

    @pl.when((step >= W_CHUNKS) & (step < W_CHUNKS + prompt_tiles))
    def _():
        op_ref[...] = _swiglu_update(xp_ref[...], mod_ref, 0, gain_ref[...], w_in_s, w_out_s)

    @pl.when(step >= W_CHUNKS + prompt_tiles)
    def _():
        os_ref[...] = _swiglu_update(xs_ref[...], mod_ref, 0, gain_ref[...], w_in_s, w_out_s)


def _ffn1(xp, xs, mod, gain, w_in, w_out, sample_rows_per_cond):
    tm = TOKEN_TILE
    n_p, n_s = xp.shape[0] // tm, xs.shape[0] // tm
    first_s = W_CHUNKS + n_p
    tile_p = lambda s: jnp.clip(s - W_CHUNKS, 0, n_p - 1)
    tile_s = lambda s: jnp.clip(s - first_s, 0, n_s - 1)
    row_p = pl.BlockSpec((tm, D_MODEL), lambda s: (tile_p(s), 0))
    row_s = pl.BlockSpec((tm, D_MODEL), lambda s: (tile_s(s), 0))
    mod_spec = pl.BlockSpec(
        (None, N_MOD, D_MODEL),
        lambda s: (jnp.where(s < first_s, 0, 1 + (tile_s(s) * tm) // sample_rows_per_cond), 0, 0))
    w_in_specs, _, _, w_scratch = _ffn_weight_plumbing(True)
    return pl.pallas_call(
        functools.partial(_ffn1_kernel, prompt_tiles=n_p),
        grid=(first_s + n_s,),
        in_specs=[row_p, row_s, mod_spec, _resident((1, D_MODEL))] + w_in_specs,
        out_specs=[row_p, row_s],
        out_shape=[jax.ShapeDtypeStruct(xp.shape, F32), jax.ShapeDtypeStruct(xs.shape, F32)],
        scratch_shapes=w_scratch,
        compiler_params=_cparams(("arbitrary",)),
        name="ffn1",
    )(xp, xs, mod, gain, w_in, w_out)


_IN_SEGS = (("qkv", 0, 3 * A_W), ("z", 3 * A_W, A_W), ("bq", 0, B_W), ("bk", B_W, B_W), ("bv", 2 * B_W, B_W),
            ("dec", GATE_COL, LANES), ("bet", GATE_COL, LANES))


HALO = 8
CONV_SLAB = 256


def _inproj_kernel(*refs, tm, seq_len, emit_maps):
    x_ref, xp_ref, xn_ref, mod_ref, gain_ref, w_ref, cw_ref, gpar_ref = refs[:8]
    qkv_ref, z_ref, bq_ref, bk_ref, bv_ref, gcs_ref, beta_ref = refs[8:15]
    hs, xpad, ybuf = refs[-3:]
    wdiff_s = refs[17 if emit_maps else 15]

    @pl.when(pl.program_id(0) == 0)
    def _():
        for r in range(0, D_MODEL, LANES):
            wdiff_s[r:r + LANES, :] = w_ref[r:r + LANES, DIFF_COL:DIFF_COL + 3 * B_W]

    shift, scale = mod_ref[3:4, :], mod_ref[4:5, :]
    hf = _mod_norm(x_ref[...], gain_ref[...], shift, scale)
    h = hf.astype(BF16)
    segs = dict((name, (start, width)) for name, start, width in _IN_SEGS)

    def proj(name):
        start, width = segs[name]
        src = wdiff_s if name in ("bq", "bk", "bv") else w_ref
        return _dot(h, src[:, start:start + width])

    seg = min(tm, seq_len)
    n_seg = tm // seg
    ext = seg + 2 * HALO
    m = ext // 8
    pad = (CONV_K // 2) * 8
    qkv_w = 3 * A_W
    units = [(s, c) for s in range(n_seg) for c in range(qkv_w // CONV_SLAB)]
    def put_rows(lo, hi, val):
        for ct in range(D_MODEL // LANES):
            hs[ct, lo:hi, :] = val[:, ct * LANES:(ct + 1) * LANES]

    if seq_len > tm:
        tiles = seq_len // tm
        pos = pl.program_id(0) % tiles
        put_rows(0, HALO, _mod_norm(xp_ref[...], gain_ref[...], shift, scale) * jnp.where(pos > 0, 1.0, 0.0))
        put_rows(HALO + seg, ext,
                 _mod_norm(xn_ref[...], gain_ref[...], shift, scale) * jnp.where(pos < tiles - 1, 1.0, 0.0))
    else:
        for s in range(n_seg):
            put_rows(s * ext, s * ext + HALO, jnp.zeros((HALO, D_MODEL), F32))
            put_rows(s * ext + HALO + seg, (s + 1) * ext, jnp.zeros((HALO, D_MODEL), F32))
    for s in range(n_seg):
        put_rows(s * ext + HALO, s * ext + HALO + seg, hf[s * seg:(s + 1) * seg])
    h_il = [jnp.concatenate(
        [jnp.concatenate([hs[ct, pl.ds(s * ext + v, 8, stride=m), :] for v in range(m)], axis=0)
         for ct in range(D_MODEL // LANES)], axis=1).astype(BF16) for s in range(n_seg)]

    def project(s, c):
        cs = slice(c * CONV_SLAB, (c + 1) * CONV_SLAB)
        base = s * (ext + 2 * pad)
        res = _dot(h_il[s], w_ref[:, cs])
        xpad[base + pad:base + pad + ext, cs] = res
        for j in range(CONV_K // 2):
            xpad[base + 8 * j:base + 8 * (j + 1), cs] = pltpu.roll(res[ext - pad + 8 * j:ext - pad + 8 * (j + 1)], 1, 0)
            xpad[base + pad + ext + 8 * j:base + pad + ext + 8 * (j + 1), cs] = pltpu.roll(res[8 * j:8 * (j + 1)], 7, 0)

    def convolve(s, c):
        cs = slice(c * CONV_SLAB, (c + 1) * CONV_SLAB)
        base = s * (ext + 2 * pad)
        acc = None
        for tap in range(CONV_K):
            term = xpad[base + 8 * tap:base + 8 * tap + ext, cs] * cw_ref[tap:tap + 1, cs]
            acc = term if acc is None else acc + term
        y = _silu(acc)
        for j in range(CONV_SLAB // LANES):
            ct = c * (CONV_SLAB // LANES) + j
            for v in range(m):
                ybuf[ct, pl.ds(v, 8, stride=m), :] = y[8 * v:8 * (v + 1), j * LANES:(j + 1) * LANES]
            qkv_ref[s * seg:(s + 1) * seg, ct * LANES:(ct + 1) * LANES] = ybuf[ct, HALO:HALO + seg, :]

    z_ref[...] = proj("z")
    bq_ref[...] = proj("bq").astype(bq_ref.dtype)
    zk = proj("bk")
    zv = proj("bv")
    bk_ref[...] = zk.astype(bk_ref.dtype)
    bv_ref[...] = zv.astype(bv_ref.dtype)
    if emit_maps:
        bk8_ref, bv4_ref = refs[15:17]
        maps_s = refs[-4]
        pitch = tm + 8
        for c in range(H_B):
            pair = zk[:, c * LANES:(c + 1) * LANES]
            maps_s[2 * c * pitch:2 * c * pitch + tm, :] = pair
            maps_s[(2 * c + 1) * pitch:(2 * c + 1) * pitch + tm, :] = pltpu.roll(pair, DQK_B, axis=1)
            maps_s[(2 * H_B + c) * pitch:(2 * H_B + c) * pitch + tm, :] = zv[:, c * DV_B:(c + 1) * DV_B]

        for t in range(tm):
            bk8_ref[t] = maps_s[pl.ds(t, 2 * H_B, stride=pitch), :][:, 0:DQK_B]
            bv4_ref[t] = maps_s[pl.ds(2 * H_B * pitch + t, H_B, stride=pitch), :]
    for unit in units:
        project(*unit)
        convolve(*unit)

    gpar = gpar_ref[...]
    gate_pre = proj("dec")
    xg = gate_pre + gpar[1:2, :]
    softplus = jnp.maximum(xg, 0.0) + jnp.log1p(jnp.exp(-jnp.abs(xg)))
    lane = lax.broadcasted_iota(jnp.int32, (tm, LANES), 1)
    g = jnp.where(lane < 2 * H_A, -jnp.exp(gpar[0:1, :]) * softplus, 0.0)
    beta_ref[...] = jax.nn.sigmoid(pltpu.roll(gate_pre, LANES - 2 * H_A, axis=1))
    pos_c = lax.broadcasted_iota(jnp.int32, (tm, 1), 0) & (CHUNK - 1)
    gcf = g
    gcb = g
    for sft in (1, 2, 4, 8, 16, 32):
        gcf = gcf + jnp.where(pos_c >= sft, pltpu.roll(gcf, sft, axis=0), 0.0)
        gcb = gcb + jnp.where(pos_c < CHUNK - sft, pltpu.roll(gcb, tm - sft, axis=0), 0.0)
    gcs_ref[...] = jnp.where(lane < H_A, gcf, gcb)


def _inproj(x, mod, gain, w_mix_in, conv_w, gpar, rows_per_cond, seq_len, emit_maps):
    t = x.shape[0]
    tm = TOKEN_TILE
    hb = tm // HALO
    n_seg = max(1, tm // seq_len)
    ext = tm // n_seg + 2 * HALO
    row = pl.BlockSpec((tm, D_MODEL), lambda i: (i, 0))
    prev = pl.BlockSpec((HALO, D_MODEL), lambda i: (jnp.maximum(i * hb - 1, 0), 0))
    nxt = pl.BlockSpec((HALO, D_MODEL), lambda i: (jnp.minimum((i + 1) * hb, t // HALO - 1), 0))
    widths = [w for _, _, w in _IN_SEGS]
    out_specs = [pl.BlockSpec((tm, w), lambda i: (i, 0)) for w in widths]
    attn_dtype = BF16 if emit_maps else F32
    out_shape = [jax.ShapeDtypeStruct((t, w), attn_dtype if name in ("bq", "bk", "bv") else F32)
                 for name, _, w in _IN_SEGS]
    if emit_maps:
        out_specs += [pl.BlockSpec((tm, 2 * H_B, DQK_B), lambda i: (i, 0, 0)),
                      pl.BlockSpec((tm, H_B, DV_B), lambda i: (i, 0, 0))]
        out_shape += [jax.ShapeDtypeStruct((t, 2 * H_B, DQK_B), F32), jax.ShapeDtypeStruct((t, H_B, DV_B), F32)]
    return pl.pallas_call(
        functools.partial(_inproj_kernel, tm=tm, seq_len=seq_len, emit_maps=emit_maps),
        grid=(t // tm,),
        in_specs=[row, prev, nxt, _mod_spec(rows_per_cond, tm, lambda s: s), _resident((1, D_MODEL)),
                  _resident((D_MODEL, IN_COLS)), _resident((CONV_K, 3 * A_W)), _resident((8, LANES))],
        out_specs=out_specs,
        out_shape=out_shape,
        scratch_shapes=[pltpu.VMEM((D_MODEL, 3 * B_W), BF16)]
        + ([pltpu.VMEM((3 * H_B * (tm + 8), LANES), F32)] if emit_maps else [])
        + [pltpu.VMEM((D_MODEL // LANES, n_seg * ext, LANES), F32),
                        pltpu.VMEM((n_seg * (ext + 2 * (CONV_K // 2) * 8), 3 * A_W), F32),
                        pltpu.VMEM((3 * A_W // LANES, ext, LANES), F32)],
        compiler_params=_cparams(("arbitrary",)),
        name="inproj",
    )(x, x, x, mod, gain, w_mix_in, conv_w, gpar)


def _post_kernel(*refs, cast):
    x_ref, oa_ref, ob_ref, mod_ref, gain_ref, gfin_ref, wmix_ref, w_in_ref, w_out_ref, o_ref = refs[:10]

    def tokens(w_in, w_out):
        y = _dot(oa_ref[...].astype(BF16), wmix_ref[0:A_W, :]) + _dot(ob_ref[...].astype(BF16), wmix_ref[A_W:, :])
        x = x_ref[...] + mod_ref[5:6, :] * y
        x = _swiglu_update(x, mod_ref, 6, gain_ref[...], w_in, w_out)
        o_ref[...] = (x * lax.rsqrt(jnp.mean(x * x, axis=-1, keepdims=True) + EPS)) * gfin_ref[...]

    if cast:
        step = pl.program_id(0)
        w_in_s, w_out_s = refs[12:14]
        _stage_ffn_weights(step, w_in_ref, w_out_ref, refs[10], refs[11], w_in_s, w_out_s)
        pl.when(step >= W_CHUNKS)(lambda: tokens(w_in_s, w_out_s))
    else:
        tokens(w_in_ref, w_out_ref)


def _post(x, oa, ob, mod, gain, gfin, w_mix, w_in, w_out, rows_per_cond):
    t = x.shape[0]
    tm = TOKEN_TILE
    cast = w_in.dtype == F32
    tile = _tile_of(cast)
    row = pl.BlockSpec((tm, D_MODEL), lambda s: (tile(s), 0))
    half = pl.BlockSpec((tm, A_W), lambda s: (tile(s), 0))
    w_in_specs, w_out_specs, w_shapes, w_scratch = _ffn_weight_plumbing(cast)
    outs = pl.pallas_call(
        functools.partial(_post_kernel, cast=cast),
        grid=(t // tm + (W_CHUNKS if cast else 0),),
        in_specs=[row, half, half, _mod_spec(rows_per_cond, tm, tile), _resident((1, D_MODEL)),
                  _resident((1, D_MODEL)), _resident((A_W + B_W, D_MODEL))] + w_in_specs,
        out_specs=[row] + w_out_specs,
        out_shape=[jax.ShapeDtypeStruct((t, D_MODEL), F32)] + w_shapes,
        scratch_shapes=w_scratch,
        compiler_params=_cparams(("arbitrary",)),
        name="post",
    )(x, oa, ob, mod, gain, gfin, w_mix, w_in, w_out)
    return (outs[0], outs[1], outs[2]) if cast else (outs[0], w_in, w_out)


def _rope(x, cos, sin_signed):
    w = x.shape[-1]
    lane = lax.broadcasted_iota(jnp.int32, x.shape, 1)
    p = ROPE_PAIRS
    partner = jnp.where((lane & (2 * p - 1)) < p, pltpu.roll(x, w - p, axis=1), pltpu.roll(x, p, axis=1))
    return x * cos + partner * sin_signed


def _store_values(v_s, rows, v):
    for h in range(H_B):
        vh = v[:, h * DV_B:(h + 1) * DV_B]
        v_s[rows, 2 * h * DV_B:(2 * h + 1) * DV_B] = vh.astype(BF16)
        v_s[rows, (2 * h + 1) * DV_B:(2 * h + 2) * DV_B] = jnp.ones(vh.shape, BF16)


def _attn_core(q, k_ref, v_ref, lam, gain_ref, o_ref):
    tq = q.shape[0]
    lane = lax.broadcasted_iota(jnp.int32, (tq, DV_B), 1)
    qs = q * (DQK_B ** -0.5)
    heads = [slice(h * DV_B, (h + 1) * DV_B) for h in range(H_B)]
    scores = []
    for hs in heads:
        qh = qs[:, hs]
        q2 = jnp.concatenate([jnp.where(lane < DQK_B, qh, 0.0), jnp.where(lane >= DQK_B, qh, 0.0)], axis=0)
        scores.append(lax.dot_general(q2.astype(BF16), k_ref[:, hs], (((1,), (1,)), ((), ())),
                                      preferred_element_type=F32))
    for h, (hs, s) in enumerate(zip(heads, scores)):
        e = jnp.exp(s - jnp.max(s, axis=-1, keepdims=True))
        rs = _dot(e.astype(BF16), v_ref[:, 2 * h * DV_B:(2 * h + 2) * DV_B])
        r = rs[:, :DV_B] / rs[:, DV_B:]
        o = r[:tq] - lam * r[tq:]
        o = o * lax.rsqrt(jnp.mean(o * o, axis=-1, keepdims=True) + EPS)
        o_ref[:, hs] = (o * gain_ref[...]) * (1.0 - LAM_INIT)


def _lambda(lam_ref):
    l = lam_ref[...]
    s1 = jnp.sum(l[0:1] * l[1:2], axis=-1, keepdims=True)
    s2 = jnp.sum(l[2:3] * l[3:4], axis=-1, keepdims=True)
    return jnp.exp(s1) - jnp.exp(s2) + LAM_INIT


CTX_SEQS = 4


def _attn_ctx_kernel(q_ref, k_ref, v_ref, lam_ref, gain_ref, o_ref, v_s):
    lam = _lambda(lam_ref)

    def one_sequence(bi, carry):
        _store_values(v_s, slice(None), v_ref[bi])
        _attn_core(q_ref[bi], k_ref.at[bi], v_s, lam, gain_ref, o_ref.at[bi])
        return carry

    lax.fori_loop(0, CTX_SEQS, one_sequence, 0)


def _attn_ctx(q, k, v, lam_pack, gain):
    b, l, _ = q.shape
    blk = pl.BlockSpec((CTX_SEQS, l, B_W), lambda i: (i, 0, 0))
    return pl.pallas_call(
        _attn_ctx_kernel,
        grid=(b // CTX_SEQS,),
        in_specs=[blk, blk, blk, _resident((8, LANES)), _resident((1, DV_B))],
        out_specs=blk,
        out_shape=jax.ShapeDtypeStruct((b, l, B_W), F32),
        scratch_shapes=[pltpu.VMEM((l, 2 * B_W), BF16)],
        compiler_params=_cparams(("arbitrary",)),
        name="attn_ctx",
    )(q, k, v, lam_pack, gain)


def _attn_lat_kernel(q_ref, k_ref, v_ref, ck_ref, cv_ref, cosq_ref, sinq_ref, cos_ref, sin_ref,
                     lam_ref, gain_ref, o_ref, k_s, v_s, *, past):
    @pl.when(pl.program_id(1) == 0)
    def _():
        k_s[0:past, :] = ck_ref[...].astype(BF16)
        _store_values(v_s, slice(0, past), cv_ref[...])
        k_s[past:, :] = _rope(k_ref[...], cos_ref[...], sin_ref[...]).astype(BF16)
        _store_values(v_s, slice(past, None), v_ref[...])

    q = _rope(q_ref[...], cosq_ref[...], sinq_ref[...])
    _attn_core(q, k_s, v_s, _lambda(lam_ref), gain_ref, o_ref)


def _attn_lat(q, k, v, ck, cv, cos, sin, lam_pack, gain):
    b, l, _ = q.shape
    past = ck.shape[1]
    tq = Q_TILE
    full = pl.BlockSpec((None, l, B_W), lambda i, j: (i, 0, 0))
    cache_k = cache_v = pl.BlockSpec((None, past, B_W), lambda i, j: (i, 0, 0))
    qblk = pl.BlockSpec((None, tq, B_W), lambda i, j: (i, j, 0))
    tab_q = pl.BlockSpec((tq, B_W), lambda i, j: (j, 0))
    tab = pl.BlockSpec((l, B_W), lambda i, j: (0, 0))
    return pl.pallas_call(
        functools.partial(_attn_lat_kernel, past=past),
        grid=(b, l // tq),
        in_specs=[qblk, full, full, cache_k, cache_v, tab_q, tab_q, tab, tab,
                  pl.BlockSpec((8, LANES), lambda i, j: (0, 0)), pl.BlockSpec((1, DV_B), lambda i, j: (0, 0))],
        out_specs=qblk,
        out_shape=jax.ShapeDtypeStruct((b, l, B_W), F32),
        scratch_shapes=[pltpu.VMEM((past + l, B_W), BF16), pltpu.VMEM((past + l, 2 * B_W), BF16)],
        compiler_params=_cparams(("arbitrary", "arbitrary")),
        name="attn_lat",
    )(q, k, v, ck, cv, cos, sin, cos, sin, lam_pack, gain)


def _rope_tables(length):
    pairs = ROPE_PAIRS
    n_rows = length // GRID_W
    pos_row = np.repeat(np.arange(n_rows), GRID_W).astype(np.float32)
    pos_col = np.tile(np.arange(GRID_W), n_rows).astype(np.float32)
    inv = (np.float32(ROPE_THETA) ** (-np.arange(pairs, dtype=np.float32) / np.float32(pairs))).astype(np.float32)
    ang_r = pos_row[:, None] * inv
    ang_c = pos_col[:, None] * inv
    cos = np.concatenate([np.cos(ang_r), np.cos(ang_r), np.cos(ang_c), np.cos(ang_c)], axis=-1)
    sin = np.concatenate([-np.sin(ang_r), np.sin(ang_r), -np.sin(ang_c), np.sin(ang_c)], axis=-1)
    reps = B_W // DQK_B
    return jnp.asarray(np.tile(cos, (1, reps)), F32), jnp.asarray(np.tile(sin, (1, reps)), F32)


def _block_diag(x):
    lo, hi = x[:, :LANES], x[:, LANES:]
    first = lax.broadcasted_iota(jnp.int32, lo.shape, 1) < DK_A
    top = jnp.concatenate([jnp.where(first, lo, 0), jnp.where(first, 0, lo)], axis=0)
    bot = jnp.concatenate([jnp.where(first, hi, 0), jnp.where(first, 0, hi)], axis=0)
    zero = jnp.zeros_like(top)
    return jnp.concatenate([jnp.concatenate([top, zero], axis=1), jnp.concatenate([zero, bot], axis=1)], axis=0)


SOLVE_CHUNKS = 8
DELTA_ROWS = 1024


def _delta_kernel(*refs, nb, length, has_state):
    if has_state:
        (q_ref, k_ref, v_ref, z_ref, gcs_ref, beta_ref, dn_ref, s0_ref, o_ref, st_ref, *scr) = refs
    else:
        (q_ref, k_ref, v_ref, z_ref, gcs_ref, beta_ref, dn_ref, o_ref, st_ref, *scr) = refs
        s0_ref = None
    qs, ks, vs, gexp, bexp, u0_s, kg_s, dec_s, w_s, qg_s, qkm_s, of_s, ob_s, s_s = scr
    L = length
    R = nb * L
    n_chunks = L // CHUNK
    grp = pl.program_id(1)
    W = GROUP_W
    N_GRP = A_W // W

    r256 = lax.broadcasted_iota(jnp.int32, (W, W), 0)
    c256 = lax.broadcasted_iota(jnp.int32, (W, W), 1)
    dk_bits, w_bits = DK_A.bit_length() - 1, W.bit_length() - 1
    bd_mask = (r256 >> dk_bits) == (c256 >> dk_bits)
    ones_bd = jnp.where(bd_mask, 1.0, 0.0).astype(BF16)
    ri = lax.broadcasted_iota(jnp.int32, (CHUNK, W), 0)
    cj = lax.broadcasted_iota(jnp.int32, (CHUNK, W), 1) & (CHUNK - 1)
    eye_p = ri == cj
    NS = 2 * H_A
    er = lax.broadcasted_iota(jnp.int32, (LANES, 4 * W), 0)
    ec = lax.broadcasted_iota(jnp.int32, (LANES, 4 * W), 1)
    src = ((ec >> w_bits) & 1) * H_A + grp * 4 + ((ec & (W - 1)) >> dk_bits)
    ns_bits = NS.bit_length() - 1
    expand = jnp.where((er < 4 * NS) & ((er & (NS - 1)) == src) & (((er >> ns_bits) & 1) == (ec >> (w_bits + 1))),
                       1.0, 0.0).astype(BF16)
    lane_s = lax.broadcasted_iota(jnp.int32, (L, LANES), 1)

    bd = _block_diag

    def head_sums(y):
        return _dot(y.astype(BF16), ones_bd)

    def prologue(bi, carry):
        rows_b = pl.ds(pl.multiple_of(bi * L, L), L)

        def l2n(y):
            return y * lax.rsqrt(head_sums(y * y) + EPS)

        qs[rows_b, :] = l2n(q_ref[bi]) * (DK_A ** -0.5)
        ks[rows_b, :] = l2n(k_ref[bi])
        vs[rows_b, :] = v_ref[bi]
        gc = gcs_ref[bi]
        bt = jnp.where(lane_s < NS, beta_ref[bi], 0.0)
        gc_hi = gc.astype(BF16).astype(F32)
        bt_hi = bt.astype(BF16).astype(F32)
        packed = (gc_hi + pltpu.roll(bt_hi, NS, axis=1) + pltpu.roll(gc - gc_hi, 2 * NS, axis=1)
                  + pltpu.roll(bt - bt_hi, 3 * NS, axis=1))
        both = _dot(packed.astype(BF16), expand)
        for d in range(2):
            gexp[d, rows_b, :] = both[:, d * W:(d + 1) * W]
            bexp[d, rows_b, :] = both[:, 2 * W + d * W:2 * W + (d + 1) * W]
        return carry

    lax.fori_loop(0, nb, prologue, 0)

    def solve_body(j, carry):
        chunks = []
        for k in range(SOLVE_CHUNKS):
            chunk = j * SOLVE_CHUNKS + k
            rows = pl.ds(pl.multiple_of(chunk * CHUNK, CHUNK), CHUNK)
            kc = ks[rows, :]
            qc = qs[rows, :]
            kb = kc.astype(BF16)
            res = lax.dot_general(jnp.concatenate([qc.astype(BF16), kb], axis=0), bd(kb),
                                  (((1,), (1,)), ((), ())), preferred_element_type=F32)
            qk, kk = res[:CHUNK], res[CHUNK:]
            ch = dict(rows=rows, bd_k=bd(kb), bd_v=bd(vs[rows, :].astype(BF16)), t=[], scale_u=[], scale_w=[])
            nsum = None
            for d in range(2):
                ge = gexp[d, rows, :]
                be = bexp[d, rows, :]
                if d == 0:
                    incl, strict, gl = ri >= cj, ri > cj, ge[CHUNK - 1:CHUNK, :]
                else:
                    incl, strict, gl = ri <= cj, ri < cj, ge[0:1, :]
                g_row = jnp.sum(jnp.where(eye_p, ge, 0.0), axis=0, keepdims=True)
                b_row = jnp.sum(jnp.where(eye_p, be, 0.0), axis=0, keepdims=True)
                dec_m = jnp.exp(jnp.where(incl, ge - g_row, -jnp.inf))
                nmat = jnp.where(strict, be * kk * dec_m, 0.0)
                nsum = nmat if nsum is None else nsum + nmat
                qkm_s[grp, d, rows, :] = jnp.where(incl, qk * dec_m, 0.0).astype(BF16)
                qg_s[grp, d, rows, :] = (qc * jnp.exp(ge)).astype(BF16)
                kgt = (kc * jnp.exp(gl - ge)).T
                kg_s[grp, d, rows, :] = jnp.concatenate(
                    [kgt[h * DK_A:(h + 1) * DK_A, :] for h in range(4)], axis=1).astype(BF16)
                dec_s[grp, d, pl.ds(pl.multiple_of(chunk * 8, 8), 8), :] = jnp.broadcast_to(jnp.exp(gl), (8, W))
                ch["scale_u"].append(b_row)
                ch["scale_w"].append(b_row * jnp.exp(g_row))
                ch["t"].append(jnp.where(eye_p, 1.0, 0.0) - jnp.where((ri >> 1) == (cj >> 1), nmat, 0.0))
            ch["n_both"] = bd(nsum.astype(BF16))
            chunks.append(ch)

        for lb in range(1, 6):
            same = (ri >> (lb + 1)) == (cj >> (lb + 1))
            pair = (same & (((ri >> lb) & 1) == 1) & (((cj >> lb) & 1) == 0),
                    same & (((cj >> lb) & 1) == 1) & (((ri >> lb) & 1) == 0))
            for ch in chunks:
                ch["tb"] = [t.astype(BF16) for t in ch["t"]]
                ch["y"] = _dot(jnp.concatenate(ch["tb"], axis=0), ch["n_both"]).astype(BF16)
            even_rows = ((ri >> lb) & 1) == 0
            for ch in chunks:
                ys = jnp.concatenate([jnp.where(pair[0], ch["y"][:CHUNK], 0),
                                      jnp.where(pair[1], ch["y"][CHUNK:], 0)], axis=0)
                upd = _dot(ys, bd(jnp.where(even_rows, ch["tb"][0], ch["tb"][1])))
                ch["t"] = [ch["t"][d] - upd[d * CHUNK:(d + 1) * CHUNK] for d in range(2)]
        for ch in chunks:
            tu = jnp.concatenate([(ch["t"][d] * ch["scale_u"][d]).astype(BF16) for d in range(2)], axis=0)
            tw = jnp.concatenate([(ch["t"][d] * ch["scale_w"][d]).astype(BF16) for d in range(2)], axis=0)
            u0 = _dot(tu, ch["bd_v"])
            w = _dot(tw, ch["bd_k"]).astype(BF16)
            for d in range(2):
                u0_s[grp, d, ch["rows"], :] = u0[d * CHUNK:(d + 1) * CHUNK]
                w_s[grp, d, ch["rows"], :] = w[d * CHUNK:(d + 1) * CHUNK]
        return carry

    lax.fori_loop(0, R // (SOLVE_CHUNKS * CHUNK), solve_body, 0)

    if has_state:
        for bi in range(nb):
            for d in range(2):
                for h in range(4):
                    s_s[grp, bi, d, :, h * DK_A:(h + 1) * DK_A] = s0_ref[bi, d, h]
    else:
        s_s[grp] = jnp.zeros(s_s.shape[1:], F32)

    def scan_body(i, carry):
        chains = []
        for gi in range(N_GRP):
            for bi in range(nb):
                for d in range(2):
                    c = i if d == 0 else n_chunks - 1 - i
                    rows = pl.ds(pl.multiple_of(bi * L + c * CHUNK, CHUNK), CHUNK)
                    tile = pl.ds(pl.multiple_of((bi * n_chunks + c) * 8, 8), 8)
                    chains.append(dict(
                        gi=gi, bi=bi, d=d, rows=rows, s=s_s[gi, bi, d], u0=u0_s[gi, d, rows, :],
                        kg=kg_s[gi, d, rows, :], qkm=qkm_s[gi, d, rows, :], decay=dec_s[gi, d, tile, :][0:1],
                        wq=jnp.concatenate([w_s[gi, d, rows, :], qg_s[gi, d, rows, :]], axis=0)))
        for c in chains:
            c["ws"] = _dot(c["wq"], bd(c["s"].astype(BF16)))
        for c in chains:
            c["ub"] = (c["u0"] - c["ws"][:CHUNK]).astype(BF16)
        for c in chains:
            both = _dot(jnp.concatenate([c["qkm"], c["kg"]], axis=0), bd(c["ub"]))
            c["o"] = c["ws"][CHUNK:] + both[:CHUNK]
            c["s_new"] = c["s"] * c["decay"] + both[CHUNK:]
        for c in chains:
            s_s[c["gi"], c["bi"], c["d"]] = c["s_new"]
            if c["d"] == 0:
                of_s[c["gi"], c["rows"], :] = c["o"]
            else:
                ob_s[c["gi"], c["rows"], :] = c["o"]
        return carry

    def epilogue(bi, carry):
        rows_b = pl.ds(pl.multiple_of(bi * L, L), L)
        for gi in range(N_GRP):
            lanes = slice(gi * W, (gi + 1) * W)
            o = of_s[gi, rows_b, :] + ob_s[gi, rows_b, :]
            ms = head_sums(o * o) * (1.0 / DK_A)
            o = o * lax.rsqrt(ms + EPS) * dn_ref[...]
            o_ref[bi, :, lanes] = o * _silu(z_ref[bi, :, lanes])
        return carry

    @pl.when(grp == N_GRP - 1)
    def _():
        lax.fori_loop(0, n_chunks, scan_body, 0)
        lax.fori_loop(0, nb, epilogue, 0)
        for gi in range(N_GRP):
            for bi in range(nb):
                for d in range(2):
                    for h in range(4):
                        st_ref[bi, d, gi * 4 + h] = s_s[gi, bi, d, :, h * DK_A:(h + 1) * DK_A]


def _delta(qkv, z, gcs, beta, dnorm, s0):
    b, l, _ = qkv.shape
    has_state = s0 is not None
    w = GROUP_W
    n_grp = A_W // w
    nb = DELTA_ROWS // l
    r = nb * l

    def slab(k):
        return pl.BlockSpec((nb, l, w), lambda i, g: (i, 0, k * n_grp + g))

    narrow = pl.BlockSpec((nb, l, LANES), lambda i, g: (i, 0, 0))
    wide = pl.BlockSpec((nb, l, A_W), lambda i, g: (i, 0, 0))
    in_specs = [slab(0), slab(1), slab(2), wide, narrow, narrow,
                pl.BlockSpec((1, w), lambda i, g: (0, 0))]
    args = [qkv, qkv, qkv, z, gcs, beta, dnorm]
    if has_state:
        in_specs.append(pl.BlockSpec((nb, None, 2, 4, DK_A, DK_A), lambda i, g: (i, 0, 0, g, 0, 0)))
        args.append(s0)
    per_grp = lambda shape, dt: pltpu.VMEM((n_grp,) + shape, dt)
    scratch = ([pltpu.VMEM((r, w), F32)] * 3 + [pltpu.VMEM((2, r, w), F32)] * 2
               + [per_grp((2, r, w), F32), per_grp((2, r, w), BF16), per_grp((2, r // CHUNK * 8, w), F32)]
               + [per_grp((2, r, w), BF16)] * 3 + [per_grp((r, w), F32)] * 2
               + [per_grp((nb, 2, DK_A, w), F32)])
    return pl.pallas_call(
        functools.partial(_delta_kernel, nb=nb, length=l, has_state=has_state),
        grid=(b // nb, n_grp),
        in_specs=in_specs,
        out_specs=[wide, pl.BlockSpec((nb, None, 2, H_A, DK_A, DK_A), lambda i, g: (i, 0, 0, 0, 0, 0))],
        out_shape=[jax.ShapeDtypeStruct((b, l, A_W), F32),
                   jax.ShapeDtypeStruct((b, 1, 2, H_A, DK_A, DK_A), F32)],
        scratch_shapes=scratch,
        compiler_params=_cparams(("arbitrary", "arbitrary")),
        name="delta_state" if has_state else "delta",
    )(*args)


def kernel(x_prompt, x_sample, cache_diff_k, cache_diff_v, state_delta, c, c_ctx, w_ada, b_ada, norm_ffn1,
           w_ffn1_in, w_ffn1_out, norm_mix, w_in, conv_w, a_log, dt_bias, delta_norm, lambda_q1, lambda_k1,
           lambda_q2, lambda_k2, diff_norm, w_out, norm_ffn2, w_ffn2_in, w_ffn2_out, norm_final):
    bp, lp, _ = x_prompt.shape
    bs, ls, _ = x_sample.shape
    past = cache_diff_k.shape[2]

    w_mix_out = w_out[0].astype(BF16)
    w_mix_in = w_in[0].astype(BF16)
    g1 = norm_ffn1[0].reshape(1, D_MODEL)
    gm = norm_mix[0].reshape(1, D_MODEL)
    g2 = norm_ffn2[0].reshape(1, D_MODEL)
    gf = norm_final.reshape(1, D_MODEL)
    gpar = jnp.pad(jnp.stack([a_log[0].reshape(-1), dt_bias[0].reshape(-1)]), ((0, 6), (0, LANES - 2 * H_A)))
    dnorm = jnp.tile(delta_norm[0], 4).reshape(1, GROUP_W)
    lam_pack = jnp.pad(jnp.concatenate([lambda_q1, lambda_k1, lambda_q2, lambda_k2], axis=0),
                       ((0, 4), (0, LANES - DQK_B)))
    dgain = diff_norm[0].reshape(1, DV_B)
    cos, sin = _rope_tables(ls)

    cond = jnp.pad(jnp.concatenate([c_ctx[None, :], c], axis=0), ((0, 8 - 1 - bs), (0, 0)))
    mod = _adaln(cond, w_ada[0], b_ada[0]).reshape(8, N_MOD, D_MODEL)

    x1_prompt, x1_sample = _ffn1(x_prompt.reshape(bp * lp, D_MODEL), x_sample.reshape(bs * ls, D_MODEL), mod, g1,
                                 w_ffn1_in[0], w_ffn1_out[0], ls)

    def trunk(x1, b, l, rows_per_cond, ctx, w2):
        outs = _inproj(x1, mod, gm, w_mix_in, conv_w[0], gpar, rows_per_cond, l, ctx is None)
        qkv, z, bq, bk, bv, gcs, beta = outs[:7]
        r3 = lambda a: a.reshape(b, l, a.shape[-1])
        s0 = None if ctx is None else ctx[2]
        oa, st = _delta(r3(qkv), r3(z), r3(gcs), r3(beta), dnorm, s0)
        if ctx is None:
            ob = _attn_ctx(r3(bq), r3(bk), r3(bv), lam_pack, dgain)
        else:
            ob = _attn_lat(r3(bq), r3(bk), r3(bv), ctx[0], ctx[1], cos, sin, lam_pack, dgain)
        y, *w2 = _post(x1, oa.reshape(b * l, A_W), ob.reshape(b * l, B_W), mod, g2, gf, w_mix_out, *w2,
                       rows_per_cond)
        return y.reshape(b, l, D_MODEL), outs[7:], st, w2

    y_prompt, (k_maps, v_heads), s_ctx, w2 = trunk(x1_prompt, bp, lp, None, None, (w_ffn2_in[0], w_ffn2_out[0]))
    ctx = (cache_diff_k[:, 0].reshape(bs, past, B_W), cache_diff_v[:, 0].reshape(bs, past, B_W), state_delta)
    y_sample = trunk(x1_sample, bs, ls, ls, ctx, w2)[0]
    new_k = k_maps.reshape(bp, 1, lp, H_B, 2, DQK_B)
    new_v = v_heads.reshape(bp, 1, lp, H_B, DV_B)
    return (y_prompt, y_sample, new_k, new_v, s_ctx)
```
